```python
import math
import jax, jax.numpy as jnp
from jax import lax
import numpy as np

D_MODEL = 1024
BATCH = 8
SEQ = 2048
DEPTH = 2

CTX_LEN = 256
GRID_W = 64
N_EVEN = (DEPTH + 1) // 2
N_ODD = DEPTH // 2
N_MOD = 6
EPS = 1e-6
CHUNK = 128

S5_WIDTH = D_MODEL // 2
S5_GROUP = 16
S5_GROUPS = S5_WIDTH // S5_GROUP
S5_STATE = 64

ML_WIDTH = D_MODEL // 2
ML_HEADS = 4
ML_HEAD_DIM = ML_WIDTH // ML_HEADS
ML_CONV = 5
AB_IN = S5_WIDTH + 2 * ML_WIDTH + 4 * ML_HEADS
AB_OUT = S5_WIDTH + ML_WIDTH

RET_HEADS = 4
RET_QK = D_MODEL
RET_V = 2 * D_MODEL
RET_HK = RET_QK // RET_HEADS
RET_HV = RET_V // RET_HEADS
RET_IN = 2 * RET_QK + 2 * RET_V
ROPE_BASE = 10000.0

FFN_HIDDEN = -(-(8 * D_MODEL) // (3 * 256)) * 256

kernel_name = 'hybrid_s5_mlstm_retention_dit'


def rmsnorm(x, g):
    xf = x.astype(jnp.float32)
    y = xf * lax.rsqrt(jnp.mean(xf * xf, -1, keepdims=True) + EPS)
    return (y * g.astype(jnp.float32)).astype(x.dtype)


def headwise_norm(h, g):
    mu = jnp.mean(h, -1, keepdims=True)
    var = jnp.mean(jnp.square(h - mu), -1, keepdims=True)
    y = (h - mu) * lax.rsqrt(var + EPS)
    bn, ln, hn, dn = h.shape
    return y.reshape(bn, ln, hn * dn) * g.astype(jnp.float32)


def modulate(h, shift, scale):
    return h * (1.0 + scale) + shift


def mod_parts(vec, w, b):
    m = jax.nn.silu(vec) @ w + b
    return m.reshape(m.shape[0], 1, N_MOD, D_MODEL)


def swiglu(h, w1, w3, w2):
    return (jax.nn.silu(h @ w1) * (h @ w3)) @ w2


def dwconv_centred(x, w, b):
    y = lax.conv_general_dilated(x.astype(w.dtype), w[:, None, :], window_strides=(1,), padding='SAME',
                                 dimension_numbers=('NWC', 'WIO', 'NWC'), feature_group_count=x.shape[-1])
    return y + b


def grid_rope(rows):
    row = jnp.repeat(jnp.arange(rows, dtype=jnp.float32), GRID_W)
    col = jnp.tile(jnp.arange(GRID_W, dtype=jnp.float32), rows)
    n_freq = RET_HK // 4
    inv = ROPE_BASE ** (-jnp.arange(n_freq, dtype=jnp.float32) / n_freq)
    ang = jnp.concatenate([row[:, None] * inv, col[:, None] * inv], -1)
    return jnp.cos(ang), jnp.sin(ang)


def apply_rope(t, cos, sin):
    half = t.shape[-1] // 2
    t1, t2 = t[..., :half], t[..., half:]
    cs, sn = cos[None, :, None, :], sin[None, :, None, :]
    return jnp.concatenate([t1 * cs - t2 * sn, t2 * cs + t1 * sn], -1)


def maybe_flip(t, flag, axis):
    return jnp.flip(t, axis) if flag else t


def s5_discretise(a_re, a_im, log_dt):
    a_re = a_re.astype(jnp.float32)
    a_im = a_im.astype(jnp.float32)
    dt = jnp.exp(log_dt.astype(jnp.float32))[:, None]
    mag = jnp.exp(a_re * dt)
    abar_re, abar_im = mag * jnp.cos(a_im * dt), mag * jnp.sin(a_im * dt)
    den = a_re * a_re + a_im * a_im
    nr = abar_re - 1.0
    coef_re = (nr * a_re + abar_im * a_im) / den
    coef_im = (abar_im * a_re - nr * a_im) / den
    return abar_re, abar_im, coef_re, coef_im


def _complex_affine_combine(e1, e2):
    a1r, a1i, b1r, b1i = e1
    a2r, a2i, b2r, b2i = e2
    return (a1r * a2r - a1i * a2i, a1r * a2i + a1i * a2r,
            a2r * b1r - a2i * b1i + b2r, a2r * b1i + a2i * b1r + b2i)


def s5_scan(u, disc, b_re, b_im, h0_re, h0_im):
    abar_re, abar_im, coef_re, coef_im = disc
    bu_re = jnp.einsum('blgc,gpc->blgp', u, b_re)
    bu_im = jnp.einsum('blgc,gpc->blgp', u, b_im)
    x_re = coef_re * bu_re - coef_im * bu_im
    x_im = coef_re * bu_im + coef_im * bu_re
    x_re = x_re.at[:, 0].add(abar_re * h0_re - abar_im * h0_im)
    x_im = x_im.at[:, 0].add(abar_re * h0_im + abar_im * h0_re)
    n_pos = u.shape[1]
    a_seq_re = jnp.broadcast_to(abar_re[None, None], (1, n_pos) + abar_re.shape)
    a_seq_im = jnp.broadcast_to(abar_im[None, None], (1, n_pos) + abar_im.shape)
    _, _, s_re, s_im = lax.associative_scan(_complex_affine_combine, (a_seq_re, a_seq_im, x_re, x_im), axis=1)
    return s_re, s_im


def s5_readout(c_re, c_im, s_re, s_im):
    return jnp.einsum('gcp,blgp->blgc', c_re, s_re) - jnp.einsum('gcp,blgp->blgc', c_im, s_im)


def s5_mixer(u_lat, u_ctx, need_ctx, a_re, a_im, log_dt, b_re, b_im, c_re, c_im, d_skip, glu_w, glu_b):
    f32 = jnp.float32
    bn = u_lat.shape[0]

    def groups(u):
        return u.astype(f32).reshape(u.shape[0], u.shape[1], S5_GROUPS, S5_GROUP)

    ul, uc = groups(u_lat), groups(u_ctx)
    dsk = d_skip.astype(f32).reshape(S5_GROUPS, S5_GROUP)
    y_lat = dsk * ul
    y_ctx = dsk * uc if need_ctx else None
    zero = jnp.zeros((bn, S5_GROUPS, S5_STATE), f32)
    for dr in range(2):
        disc = s5_discretise(a_re[dr], a_im[dr], log_dt[dr])
        br, bi = b_re[dr].astype(f32), b_im[dr].astype(f32)
        cr, ci = c_re[dr].astype(f32), c_im[dr].astype(f32)
        sc_re, sc_im = s5_scan(maybe_flip(uc, dr, 1), disc, br, bi, zero, zero)
        sl_re, sl_im = s5_scan(maybe_flip(ul, dr, 1), disc, br, bi, sc_re[:, -1], sc_im[:, -1])
        y_lat = y_lat + maybe_flip(s5_readout(cr, ci, sl_re, sl_im), dr, 1)
        if need_ctx:
            y_ctx = y_ctx + maybe_flip(s5_readout(cr, ci, sc_re, sc_im), dr, 1)

    def glu(y):
        y = jax.nn.gelu(y.reshape(y.shape[0], y.shape[1], S5_WIDTH))
        return y * jax.nn.sigmoid(y @ glu_w.astype(f32) + glu_b.astype(f32))

    return glu(y_lat), (glu(y_ctx) if need_ctx else None)


def mlstm_scan(q, k, v, ig, fg, state, with_output):
    bn, hn, n_pos, dh = q.shape
    nc = n_pos // CHUNK

    def chunks(t):
        return t.reshape((bn, hn, nc, CHUNK) + t.shape[3:])

    q, k, v, ii = chunks(q), chunks(k), chunks(v), chunks(ig)
    b = jnp.cumsum(jax.nn.log_sigmoid(chunks(fg)), -1)
    g_end = b[..., -1:] - b + ii
    g_max = jnp.max(g_end, -1)
    w_end = jnp.exp(g_end - g_max[..., None])
    c_inc = jnp.einsum('bhns,bhnsv,bhnsk->bhnvk', w_end, v, k)
    n_inc = jnp.einsum('bhns,bhnsk->bhnk', w_end, k)

    def step(carry, xs):
        cm, nv, m = carry
        ci, ni, gm, bl = xs
        m_new = jnp.maximum(bl + m, gm)
        dec = jnp.exp(bl + m - m_new)
        inj = jnp.exp(gm - m_new)
        return (dec[..., None, None] * cm + inj[..., None, None] * ci,
                dec[..., None] * nv + inj[..., None] * ni, m_new), (cm, nv, m)

    xs = (jnp.moveaxis(c_inc, 2, 0), jnp.moveaxis(n_inc, 2, 0), jnp.moveaxis(g_max, 2, 0), jnp.moveaxis(b[..., -1], 2, 0))
    final, (c_in, n_in, m_in) = lax.scan(step, state, xs)
    if not with_output:
        return None, final
    c_in, n_in, m_in = jnp.moveaxis(c_in, 0, 2), jnp.moveaxis(n_in, 0, 2), jnp.moveaxis(m_in, 0, 2)
    causal = jnp.tril(jnp.ones((CHUNK, CHUNK), dtype=bool))
    dmat = jnp.where(causal, b[..., :, None] - b[..., None, :] + ii[..., None, :], -jnp.inf)
    m_t = jnp.maximum(b + m_in[..., None], jnp.max(dmat, -1))
    inter = jnp.exp(b + m_in[..., None] - m_t)
    s = jnp.exp(dmat - m_t[..., None]) * jnp.einsum('bhntk,bhnsk->bhnts', q, k)
    num = jnp.einsum('bhnts,bhnsv->bhntv', s, v) + inter[..., None] * jnp.einsum('bhnvk,bhntk->bhntv', c_in, q)
    den = jnp.sum(s, -1) + inter * jnp.einsum('bhnk,bhntk->bhnt', n_in, q)
    h = num / jnp.maximum(jnp.abs(den), jnp.exp(-m_t))[..., None]
    return h.reshape(bn, hn, n_pos, dh), final


def mlstm_mixer(p_lat, p_ctx, need_ctx, conv_w, conv_b, wq, wk, wv, gate_b, norm_g):
    f32 = jnp.float32

    def prepare(p):
        bn, n_pos, _ = p.shape
        xm = p[..., :ML_WIDTH]
        og = p[..., ML_WIDTH:2 * ML_WIDTH]
        gates = (p[..., 2 * ML_WIDTH:] + gate_b).astype(f32).reshape(bn, n_pos, 4, ML_HEADS)
        gates = jnp.transpose(gates, (2, 0, 3, 1))
        xc = jax.nn.silu(dwconv_centred(xm, conv_w, conv_b))
        xc_h = xc.astype(f32).reshape(bn, n_pos, ML_HEADS, ML_HEAD_DIM)
        xm_h = xm.astype(f32).reshape(bn, n_pos, ML_HEADS, ML_HEAD_DIM)
        q = jnp.einsum('blhd,hde->bhle', xc_h, wq.astype(f32))
        k = jnp.einsum('blhd,hde->bhle', xc_h, wk.astype(f32)) * (ML_HEAD_DIM ** -0.5)
        v = jnp.einsum('blhd,hde->bhle', xm_h, wv.astype(f32))
        return q, k, v, gates, og

    ql, kl, vl, gl, ogl = prepare(p_lat)
    qc, kc, vc, gc, ogc = prepare(p_ctx)
    bn = ql.shape[0]
    zero_state = (jnp.zeros((bn, ML_HEADS, ML_HEAD_DIM, ML_HEAD_DIM), f32),
                  jnp.zeros((bn, ML_HEADS, ML_HEAD_DIM), f32),
                  jnp.zeros((bn, ML_HEADS), f32))
    h_lat, h_ctx = None, None
    for dr in range(2):
        hc, st = mlstm_scan(maybe_flip(qc, dr, 2), maybe_flip(kc, dr, 2), maybe_flip(vc, dr, 2),
                            maybe_flip(gc[2 * dr], dr, 2), maybe_flip(gc[2 * dr + 1], dr, 2), zero_state, need_ctx)
        hl, _ = mlstm_scan(maybe_flip(ql, dr, 2), maybe_flip(kl, dr, 2), maybe_flip(vl, dr, 2),
                           maybe_flip(gl[2 * dr], dr, 2), maybe_flip(gl[2 * dr + 1], dr, 2), st, True)
        hl = maybe_flip(hl, dr, 2)
        h_lat = hl if h_lat is None else h_lat + hl
        if need_ctx:
            hc = maybe_flip(hc, dr, 2)
            h_ctx = hc if h_ctx is None else h_ctx + hc

    def finish(h, og):
        h = jnp.transpose(h, (0, 2, 1, 3))
        return jax.nn.sigmoid(og.astype(f32)) * headwise_norm(h, norm_g)

    return finish(h_lat, ogl), (finish(h_ctx, ogc) if need_ctx else None)


def even_mixer(h_lat, h_ctx, need_ctx, w_in, w_out, s5_params, ml_params):
    p_lat = h_lat @ w_in
    p_ctx = h_ctx @ w_in
    s_lat, s_ctx = s5_mixer(p_lat[..., :S5_WIDTH], p_ctx[..., :S5_WIDTH], need_ctx, *s5_params)
    m_lat, m_ctx = mlstm_mixer(p_lat[..., S5_WIDTH:], p_ctx[..., S5_WIDTH:], need_ctx, *ml_params)
    y_lat = jnp.concatenate([s_lat, m_lat], -1) @ w_out
    y_ctx = (jnp.concatenate([s_ctx, m_ctx], -1) @ w_out) if need_ctx else None
    return y_lat, y_ctx


def retention_scan(q, k, v, log_gamma, state, include_diag, with_output):
    bn, hn, n_pos, dk = q.shape
    nc = n_pos // CHUNK
    q = q.reshape(bn, hn, nc, CHUNK, dk)
    k = k.reshape(bn, hn, nc, CHUNK, dk)
    v = v.reshape(bn, hn, nc, CHUNK, v.shape[-1])
    lg = log_gamma.astype(jnp.float32)[:, None]
    pos = jnp.arange(CHUNK, dtype=jnp.float32)
    w_end = jnp.exp(lg * (CHUNK - 1.0 - pos))
    s_inc = jnp.einsum('hs,bhnsk,bhnsv->bhnkv', w_end, k, v)
    chunk_decay = jnp.exp(lg[:, 0] * CHUNK)[None, :, None, None]

    def step(s_state, inc):
        return chunk_decay * s_state + inc, s_state

    final, s_in = lax.scan(step, state, jnp.moveaxis(s_inc, 2, 0))
    if not with_output:
        return None, final
    s_in = jnp.moveaxis(s_in, 0, 2)
    diff = pos[:, None] - pos[None, :]
    mask = (diff >= 0) if include_diag else (diff > 0)
    decay = jnp.where(mask, jnp.exp(lg[:, :, None] * jnp.maximum(diff, 0.0)), 0.0)
    scores = jnp.einsum('bhntk,bhnsk->bhnts', q, k) * decay[None, :, None]
    inter = jnp.exp(lg * (pos + 1.0))
    out = jnp.einsum('bhnts,bhnsv->bhntv', scores, v) + \
        inter[None, :, None, :, None] * jnp.einsum('bhntk,bhnkv->bhntv', q, s_in)
    return out.reshape(bn, hn, n_pos, -1), final


def retention_mixer(h_lat, h_ctx, need_ctx, rope, w_in, w_out, log_gamma, norm_g):
    f32 = jnp.float32

    def prepare(h, rotate):
        bn, n_pos, _ = h.shape
        p = (h @ w_in).astype(f32)
        q = p[..., :RET_QK].reshape(bn, n_pos, RET_HEADS, RET_HK)
        k = p[..., RET_QK:2 * RET_QK].reshape(bn, n_pos, RET_HEADS, RET_HK)
        v = p[..., 2 * RET_QK:2 * RET_QK + RET_V].reshape(bn, n_pos, RET_HEADS, RET_HV)
        g = p[..., 2 * RET_QK + RET_V:]
        if rotate:
            q, k = apply_rope(q, *rope), apply_rope(k, *rope)
        q = q * (RET_HK ** -0.5)
        return (jnp.transpose(q, (0, 2, 1, 3)), jnp.transpose(k, (0, 2, 1, 3)), jnp.transpose(v, (0, 2, 1, 3)), g)

    ql, kl, vl, gl = prepare(h_lat, True)
    qc, kc, vc, gc = prepare(h_ctx, False)
    bn = ql.shape[0]
    zero = jnp.zeros((bn, RET_HEADS, RET_HK, RET_HV), f32)
    o_lat, o_ctx = None, None
    for dr in range(2):
        oc, st = retention_scan(maybe_flip(qc, dr, 2), maybe_flip(kc, dr, 2), maybe_flip(vc, dr, 2),
                                log_gamma[dr], zero, dr == 0, need_ctx)
        ol, _ = retention_scan(maybe_flip(ql, dr, 2), maybe_flip(kl, dr, 2), maybe_flip(vl, dr, 2),
                               log_gamma[dr], st, dr == 0, True)
        ol = maybe_flip(ol, dr, 2)
        o_lat = ol if o_lat is None else o_lat + ol
        if need_ctx:
            oc = maybe_flip(oc, dr, 2)
            o_ctx = oc if o_ctx is None else o_ctx + oc

    def finish(o, g):
        return (jax.nn.silu(g) * headwise_norm(jnp.transpose(o, (0, 2, 1, 3)), norm_g)) @ w_out

    return finish(o_lat, gl), (finish(o_ctx, gc) if need_ctx else None)


def setup_inputs(seed: int = 0) -> dict:
    key = jax.random.key(seed)
    keys = iter(jax.random.split(key, 48))

    def nrm(shape, scale):
        return scale * jax.random.normal(next(keys), shape, jnp.float32)

    d = D_MODEL
    f_bias = jnp.linspace(3.0, 6.0, ML_HEADS, dtype=jnp.float32)
    i_bias = jnp.zeros((ML_HEADS,), jnp.float32)
    gate_base = jnp.concatenate([i_bias, f_bias, i_bias, f_bias])
    gamma_base = jnp.log(1.0 - 2.0 ** (-5.0 - jnp.arange(RET_HEADS, dtype=jnp.float32)))
    return {
        'x': nrm((BATCH, SEQ, d), 1.0),
        'c': nrm((BATCH, d), 1.0),
        'ctx': nrm((BATCH, CTX_LEN, d), 1.0),
        'c_ctx': nrm((d,), 1.0),
        'ada_w': nrm((DEPTH, d, N_MOD * d), 0.5 * d ** -0.5),
        'ada_b': nrm((DEPTH, N_MOD * d), 0.02),
        'norm1_g': 1.0 + nrm((DEPTH, d), 0.02),
        'norm2_g': 1.0 + nrm((DEPTH, d), 0.02),
        'ab_w_in': nrm((N_EVEN, d, AB_IN), d ** -0.5),
        'ab_w_out': nrm((N_EVEN, AB_OUT, d), AB_OUT ** -0.5),
        's5_a_re': -0.5 + nrm((N_EVEN, 2, S5_GROUPS, S5_STATE), 0.01),
        's5_a_im': math.pi * jnp.arange(S5_STATE, dtype=jnp.float32) + nrm((N_EVEN, 2, S5_GROUPS, S5_STATE), 0.01),
        's5_log_dt': jax.random.uniform(next(keys), (N_EVEN, 2, S5_GROUPS), jnp.float32, math.log(1e-3), math.log(1e-1)),
        's5_b_re': nrm((N_EVEN, 2, S5_GROUPS, S5_STATE, S5_GROUP), (2 * S5_GROUP) ** -0.5),
        's5_b_im': nrm((N_EVEN, 2, S5_GROUPS, S5_STATE, S5_GROUP), (2 * S5_GROUP) ** -0.5),
        's5_c_re': nrm((N_EVEN, 2, S5_GROUPS, S5_GROUP, S5_STATE), (2 * S5_STATE) ** -0.5),
        's5_c_im': nrm((N_EVEN, 2, S5_GROUPS, S5_GROUP, S5_STATE), (2 * S5_STATE) ** -0.5),
        's5_d': nrm((N_EVEN, S5_WIDTH), 1.0),
        's5_glu_w': nrm((N_EVEN, S5_WIDTH, S5_WIDTH), S5_WIDTH ** -0.5),
        's5_glu_b': nrm((N_EVEN, S5_WIDTH), 0.02),
        'ml_conv_w': nrm((N_EVEN, ML_CONV, ML_WIDTH), ML_CONV ** -0.5),
        'ml_conv_b': nrm((N_EVEN, ML_WIDTH), 0.02),
        'ml_wq': nrm((N_EVEN, ML_HEADS, ML_HEAD_DIM, ML_HEAD_DIM), ML_HEAD_DIM ** -0.5),
        'ml_wk': nrm((N_EVEN, ML_HEADS, ML_HEAD_DIM, ML_HEAD_DIM), ML_HEAD_DIM ** -0.5),
        'ml_wv': nrm((N_EVEN, ML_HEADS, ML_HEAD_DIM, ML_HEAD_DIM), ML_HEAD_DIM ** -0.5),
        'ml_gate_b': gate_base + nrm((N_EVEN, 4 * ML_HEADS), 0.1),
        'ml_norm_g': 1.0 + nrm((N_EVEN, ML_WIDTH), 0.02),
        'ret_w_in': nrm((N_ODD, d, RET_IN), d ** -0.5),
        'ret_w_out': nrm((N_ODD, RET_V, d), RET_V ** -0.5),
        'ret_log_gamma': gamma_base * (1.0 + nrm((N_ODD, 2, RET_HEADS), 0.05)),
        'ret_norm_g': 1.0 + nrm((N_ODD, RET_V), 0.02),
        'ffn_w1': nrm((DEPTH, d, FFN_HIDDEN), d ** -0.5),
        'ffn_w3': nrm((DEPTH, d, FFN_HIDDEN), d ** -0.5),
        'ffn_w2': nrm((DEPTH, FFN_HIDDEN, d), FFN_HIDDEN ** -0.5),
        'final_g': 1.0 + nrm((d,), 0.02),
    }


def reference(x, c, ctx, c_ctx, ada_w, ada_b, norm1_g, norm2_g, ab_w_in, ab_w_out,
              s5_a_re, s5_a_im, s5_log_dt, s5_b_re, s5_b_im, s5_c_re, s5_c_im, s5_d, s5_glu_w, s5_glu_b,
              ml_conv_w, ml_conv_b, ml_wq, ml_wk, ml_wv, ml_gate_b, ml_norm_g,
              ret_w_in, ret_w_out, ret_log_gamma, ret_norm_g, ffn_w1, ffn_w3, ffn_w2, final_g):
    rows = x.shape[1] // GRID_W
    rope = grid_rope(rows)
    for layer in range(DEPTH):
        need_ctx = layer < DEPTH - 1
        j = layer // 2
        ml = mod_parts(c, ada_w[layer], ada_b[layer])
        mc = mod_parts(c_ctx[None, :], ada_w[layer], ada_b[layer])
        h_lat = modulate(rmsnorm(x, norm1_g[layer]), ml[:, :, 0], ml[:, :, 1])
        h_ctx = modulate(rmsnorm(ctx, norm1_g[layer]), mc[:, :, 0], mc[:, :, 1])
        if layer % 2 == 0:
            s5_params = (s5_a_re[j], s5_a_im[j], s5_log_dt[j], s5_b_re[j], s5_b_im[j], s5_c_re[j], s5_c_im[j],
                         s5_d[j], s5_glu_w[j], s5_glu_b[j])
            ml_params = (ml_conv_w[j], ml_conv_b[j], ml_wq[j], ml_wk[j], ml_wv[j], ml_gate_b[j], ml_norm_g[j])
            y_lat, y_ctx = even_mixer(h_lat, h_ctx, need_ctx, ab_w_in[j], ab_w_out[j], s5_params, ml_params)
        else:
            y_lat, y_ctx = retention_mixer(h_lat, h_ctx, need_ctx, rope, ret_w_in[j], ret_w_out[j],
                                           ret_log_gamma[j], ret_norm_g[j])
        x = x + ml[:, :, 2] * y_lat
        x = x + ml[:, :, 5] * swiglu(modulate(rmsnorm(x, norm2_g[layer]), ml[:, :, 3], ml[:, :, 4]),
                                     ffn_w1[layer], ffn_w3[layer], ffn_w2[layer])
        if need_ctx:
            ctx = ctx + mc[:, :, 2] * y_ctx
            ctx = ctx + mc[:, :, 5] * swiglu(modulate(rmsnorm(ctx, norm2_g[layer]), mc[:, :, 3], mc[:, :, 4]),
                                             ffn_w1[layer], ffn_w3[layer], ffn_w2[layer])
    return rmsnorm(x, final_g)
```

```python
import functools
import math

import jax
import jax.numpy as jnp
from jax import lax
from jax.experimental import pallas as pl
from jax.experimental.pallas import tpu as pltpu

F32 = jnp.float32
BF16 = jnp.bfloat16

EPS = 1e-6
N_MOD = 6
GRID_W = 64
ROPE_BASE = 10000.0

S5_GROUP = 16
S5_STATE = 64
S5_STEP = 16
ML_HEADS = 4
ML_CHUNK = 128
RET_HEADS = 4
SUBLANES = 8
LANES = 128
NEG_BIG = -1e30
VMEM_LIMIT = 56 * 1024 * 1024


def _cparams(sem):
    return pltpu.CompilerParams(dimension_semantics=sem, vmem_limit_bytes=VMEM_LIMIT)


def _resident(shape, index_map):
    return pl.BlockSpec(shape, index_map, pipeline_mode=pl.Buffered(1))


def _token_tile(n):
    for t in (512, 256, 128):
        if n % t == 0:
            return t
    raise ValueError(f"sequence length {n} must be a multiple of 128")


def _silu(v):
    return v * jax.nn.sigmoid(v)


def _norm_mod(x, g, shift, scale):
    y = x * lax.rsqrt(jnp.mean(x * x, axis=-1, keepdims=True) + EPS)
    return (y * g) * (1.0 + scale) + shift


def _dot(a, b):
    return jnp.dot(a, b, preferred_element_type=F32)


def _dot_nt(a, b):
    return lax.dot_general(a, b, (((1,), (1,)), ((), ())), preferred_element_type=F32)


def _dot_tn(a, b):
    return lax.dot_general(a, b, (((0,), (0,)), ((), ())), preferred_element_type=F32)


def _mod_kernel(v_ref, w_ref, b_ref, o_ref):
    s = _silu(v_ref[...]).astype(BF16)
    o_ref[0] = _dot(s, w_ref[0].astype(BF16)) + b_ref[0]


def _modulation(vec, ada_w, ada_b):
    depth, d, n = ada_w.shape
    r = vec.shape[0]
    tn = 1024
    return pl.pallas_call(
        _mod_kernel,
        grid=(depth, n // tn),
        in_specs=[
            pl.BlockSpec((r, d), lambda l, j: (0, 0)),
            pl.BlockSpec((1, d, tn), lambda l, j: (l, 0, j)),
            pl.BlockSpec((1, 1, tn), lambda l, j: (l, 0, j)),
        ],
        out_specs=pl.BlockSpec((1, r, tn), lambda l, j: (l, 0, j)),
        out_shape=jax.ShapeDtypeStruct((depth, r, n), F32),
        compiler_params=_cparams(("arbitrary", "arbitrary")),
        name="adaln_modulation",
    )(vec, ada_w, ada_b.reshape(depth, 1, n))


def _inproj_even_kernel(x_ref, sh_ref, sc_ref, g_ref, w_ref, u_ref, p_ref, *, s5_width):
    h = _norm_mod(x_ref[0], g_ref[...], sh_ref[0], sc_ref[0]).astype(BF16)
    p = _dot(h, w_ref[...])
    u_ref[0] = p[:, :s5_width].astype(BF16)
    p_ref[0] = p[:, s5_width:]


def _inproj_even(x, mods, row_of_batch, g, w, s5_width):
    bn, ln, d = x.shape
    n = w.shape[1]
    tm = _token_tile(ln)
    return pl.pallas_call(
        functools.partial(_inproj_even_kernel, s5_width=s5_width),
        grid=(bn, ln // tm),
        in_specs=[
            pl.BlockSpec((1, tm, d), lambda b, i: (b, i, 0)),
            pl.BlockSpec((1, 1, d), lambda b, i: (row_of_batch(b) * N_MOD + 0, 0, 0)),
            pl.BlockSpec((1, 1, d), lambda b, i: (row_of_batch(b) * N_MOD + 1, 0, 0)),
            pl.BlockSpec((1, d), lambda b, i: (0, 0)),
            _resident((d, n), lambda b, i: (0, 0)),
        ],
        out_specs=[
            pl.BlockSpec((1, tm, s5_width), lambda b, i: (b, i, 0)),
            pl.BlockSpec((1, tm, n - s5_width), lambda b, i: (b, i, 0)),
        ],
        out_shape=[
            jax.ShapeDtypeStruct((bn, ln, s5_width), BF16),
            jax.ShapeDtypeStruct((bn, ln, n - s5_width), F32),
        ],
        compiler_params=_cparams(("arbitrary", "arbitrary")),
        name="inproj_even",
    )(x, mods, mods, g, w)


def _s5_matrices(a_re, a_im, log_dt, b_re, b_im, c_re, c_im, d_skip):
    hp = lax.Precision.HIGHEST
    k = S5_STEP
    a_re, a_im = a_re.astype(F32), a_im.astype(F32)
    dt = jnp.exp(log_dt.astype(F32))[..., None]
    lam_re, lam_im = a_re * dt, a_im * dt
    steps = jnp.arange(k + 1, dtype=F32)[:, None, None, None]
    mag = jnp.exp(steps * lam_re)
    pr, pi = mag * jnp.cos(steps * lam_im), mag * jnp.sin(steps * lam_im)
    den = a_re * a_re + a_im * a_im
    nr = pr[1] - 1.0
    coef_re = (nr * a_re + pi[1] * a_im) / den
    coef_im = (pi[1] * a_re - nr * a_im) / den
    b_re, b_im = b_re.astype(F32), b_im.astype(F32)
    bb_re = coef_re[..., None] * b_re - coef_im[..., None] * b_im
    bb_im = coef_re[..., None] * b_im + coef_im[..., None] * b_re
    wr = pr[..., None] * bb_re - pi[..., None] * bb_im
    wi = pr[..., None] * bb_im + pi[..., None] * bb_re
    c_re, c_im = c_re.astype(F32), c_im.astype(F32)
    resp = (jnp.einsum('dgop,ldgpc->ldgoc', c_re, wr, precision=hp)
            - jnp.einsum('dgop,ldgpc->ldgoc', c_im, wi, precision=hp))
    g_n, c_n = a_re.shape[1], b_re.shape[-1]
    s_idx = jnp.arange(k)[:, None]
    t_idx = jnp.arange(k)[None, :]
    lag = t_idx - s_idx
    fwd = jnp.where((lag >= 0)[:, :, None, None, None], resp[jnp.clip(lag, 0, k), 0], 0.0)
    bwd = jnp.where((lag <= 0)[:, :, None, None, None], resp[jnp.clip(-lag, 0, k), 1], 0.0)
    skip = (jnp.eye(k, dtype=F32)[:, :, None, None, None]
            * (jnp.eye(c_n, dtype=F32)[None, None, None] * d_skip.astype(F32).reshape(g_n, 1, c_n)[None, None]))
    toep = jnp.transpose(fwd + bwd + skip, (2, 0, 4, 1, 3)).reshape(g_n, k * c_n, k * c_n)
    ef_re = jnp.transpose(wr[::-1][1:, 0], (1, 0, 3, 2))
    ef_im = jnp.transpose(wi[::-1][1:, 0], (1, 0, 3, 2))
    eb_re = jnp.transpose(wr[:k, 1], (1, 0, 3, 2))
    eb_im = jnp.transpose(wi[:k, 1], (1, 0, 3, 2))
    endw = jnp.concatenate([ef_re, eb_re, ef_im, eb_im], -1).reshape(g_n, k * c_n, 4 * a_re.shape[-1])

    def out_rows(d, powers_re, powers_im):
        cr, ci = c_re[d][:, None], c_im[d][:, None]
        pre, pim = powers_re[:, :, None, :], powers_im[:, :, None, :]
        o_re = cr * pre - ci * pim
        o_im = -(cr * pim + ci * pre)
        tr = lambda m: jnp.transpose(m, (0, 3, 1, 2)).reshape(g_n, -1, k * c_n)
        return tr(o_re), tr(o_im)

    pf_re, pf_im = jnp.transpose(pr[1:, 0], (1, 0, 2)), jnp.transpose(pi[1:, 0], (1, 0, 2))
    pb_re, pb_im = jnp.transpose(pr[::-1][:k, 1], (1, 0, 2)), jnp.transpose(pi[::-1][:k, 1], (1, 0, 2))
    of_re, of_im = out_rows(0, pf_re, pf_im)
    ob_re, ob_im = out_rows(1, pb_re, pb_im)
    outw = jnp.concatenate([of_re, ob_re, of_im, ob_im], 1)
    a0 = jnp.concatenate([pr[k, 0], pr[k, 1]], -1)[:, None, :]
    a1 = jnp.concatenate([pi[k, 0], pi[k, 1]], -1)[:, None, :]
    return toep.astype(BF16), endw.astype(BF16), outw.astype(BF16), a0, a1


def _s5_kernel(uc_ref, ul_ref, toep_ref, endw_ref, outw_ref, a0_ref, a1_ref, yc_ref, yl_ref, ec_ref, el_ref,
               *, groups, rows, n_ctx, n_lat):
    p = S5_STATE
    for g in range(groups):
        ec_ref[g] = _dot(uc_ref[g], endw_ref[g])
        el_ref[g] = _dot(ul_ref[g], endw_ref[g])
    a0 = a0_ref[...]
    a1 = a1_ref[...]
    fwd_lane = lax.broadcasted_iota(jnp.int32, (groups, rows, 2 * p), 2) < p

    def scan(e_ref, n, carry):
        def body(i, c):
            s0, s1 = c
            rf = pl.ds(pl.multiple_of(i * rows, rows), rows)
            rb = pl.ds(pl.multiple_of((n - 1 - i) * rows, rows), rows)
            ef = e_ref[:, rf, :]
            eb = e_ref[:, rb, :]
            e_ref[:, rf, 0:p] = s0[:, :, 0:p]
            e_ref[:, rb, p:2 * p] = s0[:, :, p:2 * p]
            e_ref[:, rf, 2 * p:3 * p] = s1[:, :, 0:p]
            e_ref[:, rb, 3 * p:4 * p] = s1[:, :, p:2 * p]
            e0 = jnp.where(fwd_lane, ef[:, :, 0:2 * p], eb[:, :, 0:2 * p])
            e1 = jnp.where(fwd_lane, ef[:, :, 2 * p:4 * p], eb[:, :, 2 * p:4 * p])
            return a0 * s0 - a1 * s1 + e0, a0 * s1 + a1 * s0 + e1
        return lax.fori_loop(0, n, body, carry)

    zero = jnp.zeros((groups, rows, 2 * p), F32)
    carry = scan(ec_ref, n_ctx, (zero, zero))
    scan(el_ref, n_lat, carry)
    for g in range(groups):
        yc_ref[g] = _dot(uc_ref[g], toep_ref[g]) + _dot(ec_ref[g].astype(BF16), outw_ref[g])
        yl_ref[g] = _dot(ul_ref[g], toep_ref[g]) + _dot(el_ref[g].astype(BF16), outw_ref[g])


def _s5_stack(u, bpad):
    bn, ln, w = u.shape
    g_n = w // S5_GROUP
    t = u.reshape(bn, ln // S5_STEP, S5_STEP, g_n, S5_GROUP)
    t = jnp.transpose(t, (3, 1, 0, 2, 4))
    if bpad != bn:
        t = jnp.pad(t, ((0, 0), (0, 0), (0, bpad - bn), (0, 0), (0, 0)))
    return t.reshape(g_n, (ln // S5_STEP) * bpad, S5_STEP * S5_GROUP)


def _s5_unstack(y, bn, bpad):
    g_n = y.shape[0]
    t = y.reshape(g_n, -1, bpad, S5_STEP, S5_GROUP)[:, :, :bn]
    t = jnp.transpose(t, (2, 1, 3, 0, 4))
    return t.reshape(bn, -1, g_n * S5_GROUP)


def _s5_mix(u_ctx, u_lat, mats):
    toep, endw, outw, a0, a1 = mats
    bn = u_lat.shape[0]
    bpad = -(-bn // SUBLANES) * SUBLANES
    uc, ul = _s5_stack(u_ctx, bpad), _s5_stack(u_lat, bpad)
    g_n, rc, kc = uc.shape
    rl = ul.shape[1]
    gb = 8
    blk3 = lambda r, c: pl.BlockSpec((gb, r, c), lambda i: (i, 0, 0))
    yc, yl = pl.pallas_call(
        functools.partial(_s5_kernel, groups=gb, rows=bpad, n_ctx=rc // bpad, n_lat=rl // bpad),
        grid=(g_n // gb,),
        in_specs=[blk3(rc, kc), blk3(rl, kc), blk3(kc, kc), blk3(kc, kc), blk3(kc, kc),
                  blk3(1, 2 * S5_STATE), blk3(1, 2 * S5_STATE)],
        out_specs=[blk3(rc, kc), blk3(rl, kc)],
        out_shape=[jax.ShapeDtypeStruct((g_n, rc, kc), F32), jax.ShapeDtypeStruct((g_n, rl, kc), F32)],
        scratch_shapes=[pltpu.VMEM((gb, rc, kc), F32), pltpu.VMEM((gb, rl, kc), F32)],
        compiler_params=_cparams(("arbitrary",)),
        name="s5_scan",
    )(uc, ul, toep, endw, outw, a0, a1)
    return _s5_unstack(yc, bn, bpad), _s5_unstack(yl, bn, bpad)


def _mlstm_kernel(pc_ref, pl_ref, cw_ref, cb_ref, wq_ref, wk_ref, wv_ref, gb_ref, ng_ref, mc_ref, ml_ref,
                  q_s, k_s, v_s, h_s, bcol_s, rcol_s, rrow_s, *, n_ctx_rows, n_lat_rows):
    t_n = ML_CHUNK
    width = ML_HEADS * LANES
    nc_ctx, nc_lat = n_ctx_rows // t_n, n_lat_rows // t_n
    nc = nc_ctx + nc_lat
    scale_k = LANES ** -0.5

    def project(p_ref, n_rows, base):
        row = lax.broadcasted_iota(jnp.int32, (n_rows, LANES), 0)
        for h in range(ML_HEADS):
            cols = slice(h * LANES, (h + 1) * LANES)
            xm = p_ref[0, :, cols]
            taps = cw_ref[:, cols]
            n_tap = taps.shape[0]
            acc = jnp.zeros_like(xm) + cb_ref[:, cols]
            for j in range(n_tap):
                d = j - n_tap // 2
                if d == 0:
                    sh = xm
                else:
                    sh = pltpu.roll(xm, (-d) % n_rows, 0)
                    sh = jnp.where((row + d >= 0) & (row + d < n_rows), sh, 0.0)
                acc = acc + sh * taps[j:j + 1, :]
            xc = _silu(acc).astype(BF16)
            q_s[base:base + n_rows, cols] = _dot(xc, wq_ref[h]).astype(BF16)
            k_s[base:base + n_rows, cols] = (_dot(xc, wk_ref[h]) * scale_k).astype(BF16)
            v_s[base:base + n_rows, cols] = _dot(xm.astype(BF16), wv_ref[h]).astype(BF16)

    project(pc_ref, n_ctx_rows, 0)
    project(pl_ref, n_lat_rows, n_ctx_rows)

    ti = lax.broadcasted_iota(jnp.int32, (t_n, t_n), 0)
    si = lax.broadcasted_iota(jnp.int32, (t_n, t_n), 1)
    tri_f = (si <= ti).astype(BF16)
    tri_b = (si >= ti).astype(BF16)
    lane = lax.broadcasted_iota(jnp.int32, (t_n, LANES), 1)
    gate_off = 2 * width

    def gate_prep(p_ref, n_chunks, base_chunk):
        def body(c, _):
            rows = pl.ds(pl.multiple_of(c * t_n, t_n), t_n)
            gcol = p_ref[0, rows, gate_off:gate_off + LANES] + gb_ref[...]
            lf = jax.nn.log_sigmoid(gcol)
            hi = lf.astype(BF16)
            lo = (lf - hi.astype(F32)).astype(BF16)
            pre = _dot(tri_f, hi) + _dot(tri_f, lo)
            suf = _dot(tri_b, hi) + _dot(tri_b, lo)
            bsum = jnp.where(lane < 2 * ML_HEADS, pre, suf)
            rcol = gcol - pltpu.roll(bsum, LANES - ML_HEADS, 1)
            bcol_s[base_chunk + c] = bsum
            rcol_s[base_chunk + c] = rcol
            rrow_s[base_chunk + c] = rcol.T[0:4 * ML_HEADS, :]
            return 0
        lax.fori_loop(0, n_chunks, body, 0)

    gate_prep(pc_ref, nc_ctx, 0)
    gate_prep(pl_ref, nc_lat, nc_ctx)

    for h in range(ML_HEADS):
        cols = slice(h * LANES, (h + 1) * LANES)
        for d in range(2):
            li = 2 * d * ML_HEADS + h
            lf_lane = li + ML_HEADS
            mask = (si <= ti) if d == 0 else (si >= ti)
            last = t_n - 1 if d == 0 else 0

            def body(i, carry, cols=cols, d=d, li=li, lf_lane=lf_lane, mask=mask, last=last):
                ct, nv, m_in = carry
                if d == 0:
                    c = i
                else:
                    c = jnp.where(i < nc_ctx, nc_ctx - 1 - i, nc + nc_ctx - 1 - i)
                rows = pl.ds(pl.multiple_of(c * t_n, t_n), t_n)
                q = q_s[rows, cols]
                k = k_s[rows, cols]
                v = v_s[rows, cols]
                r_col = rcol_s[c][:, li:li + 1]
                b_col = bcol_s[c][:, lf_lane:lf_lane + 1]
                r_row = rrow_s[c][li:li + 1, :]
                r_mat = jnp.where(mask, r_row, NEG_BIG)
                run_max = jnp.max(r_mat, axis=1, keepdims=True)
                mm = jnp.maximum(m_in, run_max)
                dmat = jnp.exp(r_mat - mm)
                sm = dmat * _dot_nt(q, k)
                inter = jnp.exp(m_in - mm)
                num = _dot(sm.astype(BF16), v) + inter * _dot(q, ct.astype(BF16))
                den = (jnp.sum(sm, axis=1, keepdims=True)
                       + inter * jnp.sum(q.astype(F32) * nv, axis=1, keepdims=True))
                hout = num / jnp.maximum(jnp.abs(den), jnp.exp(-b_col - mm))
                if d == 0:
                    h_s[rows, cols] = hout
                else:
                    h_s[rows, cols] = h_s[rows, cols] + hout
                r_top = jnp.max(r_row, axis=1, keepdims=True)
                b_last = b_col[last:last + 1, :]
                w_col = jnp.exp(r_col - r_top)
                kf = k.astype(F32)
                ct_inc = _dot_tn(k, (w_col * v.astype(F32)).astype(BF16))
                n_inc = jnp.sum(w_col * kf, axis=0, keepdims=True)
                g_max = b_last + r_top
                m_new = jnp.maximum(b_last + m_in, g_max)
                dec = jnp.exp(b_last + m_in - m_new)
                inj = jnp.exp(g_max - m_new)
                return dec * ct + inj * ct_inc, dec * nv + inj * n_inc, m_new

            init = (jnp.zeros((LANES, LANES), F32), jnp.zeros((1, LANES), F32), jnp.zeros((1, 1), F32))
            lax.fori_loop(0, nc, body, init)

    def finish(p_ref, o_ref, n_chunks, base_chunk):
        def body(c, _):
            rows = pl.ds(pl.multiple_of(c * t_n, t_n), t_n)
            srows = pl.ds(pl.multiple_of((base_chunk + c) * t_n, t_n), t_n)
            for h in range(ML_HEADS):
                cols = slice(h * LANES, (h + 1) * LANES)
                hh = h_s[srows, cols]
                mu = jnp.mean(hh, axis=1, keepdims=True)
                cen = hh - mu
                var = jnp.mean(cen * cen, axis=1, keepdims=True)
                y = cen * lax.rsqrt(var + EPS) * ng_ref[:, cols]
                og = p_ref[0, rows, width + h * LANES:width + (h + 1) * LANES]
                o_ref[0, rows, cols] = (jax.nn.sigmoid(og) * y).astype(BF16)
            return 0
        lax.fori_loop(0, n_chunks, body, 0)

    finish(pc_ref, mc_ref, nc_ctx, 0)
    finish(pl_ref, ml_ref, nc_lat, nc_ctx)


def _mlstm_mix(p_ctx, p_lat, conv_w, conv_b, wq, wk, wv, gate_b, norm_g):
    bn, lc, pw = p_ctx.shape
    ll = p_lat.shape[1]
    width = ML_HEADS * LANES
    lt = lc + ll
    nct = lt // ML_CHUNK
    gb = jnp.zeros((1, LANES), F32).at[0, :4 * ML_HEADS].set(gate_b.astype(F32))
    full2 = lambda a: pl.BlockSpec(a.shape, lambda b: (0,) * a.ndim)
    conv_b2, norm_g2 = conv_b.reshape(1, width), norm_g.reshape(1, width)
    wq, wk, wv = wq.astype(BF16), wk.astype(BF16), wv.astype(BF16)
    return pl.pallas_call(
        functools.partial(_mlstm_kernel, n_ctx_rows=lc, n_lat_rows=ll),
        grid=(bn,),
        in_specs=[
            pl.BlockSpec((1, lc, pw), lambda b: (b, 0, 0)),
            pl.BlockSpec((1, ll, pw), lambda b: (b, 0, 0)),
            full2(conv_w), full2(conv_b2), full2(wq), full2(wk), full2(wv), full2(gb), full2(norm_g2),
        ],
        out_specs=[
            pl.BlockSpec((1, lc, width), lambda b: (b, 0, 0)),
            pl.BlockSpec((1, ll, width), lambda b: (b, 0, 0)),
        ],
        out_shape=[jax.ShapeDtypeStruct((bn, lc, width), BF16), jax.ShapeDtypeStruct((bn, ll, width), BF16)],
        scratch_shapes=[
            pltpu.VMEM((lt, width), BF16), pltpu.VMEM((lt, width), BF16), pltpu.VMEM((lt, width), BF16),
            pltpu.VMEM((lt, width), F32),
            pltpu.VMEM((nct, ML_CHUNK, LANES), F32), pltpu.VMEM((nct, ML_CHUNK, LANES), F32),
            pltpu.VMEM((nct, 4 * ML_HEADS, ML_CHUNK), F32),
        ],
        compiler_params=_cparams(("arbitrary",)),
        name="mlstm_mix",
    )(p_ctx, p_lat, conv_w, conv_b2, wq, wk, wv, gb, norm_g2)


def _ffn_tail(x, y, gate_mix, g2, shift, scale, gate_ffn, w1_ref, w3_ref, w2_ref, hidden_chunk):
    x1 = x + gate_mix * y
    h2 = _norm_mod(x1, g2, shift, scale).astype(BF16)
    hidden = w1_ref.shape[1]
    acc = jnp.zeros_like(x1)
    for j in range(hidden // hidden_chunk):
        cs = slice(j * hidden_chunk, (j + 1) * hidden_chunk)
        z = (_silu(_dot(h2, w1_ref[:, cs])) * _dot(h2, w3_ref[:, cs])).astype(BF16)
        acc = acc + _dot(z, w2_ref[cs, :])
    return x1 + gate_ffn * acc


def _final_norm(x, g):
    return x * lax.rsqrt(jnp.mean(x * x, axis=-1, keepdims=True) + EPS) * g


def _outproj_even_kernel(x_ref, ys_ref, m_ref, gm_ref, sh_ref, sc_ref, gf_ref, g2_ref, gw_ref, gbias_ref,
                         wos_ref, wom_ref, w1_ref, w3_ref, w2_ref, o_ref, *, hidden_chunk):
    ys = jax.nn.gelu(ys_ref[0])
    s = ys * jax.nn.sigmoid(_dot(ys.astype(BF16), gw_ref[...]) + gbias_ref[...])
    y = _dot(s.astype(BF16), wos_ref[...]) + _dot(m_ref[0], wom_ref[...])
    o_ref[0] = _ffn_tail(x_ref[0], y, gm_ref[0], g2_ref[...], sh_ref[0], sc_ref[0], gf_ref[0],
                         w1_ref, w3_ref, w2_ref, hidden_chunk)


def _outproj_odd_kernel(x_ref, a_ref, gm_ref, sh_ref, sc_ref, gf_ref, g2_ref, wo_ref, w1_ref, w3_ref, w2_ref,
                        fg_ref, o_ref, *, hidden_chunk):
    y = _dot(a_ref[0], wo_ref[...])
    out = _ffn_tail(x_ref[0], y, gm_ref[0], g2_ref[...], sh_ref[0], sc_ref[0], gf_ref[0],
                    w1_ref, w3_ref, w2_ref, hidden_chunk)
    o_ref[0] = _final_norm(out, fg_ref[...])


def _hidden_chunk(hidden):
    for c in (1408, 1024, 512, 256, 128):
        if hidden % c == 0:
            return c
    return hidden


def _mod_specs(d, row_of_batch, slots):
    return [pl.BlockSpec((1, 1, d), lambda b, i, s=s: (row_of_batch(b) * N_MOD + s, 0, 0)) for s in slots]


def _outproj_even(x, ys, m, mods, row_of_batch, g2, glu_w, glu_b, w_out, w1, w3, w2):
    bn, ln, d = x.shape
    sw = ys.shape[-1]
    tm = _token_tile(ln)
    res = lambda a: _resident(a.shape, lambda b, i: (0,) * a.ndim)
    tok = lambda w: pl.BlockSpec((1, tm, w), lambda b, i: (b, i, 0))
    wos, wom = w_out[:sw], w_out[sw:]
    glu_b2 = glu_b.reshape(1, sw)
    return pl.pallas_call(
        functools.partial(_outproj_even_kernel, hidden_chunk=_hidden_chunk(w1.shape[1])),
        grid=(bn, ln // tm),
        in_specs=[tok(d), tok(sw), tok(m.shape[-1])] + _mod_specs(d, row_of_batch, (2, 3, 4, 5))
        + [pl.BlockSpec((1, d), lambda b, i: (0, 0)), res(glu_w), res(glu_b2), res(wos), res(wom),
           res(w1), res(w3), res(w2)],
        out_specs=tok(d),
        out_shape=jax.ShapeDtypeStruct((bn, ln, d), F32),
        compiler_params=_cparams(("arbitrary", "arbitrary")),
        name="outproj_ffn_even",
    )(x, ys, m, mods, mods, mods, mods, g2, glu_w, glu_b2, wos, wom, w1, w3, w2)


def _outproj_odd(x, a, mods, row_of_batch, g2, w_out, w1, w3, w2, final_g):
    bn, ln, d = x.shape
    tm = _token_tile(ln)
    res = lambda a_: _resident(a_.shape, lambda b, i: (0,) * a_.ndim)
    tok = lambda w: pl.BlockSpec((1, tm, w), lambda b, i: (b, i, 0))
    return pl.pallas_call(
        functools.partial(_outproj_odd_kernel, hidden_chunk=_hidden_chunk(w1.shape[1])),
        grid=(bn, ln // tm),
        in_specs=[tok(d), tok(a.shape[-1])] + _mod_specs(d, row_of_batch, (2, 3, 4, 5))
        + [pl.BlockSpec((1, d), lambda b, i: (0, 0)), res(w_out), res(w1), res(w3), res(w2),
           pl.BlockSpec((1, d), lambda b, i: (0, 0))],
        out_specs=tok(d),
        out_shape=jax.ShapeDtypeStruct((bn, ln, d), F32),
        compiler_params=_cparams(("arbitrary", "arbitrary")),
        name="outproj_ffn_odd",
    )(x, a, mods, mods, mods, mods, g2, w_out, w1, w3, w2, final_g)


def _rope_heads(t, cos, sin, heads, hk):
    half = hk // 2
    out = []
    for h in range(heads):
        t1 = t[:, h * hk:h * hk + half]
        t2 = t[:, h * hk + half:(h + 1) * hk]
        out.append(t1 * cos - t2 * sin)
        out.append(t2 * cos + t1 * sin)
    return jnp.concatenate(out, axis=-1)


def _inproj_odd_kernel(x_ref, sh_ref, sc_ref, g_ref, cos_ref, sin_ref, w_ref, q_ref, k_ref, v_ref, gate_ref,
                       *, qk, vw, heads):
    h = _norm_mod(x_ref[0], g_ref[...], sh_ref[0], sc_ref[0]).astype(BF16)
    hk = qk // heads
    cos, sin = cos_ref[...], sin_ref[...]
    q = _rope_heads(_dot(h, w_ref[:, 0:qk]), cos, sin, heads, hk)
    q_ref[0] = (q * (hk ** -0.5)).astype(BF16)
    k = _rope_heads(_dot(h, w_ref[:, qk:2 * qk]), cos, sin, heads, hk)
    k_ref[0] = k.astype(BF16)
    v_ref[0] = _dot(h, w_ref[:, 2 * qk:2 * qk + vw]).astype(BF16)
    gate_ref[0] = _dot(h, w_ref[:, 2 * qk + vw:2 * qk + 2 * vw]).astype(BF16)


def _inproj_odd_ctx_kernel(x_ref, sh_ref, sc_ref, g_ref, w_ref, k_ref, v_ref, *, qk, vw):
    h = _norm_mod(x_ref[0], g_ref[...], sh_ref[0], sc_ref[0]).astype(BF16)
    k_ref[0] = _dot(h, w_ref[:, 0:qk]).astype(BF16)
    v_ref[0] = _dot(h, w_ref[:, qk:qk + vw]).astype(BF16)


def _inproj_odd(x, mods, row_of_batch, g, cos, sin, w, qk, vw):
    bn, ln, d = x.shape
    tm = _token_tile(ln)
    tok = lambda w_: pl.BlockSpec((1, tm, w_), lambda b, i: (b, i, 0))
    half = cos.shape[1]
    return pl.pallas_call(
        functools.partial(_inproj_odd_kernel, qk=qk, vw=vw, heads=RET_HEADS),
        grid=(bn, ln // tm),
        in_specs=[tok(d)] + _mod_specs(d, row_of_batch, (0, 1))
        + [pl.BlockSpec((1, d), lambda b, i: (0, 0)),
           pl.BlockSpec((tm, half), lambda b, i: (i, 0)), pl.BlockSpec((tm, half), lambda b, i: (i, 0)),
           _resident(w.shape, lambda b, i: (0, 0))],
        out_specs=[tok(qk), tok(qk), tok(vw), tok(vw)],
        out_shape=[jax.ShapeDtypeStruct((bn, ln, qk), BF16), jax.ShapeDtypeStruct((bn, ln, qk), BF16),
                   jax.ShapeDtypeStruct((bn, ln, vw), BF16), jax.ShapeDtypeStruct((bn, ln, vw), BF16)],
        compiler_params=_cparams(("arbitrary", "arbitrary")),
        name="inproj_odd",
    )(x, mods, mods, g, cos, sin, w)


def _inproj_odd_ctx(x, mods, row_of_batch, g, w_kv, qk, vw):
    bn, ln, d = x.shape
    tm = _token_tile(ln)
    tok = lambda w_: pl.BlockSpec((1, tm, w_), lambda b, i: (b, i, 0))
    return pl.pallas_call(
        functools.partial(_inproj_odd_ctx_kernel, qk=qk, vw=vw),
        grid=(bn, ln // tm),
        in_specs=[tok(d)] + _mod_specs(d, row_of_batch, (0, 1))
        + [pl.BlockSpec((1, d), lambda b, i: (0, 0)), _resident(w_kv.shape, lambda b, i: (0, 0))],
        out_specs=[tok(qk), tok(vw)],
        out_shape=[jax.ShapeDtypeStruct((bn, ln, qk), BF16), jax.ShapeDtypeStruct((bn, ln, vw), BF16)],
        compiler_params=_cparams(("arbitrary", "arbitrary")),
        name="inproj_odd_ctx",
    )(x, mods, mods, g, w_kv)


def _retention_kernel(lg_ref, q_ref, k_ref, v_ref, gate_ref, kc_ref, vc_ref, ng_ref, o_ref, acc_s, sf_s, sb_s,
                      *, chunk, n_ctx_rows, n_lat_rows):
    t_n = chunk
    h = pl.program_id(1)
    lgf = jnp.full((1, 1), lg_ref[0, h], F32)
    lgb = jnp.full((1, 1), lg_ref[1, h], F32)
    nc_ctx, nc_lat = n_ctx_rows // t_n, n_lat_rows // t_n
    ti = lax.broadcasted_iota(jnp.int32, (t_n, t_n), 0)
    si = lax.broadcasted_iota(jnp.int32, (t_n, t_n), 1)
    diff = (ti - si).astype(F32)
    decay = jnp.where(diff >= 0, jnp.exp(lgf * jnp.maximum(diff, 0.0)), jnp.exp(lgb * jnp.maximum(-diff, 0.0)))
    pos = lax.broadcasted_iota(jnp.int32, (t_n, 1), 0).astype(F32)
    inter_f = jnp.exp(lgf * (pos + 1.0))
    inter_b = jnp.exp(lgb * (t_n - pos))
    wend_f = jnp.exp(lgf * (t_n - 1.0 - pos))
    wend_b = jnp.exp(lgb * pos)
    cd_f = jnp.exp(lgf * t_n)
    cd_b = jnp.exp(lgb * t_n)

    def rows_of(c):
        return pl.ds(pl.multiple_of(c * t_n, t_n), t_n)

    def bump(s_ref, k, v, wend, cd):
        kw = (k.astype(F32) * wend).astype(BF16)
        s_ref[...] = cd * s_ref[...] + _dot_tn(kw, v)

    sf_s[...] = jnp.zeros_like(sf_s)
    sb_s[...] = jnp.zeros_like(sb_s)

    def ctx_f(c, _):
        bump(sf_s, kc_ref[0, rows_of(c), :], vc_ref[0, rows_of(c), :], wend_f, cd_f)
        return 0
    lax.fori_loop(0, nc_ctx, ctx_f, 0)

    def lat_f(c, _):
        rows = rows_of(c)
        q, k, v = q_ref[0, rows, :], k_ref[0, rows, :], v_ref[0, rows, :]
        scores = (_dot_nt(q, k) * decay).astype(BF16)
        acc_s[rows, :] = _dot(scores, v) + inter_f * _dot(q, sf_s[...].astype(BF16))
        bump(sf_s, k, v, wend_f, cd_f)
        return 0
    lax.fori_loop(0, nc_lat, lat_f, 0)

    def ctx_b(i, _):
        c = nc_ctx - 1 - i
        bump(sb_s, kc_ref[0, rows_of(c), :], vc_ref[0, rows_of(c), :], wend_b, cd_b)
        return 0
    lax.fori_loop(0, nc_ctx, ctx_b, 0)

    def lat_b(i, _):
        rows = rows_of(nc_lat - 1 - i)
        q, k, v = q_ref[0, rows, :], k_ref[0, rows, :], v_ref[0, rows, :]
        o = acc_s[rows, :] + inter_b * _dot(q, sb_s[...].astype(BF16))
        mu = jnp.mean(o, axis=1, keepdims=True)
        cen = o - mu
        var = jnp.mean(cen * cen, axis=1, keepdims=True)
        y = cen * lax.rsqrt(var + EPS) * ng_ref[...]
        o_ref[0, rows, :] = (_silu(gate_ref[0, rows, :].astype(F32)) * y).astype(BF16)
        bump(sb_s, k, v, wend_b, cd_b)
        return 0
    lax.fori_loop(0, nc_lat, lat_b, 0)


def _retention_mix(q, k, v, gate, k_ctx, v_ctx, log_gamma, norm_g):
    bn, ll, qk = q.shape
    lc = k_ctx.shape[1]
    vw = v.shape[-1]
    hk, hv = qk // RET_HEADS, vw // RET_HEADS
    chunk = 256 if (ll % 256 == 0 and lc % 256 == 0) else 128
    head = lambda n_rows, w: pl.BlockSpec((1, n_rows, w), lambda b, h: (b, 0, h))
    return pl.pallas_call(
        functools.partial(_retention_kernel, chunk=chunk, n_ctx_rows=lc, n_lat_rows=ll),
        grid=(bn, RET_HEADS),
        in_specs=[
            pl.BlockSpec(memory_space=pltpu.SMEM),
            head(ll, hk), head(ll, hk), head(ll, hv), head(ll, hv), head(lc, hk), head(lc, hv),
            pl.BlockSpec((1, hv), lambda b, h: (0, h)),
        ],
        out_specs=head(ll, hv),
        out_shape=jax.ShapeDtypeStruct((bn, ll, vw), BF16),
        scratch_shapes=[pltpu.VMEM((ll, hv), F32), pltpu.VMEM((hk, hv), F32), pltpu.VMEM((hk, hv), F32)],
        compiler_params=_cparams(("arbitrary", "arbitrary")),
        name="retention_mix",
    )(log_gamma.astype(F32), q, k, v, gate, k_ctx, v_ctx, norm_g.reshape(1, vw))


def _grid_rope(n_pos, hk):
    rows = n_pos // GRID_W
    row = jnp.repeat(jnp.arange(rows, dtype=F32), GRID_W)
    col = jnp.tile(jnp.arange(GRID_W, dtype=F32), rows)
    n_freq = hk // 4
    inv = ROPE_BASE ** (-jnp.arange(n_freq, dtype=F32) / n_freq)
    ang = jnp.concatenate([row[:, None] * inv, col[:, None] * inv], -1)
    return jnp.cos(ang), jnp.sin(ang)


def kernel(x, c, ctx, c_ctx, ada_w, ada_b, norm1_g, norm2_g, ab_w_in, ab_w_out, s5_a_re, s5_a_im, s5_log_dt,
           s5_b_re, s5_b_im, s5_c_re, s5_c_im, s5_d, s5_glu_w, s5_glu_b, ml_conv_w, ml_conv_b, ml_wq, ml_wk,
           ml_wv, ml_gate_b, ml_norm_g, ret_w_in, ret_w_out, ret_log_gamma, ret_norm_g, ffn_w1, ffn_w3, ffn_w2,
           final_g):
    bn, ln, d = x.shape
    depth = ada_w.shape[0]
    assert depth == 2, "one S5 || mLSTM layer followed by one retention layer"
    s5_width = s5_d.shape[-1]
    ml_width = ml_norm_g.shape[-1]
    assert ml_width == ML_HEADS * LANES and s5_width % S5_GROUP == 0

    r_pad = -(-(bn + 1) // SUBLANES) * SUBLANES
    vec = jnp.zeros((r_pad, d), F32).at[:bn].set(c).at[bn].set(c_ctx)
    mods_all = _modulation(vec, ada_w, ada_b).reshape(depth, r_pad * N_MOD, 1, d)
    lat_row = lambda b: b
    ctx_row = lambda b: bn

    mods = mods_all[0]
    g1 = norm1_g[0].reshape(1, d)
    g2 = norm2_g[0].reshape(1, d)
    w_in = ab_w_in[0]
    n_gate = w_in.shape[1] - s5_width - 2 * ml_width
    w_in = jnp.pad(w_in, ((0, 0), (0, LANES - n_gate))).astype(BF16)
    u_lat, p_lat = _inproj_even(x, mods, lat_row, g1, w_in, s5_width)
    u_ctx, p_ctx = _inproj_even(ctx, mods, ctx_row, g1, w_in, s5_width)
    mats = _s5_matrices(s5_a_re[0], s5_a_im[0], s5_log_dt[0], s5_b_re[0], s5_b_im[0], s5_c_re[0], s5_c_im[0],
                        s5_d[0])
    ys_ctx, ys_lat = _s5_mix(u_ctx, u_lat, mats)
    m_ctx, m_lat = _mlstm_mix(p_ctx, p_lat, ml_conv_w[0], ml_conv_b[0], ml_wq[0], ml_wk[0], ml_wv[0],
                              ml_gate_b[0], ml_norm_g[0])
    glu_w = s5_glu_w[0].astype(BF16)
    w_out = ab_w_out[0].astype(BF16)
    w1, w3, w2 = ffn_w1[0].astype(BF16), ffn_w3[0].astype(BF16), ffn_w2[0].astype(BF16)
    x = _outproj_even(x, ys_lat, m_lat, mods, lat_row, g2, glu_w, s5_glu_b[0], w_out, w1, w3, w2)
    ctx = _outproj_even(ctx, ys_ctx, m_ctx, mods, ctx_row, g2, glu_w, s5_glu_b[0], w_out, w1, w3, w2)

    mods = mods_all[1]
    g1 = norm1_g[1].reshape(1, d)
    g2 = norm2_g[1].reshape(1, d)
    vw = ret_norm_g.shape[-1]
    qk = (ret_w_in.shape[-1] - 2 * vw) // 2
    w_in = ret_w_in[0].astype(BF16)
    cos, sin = _grid_rope(ln, qk // RET_HEADS)
    q, k, v, gate = _inproj_odd(x, mods, lat_row, g1, cos, sin, w_in, qk, vw)
    k_ctx, v_ctx = _inproj_odd_ctx(ctx, mods, ctx_row, g1, w_in[:, qk:2 * qk + vw], qk, vw)
    a = _retention_mix(q, k, v, gate, k_ctx, v_ctx, ret_log_gamma[0], ret_norm_g[0])
    w_out = ret_w_out[0].astype(BF16)
    w1, w3, w2 = ffn_w1[1].astype(BF16), ffn_w3[1].astype(BF16), ffn_w2[1].astype(BF16)
    return _outproj_odd(x, a, mods, lat_row, g2, w_out, w1, w3, w2, final_g.reshape(1, d))
```

```python
import functools
import math

import jax
import jax.numpy as jnp
from jax import lax
from jax.experimental import pallas as pl
from jax.experimental.pallas import tpu as pltpu

F32 = jnp.float32
BF16 = jnp.bfloat16

EPS = 1e-6
N_MOD = 6
GRID_W = 64
ROPE_BASE = 10000.0

S5_GROUP = 16
S5_STATE = 64
S5_STEP = 16
ML_HEADS = 4
ML_CHUNK = 128
RET_HEADS = 4
SUBLANES = 8
LANES = 128
NEG_BIG = -1e30
VMEM_LIMIT = 56 * 1024 * 1024


def _cparams(sem):
    return pltpu.CompilerParams(dimension_semantics=sem, vmem_limit_bytes=VMEM_LIMIT)


def _resident(shape, index_map):
    return pl.BlockSpec(shape, index_map, pipeline_mode=pl.Buffered(1))


def _token_tile(n):
    for t in (512, 256, 128):
        if n % t == 0:
            return t
    raise ValueError(f"sequence length {n} must be a multiple of 128")


def _silu(v):
    return v * jax.nn.sigmoid(v)


def _norm_mod(x, g, shift, scale):
    y = x * lax.rsqrt(jnp.mean(x * x, axis=-1, keepdims=True) + EPS)
    return (y * g) * (1.0 + scale) + shift


def _dot(a, b):
    return jnp.dot(a, b, preferred_element_type=F32)


def _dot_nt(a, b):
    return lax.dot_general(a, b, (((1,), (1,)), ((), ())), preferred_element_type=F32)


def _dot_tn(a, b):
    return lax.dot_general(a, b, (((0,), (0,)), ((), ())), preferred_element_type=F32)


def _mod_kernel(v_ref, w_ref, b_ref, o_ref):
    s = _silu(v_ref[...]).astype(BF16)
    o_ref[0] = _dot(s, w_ref[0].astype(BF16)) + b_ref[0]


def _modulation(vec, ada_w, ada_b):
    depth, d, n = ada_w.shape
    r = vec.shape[0]
    tn = 1024
    return pl.pallas_call(
        _mod_kernel,
        grid=(depth, n // tn),
        in_specs=[
            pl.BlockSpec((r, d), lambda l, j: (0, 0)),
            pl.BlockSpec((1, d, tn), lambda l, j: (l, 0, j)),
            pl.BlockSpec((1, 1, tn), lambda l, j: (l, 0, j)),
        ],
        out_specs=pl.BlockSpec((1, r, tn), lambda l, j: (l, 0, j)),
        out_shape=jax.ShapeDtypeStruct((depth, r, n), F32),
        compiler_params=_cparams(("arbitrary", "arbitrary")),
        name="adaln_modulation",
    )(vec, ada_w, ada_b.reshape(depth, 1, n))


def _inproj_even_kernel(x_ref, sh_ref, sc_ref, g_ref, w_ref, u_ref, p_ref, *, s5_width):
    h = _norm_mod(x_ref[0], g_ref[...], sh_ref[0], sc_ref[0]).astype(BF16)
    p = _dot(h, w_ref[...])
    u_ref[0] = p[:, :s5_width].astype(BF16)
    p_ref[0] = p[:, s5_width:]


def _inproj_even(x, mods, row_of_batch, g, w, s5_width):
    bn, ln, d = x.shape
    n = w.shape[1]
    tm = _token_tile(ln)
    return pl.pallas_call(
        functools.partial(_inproj_even_kernel, s5_width=s5_width),
        grid=(bn, ln // tm),
        in_specs=[
            pl.BlockSpec((1, tm, d), lambda b, i: (b, i, 0)),
            pl.BlockSpec((1, 1, d), lambda b, i: (row_of_batch(b) * N_MOD + 0, 0, 0)),
            pl.BlockSpec((1, 1, d), lambda b, i: (row_of_batch(b) * N_MOD + 1, 0, 0)),
            pl.BlockSpec((1, d), lambda b, i: (0, 0)),
            _resident((d, n), lambda b, i: (0, 0)),
        ],
        out_specs=[
            pl.BlockSpec((1, tm, s5_width), lambda b, i: (b, i, 0)),
            pl.BlockSpec((1, tm, n - s5_width), lambda b, i: (b, i, 0)),
        ],
        out_shape=[
            jax.ShapeDtypeStruct((bn, ln, s5_width), BF16),
            jax.ShapeDtypeStruct((bn, ln, n - s5_width), F32),
        ],
        compiler_params=_cparams(("arbitrary", "arbitrary")),
        name="inproj_even",
    )(x, mods, mods, g, w)


def _s5_matrices(a_re, a_im, log_dt, b_re, b_im, c_re, c_im, d_skip):
    hp = lax.Precision.HIGHEST
    k = S5_STEP
    a_re, a_im = a_re.astype(F32), a_im.astype(F32)
    dt = jnp.exp(log_dt.astype(F32))[..., None]
    lam_re, lam_im = a_re * dt, a_im * dt
    steps = jnp.arange(k + 1, dtype=F32)[:, None, None, None]
    mag = jnp.exp(steps * lam_re)
    pr, pi = mag * jnp.cos(steps * lam_im), mag * jnp.sin(steps * lam_im)
    den = a_re * a_re + a_im * a_im
    nr = pr[1] - 1.0
    coef_re = (nr * a_re + pi[1] * a_im) / den
    coef_im = (pi[1] * a_re - nr * a_im) / den
    bt_re = jnp.swapaxes(b_re.astype(F32), -1, -2)
    bt_im = jnp.swapaxes(b_im.astype(F32), -1, -2)
    bb_re = coef_re[:, :, None] * bt_re - coef_im[:, :, None] * bt_im
    bb_im = coef_re[:, :, None] * bt_im + coef_im[:, :, None] * bt_re
    wr = pr[:, :, :, None] * bb_re - pi[:, :, :, None] * bb_im
    wi = pr[:, :, :, None] * bb_im + pi[:, :, :, None] * bb_re
    c_re, c_im = c_re.astype(F32), c_im.astype(F32)
    resp = (jnp.einsum('ldgcp,dgop->dgclo', wr, c_re, precision=hp)
            - jnp.einsum('ldgcp,dgop->dgclo', wi, c_im, precision=hp))
    g_n, c_n, p_n = a_re.shape[1], b_re.shape[-1], a_re.shape[-1]
    skip = (jnp.eye(c_n, dtype=F32) * d_skip.astype(F32).reshape(g_n, c_n, 1))[:, :, None, :]
    gen = jnp.concatenate([resp[1, :, :, k - 1:0:-1], resp[0, :, :, 0:1] + resp[1, :, :, 0:1] + skip,
                           resp[0, :, :, 1:k]], axis=2).reshape(g_n, c_n, (2 * k - 1) * c_n)
    toep = jnp.stack([gen[:, :, (k - 1 - s) * c_n:(2 * k - 1 - s) * c_n] for s in range(k)], axis=1)
    toep = toep.reshape(g_n, k * c_n, k * c_n)
    endw = jnp.concatenate([wr[k - 1::-1, 0], wr[:k, 1], wi[k - 1::-1, 0], wi[:k, 1]], -1)
    endw = jnp.transpose(endw, (1, 0, 2, 3)).reshape(g_n, k * c_n, 4 * p_n)

    def out_cols(d, powers_re, powers_im):
        cr, ci = c_re[d][None], c_im[d][None]
        pre, pim = powers_re[:, :, None, :], powers_im[:, :, None, :]
        return cr * pre - ci * pim, -(cr * pim + ci * pre)

    of_re, of_im = out_cols(0, pr[1:, 0], pi[1:, 0])
    ob_re, ob_im = out_cols(1, pr[k:0:-1, 1], pi[k:0:-1, 1])
    outw_t = jnp.concatenate([of_re, ob_re, of_im, ob_im], -1)
    outw_t = jnp.transpose(outw_t, (1, 0, 2, 3)).reshape(g_n, k * c_n, 4 * p_n)
    a0 = jnp.concatenate([pr[k, 0], pr[k, 1]], -1)[:, None, :]
    a1 = jnp.concatenate([pi[k, 0], pi[k, 1]], -1)[:, None, :]
    return toep.astype(BF16), endw.astype(BF16), outw_t.astype(BF16), a0, a1


def _s5_kernel(uc_ref, ul_ref, toep_ref, endw_ref, outw_ref, a0_ref, a1_ref, yc_ref, yl_ref, ec_ref, el_ref,
               *, groups, rows, n_ctx, n_lat):
    p = S5_STATE
    for g in range(groups):
        ec_ref[g] = _dot(uc_ref[g], endw_ref[g])
        el_ref[g] = _dot(ul_ref[g], endw_ref[g])
    a0 = a0_ref[...]
    a1 = a1_ref[...]
    fwd_lane = lax.broadcasted_iota(jnp.int32, (groups, rows, 2 * p), 2) < p

    def scan(e_ref, n, carry):
        def body(i, c):
            s0, s1 = c
            rf = pl.ds(pl.multiple_of(i * rows, rows), rows)
            rb = pl.ds(pl.multiple_of((n - 1 - i) * rows, rows), rows)
            ef = e_ref[:, rf, :]
            eb = e_ref[:, rb, :]
            e_ref[:, rf, 0:p] = s0[:, :, 0:p]
            e_ref[:, rb, p:2 * p] = s0[:, :, p:2 * p]
            e_ref[:, rf, 2 * p:3 * p] = s1[:, :, 0:p]
            e_ref[:, rb, 3 * p:4 * p] = s1[:, :, p:2 * p]
            e0 = jnp.where(fwd_lane, ef[:, :, 0:2 * p], eb[:, :, 0:2 * p])
            e1 = jnp.where(fwd_lane, ef[:, :, 2 * p:4 * p], eb[:, :, 2 * p:4 * p])
            return a0 * s0 - a1 * s1 + e0, a0 * s1 + a1 * s0 + e1
        return lax.fori_loop(0, n, body, carry)

    zero = jnp.zeros((groups, rows, 2 * p), F32)
    carry = scan(ec_ref, n_ctx, (zero, zero))
    scan(el_ref, n_lat, carry)
    for g in range(groups):
        yc_ref[g] = _dot(uc_ref[g], toep_ref[g]) + _dot_nt(ec_ref[g].astype(BF16), outw_ref[g])
        yl_ref[g] = _dot(ul_ref[g], toep_ref[g]) + _dot_nt(el_ref[g].astype(BF16), outw_ref[g])


def _s5_stack(u, bpad):
    bn, ln, w = u.shape
    g_n = w // S5_GROUP
    t = u.reshape(bn, ln // S5_STEP, S5_STEP, g_n, S5_GROUP)
    t = jnp.transpose(t, (3, 1, 0, 2, 4))
    if bpad != bn:
        t = jnp.pad(t, ((0, 0), (0, 0), (0, bpad - bn), (0, 0), (0, 0)))
    return t.reshape(g_n, (ln // S5_STEP) * bpad, S5_STEP * S5_GROUP)


def _s5_unstack(y, bn, bpad):
    g_n = y.shape[0]
    t = y.reshape(g_n, -1, bpad, S5_STEP, S5_GROUP)[:, :, :bn]
    t = jnp.transpose(t, (2, 1, 3, 0, 4))
    return t.reshape(bn, -1, g_n * S5_GROUP)


def _s5_mix(u_ctx, u_lat, mats):
    toep, endw, outw, a0, a1 = mats
    bn = u_lat.shape[0]
    bpad = -(-bn // SUBLANES) * SUBLANES
    uc, ul = _s5_stack(u_ctx, bpad), _s5_stack(u_lat, bpad)
    g_n, rc, kc = uc.shape
    rl = ul.shape[1]
    gb = 8
    blk3 = lambda r, c: pl.BlockSpec((gb, r, c), lambda i: (i, 0, 0))
    yc, yl = pl.pallas_call(
        functools.partial(_s5_kernel, groups=gb, rows=bpad, n_ctx=rc // bpad, n_lat=rl // bpad),
        grid=(g_n // gb,),
        in_specs=[blk3(rc, kc), blk3(rl, kc), blk3(kc, kc), blk3(kc, kc), blk3(kc, kc),
                  blk3(1, 2 * S5_STATE), blk3(1, 2 * S5_STATE)],
        out_specs=[blk3(rc, kc), blk3(rl, kc)],
        out_shape=[jax.ShapeDtypeStruct((g_n, rc, kc), F32), jax.ShapeDtypeStruct((g_n, rl, kc), F32)],
        scratch_shapes=[pltpu.VMEM((gb, rc, kc), F32), pltpu.VMEM((gb, rl, kc), F32)],
        compiler_params=_cparams(("arbitrary",)),
        name="s5_scan",
    )(uc, ul, toep, endw, outw, a0, a1)
    return _s5_unstack(yc, bn, bpad), _s5_unstack(yl, bn, bpad)


def _mlstm_kernel(pc_ref, pl_ref, cw_ref, cb_ref, wq_ref, wkt_ref, wv_ref, gb_ref, ng_ref, mc_ref, ml_ref,
                  q_s, kt_s, vx_s, h_s, st_s, bcol_s, mcol_s, rrow_s, bd_s, md_s, *, n_ctx_rows, n_lat_rows):
    t_n = ML_CHUNK
    width = ML_HEADS * LANES
    nc_ctx, nc_lat = n_ctx_rows // t_n, n_lat_rows // t_n
    nc = nc_ctx + nc_lat
    scale_k = LANES ** -0.5

    def project(p_ref, n_rows, base):
        row = lax.broadcasted_iota(jnp.int32, (n_rows, LANES), 0)
        ones = jnp.ones((n_rows, LANES), BF16)
        for h in range(ML_HEADS):
            cols = slice(h * LANES, (h + 1) * LANES)
            xm = p_ref[0, :, cols]
            taps = cw_ref[:, cols]
            n_tap = taps.shape[0]
            acc = jnp.zeros_like(xm) + cb_ref[:, cols]
            for j in range(n_tap):
                d = j - n_tap // 2
                if d == 0:
                    sh = xm
                else:
                    sh = pltpu.roll(xm, (-d) % n_rows, 0)
                    sh = jnp.where((row + d >= 0) & (row + d < n_rows), sh, 0.0)
                acc = acc + sh * taps[j:j + 1, :]
            xc = _silu(acc).astype(BF16)
            q_s[base:base + n_rows, cols] = _dot(xc, wq_ref[h]).astype(BF16)
            kt = (_dot_nt(wkt_ref[h], xc) * scale_k).astype(BF16)
            for c in range(n_rows // t_n):
                kt_s[base // t_n + c, cols, :] = kt[:, c * t_n:(c + 1) * t_n]
            vx_s[base:base + n_rows, 2 * h * LANES:(2 * h + 1) * LANES] = (
                _dot(xm.astype(BF16), wv_ref[h]).astype(BF16))
            vx_s[base:base + n_rows, (2 * h + 1) * LANES:(2 * h + 2) * LANES] = ones

    project(pc_ref, n_ctx_rows, 0)
    project(pl_ref, n_lat_rows, n_ctx_rows)

    ti = lax.broadcasted_iota(jnp.int32, (t_n, t_n), 0)
    si = lax.broadcasted_iota(jnp.int32, (t_n, t_n), 1)
    tri_f = (si <= ti).astype(BF16)
    tri_b = (si >= ti).astype(BF16)
    lane = lax.broadcasted_iota(jnp.int32, (t_n, LANES), 1)
    trow = lax.broadcasted_iota(jnp.int32, (t_n, LANES), 0)
    gate_off = 2 * width

    def gate_prep(p_ref, n_chunks, base_chunk):
        def body(c, _):
            rows = pl.ds(pl.multiple_of(c * t_n, t_n), t_n)
            gcol = p_ref[0, rows, gate_off:gate_off + LANES] + gb_ref[...]
            lf = jax.nn.log_sigmoid(gcol)
            hi = lf.astype(BF16)
            lo = (lf - hi.astype(F32)).astype(BF16)
            pre = _dot(tri_f, hi) + _dot(tri_f, lo)
            suf = _dot(tri_b, hi) + _dot(tri_b, lo)
            bsum = jnp.where(lane < 2 * ML_HEADS, pre, suf)
            rcol = gcol - pltpu.roll(bsum, LANES - ML_HEADS, 1)
            pmax, smax = rcol, rcol
            step = 1
            while step < t_n:
                pmax = jnp.maximum(pmax, jnp.where(trow >= step, pltpu.roll(pmax, step, 0), NEG_BIG))
                smax = jnp.maximum(smax, jnp.where(trow < t_n - step, pltpu.roll(smax, t_n - step, 0), NEG_BIG))
                step *= 2
            bcol_s[base_chunk + c] = bsum
            mcol_s[base_chunk + c] = jnp.where(lane < 2 * ML_HEADS, pmax, smax)
            rrow_s[base_chunk + c] = rcol.T[0:4 * ML_HEADS, :]
            return 0
        lax.fori_loop(0, n_chunks, body, 0)

    gate_prep(pc_ref, nc_ctx, 0)
    gate_prep(pl_ref, nc_lat, nc_ctx)

    def run_head(h):
        cols = slice(h * LANES, (h + 1) * LANES)
        xcols = slice(2 * h * LANES, (2 * h + 2) * LANES)

        def dense(c, _):
            bc = bcol_s[c]
            mc = mcol_s[c]
            for d in range(2):
                li = 2 * d * ML_HEADS + h
                bd_s[c, d] = jnp.broadcast_to(bc[:, li + ML_HEADS:li + ML_HEADS + 1], (t_n, LANES))
                md_s[c, d] = jnp.broadcast_to(mc[:, li:li + 1], (t_n, LANES))
            return 0
        lax.fori_loop(0, nc, dense, 0)
        st_s[...] = jnp.zeros_like(st_s)

        def one_dir(c, d, m_in):
            li = 2 * d * ML_HEADS + h
            mask = (si <= ti) if d == 0 else (si >= ti)
            last = t_n - 1 if d == 0 else 0
            rows = pl.ds(pl.multiple_of(c * t_n, t_n), t_n)
            q = q_s[rows, cols]
            kt = kt_s[c, cols, :]
            vx = vx_s[rows, xcols]
            r_row = rrow_s[c][li:li + 1, :]
            run_max = md_s[c, d]
            b_rep = bd_s[c, d]
            mm = jnp.maximum(m_in, run_max)
            dmat = jnp.exp(jnp.where(mask, r_row - mm, NEG_BIG))
            sm = (dmat * _dot(q, kt)).astype(BF16)
            intra = _dot(sm, vx)
            cross = _dot(q, st_s[d].astype(BF16))
            inter = jnp.exp(m_in - mm)
            num = intra[:, :LANES] + inter * cross[:, :LANES]
            den = intra[:, LANES:] + inter * cross[:, LANES:]
            h_s[d, rows, :] = num / jnp.maximum(jnp.abs(den), jnp.exp(-b_rep - mm))
            r_top = run_max[last:last + 1, :]
            b_last = b_rep[last:last + 1, :]
            kw = (kt.astype(F32) * jnp.exp(r_row - r_top)).astype(BF16)
            inc = _dot(kw, vx)
            g_max = b_last + r_top
            m_new = jnp.maximum(b_last + m_in, g_max)
            dec = jnp.exp(b_last + m_in - m_new)
            inj = jnp.exp(g_max - m_new)
            st_s[d] = (jnp.concatenate([dec, dec], axis=1) * st_s[d]
                       + jnp.concatenate([inj, inj], axis=1) * inc)
            return m_new

        def body(i, carry):
            m_f, m_b = carry
            cb = jnp.where(i < nc_ctx, nc_ctx - 1 - i, nc + nc_ctx - 1 - i)
            return one_dir(i, 0, m_f), one_dir(cb, 1, m_b)

        zero = jnp.zeros((1, LANES), F32)
        lax.fori_loop(0, nc, body, (zero, zero))

        def finish(p_ref, o_ref, n_chunks, base_chunk):
            def fbody(c, _):
                rows = pl.ds(pl.multiple_of(c * t_n, t_n), t_n)
                srows = pl.ds(pl.multiple_of((base_chunk + c) * t_n, t_n), t_n)
                hh = h_s[0, srows, :] + h_s[1, srows, :]
                mu = jnp.mean(hh, axis=1, keepdims=True)
                cen = hh - mu
                var = jnp.mean(cen * cen, axis=1, keepdims=True)
                y = cen * lax.rsqrt(var + EPS) * ng_ref[:, cols]
                og = p_ref[0, rows, width + h * LANES:width + (h + 1) * LANES]
                o_ref[0, rows, cols] = (jax.nn.sigmoid(og) * y).astype(BF16)
                return 0
            lax.fori_loop(0, n_chunks, fbody, 0)

        finish(pc_ref, mc_ref, nc_ctx, 0)
        finish(pl_ref, ml_ref, nc_lat, nc_ctx)

    for h in range(ML_HEADS):
        run_head(h)


def _mlstm_mix(p_ctx, p_lat, conv_w, conv_b, wq, wk, wv, gate_b, norm_g):
    bn, lc, pw = p_ctx.shape
    ll = p_lat.shape[1]
    width = ML_HEADS * LANES
    lt = lc + ll
    nct = lt // ML_CHUNK
    gb = jnp.zeros((1, LANES), F32).at[0, :4 * ML_HEADS].set(gate_b.astype(F32))
    full2 = lambda a: pl.BlockSpec(a.shape, lambda b: (0,) * a.ndim)
    conv_b2, norm_g2 = conv_b.reshape(1, width), norm_g.reshape(1, width)
    assert ML_CHUNK == LANES
    wq, wkt, wv = wq.astype(BF16), jnp.swapaxes(wk, 1, 2).astype(BF16), wv.astype(BF16)
    return pl.pallas_call(
        functools.partial(_mlstm_kernel, n_ctx_rows=lc, n_lat_rows=ll),
        grid=(bn,),
        in_specs=[
            pl.BlockSpec((1, lc, pw), lambda b: (b, 0, 0)),
            pl.BlockSpec((1, ll, pw), lambda b: (b, 0, 0)),
            full2(conv_w), full2(conv_b2), full2(wq), full2(wkt), full2(wv), full2(gb), full2(norm_g2),
        ],
        out_specs=[
            pl.BlockSpec((1, lc, width), lambda b: (b, 0, 0)),
            pl.BlockSpec((1, ll, width), lambda b: (b, 0, 0)),
        ],
        out_shape=[jax.ShapeDtypeStruct((bn, lc, width), BF16), jax.ShapeDtypeStruct((bn, ll, width), BF16)],
        scratch_shapes=[
            pltpu.VMEM((lt, width), BF16),
            pltpu.VMEM((nct, width, ML_CHUNK), BF16),
            pltpu.VMEM((lt, 2 * width), BF16),
            pltpu.VMEM((2, lt, LANES), F32),
            pltpu.VMEM((2, LANES, 2 * LANES), F32),
            pltpu.VMEM((nct, ML_CHUNK, LANES), F32), pltpu.VMEM((nct, ML_CHUNK, LANES), F32),
            pltpu.VMEM((nct, 4 * ML_HEADS, ML_CHUNK), F32),
            pltpu.VMEM((nct, 2, ML_CHUNK, LANES), F32), pltpu.VMEM((nct, 2, ML_CHUNK, LANES), F32),
        ],
        compiler_params=_cparams(("arbitrary",)),
        name="mlstm_mix",
    )(p_ctx, p_lat, conv_w, conv_b2, wq, wkt, wv, gb, norm_g2)


def _ffn_tail(x, y, gate_mix, g2, shift, scale, gate_ffn, w1_ref, w3_ref, w2_ref, hidden_chunk):
    x1 = x + gate_mix * y
    h2 = _norm_mod(x1, g2, shift, scale).astype(BF16)
    hidden = w1_ref.shape[1]
    acc = jnp.zeros_like(x1)
    for j in range(hidden // hidden_chunk):
        cs = slice(j * hidden_chunk, (j + 1) * hidden_chunk)
        z = (_silu(_dot(h2, w1_ref[:, cs])) * _dot(h2, w3_ref[:, cs])).astype(BF16)
        acc = acc + _dot(z, w2_ref[cs, :])
    return x1 + gate_ffn * acc


def _final_norm(x, g):
    return x * lax.rsqrt(jnp.mean(x * x, axis=-1, keepdims=True) + EPS) * g


def _outproj_even_kernel(x_ref, ys_ref, m_ref, gm_ref, sh_ref, sc_ref, gf_ref, g2_ref, gw_ref, gbias_ref,
                         wos_ref, wom_ref, w1_ref, w3_ref, w2_ref, o_ref, *, hidden_chunk):
    ys = jax.nn.gelu(ys_ref[0])
    s = ys * jax.nn.sigmoid(_dot(ys.astype(BF16), gw_ref[...]) + gbias_ref[...])
    y = _dot(s.astype(BF16), wos_ref[...]) + _dot(m_ref[0], wom_ref[...])
    o_ref[0] = _ffn_tail(x_ref[0], y, gm_ref[0], g2_ref[...], sh_ref[0], sc_ref[0], gf_ref[0],
                         w1_ref, w3_ref, w2_ref, hidden_chunk)


def _outproj_odd_kernel(x_ref, a_ref, gm_ref, sh_ref, sc_ref, gf_ref, g2_ref, wo_ref, w1_ref, w3_ref, w2_ref,
                        fg_ref, o_ref, *, hidden_chunk):
    y = _dot(a_ref[0], wo_ref[...])
    out = _ffn_tail(x_ref[0], y, gm_ref[0], g2_ref[...], sh_ref[0], sc_ref[0], gf_ref[0],
                    w1_ref, w3_ref, w2_ref, hidden_chunk)
    o_ref[0] = _final_norm(out, fg_ref[...])


def _hidden_chunk(hidden):
    for c in (1408, 1024, 512, 256, 128):
        if hidden % c == 0:
            return c
    return hidden


def _mod_specs(d, row_of_batch, slots):
    return [pl.BlockSpec((1, 1, d), lambda b, i, s=s: (row_of_batch(b) * N_MOD + s, 0, 0)) for s in slots]


def _outproj_even(x, ys, m, mods, row_of_batch, g2, glu_w, glu_b, w_out, w1, w3, w2):
    bn, ln, d = x.shape
    sw = ys.shape[-1]
    tm = _token_tile(ln)
    res = lambda a: _resident(a.shape, lambda b, i: (0,) * a.ndim)
    tok = lambda w: pl.BlockSpec((1, tm, w), lambda b, i: (b, i, 0))
    wos, wom = w_out[:sw], w_out[sw:]
    glu_b2 = glu_b.reshape(1, sw)
    return pl.pallas_call(
        functools.partial(_outproj_even_kernel, hidden_chunk=_hidden_chunk(w1.shape[1])),
        grid=(bn, ln // tm),
        in_specs=[tok(d), tok(sw), tok(m.shape[-1])] + _mod_specs(d, row_of_batch, (2, 3, 4, 5))
        + [pl.BlockSpec((1, d), lambda b, i: (0, 0)), res(glu_w), res(glu_b2), res(wos), res(wom),
           res(w1), res(w3), res(w2)],
        out_specs=tok(d),
        out_shape=jax.ShapeDtypeStruct((bn, ln, d), F32),
        compiler_params=_cparams(("arbitrary", "arbitrary")),
        name="outproj_ffn_even",
    )(x, ys, m, mods, mods, mods, mods, g2, glu_w, glu_b2, wos, wom, w1, w3, w2)


def _outproj_odd(x, a, mods, row_of_batch, g2, w_out, w1, w3, w2, final_g):
    bn, ln, d = x.shape
    tm = _token_tile(ln)
    res = lambda a_: _resident(a_.shape, lambda b, i: (0,) * a_.ndim)
    tok = lambda w: pl.BlockSpec((1, tm, w), lambda b, i: (b, i, 0))
    return pl.pallas_call(
        functools.partial(_outproj_odd_kernel, hidden_chunk=_hidden_chunk(w1.shape[1])),
        grid=(bn, ln // tm),
        in_specs=[tok(d), tok(a.shape[-1])] + _mod_specs(d, row_of_batch, (2, 3, 4, 5))
        + [pl.BlockSpec((1, d), lambda b, i: (0, 0)), res(w_out), res(w1), res(w3), res(w2),
           pl.BlockSpec((1, d), lambda b, i: (0, 0))],
        out_specs=tok(d),
        out_shape=jax.ShapeDtypeStruct((bn, ln, d), F32),
        compiler_params=_cparams(("arbitrary", "arbitrary")),
        name="outproj_ffn_odd",
    )(x, a, mods, mods, mods, mods, g2, w_out, w1, w3, w2, final_g)


def _rope_heads(t, cos, sin, heads, hk):
    half = hk // 2
    out = []
    for h in range(heads):
        t1 = t[:, h * hk:h * hk + half]
        t2 = t[:, h * hk + half:(h + 1) * hk]
        out.append(t1 * cos - t2 * sin)
        out.append(t2 * cos + t1 * sin)
    return jnp.concatenate(out, axis=-1)


def _inproj_odd_kernel(x_ref, sh_ref, sc_ref, g_ref, cos_ref, sin_ref, w_ref, q_ref, k_ref, v_ref, gate_ref,
                       *, qk, vw, heads):
    h = _norm_mod(x_ref[0], g_ref[...], sh_ref[0], sc_ref[0]).astype(BF16)
    hk = qk // heads
    cos, sin = cos_ref[...], sin_ref[...]
    q = _rope_heads(_dot(h, w_ref[:, 0:qk]), cos, sin, heads, hk)
    q_ref[0] = (q * (hk ** -0.5)).astype(BF16)
    k = _rope_heads(_dot(h, w_ref[:, qk:2 * qk]), cos, sin, heads, hk)
    k_ref[0] = k.astype(BF16)
    v_ref[0] = _dot(h, w_ref[:, 2 * qk:2 * qk + vw]).astype(BF16)
    gate_ref[0] = _dot(h, w_ref[:, 2 * qk + vw:2 * qk + 2 * vw]).astype(BF16)


def _inproj_odd_ctx_kernel(x_ref, sh_ref, sc_ref, g_ref, w_ref, k_ref, v_ref, *, qk, vw):
    h = _norm_mod(x_ref[0], g_ref[...], sh_ref[0], sc_ref[0]).astype(BF16)
    k_ref[0] = _dot(h, w_ref[:, 0:qk]).astype(BF16)
    v_ref[0] = _dot(h, w_ref[:, qk:qk + vw]).astype(BF16)


def _inproj_odd(x, mods, row_of_batch, g, cos, sin, w, qk, vw):
    bn, ln, d = x.shape
    tm = _token_tile(ln)
    tok = lambda w_: pl.BlockSpec((1, tm, w_), lambda b, i: (b, i, 0))
    half = cos.shape[1]
    return pl.pallas_call(
        functools.partial(_inproj_odd_kernel, qk=qk, vw=vw, heads=RET_HEADS),
        grid=(bn, ln // tm),
        in_specs=[tok(d)] + _mod_specs(d, row_of_batch, (0, 1))
        + [pl.BlockSpec((1, d), lambda b, i: (0, 0)),
           pl.BlockSpec((tm, half), lambda b, i: (i, 0)), pl.BlockSpec((tm, half), lambda b, i: (i, 0)),
           _resident(w.shape, lambda b, i: (0, 0))],
        out_specs=[tok(qk), tok(qk), tok(vw), tok(vw)],
        out_shape=[jax.ShapeDtypeStruct((bn, ln, qk), BF16), jax.ShapeDtypeStruct((bn, ln, qk), BF16),
                   jax.ShapeDtypeStruct((bn, ln, vw), BF16), jax.ShapeDtypeStruct((bn, ln, vw), BF16)],
        compiler_params=_cparams(("arbitrary", "arbitrary")),
        name="inproj_odd",
    )(x, mods, mods, g, cos, sin, w)


def _inproj_odd_ctx(x, mods, row_of_batch, g, w_kv, qk, vw):
    bn, ln, d = x.shape
    tm = _token_tile(ln)
    tok = lambda w_: pl.BlockSpec((1, tm, w_), lambda b, i: (b, i, 0))
    return pl.pallas_call(
        functools.partial(_inproj_odd_ctx_kernel, qk=qk, vw=vw),
        grid=(bn, ln // tm),
        in_specs=[tok(d)] + _mod_specs(d, row_of_batch, (0, 1))
        + [pl.BlockSpec((1, d), lambda b, i: (0, 0)), _resident(w_kv.shape, lambda b, i: (0, 0))],
        out_specs=[tok(qk), tok(vw)],
        out_shape=[jax.ShapeDtypeStruct((bn, ln, qk), BF16), jax.ShapeDtypeStruct((bn, ln, vw), BF16)],
        compiler_params=_cparams(("arbitrary", "arbitrary")),
        name="inproj_odd_ctx",
    )(x, mods, mods, g, w_kv)


def _retention_kernel(lg_ref, q_ref, k_ref, v_ref, gate_ref, kc_ref, vc_ref, ng_ref, o_ref, acc_s, sf_s, sb_s,
                      *, chunk, n_ctx_rows, n_lat_rows):
    t_n = chunk
    h = pl.program_id(1)
    lgf = jnp.full((1, 1), lg_ref[0, h], F32)
    lgb = jnp.full((1, 1), lg_ref[1, h], F32)
    nc_ctx, nc_lat = n_ctx_rows // t_n, n_lat_rows // t_n
    ti = lax.broadcasted_iota(jnp.int32, (t_n, t_n), 0)
    si = lax.broadcasted_iota(jnp.int32, (t_n, t_n), 1)
    diff = (ti - si).astype(F32)
    decay = jnp.where(diff >= 0, jnp.exp(lgf * jnp.maximum(diff, 0.0)), jnp.exp(lgb * jnp.maximum(-diff, 0.0)))
    pos = lax.broadcasted_iota(jnp.int32, (t_n, 1), 0).astype(F32)
    inter_f = jnp.exp(lgf * (pos + 1.0))
    inter_b = jnp.exp(lgb * (t_n - pos))
    wend_f = jnp.exp(lgf * (t_n - 1.0 - pos))
    wend_b = jnp.exp(lgb * pos)
    cd_f = jnp.exp(lgf * t_n)
    cd_b = jnp.exp(lgb * t_n)

    def rows_of(c):
        return pl.ds(pl.multiple_of(c * t_n, t_n), t_n)

    def bump(s_ref, k, v, wend, cd):
        kw = (k.astype(F32) * wend).astype(BF16)
        s_ref[...] = cd * s_ref[...] + _dot_tn(kw, v)

    sf_s[...] = jnp.zeros_like(sf_s)
    sb_s[...] = jnp.zeros_like(sb_s)

    def ctx_f(c, _):
        bump(sf_s, kc_ref[0, rows_of(c), :], vc_ref[0, rows_of(c), :], wend_f, cd_f)
        return 0
    lax.fori_loop(0, nc_ctx, ctx_f, 0)

    def lat_f(c, _):
        rows = rows_of(c)
        q, k, v = q_ref[0, rows, :], k_ref[0, rows, :], v_ref[0, rows, :]
        scores = (_dot_nt(q, k) * decay).astype(BF16)
        acc_s[rows, :] = _dot(scores, v) + inter_f * _dot(q, sf_s[...].astype(BF16))
        bump(sf_s, k, v, wend_f, cd_f)
        return 0
    lax.fori_loop(0, nc_lat, lat_f, 0)

    def ctx_b(i, _):
        c = nc_ctx - 1 - i
        bump(sb_s, kc_ref[0, rows_of(c), :], vc_ref[0, rows_of(c), :], wend_b, cd_b)
        return 0
    lax.fori_loop(0, nc_ctx, ctx_b, 0)

    def lat_b(i, _):
        rows = rows_of(nc_lat - 1 - i)
        q, k, v = q_ref[0, rows, :], k_ref[0, rows, :], v_ref[0, rows, :]
        o = acc_s[rows, :] + inter_b * _dot(q, sb_s[...].astype(BF16))
        mu = jnp.mean(o, axis=1, keepdims=True)
        cen = o - mu
        var = jnp.mean(cen * cen, axis=1, keepdims=True)
        y = cen * lax.rsqrt(var + EPS) * ng_ref[...]
        o_ref[0, rows, :] = (_silu(gate_ref[0, rows, :].astype(F32)) * y).astype(BF16)
        bump(sb_s, k, v, wend_b, cd_b)
        return 0
    lax.fori_loop(0, nc_lat, lat_b, 0)


def _retention_mix(q, k, v, gate, k_ctx, v_ctx, log_gamma, norm_g):
    bn, ll, qk = q.shape
    lc = k_ctx.shape[1]
    vw = v.shape[-1]
    hk, hv = qk // RET_HEADS, vw // RET_HEADS
    chunk = 256 if (ll % 256 == 0 and lc % 256 == 0) else 128
    head = lambda n_rows, w: pl.BlockSpec((1, n_rows, w), lambda b, h: (b, 0, h))
    return pl.pallas_call(
        functools.partial(_retention_kernel, chunk=chunk, n_ctx_rows=lc, n_lat_rows=ll),
        grid=(bn, RET_HEADS),
        in_specs=[
            pl.BlockSpec(memory_space=pltpu.SMEM),
            head(ll, hk), head(ll, hk), head(ll, hv), head(ll, hv), head(lc, hk), head(lc, hv),
            pl.BlockSpec((1, hv), lambda b, h: (0, h)),
        ],
        out_specs=head(ll, hv),
        out_shape=jax.ShapeDtypeStruct((bn, ll, vw), BF16),
        scratch_shapes=[pltpu.VMEM((ll, hv), F32), pltpu.VMEM((hk, hv), F32), pltpu.VMEM((hk, hv), F32)],
        compiler_params=_cparams(("arbitrary", "arbitrary")),
        name="retention_mix",
    )(log_gamma.astype(F32), q, k, v, gate, k_ctx, v_ctx, norm_g.reshape(1, vw))


def _grid_rope(n_pos, hk):
    rows = n_pos // GRID_W
    row = jnp.repeat(jnp.arange(rows, dtype=F32), GRID_W)
    col = jnp.tile(jnp.arange(GRID_W, dtype=F32), rows)
    n_freq = hk // 4
    inv = ROPE_BASE ** (-jnp.arange(n_freq, dtype=F32) / n_freq)
    ang = jnp.concatenate([row[:, None] * inv, col[:, None] * inv], -1)
    return jnp.cos(ang), jnp.sin(ang)


def kernel(x, c, ctx, c_ctx, ada_w, ada_b, norm1_g, norm2_g, ab_w_in, ab_w_out, s5_a_re, s5_a_im, s5_log_dt,
           s5_b_re, s5_b_im, s5_c_re, s5_c_im, s5_d, s5_glu_w, s5_glu_b, ml_conv_w, ml_conv_b, ml_wq, ml_wk,
           ml_wv, ml_gate_b, ml_norm_g, ret_w_in, ret_w_out, ret_log_gamma, ret_norm_g, ffn_w1, ffn_w3, ffn_w2,
           final_g):
    bn, ln, d = x.shape
    depth = ada_w.shape[0]
    assert depth == 2, "one S5 || mLSTM layer followed by one retention layer"
    s5_width = s5_d.shape[-1]
    ml_width = ml_norm_g.shape[-1]
    assert ml_width == ML_HEADS * LANES and s5_width % S5_GROUP == 0

    r_pad = -(-(bn + 1) // SUBLANES) * SUBLANES
    vec = jnp.zeros((r_pad, d), F32).at[:bn].set(c).at[bn].set(c_ctx)
    mods_all = _modulation(vec, ada_w, ada_b).reshape(depth, r_pad * N_MOD, 1, d)
    lat_row = lambda b: b
    ctx_row = lambda b: bn

    mods = mods_all[0]
    g1 = norm1_g[0].reshape(1, d)
    g2 = norm2_g[0].reshape(1, d)
    w_in = ab_w_in[0]
    n_gate = w_in.shape[1] - s5_width - 2 * ml_width
    w_in = jnp.pad(w_in, ((0, 0), (0, LANES - n_gate))).astype(BF16)
    u_lat, p_lat = _inproj_even(x, mods, lat_row, g1, w_in, s5_width)
    u_ctx, p_ctx = _inproj_even(ctx, mods, ctx_row, g1, w_in, s5_width)
    mats = _s5_matrices(s5_a_re[0], s5_a_im[0], s5_log_dt[0], s5_b_re[0], s5_b_im[0], s5_c_re[0], s5_c_im[0],
                        s5_d[0])
    ys_ctx, ys_lat = _s5_mix(u_ctx, u_lat, mats)
    m_ctx, m_lat = _mlstm_mix(p_ctx, p_lat, ml_conv_w[0], ml_conv_b[0], ml_wq[0], ml_wk[0], ml_wv[0],
                              ml_gate_b[0], ml_norm_g[0])
    glu_w = s5_glu_w[0].astype(BF16)
    w_out = ab_w_out[0].astype(BF16)
    w1, w3, w2 = ffn_w1[0].astype(BF16), ffn_w3[0].astype(BF16), ffn_w2[0].astype(BF16)
    x = _outproj_even(x, ys_lat, m_lat, mods, lat_row, g2, glu_w, s5_glu_b[0], w_out, w1, w3, w2)
    ctx = _outproj_even(ctx, ys_ctx, m_ctx, mods, ctx_row, g2, glu_w, s5_glu_b[0], w_out, w1, w3, w2)

    mods = mods_all[1]
    g1 = norm1_g[1].reshape(1, d)
    g2 = norm2_g[1].reshape(1, d)
    vw = ret_norm_g.shape[-1]
    qk = (ret_w_in.shape[-1] - 2 * vw) // 2
    w_in = ret_w_in[0].astype(BF16)
    cos, sin = _grid_rope(ln, qk // RET_HEADS)
    q, k, v, gate = _inproj_odd(x, mods, lat_row, g1, cos, sin, w_in, qk, vw)
    k_ctx, v_ctx = _inproj_odd_ctx(ctx, mods, ctx_row, g1, w_in[:, qk:2 * qk + vw], qk, vw)
    a = _retention_mix(q, k, v, gate, k_ctx, v_ctx, ret_log_gamma[0], ret_norm_g[0])
    w_out = ret_w_out[0].astype(BF16)
    w1, w3, w2 = ffn_w1[1].astype(BF16), ffn_w3[1].astype(BF16), ffn_w2[1].astype(BF16)
    return _outproj_odd(x, a, mods, lat_row, g2, w_out, w1, w3, w2, final_g.reshape(1, d))
```

```python
import functools
import math

import jax
import jax.numpy as jnp
from jax import lax
from jax.experimental import pallas as pl
from jax.experimental.pallas import tpu as pltpu

F32 = jnp.float32
BF16 = jnp.bfloat16

EPS = 1e-6
N_MOD = 6
GRID_W = 64
ROPE_BASE = 10000.0

S5_GROUP = 16
S5_STATE = 64
S5_STEP = 16
ML_HEADS = 4
ML_CHUNK = 128
RET_HEADS = 4
SUBLANES = 8
LANES = 128
NEG_BIG = -1e30
VMEM_LIMIT = 56 * 1024 * 1024


def _cparams(sem):
    return pltpu.CompilerParams(dimension_semantics=sem, vmem_limit_bytes=VMEM_LIMIT)


def _resident(shape, index_map):
    return pl.BlockSpec(shape, index_map, pipeline_mode=pl.Buffered(1))


def _token_tile(n):
    for t in (512, 256, 128):
        if n % t == 0:
            return t
    raise ValueError(f"sequence length {n} must be a multiple of 128")


def _silu(v):
    return v * jax.nn.sigmoid(v)


def _norm_mod(x, g, shift, scale):
    y = x * lax.rsqrt(jnp.mean(x * x, axis=-1, keepdims=True) + EPS)
    return (y * g) * (1.0 + scale) + shift


def _dot(a, b):
    return jnp.dot(a, b, preferred_element_type=F32)


def _dot_nt(a, b):
    return lax.dot_general(a, b, (((1,), (1,)), ((), ())), preferred_element_type=F32)


def _dot_tn(a, b):
    return lax.dot_general(a, b, (((0,), (0,)), ((), ())), preferred_element_type=F32)


def _mod_kernel(v_ref, w_ref, b_ref, o_ref):
    s = _silu(v_ref[...]).astype(BF16)
    o_ref[0] = _dot(s, w_ref[0].astype(BF16)) + b_ref[0]


def _modulation(vec, ada_w, ada_b):
    depth, d, n = ada_w.shape
    r = vec.shape[0]
    tn = 1024
    return pl.pallas_call(
        _mod_kernel,
        grid=(depth, n // tn),
        in_specs=[
            pl.BlockSpec((r, d), lambda l, j: (0, 0)),
            pl.BlockSpec((1, d, tn), lambda l, j: (l, 0, j)),
            pl.BlockSpec((1, 1, tn), lambda l, j: (l, 0, j)),
        ],
        out_specs=pl.BlockSpec((1, r, tn), lambda l, j: (l, 0, j)),
        out_shape=jax.ShapeDtypeStruct((depth, r, n), F32),
        compiler_params=_cparams(("arbitrary", "arbitrary")),
        name="adaln_modulation",
    )(vec, ada_w, ada_b.reshape(depth, 1, n))


def _inproj_even_kernel(x_ref, sh_ref, sc_ref, g_ref, w_ref, u_ref, p_ref, *, s5_width):
    h = _norm_mod(x_ref[0], g_ref[...], sh_ref[0], sc_ref[0]).astype(BF16)
    p = _dot(h, w_ref[...])
    u_ref[0] = p[:, :s5_width].astype(BF16)
    p_ref[0] = p[:, s5_width:]


def _inproj_even(x, mods, row_of_batch, g, w, s5_width):
    bn, ln, d = x.shape
    n = w.shape[1]
    tm = _token_tile(ln)
    return pl.pallas_call(
        functools.partial(_inproj_even_kernel, s5_width=s5_width),
        grid=(bn, ln // tm),
        in_specs=[
            pl.BlockSpec((1, tm, d), lambda b, i: (b, i, 0)),
            pl.BlockSpec((1, 1, d), lambda b, i: (row_of_batch(b) * N_MOD + 0, 0, 0)),
            pl.BlockSpec((1, 1, d), lambda b, i: (row_of_batch(b) * N_MOD + 1, 0, 0)),
            pl.BlockSpec((1, d), lambda b, i: (0, 0)),
            _resident((d, n), lambda b, i: (0, 0)),
        ],
        out_specs=[
            pl.BlockSpec((1, tm, s5_width), lambda b, i: (b, i, 0)),
            pl.BlockSpec((1, tm, n - s5_width), lambda b, i: (b, i, 0)),
        ],
        out_shape=[
            jax.ShapeDtypeStruct((bn, ln, s5_width), BF16),
            jax.ShapeDtypeStruct((bn, ln, n - s5_width), F32),
        ],
        compiler_params=_cparams(("arbitrary", "arbitrary")),
        name="inproj_even",
    )(x, mods, mods, g, w)


def _s5_response_kernel(c_ref, w_ref, o_ref):
    for n in range(c_ref.shape[0]):
        o_ref[n] = lax.dot_general(c_ref[n], w_ref[n], (((1,), (1,)), ((), ())),
                                   precision=lax.Precision.HIGHEST, preferred_element_type=F32)


def _s5_response(c_cat, w_cat):
    n, co, kk = c_cat.shape
    r = w_cat.shape[1]
    nb = 8
    return pl.pallas_call(
        _s5_response_kernel,
        grid=(n // nb,),
        in_specs=[pl.BlockSpec((nb, co, kk), lambda i: (i, 0, 0)), pl.BlockSpec((nb, r, kk), lambda i: (i, 0, 0))],
        out_specs=pl.BlockSpec((nb, co, r), lambda i: (i, 0, 0)),
        out_shape=jax.ShapeDtypeStruct((n, co, r), F32),
        compiler_params=_cparams(("arbitrary",)),
        name="s5_response",
    )(c_cat, w_cat)


def _s5_matrices(a_re, a_im, log_dt, b_re, b_im, c_re, c_im, d_skip):
    k = S5_STEP
    a_re, a_im = a_re.astype(F32), a_im.astype(F32)
    dt = jnp.exp(log_dt.astype(F32))[..., None]
    lam_re, lam_im = a_re * dt, a_im * dt
    steps = jnp.arange(k + 1, dtype=F32)[:, None, None, None]
    mag = jnp.exp(steps * lam_re)
    pr, pi = mag * jnp.cos(steps * lam_im), mag * jnp.sin(steps * lam_im)
    den = a_re * a_re + a_im * a_im
    nr = pr[1] - 1.0
    coef_re = (nr * a_re + pi[1] * a_im) / den
    coef_im = (pi[1] * a_re - nr * a_im) / den
    bt_re = jnp.swapaxes(b_re.astype(F32), -1, -2)
    bt_im = jnp.swapaxes(b_im.astype(F32), -1, -2)
    bb_re = coef_re[:, :, None] * bt_re - coef_im[:, :, None] * bt_im
    bb_im = coef_re[:, :, None] * bt_im + coef_im[:, :, None] * bt_re
    wr = pr[:, :, :, None] * bb_re - pi[:, :, :, None] * bb_im
    wi = pr[:, :, :, None] * bb_im + pi[:, :, :, None] * bb_re
    c_re, c_im = c_re.astype(F32), c_im.astype(F32)
    g_n, c_n, p_n = a_re.shape[1], b_re.shape[-1], a_re.shape[-1]
    w_cat = jnp.transpose(jnp.concatenate([wr, -wi], -1), (1, 2, 0, 3, 4)).reshape(2 * g_n, (k + 1) * c_n, 2 * p_n)
    c_cat = jnp.concatenate([c_re, c_im], -1).reshape(2 * g_n, c_n, 2 * p_n)
    resp = _s5_response(c_cat, w_cat).reshape(2, g_n, c_n, k + 1, c_n)
    skip = (jnp.eye(c_n, dtype=F32) * d_skip.astype(F32).reshape(g_n, 1, c_n))[:, :, None, :]
    gen = jnp.concatenate([resp[0, :, :, k - 1:0:-1], resp[0, :, :, 0:1] + resp[1, :, :, 0:1] + skip,
                           resp[1, :, :, 1:k]], axis=2).reshape(g_n, c_n, (2 * k - 1) * c_n)
    toep = jnp.stack([gen[:, :, (k - 1 - t) * c_n:(2 * k - 1 - t) * c_n] for t in range(k)], axis=1)
    toep = toep.reshape(g_n, k * c_n, k * c_n)
    endw = jnp.concatenate([wr[k - 1::-1, 0], wr[:k, 1], wi[k - 1::-1, 0], wi[:k, 1]], -1)
    endw = jnp.transpose(endw, (1, 0, 2, 3)).reshape(g_n, k * c_n, 4 * p_n)

    def out_cols(d, powers_re, powers_im):
        cr, ci = c_re[d][None], c_im[d][None]
        pre, pim = powers_re[:, :, None, :], powers_im[:, :, None, :]
        return cr * pre - ci * pim, -(cr * pim + ci * pre)

    of_re, of_im = out_cols(0, pr[1:, 0], pi[1:, 0])
    ob_re, ob_im = out_cols(1, pr[k:0:-1, 1], pi[k:0:-1, 1])
    outw_t = jnp.concatenate([of_re, ob_re, of_im, ob_im], -1)
    outw_t = jnp.transpose(outw_t, (1, 0, 2, 3)).reshape(g_n, k * c_n, 4 * p_n)
    a0 = jnp.concatenate([pr[k, 0], pr[k, 1]], -1)[:, None, :]
    a1 = jnp.concatenate([pi[k, 0], pi[k, 1]], -1)[:, None, :]
    return toep.astype(BF16), endw.astype(BF16), outw_t.astype(BF16), a0, a1


def _s5_kernel(uc_ref, ul_ref, toep_ref, endw_ref, outw_ref, a0_ref, a1_ref, yc_ref, yl_ref, ec_ref, el_ref,
               *, groups, rows, n_ctx, n_lat):
    p = S5_STATE
    for g in range(groups):
        ec_ref[g] = _dot(uc_ref[g], endw_ref[g])
        el_ref[g] = _dot(ul_ref[g], endw_ref[g])
    a0 = a0_ref[...]
    a1 = a1_ref[...]
    fwd_lane = lax.broadcasted_iota(jnp.int32, (groups, rows, 2 * p), 2) < p

    def scan(e_ref, n, carry):
        def body(i, c):
            s0, s1 = c
            rf = pl.ds(pl.multiple_of(i * rows, rows), rows)
            rb = pl.ds(pl.multiple_of((n - 1 - i) * rows, rows), rows)
            ef = e_ref[:, rf, :]
            eb = e_ref[:, rb, :]
            e_ref[:, rf, 0:p] = s0[:, :, 0:p]
            e_ref[:, rb, p:2 * p] = s0[:, :, p:2 * p]
            e_ref[:, rf, 2 * p:3 * p] = s1[:, :, 0:p]
            e_ref[:, rb, 3 * p:4 * p] = s1[:, :, p:2 * p]
            e0 = jnp.where(fwd_lane, ef[:, :, 0:2 * p], eb[:, :, 0:2 * p])
            e1 = jnp.where(fwd_lane, ef[:, :, 2 * p:4 * p], eb[:, :, 2 * p:4 * p])
            return a0 * s0 - a1 * s1 + e0, a0 * s1 + a1 * s0 + e1
        return lax.fori_loop(0, n, body, carry)

    zero = jnp.zeros((groups, rows, 2 * p), F32)
    carry = scan(ec_ref, n_ctx, (zero, zero))
    scan(el_ref, n_lat, carry)
    for g in range(groups):
        yc_ref[g] = _dot_nt(uc_ref[g], toep_ref[g]) + _dot_nt(ec_ref[g].astype(BF16), outw_ref[g])
        yl_ref[g] = _dot_nt(ul_ref[g], toep_ref[g]) + _dot_nt(el_ref[g].astype(BF16), outw_ref[g])


def _s5_stack(u, bpad):
    bn, ln, w = u.shape
    g_n = w // S5_GROUP
    t = u.reshape(bn, ln // S5_STEP, S5_STEP, g_n, S5_GROUP)
    t = jnp.transpose(t, (3, 1, 0, 2, 4))
    if bpad != bn:
        t = jnp.pad(t, ((0, 0), (0, 0), (0, bpad - bn), (0, 0), (0, 0)))
    return t.reshape(g_n, (ln // S5_STEP) * bpad, S5_STEP * S5_GROUP)


def _s5_unstack(y, bn, bpad):
    g_n = y.shape[0]
    t = y.reshape(g_n, -1, bpad, S5_STEP, S5_GROUP)[:, :, :bn]
    t = jnp.transpose(t, (2, 1, 3, 0, 4))
    return t.reshape(bn, -1, g_n * S5_GROUP)


def _s5_mix(u_ctx, u_lat, mats):
    toep, endw, outw, a0, a1 = mats
    bn = u_lat.shape[0]
    bpad = -(-bn // SUBLANES) * SUBLANES
    uc, ul = _s5_stack(u_ctx, bpad), _s5_stack(u_lat, bpad)
    g_n, rc, kc = uc.shape
    rl = ul.shape[1]
    gb = 8
    blk3 = lambda r, c: pl.BlockSpec((gb, r, c), lambda i: (i, 0, 0))
    yc, yl = pl.pallas_call(
        functools.partial(_s5_kernel, groups=gb, rows=bpad, n_ctx=rc // bpad, n_lat=rl // bpad),
        grid=(g_n // gb,),
        in_specs=[blk3(rc, kc), blk3(rl, kc), blk3(kc, kc), blk3(kc, kc), blk3(kc, kc),
                  blk3(1, 2 * S5_STATE), blk3(1, 2 * S5_STATE)],
        out_specs=[blk3(rc, kc), blk3(rl, kc)],
        out_shape=[jax.ShapeDtypeStruct((g_n, rc, kc), F32), jax.ShapeDtypeStruct((g_n, rl, kc), F32)],
        scratch_shapes=[pltpu.VMEM((gb, rc, kc), F32), pltpu.VMEM((gb, rl, kc), F32)],
        compiler_params=_cparams(("arbitrary",)),
        name="s5_scan",
    )(uc, ul, toep, endw, outw, a0, a1)
    return _s5_unstack(yc, bn, bpad), _s5_unstack(yl, bn, bpad)


def _mlstm_kernel(pc_ref, pl_ref, cw_ref, cb_ref, wq_ref, wkt_ref, wv_ref, gb_ref, ng_ref, mc_ref, ml_ref,
                  q_s, kt_s, vx_s, h_s, st_s, bcol_s, mcol_s, rrow_s, *, n_ctx_rows, n_lat_rows):
    t_n = ML_CHUNK
    width = ML_HEADS * LANES
    nc_ctx, nc_lat = n_ctx_rows // t_n, n_lat_rows // t_n
    nc = nc_ctx + nc_lat
    scale_k = LANES ** -0.5

    def project(p_ref, n_rows, base):
        row = lax.broadcasted_iota(jnp.int32, (n_rows, LANES), 0)
        ones = jnp.ones((n_rows, LANES), BF16)
        for h in range(ML_HEADS):
            cols = slice(h * LANES, (h + 1) * LANES)
            xm = p_ref[0, :, cols]
            taps = cw_ref[:, cols]
            n_tap = taps.shape[0]
            acc = jnp.zeros_like(xm) + cb_ref[:, cols]
            for j in range(n_tap):
                d = j - n_tap // 2
                if d == 0:
                    sh = xm
                else:
                    sh = pltpu.roll(xm, (-d) % n_rows, 0)
                    sh = jnp.where((row + d >= 0) & (row + d < n_rows), sh, 0.0)
                acc = acc + sh * taps[j:j + 1, :]
            xc = _silu(acc).astype(BF16)
            q_s[base:base + n_rows, cols] = _dot(xc, wq_ref[h]).astype(BF16)
            kt = (_dot_nt(wkt_ref[h], xc) * scale_k).astype(BF16)
            for c in range(n_rows // t_n):
                kt_s[base // t_n + c, cols, :] = kt[:, c * t_n:(c + 1) * t_n]
            vx_s[base:base + n_rows, 2 * h * LANES:(2 * h + 1) * LANES] = (
                _dot(xm.astype(BF16), wv_ref[h]).astype(BF16))
            vx_s[base:base + n_rows, (2 * h + 1) * LANES:(2 * h + 2) * LANES] = ones

    project(pc_ref, n_ctx_rows, 0)
    project(pl_ref, n_lat_rows, n_ctx_rows)

    ti = lax.broadcasted_iota(jnp.int32, (t_n, t_n), 0)
    si = lax.broadcasted_iota(jnp.int32, (t_n, t_n), 1)
    tri_f = (si <= ti).astype(BF16)
    tri_b = (si >= ti).astype(BF16)
    lane = lax.broadcasted_iota(jnp.int32, (t_n, LANES), 1)
    trow = lax.broadcasted_iota(jnp.int32, (t_n, LANES), 0)
    gate_off = 2 * width

    def gate_prep(p_ref, n_chunks, base_chunk):
        def body(c, _):
            rows = pl.ds(pl.multiple_of(c * t_n, t_n), t_n)
            gcol = p_ref[0, rows, gate_off:gate_off + LANES] + gb_ref[...]
            lf = jax.nn.log_sigmoid(gcol)
            hi = lf.astype(BF16)
            lo = (lf - hi.astype(F32)).astype(BF16)
            pre = _dot(tri_f, hi) + _dot(tri_f, lo)
            suf = _dot(tri_b, hi) + _dot(tri_b, lo)
            bsum = jnp.where(lane < 2 * ML_HEADS, pre, suf)
            rcol = gcol - pltpu.roll(bsum, LANES - ML_HEADS, 1)
            pmax, smax = rcol, rcol
            step = 1
            while step < t_n:
                pmax = jnp.maximum(pmax, jnp.where(trow >= step, pltpu.roll(pmax, step, 0), NEG_BIG))
                smax = jnp.maximum(smax, jnp.where(trow < t_n - step, pltpu.roll(smax, t_n - step, 0), NEG_BIG))
                step *= 2
            bcol_s[base_chunk + c] = bsum
            mcol_s[base_chunk + c] = jnp.where(lane < 2 * ML_HEADS, pmax, smax)
            rrow_s[base_chunk + c] = rcol.T[0:4 * ML_HEADS, :]
            return 0
        lax.fori_loop(0, n_chunks, body, 0)

    gate_prep(pc_ref, nc_ctx, 0)
    gate_prep(pl_ref, nc_lat, nc_ctx)

    def one_dir(c, h, d, m_in):
        cols = slice(h * LANES, (h + 1) * LANES)
        xcols = slice(2 * h * LANES, (2 * h + 2) * LANES)
        li = 2 * d * ML_HEADS + h
        mask = (si <= ti) if d == 0 else (si >= ti)
        last = t_n - 1 if d == 0 else 0
        rows = pl.ds(pl.multiple_of(c * t_n, t_n), t_n)
        q = q_s[rows, cols]
        kt = kt_s[c, cols, :]
        vx = vx_s[rows, xcols]
        r_row = rrow_s[c][li:li + 1, :]
        run_max = jnp.broadcast_to(mcol_s[c][:, li:li + 1], (t_n, LANES))
        b_rep = jnp.broadcast_to(bcol_s[c][:, li + ML_HEADS:li + ML_HEADS + 1], (t_n, LANES))
        mm = jnp.maximum(m_in, run_max)
        dmat = jnp.exp(jnp.where(mask, r_row - mm, NEG_BIG))
        sm = (dmat * _dot(q, kt)).astype(BF16)
        intra = _dot(sm, vx)
        cross = _dot(q, st_s[2 * h + d].astype(BF16))
        inter = jnp.exp(m_in - mm)
        num = intra[:, :LANES] + inter * cross[:, :LANES]
        den = intra[:, LANES:] + inter * cross[:, LANES:]
        h_s[rows, cols] = h_s[rows, cols] + num / jnp.maximum(jnp.abs(den), jnp.exp(-b_rep - mm))
        r_top = run_max[last:last + 1, :]
        b_last = b_rep[last:last + 1, :]
        kw = (kt.astype(F32) * jnp.exp(r_row - r_top)).astype(BF16)
        inc = _dot(kw, vx)
        g_max = b_last + r_top
        m_new = jnp.maximum(b_last + m_in, g_max)
        dec = jnp.exp(b_last + m_in - m_new)
        inj = jnp.exp(g_max - m_new)
        st_s[2 * h + d] = (jnp.concatenate([dec, dec], axis=1) * st_s[2 * h + d]
                           + jnp.concatenate([inj, inj], axis=1) * inc)
        return m_new

    st_s[...] = jnp.zeros_like(st_s)
    h_s[...] = jnp.zeros_like(h_s)

    def body(i, carry):
        cb = jnp.where(i < nc_ctx, nc_ctx - 1 - i, nc + nc_ctx - 1 - i)
        return tuple(one_dir(i if d == 0 else cb, h, d, carry[2 * h + d])
                     for h in range(ML_HEADS) for d in range(2))

    zero = jnp.zeros((1, LANES), F32)
    lax.fori_loop(0, nc, body, (zero,) * (2 * ML_HEADS))

    def finish(p_ref, o_ref, n_chunks, base_chunk):
        def fbody(c, _):
            rows = pl.ds(pl.multiple_of(c * t_n, t_n), t_n)
            srows = pl.ds(pl.multiple_of((base_chunk + c) * t_n, t_n), t_n)
            for h in range(ML_HEADS):
                cols = slice(h * LANES, (h + 1) * LANES)
                hh = h_s[srows, cols]
                mu = jnp.mean(hh, axis=1, keepdims=True)
                cen = hh - mu
                var = jnp.mean(cen * cen, axis=1, keepdims=True)
                y = cen * lax.rsqrt(var + EPS) * ng_ref[:, cols]
                og = p_ref[0, rows, width + h * LANES:width + (h + 1) * LANES]
                o_ref[0, rows, cols] = (jax.nn.sigmoid(og) * y).astype(BF16)
            return 0
        lax.fori_loop(0, n_chunks, fbody, 0)

    finish(pc_ref, mc_ref, nc_ctx, 0)
    finish(pl_ref, ml_ref, nc_lat, nc_ctx)


def _mlstm_mix(p_ctx, p_lat, conv_w, conv_b, wq, wk, wv, gate_b, norm_g):
    bn, lc, pw = p_ctx.shape
    ll = p_lat.shape[1]
    width = ML_HEADS * LANES
    lt = lc + ll
    nct = lt // ML_CHUNK
    gb = jnp.zeros((1, LANES), F32).at[0, :4 * ML_HEADS].set(gate_b.astype(F32))
    full2 = lambda a: pl.BlockSpec(a.shape, lambda b: (0,) * a.ndim)
    conv_b2, norm_g2 = conv_b.reshape(1, width), norm_g.reshape(1, width)
    assert ML_CHUNK == LANES
    wq, wkt, wv = wq.astype(BF16), jnp.swapaxes(wk, 1, 2).astype(BF16), wv.astype(BF16)
    return pl.pallas_call(
        functools.partial(_mlstm_kernel, n_ctx_rows=lc, n_lat_rows=ll),
        grid=(bn,),
        in_specs=[
            pl.BlockSpec((1, lc, pw), lambda b: (b, 0, 0)),
            pl.BlockSpec((1, ll, pw), lambda b: (b, 0, 0)),
            full2(conv_w), full2(conv_b2), full2(wq), full2(wkt), full2(wv), full2(gb), full2(norm_g2),
        ],
        out_specs=[
            pl.BlockSpec((1, lc, width), lambda b: (b, 0, 0)),
            pl.BlockSpec((1, ll, width), lambda b: (b, 0, 0)),
        ],
        out_shape=[jax.ShapeDtypeStruct((bn, lc, width), BF16), jax.ShapeDtypeStruct((bn, ll, width), BF16)],
        scratch_shapes=[
            pltpu.VMEM((lt, width), BF16),
            pltpu.VMEM((nct, width, ML_CHUNK), BF16),
            pltpu.VMEM((lt, 2 * width), BF16),
            pltpu.VMEM((lt, width), F32),
            pltpu.VMEM((2 * ML_HEADS, LANES, 2 * LANES), F32),
            pltpu.VMEM((nct, ML_CHUNK, LANES), F32), pltpu.VMEM((nct, ML_CHUNK, LANES), F32),
            pltpu.VMEM((nct, 4 * ML_HEADS, ML_CHUNK), F32),
        ],
        compiler_params=_cparams(("arbitrary",)),
        name="mlstm_mix",
    )(p_ctx, p_lat, conv_w, conv_b2, wq, wkt, wv, gb, norm_g2)


def _ffn_tail(x, y, gate_mix, g2, shift, scale, gate_ffn, w1_ref, w3_ref, w2_ref, hidden_chunk):
    x1 = x + gate_mix * y
    h2 = _norm_mod(x1, g2, shift, scale).astype(BF16)
    hidden = w1_ref.shape[1]
    acc = jnp.zeros_like(x1)
    for j in range(hidden // hidden_chunk):
        cs = slice(j * hidden_chunk, (j + 1) * hidden_chunk)
        z = (_silu(_dot(h2, w1_ref[:, cs])) * _dot(h2, w3_ref[:, cs])).astype(BF16)
        acc = acc + _dot(z, w2_ref[cs, :])
    return x1 + gate_ffn * acc


def _final_norm(x, g):
    return x * lax.rsqrt(jnp.mean(x * x, axis=-1, keepdims=True) + EPS) * g


def _outproj_even_kernel(x_ref, ys_ref, m_ref, gm_ref, sh_ref, sc_ref, gf_ref, g2_ref, gw_ref, gbias_ref,
                         wos_ref, wom_ref, w1_ref, w3_ref, w2_ref, o_ref, *, hidden_chunk):
    ys = jax.nn.gelu(ys_ref[0])
    s = ys * jax.nn.sigmoid(_dot(ys.astype(BF16), gw_ref[...]) + gbias_ref[...])
    y = _dot(s.astype(BF16), wos_ref[...]) + _dot(m_ref[0], wom_ref[...])
    o_ref[0] = _ffn_tail(x_ref[0], y, gm_ref[0], g2_ref[...], sh_ref[0], sc_ref[0], gf_ref[0],
                         w1_ref, w3_ref, w2_ref, hidden_chunk)


def _outproj_odd_kernel(x_ref, a_ref, gm_ref, sh_ref, sc_ref, gf_ref, g2_ref, wo_ref, w1_ref, w3_ref, w2_ref,
                        fg_ref, o_ref, *, hidden_chunk):
    y = _dot(a_ref[0], wo_ref[...])
    out = _ffn_tail(x_ref[0], y, gm_ref[0], g2_ref[...], sh_ref[0], sc_ref[0], gf_ref[0],
                    w1_ref, w3_ref, w2_ref, hidden_chunk)
    o_ref[0] = _final_norm(out, fg_ref[...])


def _hidden_chunk(hidden):
    for c in (1408, 1024, 512, 256, 128):
        if hidden % c == 0:
            return c
    return hidden


def _mod_specs(d, row_of_batch, slots):
    return [pl.BlockSpec((1, 1, d), lambda b, i, s=s: (row_of_batch(b) * N_MOD + s, 0, 0)) for s in slots]


def _outproj_even(x, ys, m, mods, row_of_batch, g2, glu_w, glu_b, w_out, w1, w3, w2):
    bn, ln, d = x.shape
    sw = ys.shape[-1]
    tm = _token_tile(ln)
    res = lambda a: _resident(a.shape, lambda b, i: (0,) * a.ndim)
    tok = lambda w: pl.BlockSpec((1, tm, w), lambda b, i: (b, i, 0))
    wos, wom = w_out[:sw], w_out[sw:]
    glu_b2 = glu_b.reshape(1, sw)
    return pl.pallas_call(
        functools.partial(_outproj_even_kernel, hidden_chunk=_hidden_chunk(w1.shape[1])),
        grid=(bn, ln // tm),
        in_specs=[tok(d), tok(sw), tok(m.shape[-1])] + _mod_specs(d, row_of_batch, (2, 3, 4, 5))
        + [pl.BlockSpec((1, d), lambda b, i: (0, 0)), res(glu_w), res(glu_b2), res(wos), res(wom),
           res(w1), res(w3), res(w2)],
        out_specs=tok(d),
        out_shape=jax.ShapeDtypeStruct((bn, ln, d), F32),
        compiler_params=_cparams(("arbitrary", "arbitrary")),
        name="outproj_ffn_even",
    )(x, ys, m, mods, mods, mods, mods, g2, glu_w, glu_b2, wos, wom, w1, w3, w2)


def _outproj_odd(x, a, mods, row_of_batch, g2, w_out, w1, w3, w2, final_g):
    bn, ln, d = x.shape
    tm = _token_tile(ln)
    res = lambda a_: _resident(a_.shape, lambda b, i: (0,) * a_.ndim)
    tok = lambda w: pl.BlockSpec((1, tm, w), lambda b, i: (b, i, 0))
    return pl.pallas_call(
        functools.partial(_outproj_odd_kernel, hidden_chunk=_hidden_chunk(w1.shape[1])),
        grid=(bn, ln // tm),
        in_specs=[tok(d), tok(a.shape[-1])] + _mod_specs(d, row_of_batch, (2, 3, 4, 5))
        + [pl.BlockSpec((1, d), lambda b, i: (0, 0)), res(w_out), res(w1), res(w3), res(w2),
           pl.BlockSpec((1, d), lambda b, i: (0, 0))],
        out_specs=tok(d),
        out_shape=jax.ShapeDtypeStruct((bn, ln, d), F32),
        compiler_params=_cparams(("arbitrary", "arbitrary")),
        name="outproj_ffn_odd",
    )(x, a, mods, mods, mods, mods, g2, w_out, w1, w3, w2, final_g)


def _rope_heads(t, cos, sin, heads, hk):
    half = hk // 2
    out = []
    for h in range(heads):
        t1 = t[:, h * hk:h * hk + half]
        t2 = t[:, h * hk + half:(h + 1) * hk]
        out.append(t1 * cos - t2 * sin)
        out.append(t2 * cos + t1 * sin)
    return jnp.concatenate(out, axis=-1)


def _inproj_odd_kernel(x_ref, sh_ref, sc_ref, g_ref, cos_ref, sin_ref, w_ref, q_ref, k_ref, v_ref, gate_ref,
                       *, qk, vw, heads):
    h = _norm_mod(x_ref[0], g_ref[...], sh_ref[0], sc_ref[0]).astype(BF16)
    hk = qk // heads
    cos, sin = cos_ref[...], sin_ref[...]
    q = _rope_heads(_dot(h, w_ref[:, 0:qk]), cos, sin, heads, hk)
    q_ref[0] = (q * (hk ** -0.5)).astype(BF16)
    k = _rope_heads(_dot(h, w_ref[:, qk:2 * qk]), cos, sin, heads, hk)
    k_ref[0] = k.astype(BF16)
    v_ref[0] = _dot(h, w_ref[:, 2 * qk:2 * qk + vw]).astype(BF16)
    gate_ref[0] = _dot(h, w_ref[:, 2 * qk + vw:2 * qk + 2 * vw]).astype(BF16)


def _inproj_odd_ctx_kernel(x_ref, sh_ref, sc_ref, g_ref, w_ref, k_ref, v_ref, *, qk, vw):
    h = _norm_mod(x_ref[0], g_ref[...], sh_ref[0], sc_ref[0]).astype(BF16)
    k_ref[0] = _dot(h, w_ref[:, 0:qk]).astype(BF16)
    v_ref[0] = _dot(h, w_ref[:, qk:qk + vw]).astype(BF16)


def _inproj_odd(x, mods, row_of_batch, g, cos, sin, w, qk, vw):
    bn, ln, d = x.shape
    tm = _token_tile(ln)
    tok = lambda w_: pl.BlockSpec((1, tm, w_), lambda b, i: (b, i, 0))
    half = cos.shape[1]
    return pl.pallas_call(
        functools.partial(_inproj_odd_kernel, qk=qk, vw=vw, heads=RET_HEADS),
        grid=(bn, ln // tm),
        in_specs=[tok(d)] + _mod_specs(d, row_of_batch, (0, 1))
        + [pl.BlockSpec((1, d), lambda b, i: (0, 0)),
           pl.BlockSpec((tm, half), lambda b, i: (i, 0)), pl.BlockSpec((tm, half), lambda b, i: (i, 0)),
           _resident(w.shape, lambda b, i: (0, 0))],
        out_specs=[tok(qk), tok(qk), tok(vw), tok(vw)],
        out_shape=[jax.ShapeDtypeStruct((bn, ln, qk), BF16), jax.ShapeDtypeStruct((bn, ln, qk), BF16),
                   jax.ShapeDtypeStruct((bn, ln, vw), BF16), jax.ShapeDtypeStruct((bn, ln, vw), BF16)],
        compiler_params=_cparams(("arbitrary", "arbitrary")),
        name="inproj_odd",
    )(x, mods, mods, g, cos, sin, w)


def _inproj_odd_ctx(x, mods, row_of_batch, g, w_kv, qk, vw):
    bn, ln, d = x.shape
    tm = _token_tile(ln)
    tok = lambda w_: pl.BlockSpec((1, tm, w_), lambda b, i: (b, i, 0))
    return pl.pallas_call(
        functools.partial(_inproj_odd_ctx_kernel, qk=qk, vw=vw),
        grid=(bn, ln // tm),
        in_specs=[tok(d)] + _mod_specs(d, row_of_batch, (0, 1))
        + [pl.BlockSpec((1, d), lambda b, i: (0, 0)), _resident(w_kv.shape, lambda b, i: (0, 0))],
        out_specs=[tok(qk), tok(vw)],
        out_shape=[jax.ShapeDtypeStruct((bn, ln, qk), BF16), jax.ShapeDtypeStruct((bn, ln, vw), BF16)],
        compiler_params=_cparams(("arbitrary", "arbitrary")),
        name="inproj_odd_ctx",
    )(x, mods, mods, g, w_kv)


def _retention_kernel(lg_ref, q_ref, k_ref, v_ref, gate_ref, kc_ref, vc_ref, ng_ref, o_ref, acc_s, sf_s, sb_s,
                      *, chunk, n_ctx_rows, n_lat_rows):
    t_n = chunk
    h = pl.program_id(1)
    lgf = jnp.full((1, 1), lg_ref[0, h], F32)
    lgb = jnp.full((1, 1), lg_ref[1, h], F32)
    nc_ctx, nc_lat = n_ctx_rows // t_n, n_lat_rows // t_n
    ti = lax.broadcasted_iota(jnp.int32, (t_n, t_n), 0)
    si = lax.broadcasted_iota(jnp.int32, (t_n, t_n), 1)
    diff = (ti - si).astype(F32)
    decay = jnp.where(diff >= 0, jnp.exp(lgf * jnp.maximum(diff, 0.0)), jnp.exp(lgb * jnp.maximum(-diff, 0.0)))
    pos = lax.broadcasted_iota(jnp.int32, (t_n, 1), 0).astype(F32)
    inter_f = jnp.exp(lgf * (pos + 1.0))
    inter_b = jnp.exp(lgb * (t_n - pos))
    wend_f = jnp.exp(lgf * (t_n - 1.0 - pos))
    wend_b = jnp.exp(lgb * pos)
    cd_f = jnp.exp(lgf * t_n)
    cd_b = jnp.exp(lgb * t_n)

    def rows_of(c):
        return pl.ds(pl.multiple_of(c * t_n, t_n), t_n)

    def bump(s_ref, k, v, wend, cd):
        kw = (k.astype(F32) * wend).astype(BF16)
        s_ref[...] = cd * s_ref[...] + _dot_tn(kw, v)

    sf_s[...] = jnp.zeros_like(sf_s)
    sb_s[...] = jnp.zeros_like(sb_s)

    def ctx_f(c, _):
        bump(sf_s, kc_ref[0, rows_of(c), :], vc_ref[0, rows_of(c), :], wend_f, cd_f)
        return 0
    lax.fori_loop(0, nc_ctx, ctx_f, 0)

    def lat_f(c, _):
        rows = rows_of(c)
        q, k, v = q_ref[0, rows, :], k_ref[0, rows, :], v_ref[0, rows, :]
        scores = (_dot_nt(q, k) * decay).astype(BF16)
        acc_s[rows, :] = _dot(scores, v) + inter_f * _dot(q, sf_s[...].astype(BF16))
        bump(sf_s, k, v, wend_f, cd_f)
        return 0
    lax.fori_loop(0, nc_lat, lat_f, 0)

    def ctx_b(i, _):
        c = nc_ctx - 1 - i
        bump(sb_s, kc_ref[0, rows_of(c), :], vc_ref[0, rows_of(c), :], wend_b, cd_b)
        return 0
    lax.fori_loop(0, nc_ctx, ctx_b, 0)

    def lat_b(i, _):
        rows = rows_of(nc_lat - 1 - i)
        q, k, v = q_ref[0, rows, :], k_ref[0, rows, :], v_ref[0, rows, :]
        o = acc_s[rows, :] + inter_b * _dot(q, sb_s[...].astype(BF16))
        mu = jnp.mean(o, axis=1, keepdims=True)
        cen = o - mu
        var = jnp.mean(cen * cen, axis=1, keepdims=True)
        y = cen * lax.rsqrt(var + EPS) * ng_ref[...]
        o_ref[0, rows, :] = (_silu(gate_ref[0, rows, :].astype(F32)) * y).astype(BF16)
        bump(sb_s, k, v, wend_b, cd_b)
        return 0
    lax.fori_loop(0, nc_lat, lat_b, 0)


def _retention_mix(q, k, v, gate, k_ctx, v_ctx, log_gamma, norm_g):
    bn, ll, qk = q.shape
    lc = k_ctx.shape[1]
    vw = v.shape[-1]
    hk, hv = qk // RET_HEADS, vw // RET_HEADS
    chunk = 256 if (ll % 256 == 0 and lc % 256 == 0) else 128
    head = lambda n_rows, w: pl.BlockSpec((1, n_rows, w), lambda b, h: (b, 0, h))
    return pl.pallas_call(
        functools.partial(_retention_kernel, chunk=chunk, n_ctx_rows=lc, n_lat_rows=ll),
        grid=(bn, RET_HEADS),
        in_specs=[
            pl.BlockSpec(memory_space=pltpu.SMEM),
            head(ll, hk), head(ll, hk), head(ll, hv), head(ll, hv), head(lc, hk), head(lc, hv),
            pl.BlockSpec((1, hv), lambda b, h: (0, h)),
        ],
        out_specs=head(ll, hv),
        out_shape=jax.ShapeDtypeStruct((bn, ll, vw), BF16),
        scratch_shapes=[pltpu.VMEM((ll, hv), F32), pltpu.VMEM((hk, hv), F32), pltpu.VMEM((hk, hv), F32)],
        compiler_params=_cparams(("arbitrary", "arbitrary")),
        name="retention_mix",
    )(log_gamma.astype(F32), q, k, v, gate, k_ctx, v_ctx, norm_g.reshape(1, vw))


def _grid_rope(n_pos, hk):
    rows = n_pos // GRID_W
    row = jnp.repeat(jnp.arange(rows, dtype=F32), GRID_W)
    col = jnp.tile(jnp.arange(GRID_W, dtype=F32), rows)
    n_freq = hk // 4
    inv = ROPE_BASE ** (-jnp.arange(n_freq, dtype=F32) / n_freq)
    ang = jnp.concatenate([row[:, None] * inv, col[:, None] * inv], -1)
    return jnp.cos(ang), jnp.sin(ang)


def kernel(x, c, ctx, c_ctx, ada_w, ada_b, norm1_g, norm2_g, ab_w_in, ab_w_out, s5_a_re, s5_a_im, s5_log_dt,
           s5_b_re, s5_b_im, s5_c_re, s5_c_im, s5_d, s5_glu_w, s5_glu_b, ml_conv_w, ml_conv_b, ml_wq, ml_wk,
           ml_wv, ml_gate_b, ml_norm_g, ret_w_in, ret_w_out, ret_log_gamma, ret_norm_g, ffn_w1, ffn_w3, ffn_w2,
           final_g):
    bn, ln, d = x.shape
    depth = ada_w.shape[0]
    assert depth == 2, "one S5 || mLSTM layer followed by one retention layer"
    s5_width = s5_d.shape[-1]
    ml_width = ml_norm_g.shape[-1]
    assert ml_width == ML_HEADS * LANES and s5_width % S5_GROUP == 0

    r_pad = -(-(bn + 1) // SUBLANES) * SUBLANES
    vec = jnp.zeros((r_pad, d), F32).at[:bn].set(c).at[bn].set(c_ctx)
    mods_all = _modulation(vec, ada_w, ada_b).reshape(depth, r_pad * N_MOD, 1, d)
    lat_row = lambda b: b
    ctx_row = lambda b: bn

    mods = mods_all[0]
    g1 = norm1_g[0].reshape(1, d)
    g2 = norm2_g[0].reshape(1, d)
    w_in = ab_w_in[0]
    n_gate = w_in.shape[1] - s5_width - 2 * ml_width
    w_in = jnp.pad(w_in, ((0, 0), (0, LANES - n_gate))).astype(BF16)
    u_lat, p_lat = _inproj_even(x, mods, lat_row, g1, w_in, s5_width)
    u_ctx, p_ctx = _inproj_even(ctx, mods, ctx_row, g1, w_in, s5_width)
    mats = _s5_matrices(s5_a_re[0], s5_a_im[0], s5_log_dt[0], s5_b_re[0], s5_b_im[0], s5_c_re[0], s5_c_im[0],
                        s5_d[0])
    ys_ctx, ys_lat = _s5_mix(u_ctx, u_lat, mats)
    m_ctx, m_lat = _mlstm_mix(p_ctx, p_lat, ml_conv_w[0], ml_conv_b[0], ml_wq[0], ml_wk[0], ml_wv[0],
                              ml_gate_b[0], ml_norm_g[0])
    glu_w = s5_glu_w[0].astype(BF16)
    w_out = ab_w_out[0].astype(BF16)
    w1, w3, w2 = ffn_w1[0].astype(BF16), ffn_w3[0].astype(BF16), ffn_w2[0].astype(BF16)
    x = _outproj_even(x, ys_lat, m_lat, mods, lat_row, g2, glu_w, s5_glu_b[0], w_out, w1, w3, w2)
    ctx = _outproj_even(ctx, ys_ctx, m_ctx, mods, ctx_row, g2, glu_w, s5_glu_b[0], w_out, w1, w3, w2)

    mods = mods_all[1]
    g1 = norm1_g[1].reshape(1, d)
    g2 = norm2_g[1].reshape(1, d)
    vw = ret_norm_g.shape[-1]
    qk = (ret_w_in.shape[-1] - 2 * vw) // 2
    w_in = ret_w_in[0].astype(BF16)
    cos, sin = _grid_rope(ln, qk // RET_HEADS)
    q, k, v, gate = _inproj_odd(x, mods, lat_row, g1, cos, sin, w_in, qk, vw)
    k_ctx, v_ctx = _inproj_odd_ctx(ctx, mods, ctx_row, g1, w_in[:, qk:2 * qk + vw], qk, vw)
    a = _retention_mix(q, k, v, gate, k_ctx, v_ctx, ret_log_gamma[0], ret_norm_g[0])
    w_out = ret_w_out[0].astype(BF16)
    w1, w3, w2 = ffn_w1[1].astype(BF16), ffn_w3[1].astype(BF16), ffn_w2[1].astype(BF16)
    return _outproj_odd(x, a, mods, lat_row, g2, w_out, w1, w3, w2, final_g.reshape(1, d))
```

```python
import functools
import math

import jax
import jax.numpy as jnp
from jax import lax
from jax.experimental import pallas as pl
from jax.experimental.pallas import tpu as pltpu

F32 = jnp.float32
BF16 = jnp.bfloat16

EPS = 1e-6
N_MOD = 6
GRID_W = 64
ROPE_BASE = 10000.0

S5_GROUP = 16
S5_STATE = 64
S5_STEP = 16
ML_HEADS = 4
ML_CHUNK = 128
RET_HEADS = 4
SUBLANES = 8
LANES = 128
NEG_BIG = -1e30
VMEM_LIMIT = 56 * 1024 * 1024


def _cparams(sem):
    return pltpu.CompilerParams(dimension_semantics=sem, vmem_limit_bytes=VMEM_LIMIT)


def _resident(shape, index_map):
    return pl.BlockSpec(shape, index_map, pipeline_mode=pl.Buffered(1))


def _token_tile(n):
    for t in (512, 256, 128):
        if n % t == 0:
            return t
    raise ValueError(f"sequence length {n} must be a multiple of 128")


def _silu(v):
    return v * jax.nn.sigmoid(v)


def _norm_mod(x, g, shift, scale):
    y = x * lax.rsqrt(jnp.mean(x * x, axis=-1, keepdims=True) + EPS)
    return (y * g) * (1.0 + scale) + shift


def _dot(a, b):
    return jnp.dot(a, b, preferred_element_type=F32)


def _dot_nt(a, b):
    return lax.dot_general(a, b, (((1,), (1,)), ((), ())), preferred_element_type=F32)


def _dot_tn(a, b):
    return lax.dot_general(a, b, (((0,), (0,)), ((), ())), preferred_element_type=F32)


def _mod_kernel(v_ref, w_ref, b_ref, o_ref):
    s = _silu(v_ref[...]).astype(BF16)
    o_ref[0] = _dot(s, w_ref[0].astype(BF16)) + b_ref[0]


def _modulation(vec, ada_w, ada_b):
    depth, d, n = ada_w.shape
    r = vec.shape[0]
    tn = 1024
    return pl.pallas_call(
        _mod_kernel,
        grid=(depth, n // tn),
        in_specs=[
            pl.BlockSpec((r, d), lambda l, j: (0, 0)),
            pl.BlockSpec((1, d, tn), lambda l, j: (l, 0, j)),
            pl.BlockSpec((1, 1, tn), lambda l, j: (l, 0, j)),
        ],
        out_specs=pl.BlockSpec((1, r, tn), lambda l, j: (l, 0, j)),
        out_shape=jax.ShapeDtypeStruct((depth, r, n), F32),
        compiler_params=_cparams(("arbitrary", "arbitrary")),
        name="adaln_modulation",
    )(vec, ada_w, ada_b.reshape(depth, 1, n))


def _inproj_even_kernel(x_ref, sh_ref, sc_ref, g_ref, w_ref, u_ref, p_ref, *, s5_width):
    h = _norm_mod(x_ref[0], g_ref[...], sh_ref[0], sc_ref[0]).astype(BF16)
    p = _dot(h, w_ref[...])
    u_ref[0] = p[:, :s5_width]
    p_ref[0] = p[:, s5_width:]


def _inproj_even(x, mods, row_of_batch, g, w, s5_width):
    bn, ln, d = x.shape
    n = w.shape[1]
    tm = _token_tile(ln)
    return pl.pallas_call(
        functools.partial(_inproj_even_kernel, s5_width=s5_width),
        grid=(bn, ln // tm),
        in_specs=[
            pl.BlockSpec((1, tm, d), lambda b, i: (b, i, 0)),
            pl.BlockSpec((1, 1, d), lambda b, i: (row_of_batch(b) * N_MOD + 0, 0, 0)),
            pl.BlockSpec((1, 1, d), lambda b, i: (row_of_batch(b) * N_MOD + 1, 0, 0)),
            pl.BlockSpec((1, d), lambda b, i: (0, 0)),
            _resident((d, n), lambda b, i: (0, 0)),
        ],
        out_specs=[
            pl.BlockSpec((1, tm, s5_width), lambda b, i: (b, i, 0)),
            pl.BlockSpec((1, tm, n - s5_width), lambda b, i: (b, i, 0)),
        ],
        out_shape=[
            jax.ShapeDtypeStruct((bn, ln, s5_width), F32),
            jax.ShapeDtypeStruct((bn, ln, n - s5_width), F32),
        ],
        compiler_params=_cparams(("arbitrary", "arbitrary")),
        name="inproj_even",
    )(x, mods, mods, g, w)


def _s5_response_kernel(c_ref, w_ref, o_ref):
    for n in range(c_ref.shape[0]):
        o_ref[n] = lax.dot_general(c_ref[n], w_ref[n], (((1,), (1,)), ((), ())),
                                   precision=lax.Precision.HIGHEST, preferred_element_type=F32)


def _s5_response(c_cat, w_cat):
    n, co, kk = c_cat.shape
    r = w_cat.shape[1]
    nb = 8
    return pl.pallas_call(
        _s5_response_kernel,
        grid=(n // nb,),
        in_specs=[pl.BlockSpec((nb, co, kk), lambda i: (i, 0, 0)), pl.BlockSpec((nb, r, kk), lambda i: (i, 0, 0))],
        out_specs=pl.BlockSpec((nb, co, r), lambda i: (i, 0, 0)),
        out_shape=jax.ShapeDtypeStruct((n, co, r), F32),
        compiler_params=_cparams(("arbitrary",)),
        name="s5_response",
    )(c_cat, w_cat)


def _s5_matrices(a_re, a_im, log_dt, b_re, b_im, c_re, c_im, d_skip):
    k = S5_STEP
    a_re, a_im = a_re.astype(F32), a_im.astype(F32)
    dt = jnp.exp(log_dt.astype(F32))[..., None]
    lam_re, lam_im = a_re * dt, a_im * dt
    steps = jnp.arange(k + 1, dtype=F32)[:, None, None, None]
    mag = jnp.exp(steps * lam_re)
    pr, pi = mag * jnp.cos(steps * lam_im), mag * jnp.sin(steps * lam_im)
    den = a_re * a_re + a_im * a_im
    nr = pr[1] - 1.0
    coef_re = (nr * a_re + pi[1] * a_im) / den
    coef_im = (pi[1] * a_re - nr * a_im) / den
    bt_re = jnp.swapaxes(b_re.astype(F32), -1, -2)
    bt_im = jnp.swapaxes(b_im.astype(F32), -1, -2)
    bb_re = coef_re[:, :, None] * bt_re - coef_im[:, :, None] * bt_im
    bb_im = coef_re[:, :, None] * bt_im + coef_im[:, :, None] * bt_re
    wr = pr[:, :, :, None] * bb_re - pi[:, :, :, None] * bb_im
    wi = pr[:, :, :, None] * bb_im + pi[:, :, :, None] * bb_re
    c_re, c_im = c_re.astype(F32), c_im.astype(F32)
    g_n, c_n, p_n = a_re.shape[1], b_re.shape[-1], a_re.shape[-1]
    w_cat = jnp.transpose(jnp.concatenate([wr, -wi], -1), (1, 2, 0, 3, 4)).reshape(2 * g_n, (k + 1) * c_n, 2 * p_n)
    c_cat = jnp.concatenate([c_re, c_im], -1).reshape(2 * g_n, c_n, 2 * p_n)
    resp = _s5_response(c_cat, w_cat).reshape(2, g_n, c_n, k + 1, c_n)
    skip = (jnp.eye(c_n, dtype=F32) * d_skip.astype(F32).reshape(g_n, 1, c_n))[:, :, None, :]
    gen = jnp.concatenate([resp[0, :, :, k - 1:0:-1], resp[0, :, :, 0:1] + resp[1, :, :, 0:1] + skip,
                           resp[1, :, :, 1:k]], axis=2).reshape(g_n, c_n, (2 * k - 1) * c_n)
    toep = jnp.stack([gen[:, :, (k - 1 - t) * c_n:(2 * k - 1 - t) * c_n] for t in range(k)], axis=1)
    toep = toep.reshape(g_n, k * c_n, k * c_n)
    endw = jnp.concatenate([wr[k - 1::-1, 0], wr[:k, 1], wi[k - 1::-1, 0], wi[:k, 1]], -1)
    endw = jnp.transpose(endw, (1, 0, 2, 3)).reshape(g_n, k * c_n, 4 * p_n)

    def out_cols(d, powers_re, powers_im):
        cr, ci = c_re[d][None], c_im[d][None]
        pre, pim = powers_re[:, :, None, :], powers_im[:, :, None, :]
        return cr * pre - ci * pim, -(cr * pim + ci * pre)

    of_re, of_im = out_cols(0, pr[1:, 0], pi[1:, 0])
    ob_re, ob_im = out_cols(1, pr[k:0:-1, 1], pi[k:0:-1, 1])
    outw_t = jnp.concatenate([of_re, ob_re, of_im, ob_im], -1)
    outw_t = jnp.transpose(outw_t, (1, 0, 2, 3)).reshape(g_n, k * c_n, 4 * p_n)
    a0 = jnp.concatenate([pr[k, 0], pr[k, 1]], -1)
    a1 = jnp.concatenate([pi[k, 0], pi[k, 1]], -1)
    return toep.astype(BF16), endw.astype(BF16), outw_t.astype(BF16), a0, a1


S5_LANE_GROUPS = LANES // S5_GROUP


def _chunk_transpose(xs):
    n_arr = len(xs)
    lane_chunk = lax.broadcasted_iota(jnp.int32, xs[0].shape, 1) // S5_GROUP
    d = 1
    while d < n_arr:
        keep = (lane_chunk & d) == 0
        out = list(xs)
        for i in range(n_arr):
            if i & d:
                continue
            lo, hi = xs[i], xs[i + d]
            out[i] = jnp.where(keep, lo, pltpu.roll(hi, d * S5_GROUP, 1))
            out[i + d] = jnp.where(keep, pltpu.roll(lo, LANES - d * S5_GROUP, 1), hi)
        xs = out
        d *= 2
    return xs


def _s5_kernel(uc_ref, ul_ref, toep_ref, endw_ref, outw_ref, a0_ref, a1_ref, yc_ref, yl_ref,
               uc_s, ul_s, e_s, yc_s, yl_s, *, batch, n_ctx, n_lat):
    p = S5_STATE
    gl = S5_LANE_GROUPS
    k = S5_STEP

    def stack(u_ref, us_ref, n_blk):
        for b in range(batch):
            for half in range(k // gl):
                xs = [u_ref[b, pl.ds(half * gl + s, n_blk, stride=k), :] for s in range(gl)]
                ys = _chunk_transpose(xs)
                for g in range(gl):
                    us_ref[g, b * n_blk:(b + 1) * n_blk, half * LANES:(half + 1) * LANES] = ys[g].astype(BF16)

    stack(uc_ref, uc_s, n_ctx)
    stack(ul_ref, ul_s, n_lat)

    def increments(us_ref, n_blk, first_blk):
        for g in range(gl):
            e = _dot(us_ref[g], endw_ref[g])
            for b in range(batch):
                for part in range(2):
                    e_s[part, b, pl.ds(first_blk * gl + g, n_blk, stride=gl), :] = (
                        e[b * n_blk:(b + 1) * n_blk, part * LANES:(part + 1) * LANES])

    increments(uc_s, n_ctx, 0)
    increments(ul_s, n_lat, n_ctx)

    a0 = a0_ref[...]
    a1 = a1_ref[...]
    fwd_lane = lax.broadcasted_iota(jnp.int32, (batch, gl, 2 * p), 2) < p

    def scan(first_blk, n, carry):
        def body(i, c):
            s0, s1 = c
            rf = pl.ds(pl.multiple_of((first_blk + i) * gl, gl), gl)
            rb = pl.ds(pl.multiple_of((first_blk + n - 1 - i) * gl, gl), gl)
            e0 = jnp.where(fwd_lane, e_s[0, :, rf, :], e_s[0, :, rb, :])
            e1 = jnp.where(fwd_lane, e_s[1, :, rf, :], e_s[1, :, rb, :])
            e_s[0, :, rf, 0:p] = s0[:, :, 0:p]
            e_s[0, :, rb, p:2 * p] = s0[:, :, p:2 * p]
            e_s[1, :, rf, 0:p] = s1[:, :, 0:p]
            e_s[1, :, rb, p:2 * p] = s1[:, :, p:2 * p]
            return a0 * s0 - a1 * s1 + e0, a0 * s1 + a1 * s0 + e1
        return lax.fori_loop(0, n, body, carry)

    zero = jnp.zeros((batch, gl, 2 * p), F32)
    carry = scan(0, n_ctx, (zero, zero))
    scan(n_ctx, n_lat, carry)

    def outputs(us_ref, ys_ref, y_ref, n_blk, first_blk):
        for g in range(gl):
            state = jnp.concatenate(
                [jnp.concatenate([e_s[part, b, pl.ds(first_blk * gl + g, n_blk, stride=gl), :]
                                  for b in range(batch)], axis=0) for part in range(2)], axis=1)
            ys_ref[g] = _dot_nt(us_ref[g], toep_ref[g]) + _dot_nt(state.astype(BF16), outw_ref[g])
        for b in range(batch):
            for half in range(k // gl):
                ys = [ys_ref[g, b * n_blk:(b + 1) * n_blk, half * LANES:(half + 1) * LANES] for g in range(gl)]
                xs = _chunk_transpose(ys)
                for s in range(gl):
                    y_ref[b, pl.ds(half * gl + s, n_blk, stride=k), :] = xs[s]

    outputs(uc_s, yc_s, yc_ref, n_ctx, 0)
    outputs(ul_s, yl_s, yl_ref, n_lat, n_ctx)


def _s5_mix(u_ctx, u_lat, mats):
    toep, endw, outw, a0, a1 = mats
    bn, lc, w = u_ctx.shape
    ll = u_lat.shape[1]
    gl = S5_LANE_GROUPS
    kc = S5_STEP * S5_GROUP
    n_ctx, n_lat = lc // S5_STEP, ll // S5_STEP
    bb = 4 if bn % 4 == 0 else bn
    seq = lambda n_rows: pl.BlockSpec((bb, n_rows, LANES), lambda i, j: (j, 0, i))
    grp = lambda r, c: pl.BlockSpec((gl, r, c), lambda i, j: (i, 0, 0))
    return pl.pallas_call(
        functools.partial(_s5_kernel, batch=bb, n_ctx=n_ctx, n_lat=n_lat),
        grid=(w // LANES, bn // bb),
        in_specs=[seq(lc), seq(ll), grp(kc, kc), grp(kc, kc), grp(kc, kc),
                  pl.BlockSpec((gl, 2 * S5_STATE), lambda i, j: (i, 0)),
                  pl.BlockSpec((gl, 2 * S5_STATE), lambda i, j: (i, 0))],
        out_specs=[seq(lc), seq(ll)],
        out_shape=[jax.ShapeDtypeStruct((bn, lc, w), F32), jax.ShapeDtypeStruct((bn, ll, w), F32)],
        scratch_shapes=[
            pltpu.VMEM((gl, bb * n_ctx, kc), BF16), pltpu.VMEM((gl, bb * n_lat, kc), BF16),
            pltpu.VMEM((2, bb, (n_ctx + n_lat) * gl, LANES), F32),
            pltpu.VMEM((gl, bb * n_ctx, kc), F32), pltpu.VMEM((gl, bb * n_lat, kc), F32),
        ],
        compiler_params=_cparams(("arbitrary", "arbitrary")),
        name="s5_scan",
    )(u_ctx, u_lat, toep, endw, outw, a0, a1)


def _mlstm_kernel(pc_ref, pl_ref, cw_ref, cb_ref, wq_ref, wkt_ref, wv_ref, gb_ref, ng_ref, mc_ref, ml_ref,
                  q_s, kt_s, vx_s, h_s, st_s, bcol_s, mcol_s, rrow_s, *, n_ctx_rows, n_lat_rows):
    t_n = ML_CHUNK
    width = ML_HEADS * LANES
    nc_ctx, nc_lat = n_ctx_rows // t_n, n_lat_rows // t_n
    nc = nc_ctx + nc_lat
    scale_k = LANES ** -0.5

    def project(p_ref, n_rows, base):
        row = lax.broadcasted_iota(jnp.int32, (n_rows, LANES), 0)
        ones = jnp.ones((n_rows, LANES), BF16)
        for h in range(ML_HEADS):
            cols = slice(h * LANES, (h + 1) * LANES)
            xm = p_ref[0, :, cols]
            taps = cw_ref[:, cols]
            n_tap = taps.shape[0]
            acc = jnp.zeros_like(xm) + cb_ref[:, cols]
            for j in range(n_tap):
                d = j - n_tap // 2
                if d == 0:
                    sh = xm
                else:
                    sh = pltpu.roll(xm, (-d) % n_rows, 0)
                    sh = jnp.where((row + d >= 0) & (row + d < n_rows), sh, 0.0)
                acc = acc + sh * taps[j:j + 1, :]
            xc = _silu(acc).astype(BF16)
            q_s[base:base + n_rows, cols] = _dot(xc, wq_ref[h]).astype(BF16)
            kt = (_dot_nt(wkt_ref[h], xc) * scale_k).astype(BF16)
            for c in range(n_rows // t_n):
                kt_s[base // t_n + c, cols, :] = kt[:, c * t_n:(c + 1) * t_n]
            vx_s[base:base + n_rows, 2 * h * LANES:(2 * h + 1) * LANES] = (
                _dot(xm.astype(BF16), wv_ref[h]).astype(BF16))
            vx_s[base:base + n_rows, (2 * h + 1) * LANES:(2 * h + 2) * LANES] = ones

    project(pc_ref, n_ctx_rows, 0)
    project(pl_ref, n_lat_rows, n_ctx_rows)

    ti = lax.broadcasted_iota(jnp.int32, (t_n, t_n), 0)
    si = lax.broadcasted_iota(jnp.int32, (t_n, t_n), 1)
    tri_f = (si <= ti).astype(BF16)
    tri_b = (si >= ti).astype(BF16)
    lane = lax.broadcasted_iota(jnp.int32, (t_n, LANES), 1)
    trow = lax.broadcasted_iota(jnp.int32, (t_n, LANES), 0)
    gate_off = 2 * width

    def gate_prep(p_ref, n_chunks, base_chunk):
        def body(c, _):
            rows = pl.ds(pl.multiple_of(c * t_n, t_n), t_n)
            gcol = p_ref[0, rows, gate_off:gate_off + LANES] + gb_ref[...]
            lf = jax.nn.log_sigmoid(gcol)
            hi = lf.astype(BF16)
            lo = (lf - hi.astype(F32)).astype(BF16)
            pre = _dot(tri_f, hi) + _dot(tri_f, lo)
            suf = _dot(tri_b, hi) + _dot(tri_b, lo)
            bsum = jnp.where(lane < 2 * ML_HEADS, pre, suf)
            rcol = gcol - pltpu.roll(bsum, LANES - ML_HEADS, 1)
            pmax, smax = rcol, rcol
            step = 1
            while step < t_n:
                pmax = jnp.maximum(pmax, jnp.where(trow >= step, pltpu.roll(pmax, step, 0), NEG_BIG))
                smax = jnp.maximum(smax, jnp.where(trow < t_n - step, pltpu.roll(smax, t_n - step, 0), NEG_BIG))
                step *= 2
            bcol_s[base_chunk + c] = bsum
            mcol_s[base_chunk + c] = jnp.where(lane < 2 * ML_HEADS, pmax, smax)
            rrow_s[base_chunk + c] = rcol.T[0:4 * ML_HEADS, :]
            return 0
        lax.fori_loop(0, n_chunks, body, 0)

    gate_prep(pc_ref, nc_ctx, 0)
    gate_prep(pl_ref, nc_lat, nc_ctx)

    def one_dir(c, h, d, m_in):
        cols = slice(h * LANES, (h + 1) * LANES)
        xcols = slice(2 * h * LANES, (2 * h + 2) * LANES)
        li = 2 * d * ML_HEADS + h
        mask = (si <= ti) if d == 0 else (si >= ti)
        last = t_n - 1 if d == 0 else 0
        rows = pl.ds(pl.multiple_of(c * t_n, t_n), t_n)
        q = q_s[rows, cols]
        kt = kt_s[c, cols, :]
        vx = vx_s[rows, xcols]
        r_row = rrow_s[c][li:li + 1, :]
        run_max = jnp.broadcast_to(mcol_s[c][:, li:li + 1], (t_n, LANES))
        b_rep = jnp.broadcast_to(bcol_s[c][:, li + ML_HEADS:li + ML_HEADS + 1], (t_n, LANES))
        mm = jnp.maximum(m_in, run_max)
        dmat = jnp.exp(jnp.where(mask, r_row - mm, NEG_BIG))
        sm = (dmat * _dot(q, kt)).astype(BF16)
        intra = _dot(sm, vx)
        cross = _dot(q, st_s[2 * h + d].astype(BF16))
        inter = jnp.exp(m_in - mm)
        num = intra[:, :LANES] + inter * cross[:, :LANES]
        den = intra[:, LANES:] + inter * cross[:, LANES:]
        h_s[rows, cols] = h_s[rows, cols] + num / jnp.maximum(jnp.abs(den), jnp.exp(-b_rep - mm))
        r_top = run_max[last:last + 1, :]
        b_last = b_rep[last:last + 1, :]
        kw = (kt.astype(F32) * jnp.exp(r_row - r_top)).astype(BF16)
        inc = _dot(kw, vx)
        g_max = b_last + r_top
        m_new = jnp.maximum(b_last + m_in, g_max)
        dec = jnp.exp(b_last + m_in - m_new)
        inj = jnp.exp(g_max - m_new)
        st_s[2 * h + d] = (jnp.concatenate([dec, dec], axis=1) * st_s[2 * h + d]
                           + jnp.concatenate([inj, inj], axis=1) * inc)
        return m_new

    st_s[...] = jnp.zeros_like(st_s)
    h_s[...] = jnp.zeros_like(h_s)

    def body(i, carry):
        cb = jnp.where(i < nc_ctx, nc_ctx - 1 - i, nc + nc_ctx - 1 - i)
        return tuple(one_dir(i if d == 0 else cb, h, d, carry[2 * h + d])
                     for h in range(ML_HEADS) for d in range(2))

    zero = jnp.zeros((1, LANES), F32)
    lax.fori_loop(0, nc, body, (zero,) * (2 * ML_HEADS))

    def finish(p_ref, o_ref, n_chunks, base_chunk):
        def fbody(c, _):
            rows = pl.ds(pl.multiple_of(c * t_n, t_n), t_n)
            srows = pl.ds(pl.multiple_of((base_chunk + c) * t_n, t_n), t_n)
            for h in range(ML_HEADS):
                cols = slice(h * LANES, (h + 1) * LANES)
                hh = h_s[srows, cols]
                mu = jnp.mean(hh, axis=1, keepdims=True)
                cen = hh - mu
                var = jnp.mean(cen * cen, axis=1, keepdims=True)
                y = cen * lax.rsqrt(var + EPS) * ng_ref[:, cols]
                og = p_ref[0, rows, width + h * LANES:width + (h + 1) * LANES]
                o_ref[0, rows, cols] = (jax.nn.sigmoid(og) * y).astype(BF16)
            return 0
        lax.fori_loop(0, n_chunks, fbody, 0)

    finish(pc_ref, mc_ref, nc_ctx, 0)
    finish(pl_ref, ml_ref, nc_lat, nc_ctx)


def _mlstm_mix(p_ctx, p_lat, conv_w, conv_b, wq, wk, wv, gate_b, norm_g):
    bn, lc, pw = p_ctx.shape
    ll = p_lat.shape[1]
    width = ML_HEADS * LANES
    lt = lc + ll
    nct = lt // ML_CHUNK
    gb = jnp.zeros((1, LANES), F32).at[0, :4 * ML_HEADS].set(gate_b.astype(F32))
    full2 = lambda a: pl.BlockSpec(a.shape, lambda b: (0,) * a.ndim)
    conv_b2, norm_g2 = conv_b.reshape(1, width), norm_g.reshape(1, width)
    assert ML_CHUNK == LANES
    wq, wkt, wv = wq.astype(BF16), jnp.swapaxes(wk, 1, 2).astype(BF16), wv.astype(BF16)
    return pl.pallas_call(
        functools.partial(_mlstm_kernel, n_ctx_rows=lc, n_lat_rows=ll),
        grid=(bn,),
        in_specs=[
            pl.BlockSpec((1, lc, pw), lambda b: (b, 0, 0)),
            pl.BlockSpec((1, ll, pw), lambda b: (b, 0, 0)),
            full2(conv_w), full2(conv_b2), full2(wq), full2(wkt), full2(wv), full2(gb), full2(norm_g2),
        ],
        out_specs=[
            pl.BlockSpec((1, lc, width), lambda b: (b, 0, 0)),
            pl.BlockSpec((1, ll, width), lambda b: (b, 0, 0)),
        ],
        out_shape=[jax.ShapeDtypeStruct((bn, lc, width), BF16), jax.ShapeDtypeStruct((bn, ll, width), BF16)],
        scratch_shapes=[
            pltpu.VMEM((lt, width), BF16),
            pltpu.VMEM((nct, width, ML_CHUNK), BF16),
            pltpu.VMEM((lt, 2 * width), BF16),
            pltpu.VMEM((lt, width), F32),
            pltpu.VMEM((2 * ML_HEADS, LANES, 2 * LANES), F32),
            pltpu.VMEM((nct, ML_CHUNK, LANES), F32), pltpu.VMEM((nct, ML_CHUNK, LANES), F32),
            pltpu.VMEM((nct, 4 * ML_HEADS, ML_CHUNK), F32),
        ],
        compiler_params=_cparams(("arbitrary",)),
        name="mlstm_mix",
    )(p_ctx, p_lat, conv_w, conv_b2, wq, wkt, wv, gb, norm_g2)


def _ffn_tail(x, y, gate_mix, g2, shift, scale, gate_ffn, w1_ref, w3_ref, w2_ref, hidden_chunk):
    x1 = x + gate_mix * y
    h2 = _norm_mod(x1, g2, shift, scale).astype(BF16)
    hidden = w1_ref.shape[1]
    acc = jnp.zeros_like(x1)
    for j in range(hidden // hidden_chunk):
        cs = slice(j * hidden_chunk, (j + 1) * hidden_chunk)
        z = (_silu(_dot(h2, w1_ref[:, cs])) * _dot(h2, w3_ref[:, cs])).astype(BF16)
        acc = acc + _dot(z, w2_ref[cs, :])
    return x1 + gate_ffn * acc


def _final_norm(x, g):
    return x * lax.rsqrt(jnp.mean(x * x, axis=-1, keepdims=True) + EPS) * g


def _outproj_even_kernel(x_ref, ys_ref, m_ref, gm_ref, sh_ref, sc_ref, gf_ref, g2_ref, gw_ref, gbias_ref,
                         wos_ref, wom_ref, w1_ref, w3_ref, w2_ref, o_ref, *, hidden_chunk):
    ys = jax.nn.gelu(ys_ref[0])
    s = ys * jax.nn.sigmoid(_dot(ys.astype(BF16), gw_ref[...]) + gbias_ref[...])
    y = _dot(s.astype(BF16), wos_ref[...]) + _dot(m_ref[0], wom_ref[...])
    o_ref[0] = _ffn_tail(x_ref[0], y, gm_ref[0], g2_ref[...], sh_ref[0], sc_ref[0], gf_ref[0],
                         w1_ref, w3_ref, w2_ref, hidden_chunk)


def _outproj_odd_kernel(x_ref, a_ref, gm_ref, sh_ref, sc_ref, gf_ref, g2_ref, wo_ref, w1_ref, w3_ref, w2_ref,
                        fg_ref, o_ref, *, hidden_chunk):
    y = _dot(a_ref[0], wo_ref[...])
    out = _ffn_tail(x_ref[0], y, gm_ref[0], g2_ref[...], sh_ref[0], sc_ref[0], gf_ref[0],
                    w1_ref, w3_ref, w2_ref, hidden_chunk)
    o_ref[0] = _final_norm(out, fg_ref[...])


def _hidden_chunk(hidden):
    for c in (1408, 1024, 512, 256, 128):
        if hidden % c == 0:
            return c
    return hidden


def _mod_specs(d, row_of_batch, slots):
    return [pl.BlockSpec((1, 1, d), lambda b, i, s=s: (row_of_batch(b) * N_MOD + s, 0, 0)) for s in slots]


def _outproj_even(x, ys, m, mods, row_of_batch, g2, glu_w, glu_b, w_out, w1, w3, w2):
    bn, ln, d = x.shape
    sw = ys.shape[-1]
    tm = _token_tile(ln)
    res = lambda a: _resident(a.shape, lambda b, i: (0,) * a.ndim)
    tok = lambda w: pl.BlockSpec((1, tm, w), lambda b, i: (b, i, 0))
    wos, wom = w_out[:sw], w_out[sw:]
    glu_b2 = glu_b.reshape(1, sw)
    return pl.pallas_call(
        functools.partial(_outproj_even_kernel, hidden_chunk=_hidden_chunk(w1.shape[1])),
        grid=(bn, ln // tm),
        in_specs=[tok(d), tok(sw), tok(m.shape[-1])] + _mod_specs(d, row_of_batch, (2, 3, 4, 5))
        + [pl.BlockSpec((1, d), lambda b, i: (0, 0)), res(glu_w), res(glu_b2), res(wos), res(wom),
           res(w1), res(w3), res(w2)],
        out_specs=tok(d),
        out_shape=jax.ShapeDtypeStruct((bn, ln, d), F32),
        compiler_params=_cparams(("arbitrary", "arbitrary")),
        name="outproj_ffn_even",
    )(x, ys, m, mods, mods, mods, mods, g2, glu_w, glu_b2, wos, wom, w1, w3, w2)


def _outproj_odd(x, a, mods, row_of_batch, g2, w_out, w1, w3, w2, final_g):
    bn, ln, d = x.shape
    tm = _token_tile(ln)
    res = lambda a_: _resident(a_.shape, lambda b, i: (0,) * a_.ndim)
    tok = lambda w: pl.BlockSpec((1, tm, w), lambda b, i: (b, i, 0))
    return pl.pallas_call(
        functools.partial(_outproj_odd_kernel, hidden_chunk=_hidden_chunk(w1.shape[1])),
        grid=(bn, ln // tm),
        in_specs=[tok(d), tok(a.shape[-1])] + _mod_specs(d, row_of_batch, (2, 3, 4, 5))
        + [pl.BlockSpec((1, d), lambda b, i: (0, 0)), res(w_out), res(w1), res(w3), res(w2),
           pl.BlockSpec((1, d), lambda b, i: (0, 0))],
        out_specs=tok(d),
        out_shape=jax.ShapeDtypeStruct((bn, ln, d), F32),
        compiler_params=_cparams(("arbitrary", "arbitrary")),
        name="outproj_ffn_odd",
    )(x, a, mods, mods, mods, mods, g2, w_out, w1, w3, w2, final_g)


def _rope_heads(t, cos, sin, heads, hk):
    half = hk // 2
    out = []
    for h in range(heads):
        t1 = t[:, h * hk:h * hk + half]
        t2 = t[:, h * hk + half:(h + 1) * hk]
        out.append(t1 * cos - t2 * sin)
        out.append(t2 * cos + t1 * sin)
    return jnp.concatenate(out, axis=-1)


def _inproj_odd_kernel(x_ref, sh_ref, sc_ref, g_ref, cos_ref, sin_ref, w_ref, q_ref, k_ref, v_ref, gate_ref,
                       *, qk, vw, heads):
    h = _norm_mod(x_ref[0], g_ref[...], sh_ref[0], sc_ref[0]).astype(BF16)
    hk = qk // heads
    cos, sin = cos_ref[...], sin_ref[...]
    q = _rope_heads(_dot(h, w_ref[:, 0:qk]), cos, sin, heads, hk)
    q_ref[0] = (q * (hk ** -0.5)).astype(BF16)
    k = _rope_heads(_dot(h, w_ref[:, qk:2 * qk]), cos, sin, heads, hk)
    k_ref[0] = k.astype(BF16)
    v_ref[0] = _dot(h, w_ref[:, 2 * qk:2 * qk + vw]).astype(BF16)
    gate_ref[0] = _dot(h, w_ref[:, 2 * qk + vw:2 * qk + 2 * vw]).astype(BF16)


def _inproj_odd_ctx_kernel(x_ref, sh_ref, sc_ref, g_ref, w_ref, k_ref, v_ref, *, qk, vw):
    h = _norm_mod(x_ref[0], g_ref[...], sh_ref[0], sc_ref[0]).astype(BF16)
    k_ref[0] = _dot(h, w_ref[:, 0:qk]).astype(BF16)
    v_ref[0] = _dot(h, w_ref[:, qk:qk + vw]).astype(BF16)


def _inproj_odd(x, mods, row_of_batch, g, cos, sin, w, qk, vw):
    bn, ln, d = x.shape
    tm = _token_tile(ln)
    tok = lambda w_: pl.BlockSpec((1, tm, w_), lambda b, i: (b, i, 0))
    half = cos.shape[1]
    return pl.pallas_call(
        functools.partial(_inproj_odd_kernel, qk=qk, vw=vw, heads=RET_HEADS),
        grid=(bn, ln // tm),
        in_specs=[tok(d)] + _mod_specs(d, row_of_batch, (0, 1))
        + [pl.BlockSpec((1, d), lambda b, i: (0, 0)),
           pl.BlockSpec((tm, half), lambda b, i: (i, 0)), pl.BlockSpec((tm, half), lambda b, i: (i, 0)),
           _resident(w.shape, lambda b, i: (0, 0))],
        out_specs=[tok(qk), tok(qk), tok(vw), tok(vw)],
        out_shape=[jax.ShapeDtypeStruct((bn, ln, qk), BF16), jax.ShapeDtypeStruct((bn, ln, qk), BF16),
                   jax.ShapeDtypeStruct((bn, ln, vw), BF16), jax.ShapeDtypeStruct((bn, ln, vw), BF16)],
        compiler_params=_cparams(("arbitrary", "arbitrary")),
        name="inproj_odd",
    )(x, mods, mods, g, cos, sin, w)


def _inproj_odd_ctx(x, mods, row_of_batch, g, w_kv, qk, vw):
    bn, ln, d = x.shape
    tm = _token_tile(ln)
    tok = lambda w_: pl.BlockSpec((1, tm, w_), lambda b, i: (b, i, 0))
    return pl.pallas_call(
        functools.partial(_inproj_odd_ctx_kernel, qk=qk, vw=vw),
        grid=(bn, ln // tm),
        in_specs=[tok(d)] + _mod_specs(d, row_of_batch, (0, 1))
        + [pl.BlockSpec((1, d), lambda b, i: (0, 0)), _resident(w_kv.shape, lambda b, i: (0, 0))],
        out_specs=[tok(qk), tok(vw)],
        out_shape=[jax.ShapeDtypeStruct((bn, ln, qk), BF16), jax.ShapeDtypeStruct((bn, ln, vw), BF16)],
        compiler_params=_cparams(("arbitrary", "arbitrary")),
        name="inproj_odd_ctx",
    )(x, mods, mods, g, w_kv)


def _retention_kernel(lg_ref, q_ref, k_ref, v_ref, gate_ref, kc_ref, vc_ref, ng_ref, o_ref, acc_s, sf_s, sb_s,
                      *, chunk, n_ctx_rows, n_lat_rows):
    t_n = chunk
    h = pl.program_id(1)
    lgf = jnp.full((1, 1), lg_ref[0, h], F32)
    lgb = jnp.full((1, 1), lg_ref[1, h], F32)
    nc_ctx, nc_lat = n_ctx_rows // t_n, n_lat_rows // t_n
    ti = lax.broadcasted_iota(jnp.int32, (t_n, t_n), 0)
    si = lax.broadcasted_iota(jnp.int32, (t_n, t_n), 1)
    diff = (ti - si).astype(F32)
    decay = jnp.where(diff >= 0, jnp.exp(lgf * jnp.maximum(diff, 0.0)), jnp.exp(lgb * jnp.maximum(-diff, 0.0)))
    pos = lax.broadcasted_iota(jnp.int32, (t_n, 1), 0).astype(F32)
    inter_f = jnp.exp(lgf * (pos + 1.0))
    inter_b = jnp.exp(lgb * (t_n - pos))
    wend_f = jnp.exp(lgf * (t_n - 1.0 - pos))
    wend_b = jnp.exp(lgb * pos)
    cd_f = jnp.exp(lgf * t_n)
    cd_b = jnp.exp(lgb * t_n)

    def rows_of(c):
        return pl.ds(pl.multiple_of(c * t_n, t_n), t_n)

    def bump(s_ref, k, v, wend, cd):
        kw = (k.astype(F32) * wend).astype(BF16)
        s_ref[...] = cd * s_ref[...] + _dot_tn(kw, v)

    sf_s[...] = jnp.zeros_like(sf_s)
    sb_s[...] = jnp.zeros_like(sb_s)

    def ctx_f(c, _):
        bump(sf_s, kc_ref[0, rows_of(c), :], vc_ref[0, rows_of(c), :], wend_f, cd_f)
        return 0
    lax.fori_loop(0, nc_ctx, ctx_f, 0)

    def lat_f(c, _):
        rows = rows_of(c)
        q, k, v = q_ref[0, rows, :], k_ref[0, rows, :], v_ref[0, rows, :]
        scores = (_dot_nt(q, k) * decay).astype(BF16)
        acc_s[rows, :] = _dot(scores, v) + inter_f * _dot(q, sf_s[...].astype(BF16))
        bump(sf_s, k, v, wend_f, cd_f)
        return 0
    lax.fori_loop(0, nc_lat, lat_f, 0)

    def ctx_b(i, _):
        c = nc_ctx - 1 - i
        bump(sb_s, kc_ref[0, rows_of(c), :], vc_ref[0, rows_of(c), :], wend_b, cd_b)
        return 0
    lax.fori_loop(0, nc_ctx, ctx_b, 0)

    def lat_b(i, _):
        rows = rows_of(nc_lat - 1 - i)
        q, k, v = q_ref[0, rows, :], k_ref[0, rows, :], v_ref[0, rows, :]
        o = acc_s[rows, :] + inter_b * _dot(q, sb_s[...].astype(BF16))
        mu = jnp.mean(o, axis=1, keepdims=True)
        cen = o - mu
        var = jnp.mean(cen * cen, axis=1, keepdims=True)
        y = cen * lax.rsqrt(var + EPS) * ng_ref[...]
        o_ref[0, rows, :] = (_silu(gate_ref[0, rows, :].astype(F32)) * y).astype(BF16)
        bump(sb_s, k, v, wend_b, cd_b)
        return 0
    lax.fori_loop(0, nc_lat, lat_b, 0)


def _retention_mix(q, k, v, gate, k_ctx, v_ctx, log_gamma, norm_g):
    bn, ll, qk = q.shape
    lc = k_ctx.shape[1]
    vw = v.shape[-1]
    hk, hv = qk // RET_HEADS, vw // RET_HEADS
    chunk = 256 if (ll % 256 == 0 and lc % 256 == 0) else 128
    head = lambda n_rows, w: pl.BlockSpec((1, n_rows, w), lambda b, h: (b, 0, h))
    return pl.pallas_call(
        functools.partial(_retention_kernel, chunk=chunk, n_ctx_rows=lc, n_lat_rows=ll),
        grid=(bn, RET_HEADS),
        in_specs=[
            pl.BlockSpec(memory_space=pltpu.SMEM),
            head(ll, hk), head(ll, hk), head(ll, hv), head(ll, hv), head(lc, hk), head(lc, hv),
            pl.BlockSpec((1, hv), lambda b, h: (0, h)),
        ],
        out_specs=head(ll, hv),
        out_shape=jax.ShapeDtypeStruct((bn, ll, vw), BF16),
        scratch_shapes=[pltpu.VMEM((ll, hv), F32), pltpu.VMEM((hk, hv), F32), pltpu.VMEM((hk, hv), F32)],
        compiler_params=_cparams(("arbitrary", "arbitrary")),
        name="retention_mix",
    )(log_gamma.astype(F32), q, k, v, gate, k_ctx, v_ctx, norm_g.reshape(1, vw))


def _grid_rope(n_pos, hk):
    rows = n_pos // GRID_W
    row = jnp.repeat(jnp.arange(rows, dtype=F32), GRID_W)
    col = jnp.tile(jnp.arange(GRID_W, dtype=F32), rows)
    n_freq = hk // 4
    inv = ROPE_BASE ** (-jnp.arange(n_freq, dtype=F32) / n_freq)
    ang = jnp.concatenate([row[:, None] * inv, col[:, None] * inv], -1)
    return jnp.cos(ang), jnp.sin(ang)


def kernel(x, c, ctx, c_ctx, ada_w, ada_b, norm1_g, norm2_g, ab_w_in, ab_w_out, s5_a_re, s5_a_im, s5_log_dt,
           s5_b_re, s5_b_im, s5_c_re, s5_c_im, s5_d, s5_glu_w, s5_glu_b, ml_conv_w, ml_conv_b, ml_wq, ml_wk,
           ml_wv, ml_gate_b, ml_norm_g, ret_w_in, ret_w_out, ret_log_gamma, ret_norm_g, ffn_w1, ffn_w3, ffn_w2,
           final_g):
    bn, ln, d = x.shape
    depth = ada_w.shape[0]
    assert depth == 2, "one S5 || mLSTM layer followed by one retention layer"
    s5_width = s5_d.shape[-1]
    ml_width = ml_norm_g.shape[-1]
    assert ml_width == ML_HEADS * LANES and s5_width % S5_GROUP == 0

    r_pad = -(-(bn + 1) // SUBLANES) * SUBLANES
    vec = jnp.zeros((r_pad, d), F32).at[:bn].set(c).at[bn].set(c_ctx)
    mods_all = _modulation(vec, ada_w, ada_b).reshape(depth, r_pad * N_MOD, 1, d)
    lat_row = lambda b: b
    ctx_row = lambda b: bn

    mods = mods_all[0]
    g1 = norm1_g[0].reshape(1, d)
    g2 = norm2_g[0].reshape(1, d)
    w_in = ab_w_in[0]
    n_gate = w_in.shape[1] - s5_width - 2 * ml_width
    w_in = jnp.pad(w_in, ((0, 0), (0, LANES - n_gate))).astype(BF16)
    u_lat, p_lat = _inproj_even(x, mods, lat_row, g1, w_in, s5_width)
    u_ctx, p_ctx = _inproj_even(ctx, mods, ctx_row, g1, w_in, s5_width)
    mats = _s5_matrices(s5_a_re[0], s5_a_im[0], s5_log_dt[0], s5_b_re[0], s5_b_im[0], s5_c_re[0], s5_c_im[0],
                        s5_d[0])
    ys_ctx, ys_lat = _s5_mix(u_ctx, u_lat, mats)
    m_ctx, m_lat = _mlstm_mix(p_ctx, p_lat, ml_conv_w[0], ml_conv_b[0], ml_wq[0], ml_wk[0], ml_wv[0],
                              ml_gate_b[0], ml_norm_g[0])
    glu_w = s5_glu_w[0].astype(BF16)
    w_out = ab_w_out[0].astype(BF16)
    w1, w3, w2 = ffn_w1[0].astype(BF16), ffn_w3[0].astype(BF16), ffn_w2[0].astype(BF16)
    x = _outproj_even(x, ys_lat, m_lat, mods, lat_row, g2, glu_w, s5_glu_b[0], w_out, w1, w3, w2)
    ctx = _outproj_even(ctx, ys_ctx, m_ctx, mods, ctx_row, g2, glu_w, s5_glu_b[0], w_out, w1, w3, w2)

    mods = mods_all[1]
    g1 = norm1_g[1].reshape(1, d)
    g2 = norm2_g[1].reshape(1, d)
    vw = ret_norm_g.shape[-1]
    qk = (ret_w_in.shape[-1] - 2 * vw) // 2
    w_in = ret_w_in[0].astype(BF16)
    cos, sin = _grid_rope(ln, qk // RET_HEADS)
    q, k, v, gate = _inproj_odd(x, mods, lat_row, g1, cos, sin, w_in, qk, vw)
    k_ctx, v_ctx = _inproj_odd_ctx(ctx, mods, ctx_row, g1, w_in[:, qk:2 * qk + vw], qk, vw)
    a = _retention_mix(q, k, v, gate, k_ctx, v_ctx, ret_log_gamma[0], ret_norm_g[0])
    w_out = ret_w_out[0].astype(BF16)
    w1, w3, w2 = ffn_w1[1].astype(BF16), ffn_w3[1].astype(BF16), ffn_w2[1].astype(BF16)
    return _outproj_odd(x, a, mods, lat_row, g2, w_out, w1, w3, w2, final_g.reshape(1, d))
```

```python
import functools
import math

import jax
import jax.numpy as jnp
from jax import lax
from jax.experimental import pallas as pl
from jax.experimental.pallas import tpu as pltpu

F32 = jnp.float32
BF16 = jnp.bfloat16

EPS = 1e-6
N_MOD = 6
GRID_W = 64
ROPE_BASE = 10000.0

S5_GROUP = 16
S5_STATE = 64
S5_STEP = 16
ML_HEADS = 4
ML_CHUNK = 128
RET_HEADS = 4
SUBLANES = 8
LANES = 128
NEG_BIG = -1e30
VMEM_LIMIT = 56 * 1024 * 1024


def _cparams(sem):
    return pltpu.CompilerParams(dimension_semantics=sem, vmem_limit_bytes=VMEM_LIMIT)


def _resident(shape, index_map):
    return pl.BlockSpec(shape, index_map, pipeline_mode=pl.Buffered(1))


def _token_tile(n):
    for t in (512, 256, 128):
        if n % t == 0:
            return t
    raise ValueError(f"sequence length {n} must be a multiple of 128")


def _silu(v):
    return v * jax.nn.sigmoid(v)


def _norm_mod(x, g, shift, scale):
    y = x * lax.rsqrt(jnp.mean(x * x, axis=-1, keepdims=True) + EPS)
    return (y * g) * (1.0 + scale) + shift


def _dot(a, b):
    return jnp.dot(a, b, preferred_element_type=F32)


def _dot_nt(a, b):
    return lax.dot_general(a, b, (((1,), (1,)), ((), ())), preferred_element_type=F32)


def _dot_tn(a, b):
    return lax.dot_general(a, b, (((0,), (0,)), ((), ())), preferred_element_type=F32)


def _mod_kernel(v_ref, w_ref, b_ref, o_ref):
    s = _silu(v_ref[...]).astype(BF16)
    o_ref[0] = _dot(s, w_ref[0].astype(BF16)) + b_ref[0]


def _modulation(vec, ada_w, ada_b):
    depth, d, n = ada_w.shape
    r = vec.shape[0]
    tn = 1024
    return pl.pallas_call(
        _mod_kernel,
        grid=(depth, n // tn),
        in_specs=[
            pl.BlockSpec((r, d), lambda l, j: (0, 0)),
            pl.BlockSpec((1, d, tn), lambda l, j: (l, 0, j)),
            pl.BlockSpec((1, 1, tn), lambda l, j: (l, 0, j)),
        ],
        out_specs=pl.BlockSpec((1, r, tn), lambda l, j: (l, 0, j)),
        out_shape=jax.ShapeDtypeStruct((depth, r, n), F32),
        compiler_params=_cparams(("arbitrary", "arbitrary")),
        name="adaln_modulation",
    )(vec, ada_w, ada_b.reshape(depth, 1, n))


def _inproj_even_kernel(x_ref, sh_ref, sc_ref, g_ref, w_ref, u_ref, p_ref, *, s5_width):
    h = _norm_mod(x_ref[0], g_ref[...], sh_ref[0], sc_ref[0]).astype(BF16)
    p = _dot(h, w_ref[...])
    u_ref[0] = p[:, :s5_width]
    p_ref[0] = p[:, s5_width:]


def _inproj_even(x, mods, row_of_batch, g, w, s5_width):
    bn, ln, d = x.shape
    n = w.shape[1]
    tm = _token_tile(ln)
    return pl.pallas_call(
        functools.partial(_inproj_even_kernel, s5_width=s5_width),
        grid=(bn, ln // tm),
        in_specs=[
            pl.BlockSpec((1, tm, d), lambda b, i: (b, i, 0)),
            pl.BlockSpec((1, 1, d), lambda b, i: (row_of_batch(b) * N_MOD + 0, 0, 0)),
            pl.BlockSpec((1, 1, d), lambda b, i: (row_of_batch(b) * N_MOD + 1, 0, 0)),
            pl.BlockSpec((1, d), lambda b, i: (0, 0)),
            _resident((d, n), lambda b, i: (0, 0)),
        ],
        out_specs=[
            pl.BlockSpec((1, tm, s5_width), lambda b, i: (b, i, 0)),
            pl.BlockSpec((1, tm, n - s5_width), lambda b, i: (b, i, 0)),
        ],
        out_shape=[
            jax.ShapeDtypeStruct((bn, ln, s5_width), F32),
            jax.ShapeDtypeStruct((bn, ln, n - s5_width), F32),
        ],
        compiler_params=_cparams(("arbitrary", "arbitrary")),
        name="inproj_even",
    )(x, mods, mods, g, w)


def _s5_response_kernel(c_ref, w_ref, o_ref):
    for n in range(c_ref.shape[0]):
        o_ref[n] = lax.dot_general(c_ref[n], w_ref[n], (((1,), (1,)), ((), ())),
                                   precision=lax.Precision.HIGHEST, preferred_element_type=F32)


def _s5_response(c_cat, w_cat):
    n, co, kk = c_cat.shape
    r = w_cat.shape[1]
    nb = 8
    return pl.pallas_call(
        _s5_response_kernel,
        grid=(n // nb,),
        in_specs=[pl.BlockSpec((nb, co, kk), lambda i: (i, 0, 0)), pl.BlockSpec((nb, r, kk), lambda i: (i, 0, 0))],
        out_specs=pl.BlockSpec((nb, co, r), lambda i: (i, 0, 0)),
        out_shape=jax.ShapeDtypeStruct((n, co, r), F32),
        compiler_params=_cparams(("arbitrary",)),
        name="s5_response",
    )(c_cat, w_cat)


def _s5_matrices(a_re, a_im, log_dt, b_re, b_im, c_re, c_im, d_skip):
    k = S5_STEP
    a_re, a_im = a_re.astype(F32), a_im.astype(F32)
    dt = jnp.exp(log_dt.astype(F32))[..., None]
    lam_re, lam_im = a_re * dt, a_im * dt
    steps = jnp.arange(k + 1, dtype=F32)[:, None, None, None]
    mag = jnp.exp(steps * lam_re)
    pr, pi = mag * jnp.cos(steps * lam_im), mag * jnp.sin(steps * lam_im)
    den = a_re * a_re + a_im * a_im
    nr = pr[1] - 1.0
    coef_re = (nr * a_re + pi[1] * a_im) / den
    coef_im = (pi[1] * a_re - nr * a_im) / den
    bt_re = jnp.swapaxes(b_re.astype(F32), -1, -2)
    bt_im = jnp.swapaxes(b_im.astype(F32), -1, -2)
    bb_re = coef_re[:, :, None] * bt_re - coef_im[:, :, None] * bt_im
    bb_im = coef_re[:, :, None] * bt_im + coef_im[:, :, None] * bt_re
    wr = pr[:, :, :, None] * bb_re - pi[:, :, :, None] * bb_im
    wi = pr[:, :, :, None] * bb_im + pi[:, :, :, None] * bb_re
    c_re, c_im = c_re.astype(F32), c_im.astype(F32)
    g_n, c_n, p_n = a_re.shape[1], b_re.shape[-1], a_re.shape[-1]
    w_cat = jnp.transpose(jnp.concatenate([wr, -wi], -1), (1, 2, 0, 3, 4)).reshape(2 * g_n, (k + 1) * c_n, 2 * p_n)
    c_cat = jnp.concatenate([c_re, c_im], -1).reshape(2 * g_n, c_n, 2 * p_n)
    resp = _s5_response(c_cat, w_cat).reshape(2, g_n, c_n, k + 1, c_n)
    skip = (jnp.eye(c_n, dtype=F32) * d_skip.astype(F32).reshape(g_n, 1, c_n))[:, :, None, :]
    gen = jnp.concatenate([resp[0, :, :, k - 1:0:-1], resp[0, :, :, 0:1] + resp[1, :, :, 0:1] + skip,
                           resp[1, :, :, 1:k]], axis=2).reshape(g_n, c_n, (2 * k - 1) * c_n)
    toep = jnp.stack([gen[:, :, (k - 1 - t) * c_n:(2 * k - 1 - t) * c_n] for t in range(k)], axis=1)
    toep = toep.reshape(g_n, k * c_n, k * c_n)
    endw = jnp.concatenate([wr[k - 1::-1, 0], wr[:k, 1], wi[k - 1::-1, 0], wi[:k, 1]], -1)
    endw = jnp.transpose(endw, (1, 0, 2, 3)).reshape(g_n, k * c_n, 4 * p_n)

    def out_cols(d, powers_re, powers_im):
        cr, ci = c_re[d][None], c_im[d][None]
        pre, pim = powers_re[:, :, None, :], powers_im[:, :, None, :]
        return cr * pre - ci * pim, -(cr * pim + ci * pre)

    of_re, of_im = out_cols(0, pr[1:, 0], pi[1:, 0])
    ob_re, ob_im = out_cols(1, pr[k:0:-1, 1], pi[k:0:-1, 1])
    outw_t = jnp.concatenate([of_re, ob_re, of_im, ob_im], -1)
    outw_t = jnp.transpose(outw_t, (1, 0, 2, 3)).reshape(g_n, k * c_n, 4 * p_n)
    a0 = jnp.concatenate([pr[k, 0], pr[k, 1]], -1)
    a1 = jnp.concatenate([pi[k, 0], pi[k, 1]], -1)
    return toep.astype(BF16), endw.astype(BF16), outw_t.astype(BF16), a0, a1


S5_LANE_GROUPS = LANES // S5_GROUP


def _chunk_transpose(xs):
    n_arr = len(xs)
    lane_chunk = lax.broadcasted_iota(jnp.int32, xs[0].shape, 1) // S5_GROUP
    d = 1
    while d < n_arr:
        keep = (lane_chunk & d) == 0
        out = list(xs)
        for i in range(n_arr):
            if i & d:
                continue
            lo, hi = xs[i], xs[i + d]
            out[i] = jnp.where(keep, lo, pltpu.roll(hi, d * S5_GROUP, 1))
            out[i + d] = jnp.where(keep, pltpu.roll(lo, LANES - d * S5_GROUP, 1), hi)
        xs = out
        d *= 2
    return xs


def _s5_kernel(uc_ref, ul_ref, toep_ref, endw_ref, outw_ref, a0_ref, a1_ref, yc_ref, yl_ref,
               uc_s, ul_s, e_s, yc_s, yl_s, *, batch, n_ctx, n_lat):
    p = S5_STATE
    gl = S5_LANE_GROUPS
    k = S5_STEP

    def stack(u_ref, us_ref, n_blk):
        for b in range(batch):
            for half in range(k // gl):
                xs = [u_ref[b, pl.ds(half * gl + s, n_blk, stride=k), :] for s in range(gl)]
                ys = _chunk_transpose(xs)
                for g in range(gl):
                    us_ref[g, b * n_blk:(b + 1) * n_blk, half * LANES:(half + 1) * LANES] = ys[g].astype(BF16)

    stack(uc_ref, uc_s, n_ctx)
    stack(ul_ref, ul_s, n_lat)

    def increments(us_ref, n_blk, first_blk):
        for g in range(gl):
            e = _dot(us_ref[g], endw_ref[g])
            for b in range(batch):
                for part in range(2):
                    e_s[part, b, pl.ds(first_blk * gl + g, n_blk, stride=gl), :] = (
                        e[b * n_blk:(b + 1) * n_blk, part * LANES:(part + 1) * LANES])

    increments(uc_s, n_ctx, 0)
    increments(ul_s, n_lat, n_ctx)

    a0 = a0_ref[...]
    a1 = a1_ref[...]
    fwd_lane = lax.broadcasted_iota(jnp.int32, (batch, gl, 2 * p), 2) < p

    def scan(first_blk, n, carry):
        def body(i, c):
            s0, s1 = c
            rf = pl.ds(pl.multiple_of((first_blk + i) * gl, gl), gl)
            rb = pl.ds(pl.multiple_of((first_blk + n - 1 - i) * gl, gl), gl)
            e0 = jnp.where(fwd_lane, e_s[0, :, rf, :], e_s[0, :, rb, :])
            e1 = jnp.where(fwd_lane, e_s[1, :, rf, :], e_s[1, :, rb, :])
            e_s[0, :, rf, 0:p] = s0[:, :, 0:p]
            e_s[0, :, rb, p:2 * p] = s0[:, :, p:2 * p]
            e_s[1, :, rf, 0:p] = s1[:, :, 0:p]
            e_s[1, :, rb, p:2 * p] = s1[:, :, p:2 * p]
            return a0 * s0 - a1 * s1 + e0, a0 * s1 + a1 * s0 + e1
        return lax.fori_loop(0, n, body, carry)

    zero = jnp.zeros((batch, gl, 2 * p), F32)
    carry = scan(0, n_ctx, (zero, zero))
    scan(n_ctx, n_lat, carry)

    def outputs(us_ref, ys_ref, y_ref, n_blk, first_blk):
        for g in range(gl):
            state = jnp.concatenate(
                [jnp.concatenate([e_s[part, b, pl.ds(first_blk * gl + g, n_blk, stride=gl), :]
                                  for b in range(batch)], axis=0) for part in range(2)], axis=1)
            ys_ref[g] = _dot_nt(us_ref[g], toep_ref[g]) + _dot_nt(state.astype(BF16), outw_ref[g])
        for b in range(batch):
            for half in range(k // gl):
                ys = [ys_ref[g, b * n_blk:(b + 1) * n_blk, half * LANES:(half + 1) * LANES] for g in range(gl)]
                xs = _chunk_transpose(ys)
                for s in range(gl):
                    y_ref[b, pl.ds(half * gl + s, n_blk, stride=k), :] = xs[s]

    outputs(uc_s, yc_s, yc_ref, n_ctx, 0)
    outputs(ul_s, yl_s, yl_ref, n_lat, n_ctx)


def _s5_mix(u_ctx, u_lat, mats):
    toep, endw, outw, a0, a1 = mats
    bn, lc, w = u_ctx.shape
    ll = u_lat.shape[1]
    gl = S5_LANE_GROUPS
    kc = S5_STEP * S5_GROUP
    n_ctx, n_lat = lc // S5_STEP, ll // S5_STEP
    bb = 4 if bn % 4 == 0 else bn
    seq = lambda n_rows: pl.BlockSpec((bb, n_rows, LANES), lambda i, j: (j, 0, i))
    grp = lambda r, c: pl.BlockSpec((gl, r, c), lambda i, j: (i, 0, 0))
    return pl.pallas_call(
        functools.partial(_s5_kernel, batch=bb, n_ctx=n_ctx, n_lat=n_lat),
        grid=(w // LANES, bn // bb),
        in_specs=[seq(lc), seq(ll), grp(kc, kc), grp(kc, kc), grp(kc, kc),
                  pl.BlockSpec((gl, 2 * S5_STATE), lambda i, j: (i, 0)),
                  pl.BlockSpec((gl, 2 * S5_STATE), lambda i, j: (i, 0))],
        out_specs=[seq(lc), seq(ll)],
        out_shape=[jax.ShapeDtypeStruct((bn, lc, w), F32), jax.ShapeDtypeStruct((bn, ll, w), F32)],
        scratch_shapes=[
            pltpu.VMEM((gl, bb * n_ctx, kc), BF16), pltpu.VMEM((gl, bb * n_lat, kc), BF16),
            pltpu.VMEM((2, bb, (n_ctx + n_lat) * gl, LANES), F32),
            pltpu.VMEM((gl, bb * n_ctx, kc), F32), pltpu.VMEM((gl, bb * n_lat, kc), F32),
        ],
        compiler_params=_cparams(("arbitrary", "arbitrary")),
        name="s5_scan",
    )(u_ctx, u_lat, toep, endw, outw, a0, a1)


def _mlstm_kernel(pc_ref, pl_ref, cw_ref, cb_ref, wq_ref, wkt_ref, wv_ref, gb_ref, ng_ref, mc_ref, ml_ref,
                  q_s, kt_s, vx_s, h_s, st_s, bcol_s, mcol_s, rrow_s, *, n_ctx_rows, n_lat_rows):
    t_n = ML_CHUNK
    width = ML_HEADS * LANES
    nc_ctx, nc_lat = n_ctx_rows // t_n, n_lat_rows // t_n
    nc = nc_ctx + nc_lat
    scale_k = LANES ** -0.5

    def project(p_ref, n_rows, base):
        row = lax.broadcasted_iota(jnp.int32, (n_rows, LANES), 0)
        ones = jnp.ones((n_rows, LANES), BF16)
        for h in range(ML_HEADS):
            cols = slice(h * LANES, (h + 1) * LANES)
            xm = p_ref[0, :, cols]
            taps = cw_ref[:, cols]
            n_tap = taps.shape[0]
            acc = jnp.zeros_like(xm) + cb_ref[:, cols]
            for j in range(n_tap):
                d = j - n_tap // 2
                if d == 0:
                    sh = xm
                else:
                    sh = pltpu.roll(xm, (-d) % n_rows, 0)
                    sh = jnp.where((row + d >= 0) & (row + d < n_rows), sh, 0.0)
                acc = acc + sh * taps[j:j + 1, :]
            xc = _silu(acc).astype(BF16)
            q_s[base:base + n_rows, cols] = _dot(xc, wq_ref[h]).astype(BF16)
            kt = (_dot_nt(wkt_ref[h], xc) * scale_k).astype(BF16)
            for c in range(n_rows // t_n):
                kt_s[base // t_n + c, cols, :] = kt[:, c * t_n:(c + 1) * t_n]
            vx_s[base:base + n_rows, 2 * h * LANES:(2 * h + 1) * LANES] = (
                _dot(xm.astype(BF16), wv_ref[h]).astype(BF16))
            vx_s[base:base + n_rows, (2 * h + 1) * LANES:(2 * h + 2) * LANES] = ones

    project(pc_ref, n_ctx_rows, 0)
    project(pl_ref, n_lat_rows, n_ctx_rows)

    ti = lax.broadcasted_iota(jnp.int32, (t_n, t_n), 0)
    si = lax.broadcasted_iota(jnp.int32, (t_n, t_n), 1)
    tri_f = (si <= ti).astype(BF16)
    tri_b = (si >= ti).astype(BF16)
    lane = lax.broadcasted_iota(jnp.int32, (t_n, LANES), 1)
    trow = lax.broadcasted_iota(jnp.int32, (t_n, LANES), 0)
    gate_off = 2 * width

    def gate_prep(p_ref, n_chunks, base_chunk):
        def body(c, _):
            rows = pl.ds(pl.multiple_of(c * t_n, t_n), t_n)
            gcol = p_ref[0, rows, gate_off:gate_off + LANES] + gb_ref[...]
            lf = jax.nn.log_sigmoid(gcol)
            hi = lf.astype(BF16)
            lo = (lf - hi.astype(F32)).astype(BF16)
            pre = _dot(tri_f, hi) + _dot(tri_f, lo)
            suf = _dot(tri_b, hi) + _dot(tri_b, lo)
            bsum = jnp.where(lane < 2 * ML_HEADS, pre, suf)
            rcol = gcol - pltpu.roll(bsum, LANES - ML_HEADS, 1)
            pmax, smax = rcol, rcol
            step = 1
            while step < t_n:
                pmax = jnp.maximum(pmax, jnp.where(trow >= step, pltpu.roll(pmax, step, 0), NEG_BIG))
                smax = jnp.maximum(smax, jnp.where(trow < t_n - step, pltpu.roll(smax, t_n - step, 0), NEG_BIG))
                step *= 2
            bcol_s[base_chunk + c] = bsum
            mcol_s[base_chunk + c] = jnp.where(lane < 2 * ML_HEADS, pmax, smax)
            rrow_s[base_chunk + c] = rcol.T[0:4 * ML_HEADS, :]
            return 0
        lax.fori_loop(0, n_chunks, body, 0)

    gate_prep(pc_ref, nc_ctx, 0)
    gate_prep(pl_ref, nc_lat, nc_ctx)

    def one_dir(c, h, d, m_in):
        cols = slice(h * LANES, (h + 1) * LANES)
        xcols = slice(2 * h * LANES, (2 * h + 2) * LANES)
        li = 2 * d * ML_HEADS + h
        mask = (si <= ti) if d == 0 else (si >= ti)
        last = t_n - 1 if d == 0 else 0
        rows = pl.ds(pl.multiple_of(c * t_n, t_n), t_n)
        q = q_s[rows, cols]
        kt = kt_s[c, cols, :]
        vx = vx_s[rows, xcols]
        r_row = rrow_s[c][li:li + 1, :]
        run_max = jnp.broadcast_to(mcol_s[c][:, li:li + 1], (t_n, LANES))
        b_rep = jnp.broadcast_to(bcol_s[c][:, li + ML_HEADS:li + ML_HEADS + 1], (t_n, LANES))
        mm = jnp.maximum(m_in, run_max)
        dmat = jnp.exp(jnp.where(mask, r_row - mm, NEG_BIG))
        sm = (dmat * _dot(q, kt)).astype(BF16)
        intra = _dot(sm, vx)
        cross = _dot(q, st_s[2 * h + d].astype(BF16))
        inter = jnp.exp(m_in - mm)
        num = intra[:, :LANES] + inter * cross[:, :LANES]
        den = intra[:, LANES:] + inter * cross[:, LANES:]
        h_s[rows, cols] = h_s[rows, cols] + num / jnp.maximum(jnp.abs(den), jnp.exp(-b_rep - mm))
        r_top = run_max[last:last + 1, :]
        b_last = b_rep[last:last + 1, :]
        kw = (kt.astype(F32) * jnp.exp(r_row - r_top)).astype(BF16)
        inc = _dot(kw, vx)
        g_max = b_last + r_top
        m_new = jnp.maximum(b_last + m_in, g_max)
        dec = jnp.exp(b_last + m_in - m_new)
        inj = jnp.exp(g_max - m_new)
        st_s[2 * h + d] = (jnp.concatenate([dec, dec], axis=1) * st_s[2 * h + d]
                           + jnp.concatenate([inj, inj], axis=1) * inc)
        return m_new

    st_s[...] = jnp.zeros_like(st_s)
    h_s[...] = jnp.zeros_like(h_s)

    def body(i, carry):
        cb = jnp.where(i < nc_ctx, nc_ctx - 1 - i, nc + nc_ctx - 1 - i)
        return tuple(one_dir(i if d == 0 else cb, h, d, carry[2 * h + d])
                     for h in range(ML_HEADS) for d in range(2))

    zero = jnp.zeros((1, LANES), F32)
    lax.fori_loop(0, nc, body, (zero,) * (2 * ML_HEADS))

    def finish(p_ref, o_ref, n_chunks, base_chunk):
        def fbody(c, _):
            rows = pl.ds(pl.multiple_of(c * t_n, t_n), t_n)
            srows = pl.ds(pl.multiple_of((base_chunk + c) * t_n, t_n), t_n)
            for h in range(ML_HEADS):
                cols = slice(h * LANES, (h + 1) * LANES)
                hh = h_s[srows, cols]
                mu = jnp.mean(hh, axis=1, keepdims=True)
                cen = hh - mu
                var = jnp.mean(cen * cen, axis=1, keepdims=True)
                y = cen * lax.rsqrt(var + EPS) * ng_ref[:, cols]
                og = p_ref[0, rows, width + h * LANES:width + (h + 1) * LANES]
                o_ref[0, rows, cols] = (jax.nn.sigmoid(og) * y).astype(BF16)
            return 0
        lax.fori_loop(0, n_chunks, fbody, 0)

    finish(pc_ref, mc_ref, nc_ctx, 0)
    finish(pl_ref, ml_ref, nc_lat, nc_ctx)


def _mlstm_mix(p_ctx, p_lat, conv_w, conv_b, wq, wk, wv, gate_b, norm_g):
    bn, lc, pw = p_ctx.shape
    ll = p_lat.shape[1]
    width = ML_HEADS * LANES
    lt = lc + ll
    nct = lt // ML_CHUNK
    gb = jnp.zeros((1, LANES), F32).at[0, :4 * ML_HEADS].set(gate_b.astype(F32))
    full2 = lambda a: pl.BlockSpec(a.shape, lambda b: (0,) * a.ndim)
    conv_b2, norm_g2 = conv_b.reshape(1, width), norm_g.reshape(1, width)
    assert ML_CHUNK == LANES
    wq, wkt, wv = wq.astype(BF16), jnp.swapaxes(wk, 1, 2).astype(BF16), wv.astype(BF16)
    return pl.pallas_call(
        functools.partial(_mlstm_kernel, n_ctx_rows=lc, n_lat_rows=ll),
        grid=(bn,),
        in_specs=[
            pl.BlockSpec((1, lc, pw), lambda b: (b, 0, 0)),
            pl.BlockSpec((1, ll, pw), lambda b: (b, 0, 0)),
            full2(conv_w), full2(conv_b2), full2(wq), full2(wkt), full2(wv), full2(gb), full2(norm_g2),
        ],
        out_specs=[
            pl.BlockSpec((1, lc, width), lambda b: (b, 0, 0)),
            pl.BlockSpec((1, ll, width), lambda b: (b, 0, 0)),
        ],
        out_shape=[jax.ShapeDtypeStruct((bn, lc, width), BF16), jax.ShapeDtypeStruct((bn, ll, width), BF16)],
        scratch_shapes=[
            pltpu.VMEM((lt, width), BF16),
            pltpu.VMEM((nct, width, ML_CHUNK), BF16),
            pltpu.VMEM((lt, 2 * width), BF16),
            pltpu.VMEM((lt, width), F32),
            pltpu.VMEM((2 * ML_HEADS, LANES, 2 * LANES), F32),
            pltpu.VMEM((nct, ML_CHUNK, LANES), F32), pltpu.VMEM((nct, ML_CHUNK, LANES), F32),
            pltpu.VMEM((nct, 4 * ML_HEADS, ML_CHUNK), F32),
        ],
        compiler_params=_cparams(("arbitrary",)),
        name="mlstm_mix",
    )(p_ctx, p_lat, conv_w, conv_b2, wq, wkt, wv, gb, norm_g2)


def _ffn_tail(x, y, gate_mix, g2, shift, scale, gate_ffn, w1_ref, w3_ref, w2_ref, hidden_chunk):
    x1 = x + gate_mix * y
    h2 = _norm_mod(x1, g2, shift, scale).astype(BF16)
    hidden = w1_ref.shape[1]
    acc = jnp.zeros_like(x1)
    for j in range(hidden // hidden_chunk):
        cs = slice(j * hidden_chunk, (j + 1) * hidden_chunk)
        z = (_silu(_dot(h2, w1_ref[:, cs])) * _dot(h2, w3_ref[:, cs])).astype(BF16)
        acc = acc + _dot(z, w2_ref[cs, :])
    return x1 + gate_ffn * acc


def _final_norm(x, g):
    return x * lax.rsqrt(jnp.mean(x * x, axis=-1, keepdims=True) + EPS) * g


def _outproj_even_kernel(x_ref, ys_ref, m_ref, gm_ref, sh_ref, sc_ref, gf_ref, g2_ref, gw_ref, gbias_ref,
                         wos_ref, wom_ref, w1_ref, w3_ref, w2_ref, o_ref, *, hidden_chunk):
    ys = jax.nn.gelu(ys_ref[0])
    s = ys * jax.nn.sigmoid(_dot(ys.astype(BF16), gw_ref[...]) + gbias_ref[...])
    y = _dot(s.astype(BF16), wos_ref[...]) + _dot(m_ref[0], wom_ref[...])
    o_ref[0] = _ffn_tail(x_ref[0], y, gm_ref[0], g2_ref[...], sh_ref[0], sc_ref[0], gf_ref[0],
                         w1_ref, w3_ref, w2_ref, hidden_chunk)


def _outproj_odd_kernel(x_ref, a_ref, gate_ref, gm_ref, sh_ref, sc_ref, gf_ref, g2_ref, ng_ref, wo_ref, w1_ref,
                        w3_ref, w2_ref, fg_ref, o_ref, *, hidden_chunk, heads):
    hv = a_ref.shape[-1] // heads
    y = jnp.zeros(x_ref.shape[1:], F32)
    for h in range(heads):
        cols = slice(h * hv, (h + 1) * hv)
        o = a_ref[0, :, cols].astype(F32)
        mu = jnp.mean(o, axis=1, keepdims=True)
        cen = o - mu
        var = jnp.mean(cen * cen, axis=1, keepdims=True)
        normed = cen * lax.rsqrt(var + EPS) * ng_ref[:, cols]
        gated = (_silu(gate_ref[0, :, cols].astype(F32)) * normed).astype(BF16)
        y = y + _dot(gated, wo_ref[cols, :])
    out = _ffn_tail(x_ref[0], y, gm_ref[0], g2_ref[...], sh_ref[0], sc_ref[0], gf_ref[0],
                    w1_ref, w3_ref, w2_ref, hidden_chunk)
    o_ref[0] = _final_norm(out, fg_ref[...])


def _hidden_chunk(hidden):
    for c in (1408, 1024, 512, 256, 128):
        if hidden % c == 0:
            return c
    return hidden


def _mod_specs(d, row_of_batch, slots):
    return [pl.BlockSpec((1, 1, d), lambda b, i, s=s: (row_of_batch(b) * N_MOD + s, 0, 0)) for s in slots]


def _outproj_even(x, ys, m, mods, row_of_batch, g2, glu_w, glu_b, w_out, w1, w3, w2):
    bn, ln, d = x.shape
    sw = ys.shape[-1]
    tm = _token_tile(ln)
    res = lambda a: _resident(a.shape, lambda b, i: (0,) * a.ndim)
    tok = lambda w: pl.BlockSpec((1, tm, w), lambda b, i: (b, i, 0))
    wos, wom = w_out[:sw], w_out[sw:]
    glu_b2 = glu_b.reshape(1, sw)
    return pl.pallas_call(
        functools.partial(_outproj_even_kernel, hidden_chunk=_hidden_chunk(w1.shape[1])),
        grid=(bn, ln // tm),
        in_specs=[tok(d), tok(sw), tok(m.shape[-1])] + _mod_specs(d, row_of_batch, (2, 3, 4, 5))
        + [pl.BlockSpec((1, d), lambda b, i: (0, 0)), res(glu_w), res(glu_b2), res(wos), res(wom),
           res(w1), res(w3), res(w2)],
        out_specs=tok(d),
        out_shape=jax.ShapeDtypeStruct((bn, ln, d), F32),
        compiler_params=_cparams(("arbitrary", "arbitrary")),
        name="outproj_ffn_even",
    )(x, ys, m, mods, mods, mods, mods, g2, glu_w, glu_b2, wos, wom, w1, w3, w2)


def _outproj_odd(x, a, gate, mods, row_of_batch, g2, norm_g, w_out, w1, w3, w2, final_g):
    bn, ln, d = x.shape
    vw = a.shape[-1]
    tm = _token_tile(ln)
    res = lambda a_: _resident(a_.shape, lambda b, i: (0,) * a_.ndim)
    tok = lambda w: pl.BlockSpec((1, tm, w), lambda b, i: (b, i, 0))
    row = lambda w: pl.BlockSpec((1, w), lambda b, i: (0, 0))
    return pl.pallas_call(
        functools.partial(_outproj_odd_kernel, hidden_chunk=_hidden_chunk(w1.shape[1]), heads=RET_HEADS),
        grid=(bn, ln // tm),
        in_specs=[tok(d), tok(vw), tok(vw)] + _mod_specs(d, row_of_batch, (2, 3, 4, 5))
        + [row(d), row(vw), res(w_out), res(w1), res(w3), res(w2), row(d)],
        out_specs=tok(d),
        out_shape=jax.ShapeDtypeStruct((bn, ln, d), F32),
        compiler_params=_cparams(("arbitrary", "arbitrary")),
        name="outproj_ffn_odd",
    )(x, a, gate, mods, mods, mods, mods, g2, norm_g.reshape(1, vw), w_out, w1, w3, w2, final_g)


def _rope_heads(t, cos, sin, heads, hk):
    half = hk // 2
    out = []
    for h in range(heads):
        t1 = t[:, h * hk:h * hk + half]
        t2 = t[:, h * hk + half:(h + 1) * hk]
        out.append(t1 * cos - t2 * sin)
        out.append(t2 * cos + t1 * sin)
    return jnp.concatenate(out, axis=-1)


def _inproj_odd_kernel(x_ref, sh_ref, sc_ref, g_ref, cos_ref, sin_ref, w_ref, q_ref, k_ref, v_ref, gate_ref,
                       *, qk, vw, heads):
    h = _norm_mod(x_ref[0], g_ref[...], sh_ref[0], sc_ref[0]).astype(BF16)
    hk = qk // heads
    cos, sin = cos_ref[...], sin_ref[...]
    q = _rope_heads(_dot(h, w_ref[:, 0:qk]), cos, sin, heads, hk)
    q_ref[0] = (q * (hk ** -0.5)).astype(BF16)
    k = _rope_heads(_dot(h, w_ref[:, qk:2 * qk]), cos, sin, heads, hk)
    k_ref[0] = k.astype(BF16)
    v_ref[0] = _dot(h, w_ref[:, 2 * qk:2 * qk + vw]).astype(BF16)
    gate_ref[0] = _dot(h, w_ref[:, 2 * qk + vw:2 * qk + 2 * vw]).astype(BF16)


def _inproj_odd_ctx_kernel(x_ref, sh_ref, sc_ref, g_ref, w_ref, k_ref, v_ref, *, qk, vw):
    h = _norm_mod(x_ref[0], g_ref[...], sh_ref[0], sc_ref[0]).astype(BF16)
    k_ref[0] = _dot(h, w_ref[:, 0:qk]).astype(BF16)
    v_ref[0] = _dot(h, w_ref[:, qk:qk + vw]).astype(BF16)


def _inproj_odd(x, mods, row_of_batch, g, cos, sin, w, qk, vw):
    bn, ln, d = x.shape
    tm = _token_tile(ln)
    tok = lambda w_: pl.BlockSpec((1, tm, w_), lambda b, i: (b, i, 0))
    half = cos.shape[1]
    return pl.pallas_call(
        functools.partial(_inproj_odd_kernel, qk=qk, vw=vw, heads=RET_HEADS),
        grid=(bn, ln // tm),
        in_specs=[tok(d)] + _mod_specs(d, row_of_batch, (0, 1))
        + [pl.BlockSpec((1, d), lambda b, i: (0, 0)),
           pl.BlockSpec((tm, half), lambda b, i: (i, 0)), pl.BlockSpec((tm, half), lambda b, i: (i, 0)),
           _resident(w.shape, lambda b, i: (0, 0))],
        out_specs=[tok(qk), tok(qk), tok(vw), tok(vw)],
        out_shape=[jax.ShapeDtypeStruct((bn, ln, qk), BF16), jax.ShapeDtypeStruct((bn, ln, qk), BF16),
                   jax.ShapeDtypeStruct((bn, ln, vw), BF16), jax.ShapeDtypeStruct((bn, ln, vw), BF16)],
        compiler_params=_cparams(("arbitrary", "arbitrary")),
        name="inproj_odd",
    )(x, mods, mods, g, cos, sin, w)


def _inproj_odd_ctx(x, mods, row_of_batch, g, w_kv, qk, vw):
    bn, ln, d = x.shape
    tm = _token_tile(ln)
    tok = lambda w_: pl.BlockSpec((1, tm, w_), lambda b, i: (b, i, 0))
    return pl.pallas_call(
        functools.partial(_inproj_odd_ctx_kernel, qk=qk, vw=vw),
        grid=(bn, ln // tm),
        in_specs=[tok(d)] + _mod_specs(d, row_of_batch, (0, 1))
        + [pl.BlockSpec((1, d), lambda b, i: (0, 0)), _resident(w_kv.shape, lambda b, i: (0, 0))],
        out_specs=[tok(qk), tok(vw)],
        out_shape=[jax.ShapeDtypeStruct((bn, ln, qk), BF16), jax.ShapeDtypeStruct((bn, ln, vw), BF16)],
        compiler_params=_cparams(("arbitrary", "arbitrary")),
        name="inproj_odd_ctx",
    )(x, mods, mods, g, w_kv)


def _retention_kernel(lg_ref, q_ref, k_ref, v_ref, kc_ref, vc_ref, o_ref, acc_s, sf_s, sb_s,
                      *, chunk, n_ctx_rows, n_lat_rows):
    t_n = chunk
    h = pl.program_id(1)
    lgf = jnp.full((1, 1), lg_ref[0, h], F32)
    lgb = jnp.full((1, 1), lg_ref[1, h], F32)
    nc_ctx, nc_lat = n_ctx_rows // t_n, n_lat_rows // t_n
    ti = lax.broadcasted_iota(jnp.int32, (t_n, t_n), 0)
    si = lax.broadcasted_iota(jnp.int32, (t_n, t_n), 1)
    diff = (ti - si).astype(F32)
    decay = jnp.where(diff >= 0, jnp.exp(lgf * jnp.maximum(diff, 0.0)), jnp.exp(lgb * jnp.maximum(-diff, 0.0)))
    pos = lax.broadcasted_iota(jnp.int32, (t_n, 1), 0).astype(F32)
    inter_f = jnp.exp(lgf * (pos + 1.0))
    inter_b = jnp.exp(lgb * (t_n - pos))
    wend_f = jnp.exp(lgf * (t_n - 1.0 - pos))
    wend_b = jnp.exp(lgb * pos)
    cd_f = jnp.exp(lgf * t_n)
    cd_b = jnp.exp(lgb * t_n)

    def rows_of(c):
        return pl.ds(pl.multiple_of(c * t_n, t_n), t_n)

    def bump(s_ref, k, v, wend, cd):
        kw = (k.astype(F32) * wend).astype(BF16)
        s_ref[...] = cd * s_ref[...] + _dot_tn(kw, v)

    sf_s[...] = jnp.zeros_like(sf_s)
    sb_s[...] = jnp.zeros_like(sb_s)

    def ctx_f(c, _):
        bump(sf_s, kc_ref[0, rows_of(c), :], vc_ref[0, rows_of(c), :], wend_f, cd_f)
        return 0
    lax.fori_loop(0, nc_ctx, ctx_f, 0)

    def ctx_b(i, _):
        c = nc_ctx - 1 - i
        bump(sb_s, kc_ref[0, rows_of(c), :], vc_ref[0, rows_of(c), :], wend_b, cd_b)
        return 0
    lax.fori_loop(0, nc_ctx, ctx_b, 0)

    def forward_part(rows):
        q, k, v = q_ref[0, rows, :], k_ref[0, rows, :], v_ref[0, rows, :]
        scores = (_dot_nt(q, k) * decay).astype(BF16)
        part = _dot(scores, v) + inter_f * _dot(q, sf_s[...].astype(BF16))
        bump(sf_s, k, v, wend_f, cd_f)
        return part

    def backward_part(rows):
        q, k, v = q_ref[0, rows, :], k_ref[0, rows, :], v_ref[0, rows, :]
        part = inter_b * _dot(q, sb_s[...].astype(BF16))
        bump(sb_s, k, v, wend_b, cd_b)
        return part

    def first_half(i, _):
        rf, rb = rows_of(i), rows_of(nc_lat - 1 - i)
        acc_s[rf, :] = forward_part(rf)
        acc_s[rb, :] = backward_part(rb)
        return 0
    lax.fori_loop(0, nc_lat // 2, first_half, 0)

    def second_half(i, _):
        rf, rb = rows_of(i), rows_of(nc_lat - 1 - i)
        o_ref[0, rf, :] = (acc_s[rf, :] + forward_part(rf)).astype(BF16)
        o_ref[0, rb, :] = (acc_s[rb, :] + backward_part(rb)).astype(BF16)
        return 0
    lax.fori_loop(nc_lat // 2, nc_lat, second_half, 0)


def _retention_mix(q, k, v, k_ctx, v_ctx, log_gamma):
    bn, ll, qk = q.shape
    lc = k_ctx.shape[1]
    vw = v.shape[-1]
    hk, hv = qk // RET_HEADS, vw // RET_HEADS
    chunk = 256 if (ll % 512 == 0 and lc % 256 == 0) else 128
    assert (ll // chunk) % 2 == 0, "latent chunks are visited in forward/backward pairs"
    head = lambda n_rows, w: pl.BlockSpec((1, n_rows, w), lambda b, h: (b, 0, h))
    return pl.pallas_call(
        functools.partial(_retention_kernel, chunk=chunk, n_ctx_rows=lc, n_lat_rows=ll),
        grid=(bn, RET_HEADS),
        in_specs=[
            pl.BlockSpec(memory_space=pltpu.SMEM),
            head(ll, hk), head(ll, hk), head(ll, hv), head(lc, hk), head(lc, hv),
        ],
        out_specs=head(ll, hv),
        out_shape=jax.ShapeDtypeStruct((bn, ll, vw), BF16),
        scratch_shapes=[pltpu.VMEM((ll, hv), F32), pltpu.VMEM((hk, hv), F32), pltpu.VMEM((hk, hv), F32)],
        compiler_params=_cparams(("arbitrary", "arbitrary")),
        name="retention_mix",
    )(log_gamma.astype(F32), q, k, v, k_ctx, v_ctx)


def _grid_rope(n_pos, hk):
    rows = n_pos // GRID_W
    row = jnp.repeat(jnp.arange(rows, dtype=F32), GRID_W)
    col = jnp.tile(jnp.arange(GRID_W, dtype=F32), rows)
    n_freq = hk // 4
    inv = ROPE_BASE ** (-jnp.arange(n_freq, dtype=F32) / n_freq)
    ang = jnp.concatenate([row[:, None] * inv, col[:, None] * inv], -1)
    return jnp.cos(ang), jnp.sin(ang)


def kernel(x, c, ctx, c_ctx, ada_w, ada_b, norm1_g, norm2_g, ab_w_in, ab_w_out, s5_a_re, s5_a_im, s5_log_dt,
           s5_b_re, s5_b_im, s5_c_re, s5_c_im, s5_d, s5_glu_w, s5_glu_b, ml_conv_w, ml_conv_b, ml_wq, ml_wk,
           ml_wv, ml_gate_b, ml_norm_g, ret_w_in, ret_w_out, ret_log_gamma, ret_norm_g, ffn_w1, ffn_w3, ffn_w2,
           final_g):
    bn, ln, d = x.shape
    depth = ada_w.shape[0]
    assert depth == 2, "one S5 || mLSTM layer followed by one retention layer"
    s5_width = s5_d.shape[-1]
    ml_width = ml_norm_g.shape[-1]
    assert ml_width == ML_HEADS * LANES and s5_width % S5_GROUP == 0

    r_pad = -(-(bn + 1) // SUBLANES) * SUBLANES
    vec = jnp.zeros((r_pad, d), F32).at[:bn].set(c).at[bn].set(c_ctx)
    mods_all = _modulation(vec, ada_w, ada_b).reshape(depth, r_pad * N_MOD, 1, d)
    lat_row = lambda b: b
    ctx_row = lambda b: bn

    mods = mods_all[0]
    g1 = norm1_g[0].reshape(1, d)
    g2 = norm2_g[0].reshape(1, d)
    w_in = ab_w_in[0]
    n_gate = w_in.shape[1] - s5_width - 2 * ml_width
    w_in = jnp.pad(w_in, ((0, 0), (0, LANES - n_gate))).astype(BF16)
    u_lat, p_lat = _inproj_even(x, mods, lat_row, g1, w_in, s5_width)
    u_ctx, p_ctx = _inproj_even(ctx, mods, ctx_row, g1, w_in, s5_width)
    mats = _s5_matrices(s5_a_re[0], s5_a_im[0], s5_log_dt[0], s5_b_re[0], s5_b_im[0], s5_c_re[0], s5_c_im[0],
                        s5_d[0])
    ys_ctx, ys_lat = _s5_mix(u_ctx, u_lat, mats)
    m_ctx, m_lat = _mlstm_mix(p_ctx, p_lat, ml_conv_w[0], ml_conv_b[0], ml_wq[0], ml_wk[0], ml_wv[0],
                              ml_gate_b[0], ml_norm_g[0])
    glu_w = s5_glu_w[0].astype(BF16)
    w_out = ab_w_out[0].astype(BF16)
    w1, w3, w2 = ffn_w1[0].astype(BF16), ffn_w3[0].astype(BF16), ffn_w2[0].astype(BF16)
    x = _outproj_even(x, ys_lat, m_lat, mods, lat_row, g2, glu_w, s5_glu_b[0], w_out, w1, w3, w2)
    ctx = _outproj_even(ctx, ys_ctx, m_ctx, mods, ctx_row, g2, glu_w, s5_glu_b[0], w_out, w1, w3, w2)

    mods = mods_all[1]
    g1 = norm1_g[1].reshape(1, d)
    g2 = norm2_g[1].reshape(1, d)
    vw = ret_norm_g.shape[-1]
    qk = (ret_w_in.shape[-1] - 2 * vw) // 2
    w_in = ret_w_in[0].astype(BF16)
    cos, sin = _grid_rope(ln, qk // RET_HEADS)
    q, k, v, gate = _inproj_odd(x, mods, lat_row, g1, cos, sin, w_in, qk, vw)
    k_ctx, v_ctx = _inproj_odd_ctx(ctx, mods, ctx_row, g1, w_in[:, qk:2 * qk + vw], qk, vw)
    a = _retention_mix(q, k, v, k_ctx, v_ctx, ret_log_gamma[0])
    w_out = ret_w_out[0].astype(BF16)
    w1, w3, w2 = ffn_w1[1].astype(BF16), ffn_w3[1].astype(BF16), ffn_w2[1].astype(BF16)
    return _outproj_odd(x, a, gate, mods, lat_row, g2, ret_norm_g[0], w_out, w1, w3, w2, final_g.reshape(1, d))
```

```python
import functools
import math

import jax
import jax.numpy as jnp
from jax import lax
from jax.experimental import pallas as pl
from jax.experimental.pallas import tpu as pltpu

F32 = jnp.float32
BF16 = jnp.bfloat16

EPS = 1e-6
N_MOD = 6
GRID_W = 64
ROPE_BASE = 10000.0

S5_GROUP = 16
S5_STATE = 64
S5_STEP = 16
ML_HEADS = 4
ML_CHUNK = 128
RET_HEADS = 4
SUBLANES = 8
LANES = 128
NEG_BIG = -1e30
VMEM_LIMIT = 56 * 1024 * 1024


def _cparams(sem):
    return pltpu.CompilerParams(dimension_semantics=sem, vmem_limit_bytes=VMEM_LIMIT)


def _resident(shape, index_map):
    return pl.BlockSpec(shape, index_map, pipeline_mode=pl.Buffered(1))


def _token_tile(n):
    for t in (512, 256, 128):
        if n % t == 0:
            return t
    raise ValueError(f"sequence length {n} must be a multiple of 128")


def _silu(v):
    return v * jax.nn.sigmoid(v)


def _norm_mod(x, g, shift, scale):
    y = x * lax.rsqrt(jnp.mean(x * x, axis=-1, keepdims=True) + EPS)
    return (y * g) * (1.0 + scale) + shift


def _dot(a, b):
    return jnp.dot(a, b, preferred_element_type=F32)


def _dot_nt(a, b):
    return lax.dot_general(a, b, (((1,), (1,)), ((), ())), preferred_element_type=F32)


def _dot_tn(a, b):
    return lax.dot_general(a, b, (((0,), (0,)), ((), ())), preferred_element_type=F32)


def _mod_kernel(v_ref, w_ref, b_ref, o_ref):
    s = _silu(v_ref[...]).astype(BF16)
    o_ref[0] = _dot(s, w_ref[0].astype(BF16)) + b_ref[0]


def _modulation(vec, ada_w, ada_b):
    depth, d, n = ada_w.shape
    r = vec.shape[0]
    tn = 1024
    return pl.pallas_call(
        _mod_kernel,
        grid=(depth, n // tn),
        in_specs=[
            pl.BlockSpec((r, d), lambda l, j: (0, 0)),
            pl.BlockSpec((1, d, tn), lambda l, j: (l, 0, j)),
            pl.BlockSpec((1, 1, tn), lambda l, j: (l, 0, j)),
        ],
        out_specs=pl.BlockSpec((1, r, tn), lambda l, j: (l, 0, j)),
        out_shape=jax.ShapeDtypeStruct((depth, r, n), F32),
        compiler_params=_cparams(("arbitrary", "arbitrary")),
        name="adaln_modulation",
    )(vec, ada_w, ada_b.reshape(depth, 1, n))


def _inproj_even_kernel(x_ref, sh_ref, sc_ref, g_ref, w_ref, u_ref, p_ref, *, s5_width):
    h = _norm_mod(x_ref[0], g_ref[...], sh_ref[0], sc_ref[0]).astype(BF16)
    p = _dot(h, w_ref[...])
    u_ref[0] = p[:, :s5_width]
    p_ref[0] = p[:, s5_width:]


def _inproj_even(x, mods, row_of_batch, g, w, s5_width):
    bn, ln, d = x.shape
    n = w.shape[1]
    tm = _token_tile(ln)
    return pl.pallas_call(
        functools.partial(_inproj_even_kernel, s5_width=s5_width),
        grid=(bn, ln // tm),
        in_specs=[
            pl.BlockSpec((1, tm, d), lambda b, i: (b, i, 0)),
            pl.BlockSpec((1, 1, d), lambda b, i: (row_of_batch(b) * N_MOD + 0, 0, 0)),
            pl.BlockSpec((1, 1, d), lambda b, i: (row_of_batch(b) * N_MOD + 1, 0, 0)),
            pl.BlockSpec((1, d), lambda b, i: (0, 0)),
            _resident((d, n), lambda b, i: (0, 0)),
        ],
        out_specs=[
            pl.BlockSpec((1, tm, s5_width), lambda b, i: (b, i, 0)),
            pl.BlockSpec((1, tm, n - s5_width), lambda b, i: (b, i, 0)),
        ],
        out_shape=[
            jax.ShapeDtypeStruct((bn, ln, s5_width), F32),
            jax.ShapeDtypeStruct((bn, ln, n - s5_width), F32),
        ],
        compiler_params=_cparams(("arbitrary", "arbitrary")),
        name="inproj_even",
    )(x, mods, mods, g, w)


def _s5_response_kernel(c_ref, w_ref, o_ref):
    for n in range(c_ref.shape[0]):
        o_ref[n] = lax.dot_general(c_ref[n], w_ref[n], (((1,), (1,)), ((), ())),
                                   precision=lax.Precision.HIGHEST, preferred_element_type=F32)


def _s5_response(c_cat, w_cat):
    n, co, kk = c_cat.shape
    r = w_cat.shape[1]
    nb = 8
    return pl.pallas_call(
        _s5_response_kernel,
        grid=(n // nb,),
        in_specs=[pl.BlockSpec((nb, co, kk), lambda i: (i, 0, 0)), pl.BlockSpec((nb, r, kk), lambda i: (i, 0, 0))],
        out_specs=pl.BlockSpec((nb, co, r), lambda i: (i, 0, 0)),
        out_shape=jax.ShapeDtypeStruct((n, co, r), F32),
        compiler_params=_cparams(("arbitrary",)),
        name="s5_response",
    )(c_cat, w_cat)


def _s5_matrices(a_re, a_im, log_dt, b_re, b_im, c_re, c_im, d_skip):
    k = S5_STEP
    a_re, a_im = a_re.astype(F32), a_im.astype(F32)
    dt = jnp.exp(log_dt.astype(F32))[..., None]
    lam_re, lam_im = a_re * dt, a_im * dt
    steps = jnp.arange(k + 1, dtype=F32)[:, None, None, None]
    mag = jnp.exp(steps * lam_re)
    pr, pi = mag * jnp.cos(steps * lam_im), mag * jnp.sin(steps * lam_im)
    den = a_re * a_re + a_im * a_im
    nr = pr[1] - 1.0
    coef_re = (nr * a_re + pi[1] * a_im) / den
    coef_im = (pi[1] * a_re - nr * a_im) / den
    bt_re = jnp.swapaxes(b_re.astype(F32), -1, -2)
    bt_im = jnp.swapaxes(b_im.astype(F32), -1, -2)
    bb_re = coef_re[:, :, None] * bt_re - coef_im[:, :, None] * bt_im
    bb_im = coef_re[:, :, None] * bt_im + coef_im[:, :, None] * bt_re
    wr = pr[:, :, :, None] * bb_re - pi[:, :, :, None] * bb_im
    wi = pr[:, :, :, None] * bb_im + pi[:, :, :, None] * bb_re
    c_re, c_im = c_re.astype(F32), c_im.astype(F32)
    g_n, c_n, p_n = a_re.shape[1], b_re.shape[-1], a_re.shape[-1]
    w_cat = jnp.transpose(jnp.concatenate([wr, -wi], -1), (1, 2, 0, 3, 4)).reshape(2 * g_n, (k + 1) * c_n, 2 * p_n)
    c_cat = jnp.concatenate([c_re, c_im], -1).reshape(2 * g_n, c_n, 2 * p_n)
    resp = _s5_response(c_cat, w_cat).reshape(2, g_n, c_n, k + 1, c_n)
    skip = (jnp.eye(c_n, dtype=F32) * d_skip.astype(F32).reshape(g_n, 1, c_n))[:, :, None, :]
    gen = jnp.concatenate([resp[0, :, :, k - 1:0:-1], resp[0, :, :, 0:1] + resp[1, :, :, 0:1] + skip,
                           resp[1, :, :, 1:k]], axis=2).reshape(g_n, c_n, (2 * k - 1) * c_n)
    toep = jnp.stack([gen[:, :, (k - 1 - t) * c_n:(2 * k - 1 - t) * c_n] for t in range(k)], axis=1)
    toep = toep.reshape(g_n, k * c_n, k * c_n)
    endw = jnp.concatenate([wr[k - 1::-1, 0], wr[:k, 1], wi[k - 1::-1, 0], wi[:k, 1]], -1)
    endw = jnp.transpose(endw, (1, 0, 2, 3)).reshape(g_n, k * c_n, 4 * p_n)

    def out_cols(d, powers_re, powers_im):
        cr, ci = c_re[d][None], c_im[d][None]
        pre, pim = powers_re[:, :, None, :], powers_im[:, :, None, :]
        return cr * pre - ci * pim, -(cr * pim + ci * pre)

    of_re, of_im = out_cols(0, pr[1:, 0], pi[1:, 0])
    ob_re, ob_im = out_cols(1, pr[k:0:-1, 1], pi[k:0:-1, 1])
    outw_t = jnp.concatenate([of_re, ob_re, of_im, ob_im], -1)
    outw_t = jnp.transpose(outw_t, (1, 0, 2, 3)).reshape(g_n, k * c_n, 4 * p_n)
    a0 = jnp.concatenate([pr[k, 0], pr[k, 1]], -1)
    a1 = jnp.concatenate([pi[k, 0], pi[k, 1]], -1)
    return toep.astype(BF16), endw.astype(BF16), outw_t.astype(BF16), a0, a1


S5_LANE_GROUPS = LANES // S5_GROUP


def _chunk_transpose(xs):
    n_arr = len(xs)
    lane_chunk = lax.broadcasted_iota(jnp.int32, xs[0].shape, 1) // S5_GROUP
    d = 1
    while d < n_arr:
        keep = (lane_chunk & d) == 0
        out = list(xs)
        for i in range(n_arr):
            if i & d:
                continue
            lo, hi = xs[i], xs[i + d]
            out[i] = jnp.where(keep, lo, pltpu.roll(hi, d * S5_GROUP, 1))
            out[i + d] = jnp.where(keep, pltpu.roll(lo, LANES - d * S5_GROUP, 1), hi)
        xs = out
        d *= 2
    return xs


def _s5_kernel(uc_ref, ul_ref, toep_ref, endw_ref, outw_ref, a0_ref, a1_ref, yc_ref, yl_ref,
               uc_s, ul_s, e_s, yc_s, yl_s, *, batch, n_ctx, n_lat):
    p = S5_STATE
    gl = S5_LANE_GROUPS
    k = S5_STEP

    def stack(u_ref, us_ref, n_blk):
        for b in range(batch):
            for half in range(k // gl):
                xs = [u_ref[b, pl.ds(half * gl + s, n_blk, stride=k), :] for s in range(gl)]
                ys = _chunk_transpose(xs)
                for g in range(gl):
                    us_ref[g, b * n_blk:(b + 1) * n_blk, half * LANES:(half + 1) * LANES] = ys[g].astype(BF16)

    stack(uc_ref, uc_s, n_ctx)
    stack(ul_ref, ul_s, n_lat)

    def increments(us_ref, n_blk, first_blk):
        for g in range(gl):
            e = _dot(us_ref[g], endw_ref[g])
            for b in range(batch):
                for part in range(2):
                    e_s[part, b, pl.ds(first_blk * gl + g, n_blk, stride=gl), :] = (
                        e[b * n_blk:(b + 1) * n_blk, part * LANES:(part + 1) * LANES])

    increments(uc_s, n_ctx, 0)
    increments(ul_s, n_lat, n_ctx)

    a0 = a0_ref[...]
    a1 = a1_ref[...]
    fwd_lane = lax.broadcasted_iota(jnp.int32, (batch, gl, 2 * p), 2) < p

    def scan(first_blk, n, carry):
        def body(i, c):
            s0, s1 = c
            rf = pl.ds(pl.multiple_of((first_blk + i) * gl, gl), gl)
            rb = pl.ds(pl.multiple_of((first_blk + n - 1 - i) * gl, gl), gl)
            e0 = jnp.where(fwd_lane, e_s[0, :, rf, :], e_s[0, :, rb, :])
            e1 = jnp.where(fwd_lane, e_s[1, :, rf, :], e_s[1, :, rb, :])
            e_s[0, :, rf, 0:p] = s0[:, :, 0:p]
            e_s[0, :, rb, p:2 * p] = s0[:, :, p:2 * p]
            e_s[1, :, rf, 0:p] = s1[:, :, 0:p]
            e_s[1, :, rb, p:2 * p] = s1[:, :, p:2 * p]
            return a0 * s0 - a1 * s1 + e0, a0 * s1 + a1 * s0 + e1
        return lax.fori_loop(0, n, body, carry)

    zero = jnp.zeros((batch, gl, 2 * p), F32)
    carry = scan(0, n_ctx, (zero, zero))
    scan(n_ctx, n_lat, carry)

    def outputs(us_ref, ys_ref, y_ref, n_blk, first_blk):
        for g in range(gl):
            state = jnp.concatenate(
                [jnp.concatenate([e_s[part, b, pl.ds(first_blk * gl + g, n_blk, stride=gl), :]
                                  for b in range(batch)], axis=0) for part in range(2)], axis=1)
            ys_ref[g] = _dot_nt(us_ref[g], toep_ref[g]) + _dot_nt(state.astype(BF16), outw_ref[g])
        for b in range(batch):
            for half in range(k // gl):
                ys = [ys_ref[g, b * n_blk:(b + 1) * n_blk, half * LANES:(half + 1) * LANES] for g in range(gl)]
                xs = _chunk_transpose(ys)
                for s in range(gl):
                    y_ref[b, pl.ds(half * gl + s, n_blk, stride=k), :] = xs[s]

    outputs(uc_s, yc_s, yc_ref, n_ctx, 0)
    outputs(ul_s, yl_s, yl_ref, n_lat, n_ctx)


def _s5_mix(u_ctx, u_lat, mats):
    toep, endw, outw, a0, a1 = mats
    bn, lc, w = u_ctx.shape
    ll = u_lat.shape[1]
    gl = S5_LANE_GROUPS
    kc = S5_STEP * S5_GROUP
    n_ctx, n_lat = lc // S5_STEP, ll // S5_STEP
    bb = 4 if bn % 4 == 0 else bn
    seq = lambda n_rows: pl.BlockSpec((bb, n_rows, LANES), lambda i, j: (j, 0, i))
    grp = lambda r, c: pl.BlockSpec((gl, r, c), lambda i, j: (i, 0, 0))
    return pl.pallas_call(
        functools.partial(_s5_kernel, batch=bb, n_ctx=n_ctx, n_lat=n_lat),
        grid=(w // LANES, bn // bb),
        in_specs=[seq(lc), seq(ll), grp(kc, kc), grp(kc, kc), grp(kc, kc),
                  pl.BlockSpec((gl, 2 * S5_STATE), lambda i, j: (i, 0)),
                  pl.BlockSpec((gl, 2 * S5_STATE), lambda i, j: (i, 0))],
        out_specs=[seq(lc), seq(ll)],
        out_shape=[jax.ShapeDtypeStruct((bn, lc, w), F32), jax.ShapeDtypeStruct((bn, ll, w), F32)],
        scratch_shapes=[
            pltpu.VMEM((gl, bb * n_ctx, kc), BF16), pltpu.VMEM((gl, bb * n_lat, kc), BF16),
            pltpu.VMEM((2, bb, (n_ctx + n_lat) * gl, LANES), F32),
            pltpu.VMEM((gl, bb * n_ctx, kc), F32), pltpu.VMEM((gl, bb * n_lat, kc), F32),
        ],
        compiler_params=_cparams(("arbitrary", "arbitrary")),
        name="s5_scan",
    )(u_ctx, u_lat, toep, endw, outw, a0, a1)


def _mlstm_kernel(pc_ref, pl_ref, cw_ref, cb_ref, wq_ref, wkt_ref, wv_ref, gb_ref, mc_ref, ml_ref,
                  q_s, kt_s, vx_s, h_s, st_s, bcol_s, mcol_s, rrow_s, *, n_ctx_rows, n_lat_rows):
    t_n = ML_CHUNK
    width = ML_HEADS * LANES
    nc_ctx, nc_lat = n_ctx_rows // t_n, n_lat_rows // t_n
    nc = nc_ctx + nc_lat
    scale_k = LANES ** -0.5

    def project(p_ref, n_rows, base):
        row = lax.broadcasted_iota(jnp.int32, (n_rows, LANES), 0)
        ones = jnp.ones((n_rows, LANES), BF16)
        for h in range(ML_HEADS):
            cols = slice(h * LANES, (h + 1) * LANES)
            xm = p_ref[0, :, cols]
            taps = cw_ref[:, cols]
            n_tap = taps.shape[0]
            acc = jnp.zeros_like(xm) + cb_ref[:, cols]
            for j in range(n_tap):
                d = j - n_tap // 2
                if d == 0:
                    sh = xm
                else:
                    sh = pltpu.roll(xm, (-d) % n_rows, 0)
                    sh = jnp.where(row >= -d if d < 0 else row < n_rows - d, sh, 0.0)
                acc = acc + sh * taps[j:j + 1, :]
            xc = _silu(acc).astype(BF16)
            q_s[base:base + n_rows, cols] = _dot(xc, wq_ref[h]).astype(BF16)
            kt = (_dot_nt(wkt_ref[h], xc) * scale_k).astype(BF16)
            for c in range(n_rows // t_n):
                kt_s[base // t_n + c, cols, :] = kt[:, c * t_n:(c + 1) * t_n]
            vx_s[base:base + n_rows, 2 * h * LANES:(2 * h + 1) * LANES] = (
                _dot(xm.astype(BF16), wv_ref[h]).astype(BF16))
            vx_s[base:base + n_rows, (2 * h + 1) * LANES:(2 * h + 2) * LANES] = ones

    project(pc_ref, n_ctx_rows, 0)
    project(pl_ref, n_lat_rows, n_ctx_rows)

    ti = lax.broadcasted_iota(jnp.int32, (t_n, t_n), 0)
    si = lax.broadcasted_iota(jnp.int32, (t_n, t_n), 1)
    tri_f = (si <= ti).astype(BF16)
    tri_b = (si >= ti).astype(BF16)
    lane = lax.broadcasted_iota(jnp.int32, (t_n, LANES), 1)
    n_gate = 4 * ML_HEADS
    glane = lax.broadcasted_iota(jnp.int32, (n_gate, t_n), 1)
    grow = lax.broadcasted_iota(jnp.int32, (n_gate, t_n), 0)
    gate_off = 2 * width

    def gate_prep(p_ref, n_chunks, base_chunk):
        def body(c, _):
            rows = pl.ds(pl.multiple_of(c * t_n, t_n), t_n)
            gcol = p_ref[0, rows, gate_off:gate_off + LANES] + gb_ref[...]
            lf = jax.nn.log_sigmoid(gcol)
            hi = lf.astype(BF16)
            lo = (lf - hi.astype(F32)).astype(BF16)
            pre = _dot(tri_f, hi) + _dot(tri_f, lo)
            suf = _dot(tri_b, hi) + _dot(tri_b, lo)
            bsum = jnp.where(lane < 2 * ML_HEADS, pre, suf)
            rcol = gcol - pltpu.roll(bsum, LANES - ML_HEADS, 1)
            rrow = rcol.T[0:n_gate, :]
            pmax, smax = rrow, rrow
            step = 1
            while step < t_n:
                pmax = jnp.maximum(pmax, jnp.where(glane >= step, pltpu.roll(pmax, step, 1), NEG_BIG))
                smax = jnp.maximum(smax, jnp.where(glane < t_n - step, pltpu.roll(smax, t_n - step, 1), NEG_BIG))
                step *= 2
            mrow = jnp.where(grow < 2 * ML_HEADS, pmax, smax)
            bcol_s[base_chunk + c] = bsum
            mcol_s[base_chunk + c] = jnp.concatenate([mrow, jnp.zeros((LANES - n_gate, t_n), F32)], axis=0).T
            rrow_s[base_chunk + c] = rrow
            return 0
        lax.fori_loop(0, n_chunks, body, 0)

    gate_prep(pc_ref, nc_ctx, 0)
    gate_prep(pl_ref, nc_lat, nc_ctx)

    def one_dir(c, h, d, m_in):
        cols = slice(h * LANES, (h + 1) * LANES)
        xcols = slice(2 * h * LANES, (2 * h + 2) * LANES)
        li = 2 * d * ML_HEADS + h
        mask = (si <= ti) if d == 0 else (si >= ti)
        last = t_n - 1 if d == 0 else 0
        rows = pl.ds(pl.multiple_of(c * t_n, t_n), t_n)
        q = q_s[rows, cols]
        kt = kt_s[c, cols, :]
        vx = vx_s[rows, xcols]
        r_row = rrow_s[c][li:li + 1, :]
        run_max = jnp.broadcast_to(mcol_s[c][:, li:li + 1], (t_n, LANES))
        b_rep = jnp.broadcast_to(bcol_s[c][:, li + ML_HEADS:li + ML_HEADS + 1], (t_n, LANES))
        mm = jnp.maximum(m_in, run_max)
        dmat = jnp.exp(jnp.where(mask, r_row - mm, NEG_BIG))
        sm = (dmat * _dot(q, kt)).astype(BF16)
        intra = _dot(sm, vx)
        cross = _dot(q, st_s[2 * h + d].astype(BF16))
        inter = jnp.exp(m_in - mm)
        num = intra[:, :LANES] + inter * cross[:, :LANES]
        den = intra[:, LANES:] + inter * cross[:, LANES:]
        h_s[rows, cols] = h_s[rows, cols] + num / jnp.maximum(jnp.abs(den), jnp.exp(-b_rep - mm))
        r_top = run_max[last:last + 1, :]
        b_last = b_rep[last:last + 1, :]
        g_max = b_last + r_top
        m_new = jnp.maximum(b_last + m_in, g_max)
        dec = jnp.exp(b_last + m_in - m_new)
        inj = jnp.exp(g_max - m_new)
        kw = (kt.astype(F32) * (jnp.exp(r_row - r_top) * inj)).astype(BF16)
        inc = _dot(kw, vx)
        st_s[2 * h + d] = jnp.concatenate([dec, dec], axis=1) * st_s[2 * h + d] + inc
        return m_new

    st_s[...] = jnp.zeros_like(st_s)
    h_s[...] = jnp.zeros_like(h_s)

    def body(i, carry):
        cb = jnp.where(i < nc_ctx, nc_ctx - 1 - i, nc + nc_ctx - 1 - i)
        return tuple(one_dir(i if d == 0 else cb, h, d, carry[2 * h + d])
                     for h in range(ML_HEADS) for d in range(2))

    zero = jnp.zeros((1, LANES), F32)
    lax.fori_loop(0, nc, body, (zero,) * (2 * ML_HEADS))

    mc_ref[0] = h_s[0:n_ctx_rows, :].astype(BF16)
    ml_ref[0] = h_s[n_ctx_rows:n_ctx_rows + n_lat_rows, :].astype(BF16)


def _mlstm_mix(p_ctx, p_lat, conv_w, conv_b, wq, wk, wv, gate_b):
    bn, lc, pw = p_ctx.shape
    ll = p_lat.shape[1]
    width = ML_HEADS * LANES
    lt = lc + ll
    nct = lt // ML_CHUNK
    gb = jnp.zeros((1, LANES), F32).at[0, :4 * ML_HEADS].set(gate_b.astype(F32))
    full2 = lambda a: pl.BlockSpec(a.shape, lambda b: (0,) * a.ndim)
    conv_b2 = conv_b.reshape(1, width)
    assert ML_CHUNK == LANES
    wq, wkt, wv = wq.astype(BF16), jnp.swapaxes(wk, 1, 2).astype(BF16), wv.astype(BF16)
    return pl.pallas_call(
        functools.partial(_mlstm_kernel, n_ctx_rows=lc, n_lat_rows=ll),
        grid=(bn,),
        in_specs=[
            pl.BlockSpec((1, lc, pw), lambda b: (b, 0, 0)),
            pl.BlockSpec((1, ll, pw), lambda b: (b, 0, 0)),
            full2(conv_w), full2(conv_b2), full2(wq), full2(wkt), full2(wv), full2(gb),
        ],
        out_specs=[
            pl.BlockSpec((1, lc, width), lambda b: (b, 0, 0)),
            pl.BlockSpec((1, ll, width), lambda b: (b, 0, 0)),
        ],
        out_shape=[jax.ShapeDtypeStruct((bn, lc, width), BF16), jax.ShapeDtypeStruct((bn, ll, width), BF16)],
        scratch_shapes=[
            pltpu.VMEM((lt, width), BF16),
            pltpu.VMEM((nct, width, ML_CHUNK), BF16),
            pltpu.VMEM((lt, 2 * width), BF16),
            pltpu.VMEM((lt, width), F32),
            pltpu.VMEM((2 * ML_HEADS, LANES, 2 * LANES), F32),
            pltpu.VMEM((nct, ML_CHUNK, LANES), F32), pltpu.VMEM((nct, ML_CHUNK, LANES), F32),
            pltpu.VMEM((nct, 4 * ML_HEADS, ML_CHUNK), F32),
        ],
        compiler_params=_cparams(("arbitrary",)),
        name="mlstm_mix",
    )(p_ctx, p_lat, conv_w, conv_b2, wq, wkt, wv, gb)


def _ffn_tail(x, y, gate_mix, g2, shift, scale, gate_ffn, w1_ref, w3_ref, w2_ref, hidden_chunk):
    x1 = x + gate_mix * y
    h2 = _norm_mod(x1, g2, shift, scale).astype(BF16)
    hidden = w1_ref.shape[1]
    acc = jnp.zeros_like(x1)
    for j in range(hidden // hidden_chunk):
        cs = slice(j * hidden_chunk, (j + 1) * hidden_chunk)
        z = (_silu(_dot(h2, w1_ref[:, cs])) * _dot(h2, w3_ref[:, cs])).astype(BF16)
        acc = acc + _dot(z, w2_ref[cs, :])
    return x1 + gate_ffn * acc


def _final_norm(x, g):
    return x * lax.rsqrt(jnp.mean(x * x, axis=-1, keepdims=True) + EPS) * g


def _outproj_even_kernel(x_ref, ys_ref, m_ref, og_ref, gm_ref, sh_ref, sc_ref, gf_ref, g2_ref, gw_ref, gbias_ref,
                         ng_ref, wos_ref, wom_ref, w1_ref, w3_ref, w2_ref, o_ref, *, hidden_chunk, heads):
    ys = jax.nn.gelu(ys_ref[0])
    s = ys * jax.nn.sigmoid(_dot(ys.astype(BF16), gw_ref[...]) + gbias_ref[...])
    dh = m_ref.shape[-1] // heads
    gated = []
    for h in range(heads):
        cols = slice(h * dh, (h + 1) * dh)
        hh = m_ref[0, :, cols].astype(F32)
        mu = jnp.mean(hh, axis=1, keepdims=True)
        cen = hh - mu
        var = jnp.mean(cen * cen, axis=1, keepdims=True)
        normed = cen * lax.rsqrt(var + EPS) * ng_ref[:, cols]
        gated.append((jax.nn.sigmoid(og_ref[0, :, cols]) * normed).astype(BF16))
    y = _dot(s.astype(BF16), wos_ref[...]) + _dot(jnp.concatenate(gated, axis=1), wom_ref[...])
    o_ref[0] = _ffn_tail(x_ref[0], y, gm_ref[0], g2_ref[...], sh_ref[0], sc_ref[0], gf_ref[0],
                         w1_ref, w3_ref, w2_ref, hidden_chunk)


def _outproj_odd_kernel(x_ref, a_ref, gate_ref, gm_ref, sh_ref, sc_ref, gf_ref, g2_ref, ng_ref, wo_ref, w1_ref,
                        w3_ref, w2_ref, fg_ref, o_ref, *, hidden_chunk, heads):
    hv = a_ref.shape[-1] // heads
    y = jnp.zeros(x_ref.shape[1:], F32)
    for h in range(heads):
        cols = slice(h * hv, (h + 1) * hv)
        o = a_ref[0, :, cols].astype(F32)
        mu = jnp.mean(o, axis=1, keepdims=True)
        cen = o - mu
        var = jnp.mean(cen * cen, axis=1, keepdims=True)
        normed = cen * lax.rsqrt(var + EPS) * ng_ref[:, cols]
        gated = (_silu(gate_ref[0, :, cols].astype(F32)) * normed).astype(BF16)
        y = y + _dot(gated, wo_ref[cols, :])
    out = _ffn_tail(x_ref[0], y, gm_ref[0], g2_ref[...], sh_ref[0], sc_ref[0], gf_ref[0],
                    w1_ref, w3_ref, w2_ref, hidden_chunk)
    o_ref[0] = _final_norm(out, fg_ref[...])


def _hidden_chunk(hidden):
    for c in (1408, 1024, 512, 256, 128):
        if hidden % c == 0:
            return c
    return hidden


def _mod_specs(d, row_of_batch, slots):
    return [pl.BlockSpec((1, 1, d), lambda b, i, s=s: (row_of_batch(b) * N_MOD + s, 0, 0)) for s in slots]


def _outproj_even(x, ys, m, p, mods, row_of_batch, g2, glu_w, glu_b, norm_g, w_out, w1, w3, w2):
    bn, ln, d = x.shape
    sw, mw = ys.shape[-1], m.shape[-1]
    tm = _token_tile(ln)
    res = lambda a: _resident(a.shape, lambda b, i: (0,) * a.ndim)
    tok = lambda w: pl.BlockSpec((1, tm, w), lambda b, i: (b, i, 0))
    row = lambda w: pl.BlockSpec((1, w), lambda b, i: (0, 0))
    wos, wom = w_out[:sw], w_out[sw:]
    return pl.pallas_call(
        functools.partial(_outproj_even_kernel, hidden_chunk=_hidden_chunk(w1.shape[1]), heads=ML_HEADS),
        grid=(bn, ln // tm),
        in_specs=[tok(d), tok(sw), tok(mw), pl.BlockSpec((1, tm, mw), lambda b, i: (b, i, 1))]
        + _mod_specs(d, row_of_batch, (2, 3, 4, 5))
        + [row(d), res(glu_w), row(sw), row(mw), res(wos), res(wom), res(w1), res(w3), res(w2)],
        out_specs=tok(d),
        out_shape=jax.ShapeDtypeStruct((bn, ln, d), F32),
        compiler_params=_cparams(("arbitrary", "arbitrary")),
        name="outproj_ffn_even",
    )(x, ys, m, p, mods, mods, mods, mods, g2, glu_w, glu_b.reshape(1, sw), norm_g.reshape(1, mw),
      wos, wom, w1, w3, w2)


def _outproj_odd(x, a, gate, mods, row_of_batch, g2, norm_g, w_out, w1, w3, w2, final_g):
    bn, ln, d = x.shape
    vw = a.shape[-1]
    tm = _token_tile(ln)
    res = lambda a_: _resident(a_.shape, lambda b, i: (0,) * a_.ndim)
    tok = lambda w: pl.BlockSpec((1, tm, w), lambda b, i: (b, i, 0))
    row = lambda w: pl.BlockSpec((1, w), lambda b, i: (0, 0))
    return pl.pallas_call(
        functools.partial(_outproj_odd_kernel, hidden_chunk=_hidden_chunk(w1.shape[1]), heads=RET_HEADS),
        grid=(bn, ln // tm),
        in_specs=[tok(d), tok(vw), tok(vw)] + _mod_specs(d, row_of_batch, (2, 3, 4, 5))
        + [row(d), row(vw), res(w_out), res(w1), res(w3), res(w2), row(d)],
        out_specs=tok(d),
        out_shape=jax.ShapeDtypeStruct((bn, ln, d), F32),
        compiler_params=_cparams(("arbitrary", "arbitrary")),
        name="outproj_ffn_odd",
    )(x, a, gate, mods, mods, mods, mods, g2, norm_g.reshape(1, vw), w_out, w1, w3, w2, final_g)


def _rope_heads(t, cos, sin, heads, hk):
    half = hk // 2
    out = []
    for h in range(heads):
        t1 = t[:, h * hk:h * hk + half]
        t2 = t[:, h * hk + half:(h + 1) * hk]
        out.append(t1 * cos - t2 * sin)
        out.append(t2 * cos + t1 * sin)
    return jnp.concatenate(out, axis=-1)


def _inproj_odd_kernel(x_ref, sh_ref, sc_ref, g_ref, cos_ref, sin_ref, w_ref, q_ref, k_ref, v_ref, gate_ref,
                       *, qk, vw, heads):
    h = _norm_mod(x_ref[0], g_ref[...], sh_ref[0], sc_ref[0]).astype(BF16)
    hk = qk // heads
    cos, sin = cos_ref[...], sin_ref[...]
    q = _rope_heads(_dot(h, w_ref[:, 0:qk]), cos, sin, heads, hk)
    q_ref[0] = (q * (hk ** -0.5)).astype(BF16)
    k = _rope_heads(_dot(h, w_ref[:, qk:2 * qk]), cos, sin, heads, hk)
    k_ref[0] = k.astype(BF16)
    v_ref[0] = _dot(h, w_ref[:, 2 * qk:2 * qk + vw]).astype(BF16)
    gate_ref[0] = _dot(h, w_ref[:, 2 * qk + vw:2 * qk + 2 * vw]).astype(BF16)


def _inproj_odd_ctx_kernel(x_ref, sh_ref, sc_ref, g_ref, w_ref, k_ref, v_ref, *, qk, vw):
    h = _norm_mod(x_ref[0], g_ref[...], sh_ref[0], sc_ref[0]).astype(BF16)
    k_ref[0] = _dot(h, w_ref[:, 0:qk]).astype(BF16)
    v_ref[0] = _dot(h, w_ref[:, qk:qk + vw]).astype(BF16)


def _inproj_odd(x, mods, row_of_batch, g, cos, sin, w, qk, vw):
    bn, ln, d = x.shape
    tm = _token_tile(ln)
    tok = lambda w_: pl.BlockSpec((1, tm, w_), lambda b, i: (b, i, 0))
    half = cos.shape[1]
    return pl.pallas_call(
        functools.partial(_inproj_odd_kernel, qk=qk, vw=vw, heads=RET_HEADS),
        grid=(bn, ln // tm),
        in_specs=[tok(d)] + _mod_specs(d, row_of_batch, (0, 1))
        + [pl.BlockSpec((1, d), lambda b, i: (0, 0)),
           pl.BlockSpec((tm, half), lambda b, i: (i, 0)), pl.BlockSpec((tm, half), lambda b, i: (i, 0)),
           _resident(w.shape, lambda b, i: (0, 0))],
        out_specs=[tok(qk), tok(qk), tok(vw), tok(vw)],
        out_shape=[jax.ShapeDtypeStruct((bn, ln, qk), BF16), jax.ShapeDtypeStruct((bn, ln, qk), BF16),
                   jax.ShapeDtypeStruct((bn, ln, vw), BF16), jax.ShapeDtypeStruct((bn, ln, vw), BF16)],
        compiler_params=_cparams(("arbitrary", "arbitrary")),
        name="inproj_odd",
    )(x, mods, mods, g, cos, sin, w)


def _inproj_odd_ctx(x, mods, row_of_batch, g, w_kv, qk, vw):
    bn, ln, d = x.shape
    tm = _token_tile(ln)
    tok = lambda w_: pl.BlockSpec((1, tm, w_), lambda b, i: (b, i, 0))
    return pl.pallas_call(
        functools.partial(_inproj_odd_ctx_kernel, qk=qk, vw=vw),
        grid=(bn, ln // tm),
        in_specs=[tok(d)] + _mod_specs(d, row_of_batch, (0, 1))
        + [pl.BlockSpec((1, d), lambda b, i: (0, 0)), _resident(w_kv.shape, lambda b, i: (0, 0))],
        out_specs=[tok(qk), tok(vw)],
        out_shape=[jax.ShapeDtypeStruct((bn, ln, qk), BF16), jax.ShapeDtypeStruct((bn, ln, vw), BF16)],
        compiler_params=_cparams(("arbitrary", "arbitrary")),
        name="inproj_odd_ctx",
    )(x, mods, mods, g, w_kv)


def _retention_kernel(lg_ref, q_ref, k_ref, v_ref, kc_ref, vc_ref, o_ref, acc_s, sf_s, sb_s,
                      *, chunk, n_ctx_rows, n_lat_rows):
    t_n = chunk
    h = pl.program_id(1)
    lgf = jnp.full((1, 1), lg_ref[0, h], F32)
    lgb = jnp.full((1, 1), lg_ref[1, h], F32)
    nc_ctx, nc_lat = n_ctx_rows // t_n, n_lat_rows // t_n
    ti = lax.broadcasted_iota(jnp.int32, (t_n, t_n), 0)
    si = lax.broadcasted_iota(jnp.int32, (t_n, t_n), 1)
    diff = (ti - si).astype(F32)
    decay = jnp.where(diff >= 0, jnp.exp(lgf * jnp.maximum(diff, 0.0)), jnp.exp(lgb * jnp.maximum(-diff, 0.0)))
    pos = lax.broadcasted_iota(jnp.int32, (t_n, 1), 0).astype(F32)
    inter_f = jnp.exp(lgf * (pos + 1.0))
    inter_b = jnp.exp(lgb * (t_n - pos))
    wend_f = jnp.exp(lgf * (t_n - 1.0 - pos))
    wend_b = jnp.exp(lgb * pos)
    cd_f = jnp.exp(lgf * t_n)
    cd_b = jnp.exp(lgb * t_n)

    def rows_of(c):
        return pl.ds(pl.multiple_of(c * t_n, t_n), t_n)

    def bump(s_ref, k, v, wend, cd):
        kw = (k.astype(F32) * wend).astype(BF16)
        s_ref[...] = cd * s_ref[...] + _dot_tn(kw, v)

    sf_s[...] = jnp.zeros_like(sf_s)
    sb_s[...] = jnp.zeros_like(sb_s)

    def ctx_f(c, _):
        bump(sf_s, kc_ref[0, rows_of(c), :], vc_ref[0, rows_of(c), :], wend_f, cd_f)
        return 0
    lax.fori_loop(0, nc_ctx, ctx_f, 0)

    def ctx_b(i, _):
        c = nc_ctx - 1 - i
        bump(sb_s, kc_ref[0, rows_of(c), :], vc_ref[0, rows_of(c), :], wend_b, cd_b)
        return 0
    lax.fori_loop(0, nc_ctx, ctx_b, 0)

    def forward_part(rows):
        q, k, v = q_ref[0, rows, :], k_ref[0, rows, :], v_ref[0, rows, :]
        scores = (_dot_nt(q, k) * decay).astype(BF16)
        part = _dot(scores, v) + inter_f * _dot(q, sf_s[...].astype(BF16))
        bump(sf_s, k, v, wend_f, cd_f)
        return part

    def backward_part(rows):
        q, k, v = q_ref[0, rows, :], k_ref[0, rows, :], v_ref[0, rows, :]
        part = inter_b * _dot(q, sb_s[...].astype(BF16))
        bump(sb_s, k, v, wend_b, cd_b)
        return part

    def first_half(i, _):
        rf, rb = rows_of(i), rows_of(nc_lat - 1 - i)
        acc_s[rf, :] = forward_part(rf)
        acc_s[rb, :] = backward_part(rb)
        return 0
    lax.fori_loop(0, nc_lat // 2, first_half, 0)

    def second_half(i, _):
        rf, rb = rows_of(i), rows_of(nc_lat - 1 - i)
        o_ref[0, rf, :] = (acc_s[rf, :] + forward_part(rf)).astype(BF16)
        o_ref[0, rb, :] = (acc_s[rb, :] + backward_part(rb)).astype(BF16)
        return 0
    lax.fori_loop(nc_lat // 2, nc_lat, second_half, 0)


def _retention_mix(q, k, v, k_ctx, v_ctx, log_gamma):
    bn, ll, qk = q.shape
    lc = k_ctx.shape[1]
    vw = v.shape[-1]
    hk, hv = qk // RET_HEADS, vw // RET_HEADS
    chunk = 256 if (ll % 512 == 0 and lc % 256 == 0) else 128
    assert (ll // chunk) % 2 == 0, "latent chunks are visited in forward/backward pairs"
    head = lambda n_rows, w: pl.BlockSpec((1, n_rows, w), lambda b, h: (b, 0, h))
    return pl.pallas_call(
        functools.partial(_retention_kernel, chunk=chunk, n_ctx_rows=lc, n_lat_rows=ll),
        grid=(bn, RET_HEADS),
        in_specs=[
            pl.BlockSpec(memory_space=pltpu.SMEM),
            head(ll, hk), head(ll, hk), head(ll, hv), head(lc, hk), head(lc, hv),
        ],
        out_specs=head(ll, hv),
        out_shape=jax.ShapeDtypeStruct((bn, ll, vw), BF16),
        scratch_shapes=[pltpu.VMEM((ll, hv), F32), pltpu.VMEM((hk, hv), F32), pltpu.VMEM((hk, hv), F32)],
        compiler_params=_cparams(("arbitrary", "arbitrary")),
        name="retention_mix",
    )(log_gamma.astype(F32), q, k, v, k_ctx, v_ctx)


def _grid_rope(n_pos, hk):
    rows = n_pos // GRID_W
    row = jnp.repeat(jnp.arange(rows, dtype=F32), GRID_W)
    col = jnp.tile(jnp.arange(GRID_W, dtype=F32), rows)
    n_freq = hk // 4
    inv = ROPE_BASE ** (-jnp.arange(n_freq, dtype=F32) / n_freq)
    ang = jnp.concatenate([row[:, None] * inv, col[:, None] * inv], -1)
    return jnp.cos(ang), jnp.sin(ang)


def kernel(x, c, ctx, c_ctx, ada_w, ada_b, norm1_g, norm2_g, ab_w_in, ab_w_out, s5_a_re, s5_a_im, s5_log_dt,
           s5_b_re, s5_b_im, s5_c_re, s5_c_im, s5_d, s5_glu_w, s5_glu_b, ml_conv_w, ml_conv_b, ml_wq, ml_wk,
           ml_wv, ml_gate_b, ml_norm_g, ret_w_in, ret_w_out, ret_log_gamma, ret_norm_g, ffn_w1, ffn_w3, ffn_w2,
           final_g):
    bn, ln, d = x.shape
    depth = ada_w.shape[0]
    assert depth == 2, "one S5 || mLSTM layer followed by one retention layer"
    s5_width = s5_d.shape[-1]
    ml_width = ml_norm_g.shape[-1]
    assert ml_width == ML_HEADS * LANES and s5_width % S5_GROUP == 0

    r_pad = -(-(bn + 1) // SUBLANES) * SUBLANES
    vec = jnp.zeros((r_pad, d), F32).at[:bn].set(c).at[bn].set(c_ctx)
    mods_all = _modulation(vec, ada_w, ada_b).reshape(depth, r_pad * N_MOD, 1, d)
    lat_row = lambda b: b
    ctx_row = lambda b: bn

    mods = mods_all[0]
    g1 = norm1_g[0].reshape(1, d)
    g2 = norm2_g[0].reshape(1, d)
    w_in = ab_w_in[0]
    n_gate = w_in.shape[1] - s5_width - 2 * ml_width
    w_in = jnp.pad(w_in, ((0, 0), (0, LANES - n_gate))).astype(BF16)
    u_lat, p_lat = _inproj_even(x, mods, lat_row, g1, w_in, s5_width)
    u_ctx, p_ctx = _inproj_even(ctx, mods, ctx_row, g1, w_in, s5_width)
    mats = _s5_matrices(s5_a_re[0], s5_a_im[0], s5_log_dt[0], s5_b_re[0], s5_b_im[0], s5_c_re[0], s5_c_im[0],
                        s5_d[0])
    ys_ctx, ys_lat = _s5_mix(u_ctx, u_lat, mats)
    m_ctx, m_lat = _mlstm_mix(p_ctx, p_lat, ml_conv_w[0], ml_conv_b[0], ml_wq[0], ml_wk[0], ml_wv[0],
                              ml_gate_b[0])
    glu_w = s5_glu_w[0].astype(BF16)
    w_out = ab_w_out[0].astype(BF16)
    w1, w3, w2 = ffn_w1[0].astype(BF16), ffn_w3[0].astype(BF16), ffn_w2[0].astype(BF16)
    x = _outproj_even(x, ys_lat, m_lat, p_lat, mods, lat_row, g2, glu_w, s5_glu_b[0], ml_norm_g[0],
                      w_out, w1, w3, w2)
    ctx = _outproj_even(ctx, ys_ctx, m_ctx, p_ctx, mods, ctx_row, g2, glu_w, s5_glu_b[0], ml_norm_g[0],
                        w_out, w1, w3, w2)

    mods = mods_all[1]
    g1 = norm1_g[1].reshape(1, d)
    g2 = norm2_g[1].reshape(1, d)
    vw = ret_norm_g.shape[-1]
    qk = (ret_w_in.shape[-1] - 2 * vw) // 2
    w_in = ret_w_in[0].astype(BF16)
    cos, sin = _grid_rope(ln, qk // RET_HEADS)
    q, k, v, gate = _inproj_odd(x, mods, lat_row, g1, cos, sin, w_in, qk, vw)
    k_ctx, v_ctx = _inproj_odd_ctx(ctx, mods, ctx_row, g1, w_in[:, qk:2 * qk + vw], qk, vw)
    a = _retention_mix(q, k, v, k_ctx, v_ctx, ret_log_gamma[0])
    w_out = ret_w_out[0].astype(BF16)
    w1, w3, w2 = ffn_w1[1].astype(BF16), ffn_w3[1].astype(BF16), ffn_w2[1].astype(BF16)
    return _outproj_odd(x, a, gate, mods, lat_row, g2, ret_norm_g[0], w_out, w1, w3, w2, final_g.reshape(1, d))
```

```python
import functools
import math

import jax
import jax.numpy as jnp
from jax import lax
from jax.experimental import pallas as pl
from jax.experimental.pallas import tpu as pltpu

F32 = jnp.float32
BF16 = jnp.bfloat16

EPS = 1e-6
N_MOD = 6
GRID_W = 64
ROPE_BASE = 10000.0

S5_GROUP = 16
S5_STATE = 64
S5_STEP = 16
ML_HEADS = 4
ML_CHUNK = 128
RET_HEADS = 4
SUBLANES = 8
LANES = 128
NEG_BIG = -1e30
VMEM_LIMIT = 56 * 1024 * 1024


def _cparams(sem):
    return pltpu.CompilerParams(dimension_semantics=sem, vmem_limit_bytes=VMEM_LIMIT)


def _resident(shape, index_map):
    return pl.BlockSpec(shape, index_map, pipeline_mode=pl.Buffered(1))


def _token_tile(n):
    for t in (512, 256, 128):
        if n % t == 0:
            return t
    raise ValueError(f"sequence length {n} must be a multiple of 128")


def _silu(v):
    return v * jax.nn.sigmoid(v)


def _norm_mod(x, g, shift, scale):
    y = x * lax.rsqrt(jnp.mean(x * x, axis=-1, keepdims=True) + EPS)
    return (y * g) * (1.0 + scale) + shift


def _dot(a, b):
    return jnp.dot(a, b, preferred_element_type=F32)


def _dot_nt(a, b):
    return lax.dot_general(a, b, (((1,), (1,)), ((), ())), preferred_element_type=F32)


def _dot_tn(a, b):
    return lax.dot_general(a, b, (((0,), (0,)), ((), ())), preferred_element_type=F32)


def _mod_kernel(v_ref, w_ref, b_ref, o_ref):
    s = _silu(v_ref[...]).astype(BF16)
    o_ref[0] = _dot(s, w_ref[0].astype(BF16)) + b_ref[0]


def _modulation(vec, ada_w, ada_b):
    depth, d, n = ada_w.shape
    r = vec.shape[0]
    tn = 1024
    return pl.pallas_call(
        _mod_kernel,
        grid=(depth, n // tn),
        in_specs=[
            pl.BlockSpec((r, d), lambda l, j: (0, 0)),
            pl.BlockSpec((1, d, tn), lambda l, j: (l, 0, j)),
            pl.BlockSpec((1, 1, tn), lambda l, j: (l, 0, j)),
        ],
        out_specs=pl.BlockSpec((1, r, tn), lambda l, j: (l, 0, j)),
        out_shape=jax.ShapeDtypeStruct((depth, r, n), F32),
        compiler_params=_cparams(("arbitrary", "arbitrary")),
        name="adaln_modulation",
    )(vec, ada_w, ada_b.reshape(depth, 1, n))


def _inproj_even_kernel(x_ref, sh_ref, sc_ref, g_ref, w_ref, u_ref, p_ref, *, s5_width):
    h = _norm_mod(x_ref[0], g_ref[...], sh_ref[0], sc_ref[0]).astype(BF16)
    p = _dot(h, w_ref[...])
    u_ref[0] = p[:, :s5_width]
    p_ref[0] = p[:, s5_width:]


def _inproj_even(x, mods, row_of_batch, g, w, s5_width):
    bn, ln, d = x.shape
    n = w.shape[1]
    tm = _token_tile(ln)
    return pl.pallas_call(
        functools.partial(_inproj_even_kernel, s5_width=s5_width),
        grid=(bn, ln // tm),
        in_specs=[
            pl.BlockSpec((1, tm, d), lambda b, i: (b, i, 0)),
            pl.BlockSpec((1, 1, d), lambda b, i: (row_of_batch(b) * N_MOD + 0, 0, 0)),
            pl.BlockSpec((1, 1, d), lambda b, i: (row_of_batch(b) * N_MOD + 1, 0, 0)),
            pl.BlockSpec((1, d), lambda b, i: (0, 0)),
            _resident((d, n), lambda b, i: (0, 0)),
        ],
        out_specs=[
            pl.BlockSpec((1, tm, s5_width), lambda b, i: (b, i, 0)),
            pl.BlockSpec((1, tm, n - s5_width), lambda b, i: (b, i, 0)),
        ],
        out_shape=[
            jax.ShapeDtypeStruct((bn, ln, s5_width), F32),
            jax.ShapeDtypeStruct((bn, ln, n - s5_width), F32),
        ],
        compiler_params=_cparams(("arbitrary", "arbitrary")),
        name="inproj_even",
    )(x, mods, mods, g, w)


def _s5_response_kernel(c_ref, w_ref, o_ref):
    for n in range(c_ref.shape[0]):
        o_ref[n] = lax.dot_general(c_ref[n], w_ref[n], (((1,), (1,)), ((), ())),
                                   precision=lax.Precision.HIGHEST, preferred_element_type=F32)


def _s5_response(c_cat, w_cat):
    n, co, kk = c_cat.shape
    r = w_cat.shape[1]
    nb = 8
    return pl.pallas_call(
        _s5_response_kernel,
        grid=(n // nb,),
        in_specs=[pl.BlockSpec((nb, co, kk), lambda i: (i, 0, 0)), pl.BlockSpec((nb, r, kk), lambda i: (i, 0, 0))],
        out_specs=pl.BlockSpec((nb, co, r), lambda i: (i, 0, 0)),
        out_shape=jax.ShapeDtypeStruct((n, co, r), F32),
        compiler_params=_cparams(("arbitrary",)),
        name="s5_response",
    )(c_cat, w_cat)


def _s5_matrices(a_re, a_im, log_dt, b_re, b_im, c_re, c_im, d_skip):
    k = S5_STEP
    a_re, a_im = a_re.astype(F32), a_im.astype(F32)
    dt = jnp.exp(log_dt.astype(F32))[..., None]
    lam_re, lam_im = a_re * dt, a_im * dt
    steps = jnp.arange(k + 1, dtype=F32)[:, None, None, None]
    mag = jnp.exp(steps * lam_re)
    pr, pi = mag * jnp.cos(steps * lam_im), mag * jnp.sin(steps * lam_im)
    den = a_re * a_re + a_im * a_im
    nr = pr[1] - 1.0
    coef_re = (nr * a_re + pi[1] * a_im) / den
    coef_im = (pi[1] * a_re - nr * a_im) / den
    bt_re = jnp.swapaxes(b_re.astype(F32), -1, -2)
    bt_im = jnp.swapaxes(b_im.astype(F32), -1, -2)
    bb_re = coef_re[:, :, None] * bt_re - coef_im[:, :, None] * bt_im
    bb_im = coef_re[:, :, None] * bt_im + coef_im[:, :, None] * bt_re
    wr = pr[:, :, :, None] * bb_re - pi[:, :, :, None] * bb_im
    wi = pr[:, :, :, None] * bb_im + pi[:, :, :, None] * bb_re
    c_re, c_im = c_re.astype(F32), c_im.astype(F32)
    g_n, c_n, p_n = a_re.shape[1], b_re.shape[-1], a_re.shape[-1]
    w_cat = jnp.transpose(jnp.concatenate([wr, -wi], -1), (1, 2, 0, 3, 4)).reshape(2 * g_n, (k + 1) * c_n, 2 * p_n)
    c_cat = jnp.concatenate([c_re, c_im], -1).reshape(2 * g_n, c_n, 2 * p_n)
    resp = _s5_response(c_cat, w_cat).reshape(2, g_n, c_n, k + 1, c_n)
    skip = (jnp.eye(c_n, dtype=F32) * d_skip.astype(F32).reshape(g_n, 1, c_n))[:, :, None, :]
    gen = jnp.concatenate([resp[0, :, :, k - 1:0:-1], resp[0, :, :, 0:1] + resp[1, :, :, 0:1] + skip,
                           resp[1, :, :, 1:k]], axis=2).reshape(g_n, c_n, (2 * k - 1) * c_n)
    toep = jnp.stack([gen[:, :, (k - 1 - t) * c_n:(2 * k - 1 - t) * c_n] for t in range(k)], axis=1)
    toep = toep.reshape(g_n, k * c_n, k * c_n)
    endw = jnp.concatenate([wr[k - 1::-1, 0], wr[:k, 1], wi[k - 1::-1, 0], wi[:k, 1]], -1)
    endw = jnp.transpose(endw, (1, 0, 2, 3)).reshape(g_n, k * c_n, 4 * p_n)

    def out_cols(d, powers_re, powers_im):
        cr, ci = c_re[d][None], c_im[d][None]
        pre, pim = powers_re[:, :, None, :], powers_im[:, :, None, :]
        return cr * pre - ci * pim, -(cr * pim + ci * pre)

    of_re, of_im = out_cols(0, pr[1:, 0], pi[1:, 0])
    ob_re, ob_im = out_cols(1, pr[k:0:-1, 1], pi[k:0:-1, 1])
    outw_t = jnp.concatenate([of_re, ob_re, of_im, ob_im], -1)
    outw_t = jnp.transpose(outw_t, (1, 0, 2, 3)).reshape(g_n, k * c_n, 4 * p_n)
    a0 = jnp.concatenate([pr[k, 0], pr[k, 1]], -1)
    a1 = jnp.concatenate([pi[k, 0], pi[k, 1]], -1)
    return toep.astype(BF16), endw.astype(BF16), outw_t.astype(BF16), a0, a1


S5_LANE_GROUPS = LANES // S5_GROUP


def _chunk_transpose(xs):
    n_arr = len(xs)
    lane_chunk = lax.broadcasted_iota(jnp.int32, xs[0].shape, 1) // S5_GROUP
    d = 1
    while d < n_arr:
        keep = (lane_chunk & d) == 0
        out = list(xs)
        for i in range(n_arr):
            if i & d:
                continue
            lo, hi = xs[i], xs[i + d]
            out[i] = jnp.where(keep, lo, pltpu.roll(hi, d * S5_GROUP, 1))
            out[i + d] = jnp.where(keep, pltpu.roll(lo, LANES - d * S5_GROUP, 1), hi)
        xs = out
        d *= 2
    return xs


def _s5_kernel(uc_ref, ul_ref, toep_ref, endw_ref, outw_ref, a0_ref, a1_ref, yc_ref, yl_ref,
               uc_s, ul_s, e_s, yc_s, yl_s, *, batch, n_ctx, n_lat):
    p = S5_STATE
    gl = S5_LANE_GROUPS
    k = S5_STEP

    def stack(u_ref, us_ref, n_blk):
        for b in range(batch):
            for half in range(k // gl):
                xs = [u_ref[b, pl.ds(half * gl + s, n_blk, stride=k), :] for s in range(gl)]
                ys = _chunk_transpose(xs)
                for g in range(gl):
                    us_ref[g, b * n_blk:(b + 1) * n_blk, half * LANES:(half + 1) * LANES] = ys[g].astype(BF16)

    stack(uc_ref, uc_s, n_ctx)
    stack(ul_ref, ul_s, n_lat)

    def increments(us_ref, n_blk, first_blk):
        for g in range(gl):
            e = _dot(us_ref[g], endw_ref[g])
            for b in range(batch):
                for part in range(2):
                    e_s[part, b, pl.ds(first_blk * gl + g, n_blk, stride=gl), :] = (
                        e[b * n_blk:(b + 1) * n_blk, part * LANES:(part + 1) * LANES])

    increments(uc_s, n_ctx, 0)
    increments(ul_s, n_lat, n_ctx)

    a0 = a0_ref[...]
    a1 = a1_ref[...]
    fwd_lane = lax.broadcasted_iota(jnp.int32, (batch, gl, 2 * p), 2) < p

    def scan(first_blk, n, carry):
        def body(i, c):
            s0, s1 = c
            rf = pl.ds(pl.multiple_of((first_blk + i) * gl, gl), gl)
            rb = pl.ds(pl.multiple_of((first_blk + n - 1 - i) * gl, gl), gl)
            e0 = jnp.where(fwd_lane, e_s[0, :, rf, :], e_s[0, :, rb, :])
            e1 = jnp.where(fwd_lane, e_s[1, :, rf, :], e_s[1, :, rb, :])
            e_s[0, :, rf, 0:p] = s0[:, :, 0:p]
            e_s[0, :, rb, p:2 * p] = s0[:, :, p:2 * p]
            e_s[1, :, rf, 0:p] = s1[:, :, 0:p]
            e_s[1, :, rb, p:2 * p] = s1[:, :, p:2 * p]
            return a0 * s0 - a1 * s1 + e0, a0 * s1 + a1 * s0 + e1
        return lax.fori_loop(0, n, body, carry)

    zero = jnp.zeros((batch, gl, 2 * p), F32)
    carry = scan(0, n_ctx, (zero, zero))
    scan(n_ctx, n_lat, carry)

    def outputs(us_ref, ys_ref, y_ref, n_blk, first_blk):
        for g in range(gl):
            state = jnp.concatenate(
                [jnp.concatenate([e_s[part, b, pl.ds(first_blk * gl + g, n_blk, stride=gl), :]
                                  for b in range(batch)], axis=0) for part in range(2)], axis=1)
            ys_ref[g] = _dot_nt(us_ref[g], toep_ref[g]) + _dot_nt(state.astype(BF16), outw_ref[g])
        for b in range(batch):
            for half in range(k // gl):
                ys = [ys_ref[g, b * n_blk:(b + 1) * n_blk, half * LANES:(half + 1) * LANES] for g in range(gl)]
                xs = _chunk_transpose(ys)
                for s in range(gl):
                    y_ref[b, pl.ds(half * gl + s, n_blk, stride=k), :] = xs[s]

    outputs(uc_s, yc_s, yc_ref, n_ctx, 0)
    outputs(ul_s, yl_s, yl_ref, n_lat, n_ctx)


def _s5_mix(u_ctx, u_lat, mats):
    toep, endw, outw, a0, a1 = mats
    bn, lc, w = u_ctx.shape
    ll = u_lat.shape[1]
    gl = S5_LANE_GROUPS
    kc = S5_STEP * S5_GROUP
    n_ctx, n_lat = lc // S5_STEP, ll // S5_STEP
    bb = 4 if bn % 4 == 0 else bn
    seq = lambda n_rows: pl.BlockSpec((bb, n_rows, LANES), lambda i, j: (j, 0, i))
    grp = lambda r, c: pl.BlockSpec((gl, r, c), lambda i, j: (i, 0, 0))
    return pl.pallas_call(
        functools.partial(_s5_kernel, batch=bb, n_ctx=n_ctx, n_lat=n_lat),
        grid=(w // LANES, bn // bb),
        in_specs=[seq(lc), seq(ll), grp(kc, kc), grp(kc, kc), grp(kc, kc),
                  pl.BlockSpec((gl, 2 * S5_STATE), lambda i, j: (i, 0)),
                  pl.BlockSpec((gl, 2 * S5_STATE), lambda i, j: (i, 0))],
        out_specs=[seq(lc), seq(ll)],
        out_shape=[jax.ShapeDtypeStruct((bn, lc, w), F32), jax.ShapeDtypeStruct((bn, ll, w), F32)],
        scratch_shapes=[
            pltpu.VMEM((gl, bb * n_ctx, kc), BF16), pltpu.VMEM((gl, bb * n_lat, kc), BF16),
            pltpu.VMEM((2, bb, (n_ctx + n_lat) * gl, LANES), F32),
            pltpu.VMEM((gl, bb * n_ctx, kc), F32), pltpu.VMEM((gl, bb * n_lat, kc), F32),
        ],
        compiler_params=_cparams(("arbitrary", "arbitrary")),
        name="s5_scan",
    )(u_ctx, u_lat, toep, endw, outw, a0, a1)


def _mlstm_kernel(pc_ref, pl_ref, cw_ref, cb_ref, wq_ref, wkt_ref, wv_ref, gb_ref, mc_ref, ml_ref,
                  q_s, kt_s, vx_s, h_s, st_s, bcol_s, mcol_s, rrow_s, xpad_s, *, n_ctx_rows, n_lat_rows):
    t_n = ML_CHUNK
    width = ML_HEADS * LANES
    nc_ctx, nc_lat = n_ctx_rows // t_n, n_lat_rows // t_n
    nc = nc_ctx + nc_lat
    scale_k = LANES ** -0.5

    def project(p_ref, n_rows, base):
        ones = jnp.ones((n_rows, LANES), BF16)
        pad = SUBLANES
        xpad_s[0:pad, :] = jnp.zeros((pad, LANES), F32)
        xpad_s[pad + n_rows:2 * pad + n_rows, :] = jnp.zeros((pad, LANES), F32)
        for h in range(ML_HEADS):
            cols = slice(h * LANES, (h + 1) * LANES)
            xm = p_ref[0, :, cols]
            xpad_s[pad:pad + n_rows, :] = xm
            taps = cw_ref[:, cols]
            n_tap = taps.shape[0]
            assert n_tap // 2 <= pad
            acc = jnp.zeros_like(xm) + cb_ref[:, cols]
            for j in range(n_tap):
                d = j - n_tap // 2
                sh = xm if d == 0 else xpad_s[pad + d:pad + d + n_rows, :]
                acc = acc + sh * taps[j:j + 1, :]
            xc = _silu(acc).astype(BF16)
            q_s[base:base + n_rows, cols] = _dot(xc, wq_ref[h]).astype(BF16)
            kt = (_dot_nt(wkt_ref[h], xc) * scale_k).astype(BF16)
            for c in range(n_rows // t_n):
                kt_s[base // t_n + c, cols, :] = kt[:, c * t_n:(c + 1) * t_n]
            vx_s[base:base + n_rows, 2 * h * LANES:(2 * h + 1) * LANES] = (
                _dot(xm.astype(BF16), wv_ref[h]).astype(BF16))
            vx_s[base:base + n_rows, (2 * h + 1) * LANES:(2 * h + 2) * LANES] = ones

    project(pc_ref, n_ctx_rows, 0)
    project(pl_ref, n_lat_rows, n_ctx_rows)

    ti = lax.broadcasted_iota(jnp.int32, (t_n, t_n), 0)
    si = lax.broadcasted_iota(jnp.int32, (t_n, t_n), 1)
    tri_f = (si <= ti).astype(BF16)
    tri_b = (si >= ti).astype(BF16)
    lane = lax.broadcasted_iota(jnp.int32, (t_n, LANES), 1)
    trow = lax.broadcasted_iota(jnp.int32, (t_n, LANES), 0)
    gate_off = 2 * width

    def gate_prep(p_ref, n_chunks, base_chunk):
        def body(c, _):
            rows = pl.ds(pl.multiple_of(c * t_n, t_n), t_n)
            gcol = p_ref[0, rows, gate_off:gate_off + LANES] + gb_ref[...]
            lf = jax.nn.log_sigmoid(gcol)
            hi = lf.astype(BF16)
            lo = (lf - hi.astype(F32)).astype(BF16)
            pre = _dot(tri_f, hi) + _dot(tri_f, lo)
            suf = _dot(tri_b, hi) + _dot(tri_b, lo)
            bsum = jnp.where(lane < 2 * ML_HEADS, pre, suf)
            rcol = gcol - pltpu.roll(bsum, LANES - ML_HEADS, 1)
            pmax, smax = rcol, rcol
            step = 1
            while step < t_n:
                pmax = jnp.maximum(pmax, jnp.where(trow >= step, pltpu.roll(pmax, step, 0), NEG_BIG))
                smax = jnp.maximum(smax, jnp.where(trow < t_n - step, pltpu.roll(smax, t_n - step, 0), NEG_BIG))
                step *= 2
            bcol_s[base_chunk + c] = bsum
            mcol_s[base_chunk + c] = jnp.where(lane < 2 * ML_HEADS, pmax, smax)
            rrow_s[base_chunk + c] = rcol.T[0:4 * ML_HEADS, :]
            return 0
        lax.fori_loop(0, n_chunks, body, 0)

    gate_prep(pc_ref, nc_ctx, 0)
    gate_prep(pl_ref, nc_lat, nc_ctx)

    def one_dir(c, h, d, m_in):
        cols = slice(h * LANES, (h + 1) * LANES)
        xcols = slice(2 * h * LANES, (2 * h + 2) * LANES)
        li = 2 * d * ML_HEADS + h
        mask = (si <= ti) if d == 0 else (si >= ti)
        last = t_n - 1 if d == 0 else 0
        rows = pl.ds(pl.multiple_of(c * t_n, t_n), t_n)
        q = q_s[rows, cols]
        kt = kt_s[c, cols, :]
        vx = vx_s[rows, xcols]
        r_row = rrow_s[c][li:li + 1, :]
        run_max = jnp.broadcast_to(mcol_s[c][:, li:li + 1], (t_n, LANES))
        b_rep = jnp.broadcast_to(bcol_s[c][:, li + ML_HEADS:li + ML_HEADS + 1], (t_n, LANES))
        mm = jnp.maximum(m_in, run_max)
        dmat = jnp.exp(jnp.where(mask, r_row - mm, NEG_BIG))
        sm = (dmat * _dot(q, kt)).astype(BF16)
        intra = _dot(sm, vx)
        cross = _dot(q, st_s[2 * h + d].astype(BF16))
        inter = jnp.exp(m_in - mm)
        num = intra[:, :LANES] + inter * cross[:, :LANES]
        den = intra[:, LANES:] + inter * cross[:, LANES:]
        h_s[rows, cols] = h_s[rows, cols] + num / jnp.maximum(jnp.abs(den), jnp.exp(-b_rep - mm))
        r_top = run_max[last:last + 1, :]
        b_last = b_rep[last:last + 1, :]
        g_max = b_last + r_top
        m_new = jnp.maximum(b_last + m_in, g_max)
        dec = jnp.exp(b_last + m_in - m_new)
        inj = jnp.exp(g_max - m_new)
        kw = (kt.astype(F32) * (jnp.exp(r_row - r_top) * inj)).astype(BF16)
        inc = _dot(kw, vx)
        st_s[2 * h + d] = jnp.concatenate([dec, dec], axis=1) * st_s[2 * h + d] + inc
        return m_new

    st_s[...] = jnp.zeros_like(st_s)
    h_s[...] = jnp.zeros_like(h_s)

    def body(i, carry):
        cb = jnp.where(i < nc_ctx, nc_ctx - 1 - i, nc + nc_ctx - 1 - i)
        return tuple(one_dir(i if d == 0 else cb, h, d, carry[2 * h + d])
                     for h in range(ML_HEADS) for d in range(2))

    zero = jnp.zeros((1, LANES), F32)
    lax.fori_loop(0, nc, body, (zero,) * (2 * ML_HEADS))

    mc_ref[0] = h_s[0:n_ctx_rows, :].astype(BF16)
    ml_ref[0] = h_s[n_ctx_rows:n_ctx_rows + n_lat_rows, :].astype(BF16)


def _mlstm_mix(p_ctx, p_lat, conv_w, conv_b, wq, wk, wv, gate_b):
    bn, lc, pw = p_ctx.shape
    ll = p_lat.shape[1]
    width = ML_HEADS * LANES
    lt = lc + ll
    nct = lt // ML_CHUNK
    gb = jnp.zeros((1, LANES), F32).at[0, :4 * ML_HEADS].set(gate_b.astype(F32))
    full2 = lambda a: pl.BlockSpec(a.shape, lambda b: (0,) * a.ndim)
    conv_b2 = conv_b.reshape(1, width)
    assert ML_CHUNK == LANES
    wq, wkt, wv = wq.astype(BF16), jnp.swapaxes(wk, 1, 2).astype(BF16), wv.astype(BF16)
    return pl.pallas_call(
        functools.partial(_mlstm_kernel, n_ctx_rows=lc, n_lat_rows=ll),
        grid=(bn,),
        in_specs=[
            pl.BlockSpec((1, lc, pw), lambda b: (b, 0, 0)),
            pl.BlockSpec((1, ll, pw), lambda b: (b, 0, 0)),
            full2(conv_w), full2(conv_b2), full2(wq), full2(wkt), full2(wv), full2(gb),
        ],
        out_specs=[
            pl.BlockSpec((1, lc, width), lambda b: (b, 0, 0)),
            pl.BlockSpec((1, ll, width), lambda b: (b, 0, 0)),
        ],
        out_shape=[jax.ShapeDtypeStruct((bn, lc, width), BF16), jax.ShapeDtypeStruct((bn, ll, width), BF16)],
        scratch_shapes=[
            pltpu.VMEM((lt, width), BF16),
            pltpu.VMEM((nct, width, ML_CHUNK), BF16),
            pltpu.VMEM((lt, 2 * width), BF16),
            pltpu.VMEM((lt, width), F32),
            pltpu.VMEM((2 * ML_HEADS, LANES, 2 * LANES), F32),
            pltpu.VMEM((nct, ML_CHUNK, LANES), F32), pltpu.VMEM((nct, ML_CHUNK, LANES), F32),
            pltpu.VMEM((nct, 4 * ML_HEADS, ML_CHUNK), F32),
            pltpu.VMEM((max(lc, ll) + 2 * SUBLANES, LANES), F32),
        ],
        compiler_params=_cparams(("arbitrary",)),
        name="mlstm_mix",
    )(p_ctx, p_lat, conv_w, conv_b2, wq, wkt, wv, gb)


def _ffn_tail(x, y, gate_mix, g2, shift, scale, gate_ffn, w1_ref, w3_ref, w2_ref, hidden_chunk):
    x1 = x + gate_mix * y
    h2 = _norm_mod(x1, g2, shift, scale).astype(BF16)
    hidden = w1_ref.shape[1]
    acc = jnp.zeros_like(x1)
    start = 0
    while start < hidden:
        cs = slice(start, min(start + hidden_chunk, hidden))
        z = (_silu(_dot(h2, w1_ref[:, cs])) * _dot(h2, w3_ref[:, cs])).astype(BF16)
        acc = acc + _dot(z, w2_ref[cs, :])
        start += hidden_chunk
    return x1 + gate_ffn * acc


def _final_norm(x, g):
    return x * lax.rsqrt(jnp.mean(x * x, axis=-1, keepdims=True) + EPS) * g


def _outproj_even_kernel(x_ref, ys_ref, m_ref, og_ref, gm_ref, sh_ref, sc_ref, gf_ref, g2_ref, gw_ref, gbias_ref,
                         ng_ref, wos_ref, wom_ref, w1_ref, w3_ref, w2_ref, o_ref, *, hidden_chunk, heads):
    ys = jax.nn.gelu(ys_ref[0])
    s = ys * jax.nn.sigmoid(_dot(ys.astype(BF16), gw_ref[...]) + gbias_ref[...])
    dh = m_ref.shape[-1] // heads
    gated = []
    for h in range(heads):
        cols = slice(h * dh, (h + 1) * dh)
        hh = m_ref[0, :, cols].astype(F32)
        mu = jnp.mean(hh, axis=1, keepdims=True)
        cen = hh - mu
        var = jnp.mean(cen * cen, axis=1, keepdims=True)
        normed = cen * lax.rsqrt(var + EPS) * ng_ref[:, cols]
        gated.append((jax.nn.sigmoid(og_ref[0, :, cols]) * normed).astype(BF16))
    y = _dot(s.astype(BF16), wos_ref[...]) + _dot(jnp.concatenate(gated, axis=1), wom_ref[...])
    o_ref[0] = _ffn_tail(x_ref[0], y, gm_ref[0], g2_ref[...], sh_ref[0], sc_ref[0], gf_ref[0],
                         w1_ref, w3_ref, w2_ref, hidden_chunk)


def _outproj_odd_kernel(x_ref, a_ref, gate_ref, gm_ref, sh_ref, sc_ref, gf_ref, g2_ref, ng_ref, wo_ref, w1_ref,
                        w3_ref, w2_ref, fg_ref, o_ref, *, hidden_chunk, heads):
    hv = a_ref.shape[-1] // heads
    y = jnp.zeros(x_ref.shape[1:], F32)
    for h in range(heads):
        cols = slice(h * hv, (h + 1) * hv)
        o = a_ref[0, :, cols].astype(F32)
        mu = jnp.mean(o, axis=1, keepdims=True)
        cen = o - mu
        var = jnp.mean(cen * cen, axis=1, keepdims=True)
        normed = cen * lax.rsqrt(var + EPS) * ng_ref[:, cols]
        gated = (_silu(gate_ref[0, :, cols].astype(F32)) * normed).astype(BF16)
        y = y + _dot(gated, wo_ref[cols, :])
    out = _ffn_tail(x_ref[0], y, gm_ref[0], g2_ref[...], sh_ref[0], sc_ref[0], gf_ref[0],
                    w1_ref, w3_ref, w2_ref, hidden_chunk)
    o_ref[0] = _final_norm(out, fg_ref[...])


MXU_WIDTH = 256


def _hidden_chunk(hidden):
    return -(-hidden // (2 * MXU_WIDTH)) * MXU_WIDTH


def _mod_specs(d, row_of_batch, slots):
    return [pl.BlockSpec((1, 1, d), lambda b, i, s=s: (row_of_batch(b) * N_MOD + s, 0, 0)) for s in slots]


def _outproj_even(x, ys, m, p, mods, row_of_batch, g2, glu_w, glu_b, norm_g, w_out, w1, w3, w2):
    bn, ln, d = x.shape
    sw, mw = ys.shape[-1], m.shape[-1]
    tm = _token_tile(ln)
    res = lambda a: _resident(a.shape, lambda b, i: (0,) * a.ndim)
    tok = lambda w: pl.BlockSpec((1, tm, w), lambda b, i: (b, i, 0))
    row = lambda w: pl.BlockSpec((1, w), lambda b, i: (0, 0))
    wos, wom = w_out[:sw], w_out[sw:]
    return pl.pallas_call(
        functools.partial(_outproj_even_kernel, hidden_chunk=_hidden_chunk(w1.shape[1]), heads=ML_HEADS),
        grid=(bn, ln // tm),
        in_specs=[tok(d), tok(sw), tok(mw), pl.BlockSpec((1, tm, mw), lambda b, i: (b, i, 1))]
        + _mod_specs(d, row_of_batch, (2, 3, 4, 5))
        + [row(d), res(glu_w), row(sw), row(mw), res(wos), res(wom), res(w1), res(w3), res(w2)],
        out_specs=tok(d),
        out_shape=jax.ShapeDtypeStruct((bn, ln, d), F32),
        compiler_params=_cparams(("arbitrary", "arbitrary")),
        name="outproj_ffn_even",
    )(x, ys, m, p, mods, mods, mods, mods, g2, glu_w, glu_b.reshape(1, sw), norm_g.reshape(1, mw),
      wos, wom, w1, w3, w2)


def _outproj_odd(x, a, gate, mods, row_of_batch, g2, norm_g, w_out, w1, w3, w2, final_g):
    bn, ln, d = x.shape
    vw = a.shape[-1]
    tm = _token_tile(ln)
    res = lambda a_: _resident(a_.shape, lambda b, i: (0,) * a_.ndim)
    tok = lambda w: pl.BlockSpec((1, tm, w), lambda b, i: (b, i, 0))
    row = lambda w: pl.BlockSpec((1, w), lambda b, i: (0, 0))
    return pl.pallas_call(
        functools.partial(_outproj_odd_kernel, hidden_chunk=_hidden_chunk(w1.shape[1]), heads=RET_HEADS),
        grid=(bn, ln // tm),
        in_specs=[tok(d), tok(vw), tok(vw)] + _mod_specs(d, row_of_batch, (2, 3, 4, 5))
        + [row(d), row(vw), res(w_out), res(w1), res(w3), res(w2), row(d)],
        out_specs=tok(d),
        out_shape=jax.ShapeDtypeStruct((bn, ln, d), F32),
        compiler_params=_cparams(("arbitrary", "arbitrary")),
        name="outproj_ffn_odd",
    )(x, a, gate, mods, mods, mods, mods, g2, norm_g.reshape(1, vw), w_out, w1, w3, w2, final_g)


def _rope_heads(t, cos, sin, heads, hk):
    half = hk // 2
    out = []
    for h in range(heads):
        t1 = t[:, h * hk:h * hk + half]
        t2 = t[:, h * hk + half:(h + 1) * hk]
        out.append(t1 * cos - t2 * sin)
        out.append(t2 * cos + t1 * sin)
    return jnp.concatenate(out, axis=-1)


def _inproj_odd_kernel(x_ref, sh_ref, sc_ref, g_ref, cos_ref, sin_ref, w_ref, q_ref, k_ref, v_ref, gate_ref,
                       *, qk, vw, heads):
    h = _norm_mod(x_ref[0], g_ref[...], sh_ref[0], sc_ref[0]).astype(BF16)
    hk = qk // heads
    cos, sin = cos_ref[...], sin_ref[...]
    q = _rope_heads(_dot(h, w_ref[:, 0:qk]), cos, sin, heads, hk)
    q_ref[0] = (q * (hk ** -0.5)).astype(BF16)
    k = _rope_heads(_dot(h, w_ref[:, qk:2 * qk]), cos, sin, heads, hk)
    k_ref[0] = k.astype(BF16)
    v_ref[0] = _dot(h, w_ref[:, 2 * qk:2 * qk + vw]).astype(BF16)
    gate_ref[0] = _dot(h, w_ref[:, 2 * qk + vw:2 * qk + 2 * vw]).astype(BF16)


def _inproj_odd_ctx_kernel(x_ref, sh_ref, sc_ref, g_ref, w_ref, k_ref, v_ref, *, qk, vw):
    h = _norm_mod(x_ref[0], g_ref[...], sh_ref[0], sc_ref[0]).astype(BF16)
    k_ref[0] = _dot(h, w_ref[:, 0:qk]).astype(BF16)
    v_ref[0] = _dot(h, w_ref[:, qk:qk + vw]).astype(BF16)


def _inproj_odd(x, mods, row_of_batch, g, cos, sin, w, qk, vw):
    bn, ln, d = x.shape
    tm = _token_tile(ln)
    tok = lambda w_: pl.BlockSpec((1, tm, w_), lambda b, i: (b, i, 0))
    half = cos.shape[1]
    return pl.pallas_call(
        functools.partial(_inproj_odd_kernel, qk=qk, vw=vw, heads=RET_HEADS),
        grid=(bn, ln // tm),
        in_specs=[tok(d)] + _mod_specs(d, row_of_batch, (0, 1))
        + [pl.BlockSpec((1, d), lambda b, i: (0, 0)),
           pl.BlockSpec((tm, half), lambda b, i: (i, 0)), pl.BlockSpec((tm, half), lambda b, i: (i, 0)),
           _resident(w.shape, lambda b, i: (0, 0))],
        out_specs=[tok(qk), tok(qk), tok(vw), tok(vw)],
        out_shape=[jax.ShapeDtypeStruct((bn, ln, qk), BF16), jax.ShapeDtypeStruct((bn, ln, qk), BF16),
                   jax.ShapeDtypeStruct((bn, ln, vw), BF16), jax.ShapeDtypeStruct((bn, ln, vw), BF16)],
        compiler_params=_cparams(("arbitrary", "arbitrary")),
        name="inproj_odd",
    )(x, mods, mods, g, cos, sin, w)


def _inproj_odd_ctx(x, mods, row_of_batch, g, w_kv, qk, vw):
    bn, ln, d = x.shape
    tm = _token_tile(ln)
    tok = lambda w_: pl.BlockSpec((1, tm, w_), lambda b, i: (b, i, 0))
    return pl.pallas_call(
        functools.partial(_inproj_odd_ctx_kernel, qk=qk, vw=vw),
        grid=(bn, ln // tm),
        in_specs=[tok(d)] + _mod_specs(d, row_of_batch, (0, 1))
        + [pl.BlockSpec((1, d), lambda b, i: (0, 0)), _resident(w_kv.shape, lambda b, i: (0, 0))],
        out_specs=[tok(qk), tok(vw)],
        out_shape=[jax.ShapeDtypeStruct((bn, ln, qk), BF16), jax.ShapeDtypeStruct((bn, ln, vw), BF16)],
        compiler_params=_cparams(("arbitrary", "arbitrary")),
        name="inproj_odd_ctx",
    )(x, mods, mods, g, w_kv)


def _retention_kernel(lg_ref, q_ref, k_ref, v_ref, kc_ref, vc_ref, o_ref, acc_s, sf_s, sb_s,
                      *, chunk, n_ctx_rows, n_lat_rows):
    t_n = chunk
    h = pl.program_id(1)
    lgf = jnp.full((1, 1), lg_ref[0, h], F32)
    lgb = jnp.full((1, 1), lg_ref[1, h], F32)
    nc_ctx, nc_lat = n_ctx_rows // t_n, n_lat_rows // t_n
    ti = lax.broadcasted_iota(jnp.int32, (t_n, t_n), 0)
    si = lax.broadcasted_iota(jnp.int32, (t_n, t_n), 1)
    diff = (ti - si).astype(F32)
    decay = jnp.where(diff >= 0, jnp.exp(lgf * jnp.maximum(diff, 0.0)), jnp.exp(lgb * jnp.maximum(-diff, 0.0)))
    pos = lax.broadcasted_iota(jnp.int32, (t_n, 1), 0).astype(F32)
    inter_f = jnp.exp(lgf * (pos + 1.0))
    inter_b = jnp.exp(lgb * (t_n - pos))
    wend_f = jnp.exp(lgf * (t_n - 1.0 - pos))
    wend_b = jnp.exp(lgb * pos)
    cd_f = jnp.exp(lgf * t_n)
    cd_b = jnp.exp(lgb * t_n)

    def rows_of(c):
        return pl.ds(pl.multiple_of(c * t_n, t_n), t_n)

    def bump(s_ref, k, v, wend, cd):
        kw = (k.astype(F32) * wend).astype(BF16)
        s_ref[...] = cd * s_ref[...] + _dot_tn(kw, v)

    sf_s[...] = jnp.zeros_like(sf_s)
    sb_s[...] = jnp.zeros_like(sb_s)

    def ctx_f(c, _):
        bump(sf_s, kc_ref[0, rows_of(c), :], vc_ref[0, rows_of(c), :], wend_f, cd_f)
        return 0
    lax.fori_loop(0, nc_ctx, ctx_f, 0)

    def ctx_b(i, _):
        c = nc_ctx - 1 - i
        bump(sb_s, kc_ref[0, rows_of(c), :], vc_ref[0, rows_of(c), :], wend_b, cd_b)
        return 0
    lax.fori_loop(0, nc_ctx, ctx_b, 0)

    def forward_part(rows):
        q, k, v = q_ref[0, rows, :], k_ref[0, rows, :], v_ref[0, rows, :]
        scores = (_dot_nt(q, k) * decay).astype(BF16)
        part = _dot(scores, v) + inter_f * _dot(q, sf_s[...].astype(BF16))
        bump(sf_s, k, v, wend_f, cd_f)
        return part

    def backward_part(rows):
        q, k, v = q_ref[0, rows, :], k_ref[0, rows, :], v_ref[0, rows, :]
        part = inter_b * _dot(q, sb_s[...].astype(BF16))
        bump(sb_s, k, v, wend_b, cd_b)
        return part

    def first_half(i, _):
        rf, rb = rows_of(i), rows_of(nc_lat - 1 - i)
        acc_s[rf, :] = forward_part(rf)
        acc_s[rb, :] = backward_part(rb)
        return 0
    lax.fori_loop(0, nc_lat // 2, first_half, 0)

    def second_half(i, _):
        rf, rb = rows_of(i), rows_of(nc_lat - 1 - i)
        o_ref[0, rf, :] = (acc_s[rf, :] + forward_part(rf)).astype(BF16)
        o_ref[0, rb, :] = (acc_s[rb, :] + backward_part(rb)).astype(BF16)
        return 0
    lax.fori_loop(nc_lat // 2, nc_lat, second_half, 0)


def _retention_mix(q, k, v, k_ctx, v_ctx, log_gamma):
    bn, ll, qk = q.shape
    lc = k_ctx.shape[1]
    vw = v.shape[-1]
    hk, hv = qk // RET_HEADS, vw // RET_HEADS
    chunk = 256 if (ll % 512 == 0 and lc % 256 == 0) else 128
    assert (ll // chunk) % 2 == 0, "latent chunks are visited in forward/backward pairs"
    head = lambda n_rows, w: pl.BlockSpec((1, n_rows, w), lambda b, h: (b, 0, h))
    return pl.pallas_call(
        functools.partial(_retention_kernel, chunk=chunk, n_ctx_rows=lc, n_lat_rows=ll),
        grid=(bn, RET_HEADS),
        in_specs=[
            pl.BlockSpec(memory_space=pltpu.SMEM),
            head(ll, hk), head(ll, hk), head(ll, hv), head(lc, hk), head(lc, hv),
        ],
        out_specs=head(ll, hv),
        out_shape=jax.ShapeDtypeStruct((bn, ll, vw), BF16),
        scratch_shapes=[pltpu.VMEM((ll, hv), F32), pltpu.VMEM((hk, hv), F32), pltpu.VMEM((hk, hv), F32)],
        compiler_params=_cparams(("arbitrary", "arbitrary")),
        name="retention_mix",
    )(log_gamma.astype(F32), q, k, v, k_ctx, v_ctx)


def _grid_rope(n_pos, hk):
    rows = n_pos // GRID_W
    row = jnp.repeat(jnp.arange(rows, dtype=F32), GRID_W)
    col = jnp.tile(jnp.arange(GRID_W, dtype=F32), rows)
    n_freq = hk // 4
    inv = ROPE_BASE ** (-jnp.arange(n_freq, dtype=F32) / n_freq)
    ang = jnp.concatenate([row[:, None] * inv, col[:, None] * inv], -1)
    return jnp.cos(ang), jnp.sin(ang)


def kernel(x, c, ctx, c_ctx, ada_w, ada_b, norm1_g, norm2_g, ab_w_in, ab_w_out, s5_a_re, s5_a_im, s5_log_dt,
           s5_b_re, s5_b_im, s5_c_re, s5_c_im, s5_d, s5_glu_w, s5_glu_b, ml_conv_w, ml_conv_b, ml_wq, ml_wk,
           ml_wv, ml_gate_b, ml_norm_g, ret_w_in, ret_w_out, ret_log_gamma, ret_norm_g, ffn_w1, ffn_w3, ffn_w2,
           final_g):
    bn, ln, d = x.shape
    depth = ada_w.shape[0]
    assert depth == 2, "one S5 || mLSTM layer followed by one retention layer"
    s5_width = s5_d.shape[-1]
    ml_width = ml_norm_g.shape[-1]
    assert ml_width == ML_HEADS * LANES and s5_width % S5_GROUP == 0

    r_pad = -(-(bn + 1) // SUBLANES) * SUBLANES
    vec = jnp.zeros((r_pad, d), F32).at[:bn].set(c).at[bn].set(c_ctx)
    mods_all = _modulation(vec, ada_w, ada_b).reshape(depth, r_pad * N_MOD, 1, d)
    lat_row = lambda b: b
    ctx_row = lambda b: bn

    mods = mods_all[0]
    g1 = norm1_g[0].reshape(1, d)
    g2 = norm2_g[0].reshape(1, d)
    w_in = ab_w_in[0]
    n_gate = w_in.shape[1] - s5_width - 2 * ml_width
    w_in = jnp.pad(w_in, ((0, 0), (0, LANES - n_gate))).astype(BF16)
    u_lat, p_lat = _inproj_even(x, mods, lat_row, g1, w_in, s5_width)
    u_ctx, p_ctx = _inproj_even(ctx, mods, ctx_row, g1, w_in, s5_width)
    mats = _s5_matrices(s5_a_re[0], s5_a_im[0], s5_log_dt[0], s5_b_re[0], s5_b_im[0], s5_c_re[0], s5_c_im[0],
                        s5_d[0])
    ys_ctx, ys_lat = _s5_mix(u_ctx, u_lat, mats)
    m_ctx, m_lat = _mlstm_mix(p_ctx, p_lat, ml_conv_w[0], ml_conv_b[0], ml_wq[0], ml_wk[0], ml_wv[0],
                              ml_gate_b[0])
    glu_w = s5_glu_w[0].astype(BF16)
    w_out = ab_w_out[0].astype(BF16)
    w1, w3, w2 = ffn_w1[0].astype(BF16), ffn_w3[0].astype(BF16), ffn_w2[0].astype(BF16)
    x = _outproj_even(x, ys_lat, m_lat, p_lat, mods, lat_row, g2, glu_w, s5_glu_b[0], ml_norm_g[0],
                      w_out, w1, w3, w2)
    ctx = _outproj_even(ctx, ys_ctx, m_ctx, p_ctx, mods, ctx_row, g2, glu_w, s5_glu_b[0], ml_norm_g[0],
                        w_out, w1, w3, w2)

    mods = mods_all[1]
    g1 = norm1_g[1].reshape(1, d)
    g2 = norm2_g[1].reshape(1, d)
    vw = ret_norm_g.shape[-1]
    qk = (ret_w_in.shape[-1] - 2 * vw) // 2
    w_in = ret_w_in[0].astype(BF16)
    cos, sin = _grid_rope(ln, qk // RET_HEADS)
    q, k, v, gate = _inproj_odd(x, mods, lat_row, g1, cos, sin, w_in, qk, vw)
    k_ctx, v_ctx = _inproj_odd_ctx(ctx, mods, ctx_row, g1, w_in[:, qk:2 * qk + vw], qk, vw)
    a = _retention_mix(q, k, v, k_ctx, v_ctx, ret_log_gamma[0])
    w_out = ret_w_out[0].astype(BF16)
    w1, w3, w2 = ffn_w1[1].astype(BF16), ffn_w3[1].astype(BF16), ffn_w2[1].astype(BF16)
    return _outproj_odd(x, a, gate, mods, lat_row, g2, ret_norm_g[0], w_out, w1, w3, w2, final_g.reshape(1, d))
```

```python
import functools
import math

import jax
import jax.numpy as jnp
from jax import lax
from jax.experimental import pallas as pl
from jax.experimental.pallas import tpu as pltpu

F32 = jnp.float32
BF16 = jnp.bfloat16

EPS = 1e-6
N_MOD = 6
GRID_W = 64
ROPE_BASE = 10000.0

S5_GROUP = 16
S5_STATE = 64
S5_STEP = 16
ML_HEADS = 4
ML_CHUNK = 128
RET_HEADS = 4
SUBLANES = 8
LANES = 128
NEG_BIG = -1e30
VMEM_LIMIT = 56 * 1024 * 1024


def _cparams(sem):
    return pltpu.CompilerParams(dimension_semantics=sem, vmem_limit_bytes=VMEM_LIMIT)


def _resident(shape, index_map):
    return pl.BlockSpec(shape, index_map, pipeline_mode=pl.Buffered(1))


def _token_tile(n):
    for t in (512, 256, 128):
        if n % t == 0:
            return t
    raise ValueError(f"sequence length {n} must be a multiple of 128")


def _silu(v):
    return v * jax.nn.sigmoid(v)


def _norm_mod(x, g, shift, scale):
    y = x * lax.rsqrt(jnp.mean(x * x, axis=-1, keepdims=True) + EPS)
    return (y * g) * (1.0 + scale) + shift


def _dot(a, b):
    return jnp.dot(a, b, preferred_element_type=F32)


def _dot_nt(a, b):
    return lax.dot_general(a, b, (((1,), (1,)), ((), ())), preferred_element_type=F32)


def _dot_tn(a, b):
    return lax.dot_general(a, b, (((0,), (0,)), ((), ())), preferred_element_type=F32)


def _mod_kernel(v_ref, w_ref, b_ref, o_ref):
    s = _silu(v_ref[...]).astype(BF16)
    o_ref[0] = _dot(s, w_ref[0].astype(BF16)) + b_ref[0]


def _modulation(vec, ada_w, ada_b):
    depth, d, n = ada_w.shape
    r = vec.shape[0]
    tn = 1024
    return pl.pallas_call(
        _mod_kernel,
        grid=(depth, n // tn),
        in_specs=[
            pl.BlockSpec((r, d), lambda l, j: (0, 0)),
            pl.BlockSpec((1, d, tn), lambda l, j: (l, 0, j)),
            pl.BlockSpec((1, 1, tn), lambda l, j: (l, 0, j)),
        ],
        out_specs=pl.BlockSpec((1, r, tn), lambda l, j: (l, 0, j)),
        out_shape=jax.ShapeDtypeStruct((depth, r, n), F32),
        compiler_params=_cparams(("arbitrary", "arbitrary")),
        name="adaln_modulation",
    )(vec, ada_w, ada_b.reshape(depth, 1, n))


def _inproj_even_kernel(x_ref, sh_ref, sc_ref, g_ref, w_ref, u_ref, p_ref, *, s5_width):
    h = _norm_mod(x_ref[0], g_ref[...], sh_ref[0], sc_ref[0]).astype(BF16)
    p = _dot(h, w_ref[...])
    u_ref[0] = p[:, :s5_width]
    p_ref[0] = p[:, s5_width:]


def _inproj_even(x, mods, row_of_batch, g, w, s5_width):
    bn, ln, d = x.shape
    n = w.shape[1]
    tm = _token_tile(ln)
    return pl.pallas_call(
        functools.partial(_inproj_even_kernel, s5_width=s5_width),
        grid=(bn, ln // tm),
        in_specs=[
            pl.BlockSpec((1, tm, d), lambda b, i: (b, i, 0)),
            pl.BlockSpec((1, 1, d), lambda b, i: (row_of_batch(b) * N_MOD + 0, 0, 0)),
            pl.BlockSpec((1, 1, d), lambda b, i: (row_of_batch(b) * N_MOD + 1, 0, 0)),
            pl.BlockSpec((1, d), lambda b, i: (0, 0)),
            _resident((d, n), lambda b, i: (0, 0)),
        ],
        out_specs=[
            pl.BlockSpec((1, tm, s5_width), lambda b, i: (b, i, 0)),
            pl.BlockSpec((1, tm, n - s5_width), lambda b, i: (b, i, 0)),
        ],
        out_shape=[
            jax.ShapeDtypeStruct((bn, ln, s5_width), F32),
            jax.ShapeDtypeStruct((bn, ln, n - s5_width), F32),
        ],
        compiler_params=_cparams(("arbitrary", "arbitrary")),
        name="inproj_even",
    )(x, mods, mods, g, w)


def _s5_response_kernel(c_ref, w_ref, o_ref):
    for n in range(c_ref.shape[0]):
        o_ref[n] = lax.dot_general(c_ref[n], w_ref[n], (((1,), (1,)), ((), ())),
                                   precision=lax.Precision.HIGHEST, preferred_element_type=F32)


def _s5_response(c_cat, w_cat):
    n, co, kk = c_cat.shape
    r = w_cat.shape[1]
    nb = 8
    return pl.pallas_call(
        _s5_response_kernel,
        grid=(n // nb,),
        in_specs=[pl.BlockSpec((nb, co, kk), lambda i: (i, 0, 0)), pl.BlockSpec((nb, r, kk), lambda i: (i, 0, 0))],
        out_specs=pl.BlockSpec((nb, co, r), lambda i: (i, 0, 0)),
        out_shape=jax.ShapeDtypeStruct((n, co, r), F32),
        compiler_params=_cparams(("arbitrary",)),
        name="s5_response",
    )(c_cat, w_cat)


def _s5_matrices(a_re, a_im, log_dt, b_re, b_im, c_re, c_im, d_skip):
    k = S5_STEP
    a_re, a_im = a_re.astype(F32), a_im.astype(F32)
    dt = jnp.exp(log_dt.astype(F32))[..., None]
    lam_re, lam_im = a_re * dt, a_im * dt
    steps = jnp.arange(k + 1, dtype=F32)[:, None, None, None]
    mag = jnp.exp(steps * lam_re)
    pr, pi = mag * jnp.cos(steps * lam_im), mag * jnp.sin(steps * lam_im)
    den = a_re * a_re + a_im * a_im
    nr = pr[1] - 1.0
    coef_re = (nr * a_re + pi[1] * a_im) / den
    coef_im = (pi[1] * a_re - nr * a_im) / den
    bt_re = jnp.swapaxes(b_re.astype(F32), -1, -2)
    bt_im = jnp.swapaxes(b_im.astype(F32), -1, -2)
    bb_re = coef_re[:, :, None] * bt_re - coef_im[:, :, None] * bt_im
    bb_im = coef_re[:, :, None] * bt_im + coef_im[:, :, None] * bt_re
    wr = pr[:, :, :, None] * bb_re - pi[:, :, :, None] * bb_im
    wi = pr[:, :, :, None] * bb_im + pi[:, :, :, None] * bb_re
    c_re, c_im = c_re.astype(F32), c_im.astype(F32)
    g_n, c_n, p_n = a_re.shape[1], b_re.shape[-1], a_re.shape[-1]
    w_cat = jnp.stack([jnp.concatenate([wr[k - 1::-1, 0], -wi[k - 1::-1, 0]], -1),
                       jnp.concatenate([wr[:k, 1], -wi[:k, 1]], -1)])
    w_cat = jnp.transpose(w_cat, (0, 2, 1, 3, 4)).reshape(2 * g_n, k * c_n, 2 * p_n)
    c_cat = jnp.concatenate([c_re, c_im], -1).reshape(2 * g_n, c_n, 2 * p_n)
    resp = _s5_response(c_cat, w_cat).reshape(2, g_n, c_n, k, c_n)
    skip = (jnp.eye(c_n, dtype=F32) * d_skip.astype(F32).reshape(g_n, 1, c_n))[:, :, None, :]
    gen = jnp.concatenate([resp[0, :, :, :k - 1], resp[0, :, :, k - 1:] + resp[1, :, :, :1] + skip,
                           resp[1, :, :, 1:]], axis=2).reshape(g_n, c_n, (2 * k - 1) * c_n)
    shifted = jnp.tile(gen, (1, 1, k + 1))[:, :, :k * 2 * k * c_n].reshape(g_n, c_n, k, 2 * k * c_n)
    toep = jnp.transpose(shifted[:, :, ::-1, :k * c_n], (0, 2, 1, 3)).reshape(g_n, k * c_n, k * c_n)
    endw = jnp.concatenate([wr[k - 1::-1, 0], wr[:k, 1], wi[k - 1::-1, 0], wi[:k, 1]], -1)
    endw = jnp.transpose(endw, (1, 0, 2, 3)).reshape(g_n, k * c_n, 4 * p_n)

    def out_cols(d, powers_re, powers_im):
        cr, ci = c_re[d][None], c_im[d][None]
        pre, pim = powers_re[:, :, None, :], powers_im[:, :, None, :]
        return cr * pre - ci * pim, -(cr * pim + ci * pre)

    of_re, of_im = out_cols(0, pr[1:, 0], pi[1:, 0])
    ob_re, ob_im = out_cols(1, pr[k:0:-1, 1], pi[k:0:-1, 1])
    outw_t = jnp.concatenate([of_re, ob_re, of_im, ob_im], -1)
    outw_t = jnp.transpose(outw_t, (1, 0, 2, 3)).reshape(g_n, k * c_n, 4 * p_n)
    a0 = jnp.concatenate([pr[k, 0], pr[k, 1]], -1)
    a1 = jnp.concatenate([pi[k, 0], pi[k, 1]], -1)
    return toep.astype(BF16), endw.astype(BF16), outw_t.astype(BF16), a0, a1


S5_LANE_GROUPS = LANES // S5_GROUP


def _chunk_transpose(xs):
    n_arr = len(xs)
    lane_chunk = lax.broadcasted_iota(jnp.int32, xs[0].shape, 1) // S5_GROUP
    d = 1
    while d < n_arr:
        keep = (lane_chunk & d) == 0
        out = list(xs)
        for i in range(n_arr):
            if i & d:
                continue
            lo, hi = xs[i], xs[i + d]
            out[i] = jnp.where(keep, lo, pltpu.roll(hi, d * S5_GROUP, 1))
            out[i + d] = jnp.where(keep, pltpu.roll(lo, LANES - d * S5_GROUP, 1), hi)
        xs = out
        d *= 2
    return xs


def _s5_kernel(uc_ref, ul_ref, toep_ref, endw_ref, outw_ref, a0_ref, a1_ref, yc_ref, yl_ref,
               uc_s, ul_s, e_s, yc_s, yl_s, *, batch, n_ctx, n_lat):
    p = S5_STATE
    gl = S5_LANE_GROUPS
    k = S5_STEP

    def stack(u_ref, us_ref, n_blk):
        for b in range(batch):
            for half in range(k // gl):
                xs = [u_ref[b, pl.ds(half * gl + s, n_blk, stride=k), :] for s in range(gl)]
                ys = _chunk_transpose(xs)
                for g in range(gl):
                    us_ref[g, b * n_blk:(b + 1) * n_blk, half * LANES:(half + 1) * LANES] = ys[g].astype(BF16)

    stack(uc_ref, uc_s, n_ctx)
    stack(ul_ref, ul_s, n_lat)

    def increments(us_ref, n_blk, first_blk):
        for g in range(gl):
            e = _dot(us_ref[g], endw_ref[g])
            for b in range(batch):
                for part in range(2):
                    e_s[part, b, pl.ds(first_blk * gl + g, n_blk, stride=gl), :] = (
                        e[b * n_blk:(b + 1) * n_blk, part * LANES:(part + 1) * LANES])

    increments(uc_s, n_ctx, 0)
    increments(ul_s, n_lat, n_ctx)

    a0 = a0_ref[...]
    a1 = a1_ref[...]
    fwd_lane = lax.broadcasted_iota(jnp.int32, (batch, gl, 2 * p), 2) < p

    def scan(first_blk, n, carry):
        def body(i, c):
            s0, s1 = c
            rf = pl.ds(pl.multiple_of((first_blk + i) * gl, gl), gl)
            rb = pl.ds(pl.multiple_of((first_blk + n - 1 - i) * gl, gl), gl)
            e0 = jnp.where(fwd_lane, e_s[0, :, rf, :], e_s[0, :, rb, :])
            e1 = jnp.where(fwd_lane, e_s[1, :, rf, :], e_s[1, :, rb, :])
            e_s[0, :, rf, 0:p] = s0[:, :, 0:p]
            e_s[0, :, rb, p:2 * p] = s0[:, :, p:2 * p]
            e_s[1, :, rf, 0:p] = s1[:, :, 0:p]
            e_s[1, :, rb, p:2 * p] = s1[:, :, p:2 * p]
            return a0 * s0 - a1 * s1 + e0, a0 * s1 + a1 * s0 + e1
        return lax.fori_loop(0, n, body, carry)

    zero = jnp.zeros((batch, gl, 2 * p), F32)
    carry = scan(0, n_ctx, (zero, zero))
    scan(n_ctx, n_lat, carry)

    def outputs(us_ref, ys_ref, y_ref, n_blk, first_blk):
        for g in range(gl):
            state = jnp.concatenate(
                [jnp.concatenate([e_s[part, b, pl.ds(first_blk * gl + g, n_blk, stride=gl), :]
                                  for b in range(batch)], axis=0) for part in range(2)], axis=1)
            ys_ref[g] = _dot_nt(us_ref[g], toep_ref[g]) + _dot_nt(state.astype(BF16), outw_ref[g])
        for b in range(batch):
            for half in range(k // gl):
                ys = [ys_ref[g, b * n_blk:(b + 1) * n_blk, half * LANES:(half + 1) * LANES] for g in range(gl)]
                xs = _chunk_transpose(ys)
                for s in range(gl):
                    y_ref[b, pl.ds(half * gl + s, n_blk, stride=k), :] = xs[s]

    outputs(uc_s, yc_s, yc_ref, n_ctx, 0)
    outputs(ul_s, yl_s, yl_ref, n_lat, n_ctx)


def _s5_mix(u_ctx, u_lat, mats):
    toep, endw, outw, a0, a1 = mats
    bn, lc, w = u_ctx.shape
    ll = u_lat.shape[1]
    gl = S5_LANE_GROUPS
    kc = S5_STEP * S5_GROUP
    n_ctx, n_lat = lc // S5_STEP, ll // S5_STEP
    bb = 4 if bn % 4 == 0 else bn
    seq = lambda n_rows: pl.BlockSpec((bb, n_rows, LANES), lambda i, j: (j, 0, i))
    grp = lambda r, c: pl.BlockSpec((gl, r, c), lambda i, j: (i, 0, 0))
    return pl.pallas_call(
        functools.partial(_s5_kernel, batch=bb, n_ctx=n_ctx, n_lat=n_lat),
        grid=(w // LANES, bn // bb),
        in_specs=[seq(lc), seq(ll), grp(kc, kc), grp(kc, kc), grp(kc, kc),
                  pl.BlockSpec((gl, 2 * S5_STATE), lambda i, j: (i, 0)),
                  pl.BlockSpec((gl, 2 * S5_STATE), lambda i, j: (i, 0))],
        out_specs=[seq(lc), seq(ll)],
        out_shape=[jax.ShapeDtypeStruct((bn, lc, w), F32), jax.ShapeDtypeStruct((bn, ll, w), F32)],
        scratch_shapes=[
            pltpu.VMEM((gl, bb * n_ctx, kc), BF16), pltpu.VMEM((gl, bb * n_lat, kc), BF16),
            pltpu.VMEM((2, bb, (n_ctx + n_lat) * gl, LANES), F32),
            pltpu.VMEM((gl, bb * n_ctx, kc), F32), pltpu.VMEM((gl, bb * n_lat, kc), F32),
        ],
        compiler_params=_cparams(("arbitrary", "arbitrary")),
        name="s5_scan",
    )(u_ctx, u_lat, toep, endw, outw, a0, a1)


def _mlstm_kernel(pc_ref, pl_ref, cw_ref, cb_ref, wq_ref, wkt_ref, wv_ref, gb_ref, mc_ref, ml_ref,
                  q_s, kt_s, vx_s, h_s, st_s, bcol_s, mcol_s, rrow_s, xpad_s, *, n_ctx_rows, n_lat_rows):
    t_n = ML_CHUNK
    width = ML_HEADS * LANES
    nc_ctx, nc_lat = n_ctx_rows // t_n, n_lat_rows // t_n
    nc = nc_ctx + nc_lat
    scale_k = LANES ** -0.5

    def project(p_ref, n_rows, base):
        ones = jnp.ones((n_rows, LANES), BF16)
        pad = SUBLANES
        xpad_s[0:pad, :] = jnp.zeros((pad, LANES), F32)
        xpad_s[pad + n_rows:2 * pad + n_rows, :] = jnp.zeros((pad, LANES), F32)
        for h in range(ML_HEADS):
            cols = slice(h * LANES, (h + 1) * LANES)
            xm = p_ref[0, :, cols]
            xpad_s[pad:pad + n_rows, :] = xm
            taps = cw_ref[:, cols]
            n_tap = taps.shape[0]
            assert n_tap // 2 <= pad
            acc = jnp.zeros_like(xm) + cb_ref[:, cols]
            for j in range(n_tap):
                d = j - n_tap // 2
                sh = xm if d == 0 else xpad_s[pad + d:pad + d + n_rows, :]
                acc = acc + sh * taps[j:j + 1, :]
            xc = _silu(acc).astype(BF16)
            q_s[base:base + n_rows, cols] = _dot(xc, wq_ref[h]).astype(BF16)
            kt = (_dot_nt(wkt_ref[h], xc) * scale_k).astype(BF16)
            for c in range(n_rows // t_n):
                kt_s[base // t_n + c, cols, :] = kt[:, c * t_n:(c + 1) * t_n]
            vx_s[base:base + n_rows, 2 * h * LANES:(2 * h + 1) * LANES] = (
                _dot(xm.astype(BF16), wv_ref[h]).astype(BF16))
            vx_s[base:base + n_rows, (2 * h + 1) * LANES:(2 * h + 2) * LANES] = ones

    project(pc_ref, n_ctx_rows, 0)
    project(pl_ref, n_lat_rows, n_ctx_rows)

    ti = lax.broadcasted_iota(jnp.int32, (t_n, t_n), 0)
    si = lax.broadcasted_iota(jnp.int32, (t_n, t_n), 1)
    tri_f = (si <= ti).astype(BF16)
    tri_b = (si >= ti).astype(BF16)
    lane = lax.broadcasted_iota(jnp.int32, (t_n, LANES), 1)
    trow = lax.broadcasted_iota(jnp.int32, (t_n, LANES), 0)
    gate_off = 2 * width

    def gate_prep(p_ref, n_chunks, base_chunk):
        def body(c, _):
            rows = pl.ds(pl.multiple_of(c * t_n, t_n), t_n)
            gcol = p_ref[0, rows, gate_off:gate_off + LANES] + gb_ref[...]
            lf = jax.nn.log_sigmoid(gcol)
            hi = lf.astype(BF16)
            lo = (lf - hi.astype(F32)).astype(BF16)
            pre = _dot(tri_f, hi) + _dot(tri_f, lo)
            suf = _dot(tri_b, hi) + _dot(tri_b, lo)
            bsum = jnp.where(lane < 2 * ML_HEADS, pre, suf)
            rcol = gcol - pltpu.roll(bsum, LANES - ML_HEADS, 1)
            pmax, smax = rcol, rcol
            step = 1
            while step < t_n:
                pmax = jnp.maximum(pmax, jnp.where(trow >= step, pltpu.roll(pmax, step, 0), NEG_BIG))
                smax = jnp.maximum(smax, jnp.where(trow < t_n - step, pltpu.roll(smax, t_n - step, 0), NEG_BIG))
                step *= 2
            bcol_s[base_chunk + c] = bsum
            mcol_s[base_chunk + c] = jnp.where(lane < 2 * ML_HEADS, pmax, smax)
            rrow_s[base_chunk + c] = rcol.T[0:4 * ML_HEADS, :]
            return 0
        lax.fori_loop(0, n_chunks, body, 0)

    gate_prep(pc_ref, nc_ctx, 0)
    gate_prep(pl_ref, nc_lat, nc_ctx)

    def one_dir(c, h, d, m_in):
        cols = slice(h * LANES, (h + 1) * LANES)
        xcols = slice(2 * h * LANES, (2 * h + 2) * LANES)
        li = 2 * d * ML_HEADS + h
        mask = (si <= ti) if d == 0 else (si >= ti)
        last = t_n - 1 if d == 0 else 0
        rows = pl.ds(pl.multiple_of(c * t_n, t_n), t_n)
        q = q_s[rows, cols]
        kt = kt_s[c, cols, :]
        vx = vx_s[rows, xcols]
        r_row = rrow_s[c][li:li + 1, :]
        run_max = jnp.broadcast_to(mcol_s[c][:, li:li + 1], (t_n, LANES))
        b_rep = jnp.broadcast_to(bcol_s[c][:, li + ML_HEADS:li + ML_HEADS + 1], (t_n, LANES))
        mm = jnp.maximum(m_in, run_max)
        dmat = jnp.exp(jnp.where(mask, r_row - mm, NEG_BIG))
        sm = (dmat * _dot(q, kt)).astype(BF16)
        intra = _dot(sm, vx)
        cross = _dot(q, st_s[2 * h + d].astype(BF16))
        inter = jnp.exp(m_in - mm)
        num = intra[:, :LANES] + inter * cross[:, :LANES]
        den = intra[:, LANES:] + inter * cross[:, LANES:]
        h_s[rows, cols] = h_s[rows, cols] + num / jnp.maximum(jnp.abs(den), jnp.exp(-b_rep - mm))
        r_top = run_max[last:last + 1, :]
        b_last = b_rep[last:last + 1, :]
        g_max = b_last + r_top
        m_new = jnp.maximum(b_last + m_in, g_max)
        dec = jnp.exp(b_last + m_in - m_new)
        inj = jnp.exp(g_max - m_new)
        kw = (kt.astype(F32) * (jnp.exp(r_row - r_top) * inj)).astype(BF16)
        inc = _dot(kw, vx)
        st_s[2 * h + d] = jnp.concatenate([dec, dec], axis=1) * st_s[2 * h + d] + inc
        return m_new

    st_s[...] = jnp.zeros_like(st_s)
    h_s[...] = jnp.zeros_like(h_s)

    def body(i, carry):
        cb = jnp.where(i < nc_ctx, nc_ctx - 1 - i, nc + nc_ctx - 1 - i)
        return tuple(one_dir(i if d == 0 else cb, h, d, carry[2 * h + d])
                     for h in range(ML_HEADS) for d in range(2))

    zero = jnp.zeros((1, LANES), F32)
    lax.fori_loop(0, nc, body, (zero,) * (2 * ML_HEADS))

    mc_ref[0] = h_s[0:n_ctx_rows, :].astype(BF16)
    ml_ref[0] = h_s[n_ctx_rows:n_ctx_rows + n_lat_rows, :].astype(BF16)


def _mlstm_mix(p_ctx, p_lat, conv_w, conv_b, wq, wk, wv, gate_b):
    bn, lc, pw = p_ctx.shape
    ll = p_lat.shape[1]
    width = ML_HEADS * LANES
    lt = lc + ll
    nct = lt // ML_CHUNK
    gb = jnp.zeros((1, LANES), F32).at[0, :4 * ML_HEADS].set(gate_b.astype(F32))
    full2 = lambda a: pl.BlockSpec(a.shape, lambda b: (0,) * a.ndim)
    conv_b2 = conv_b.reshape(1, width)
    assert ML_CHUNK == LANES
    wq, wkt, wv = wq.astype(BF16), jnp.swapaxes(wk, 1, 2).astype(BF16), wv.astype(BF16)
    return pl.pallas_call(
        functools.partial(_mlstm_kernel, n_ctx_rows=lc, n_lat_rows=ll),
        grid=(bn,),
        in_specs=[
            pl.BlockSpec((1, lc, pw), lambda b: (b, 0, 0)),
            pl.BlockSpec((1, ll, pw), lambda b: (b, 0, 0)),
            full2(conv_w), full2(conv_b2), full2(wq), full2(wkt), full2(wv), full2(gb),
        ],
        out_specs=[
            pl.BlockSpec((1, lc, width), lambda b: (b, 0, 0)),
            pl.BlockSpec((1, ll, width), lambda b: (b, 0, 0)),
        ],
        out_shape=[jax.ShapeDtypeStruct((bn, lc, width), BF16), jax.ShapeDtypeStruct((bn, ll, width), BF16)],
        scratch_shapes=[
            pltpu.VMEM((lt, width), BF16),
            pltpu.VMEM((nct, width, ML_CHUNK), BF16),
            pltpu.VMEM((lt, 2 * width), BF16),
            pltpu.VMEM((lt, width), F32),
            pltpu.VMEM((2 * ML_HEADS, LANES, 2 * LANES), F32),
            pltpu.VMEM((nct, ML_CHUNK, LANES), F32), pltpu.VMEM((nct, ML_CHUNK, LANES), F32),
            pltpu.VMEM((nct, 4 * ML_HEADS, ML_CHUNK), F32),
            pltpu.VMEM((max(lc, ll) + 2 * SUBLANES, LANES), F32),
        ],
        compiler_params=_cparams(("arbitrary",)),
        name="mlstm_mix",
    )(p_ctx, p_lat, conv_w, conv_b2, wq, wkt, wv, gb)


def _ffn_tail(x, y, gate_mix, g2, shift, scale, gate_ffn, w1_ref, w3_ref, w2_ref, hidden_chunk):
    x1 = x + gate_mix * y
    h2 = _norm_mod(x1, g2, shift, scale).astype(BF16)
    hidden = w1_ref.shape[1]
    acc = jnp.zeros_like(x1)
    start = 0
    while start < hidden:
        cs = slice(start, min(start + hidden_chunk, hidden))
        z = (_silu(_dot(h2, w1_ref[:, cs])) * _dot(h2, w3_ref[:, cs])).astype(BF16)
        acc = acc + _dot(z, w2_ref[cs, :])
        start += hidden_chunk
    return x1 + gate_ffn * acc


def _final_norm(x, g):
    return x * lax.rsqrt(jnp.mean(x * x, axis=-1, keepdims=True) + EPS) * g


def _outproj_even_kernel(x_ref, ys_ref, m_ref, og_ref, gm_ref, sh_ref, sc_ref, gf_ref, g2_ref, gw_ref, gbias_ref,
                         ng_ref, wo_ref, w1_ref, w3_ref, w2_ref, o_ref, *, hidden_chunk, heads):
    ys = jax.nn.gelu(ys_ref[0])
    s = ys * jax.nn.sigmoid(_dot(ys.astype(BF16), gw_ref[...]) + gbias_ref[...])
    dh = m_ref.shape[-1] // heads
    gated = []
    for h in range(heads):
        cols = slice(h * dh, (h + 1) * dh)
        hh = m_ref[0, :, cols].astype(F32)
        mu = jnp.mean(hh, axis=1, keepdims=True)
        cen = hh - mu
        var = jnp.mean(cen * cen, axis=1, keepdims=True)
        normed = cen * lax.rsqrt(var + EPS) * ng_ref[:, cols]
        gated.append((jax.nn.sigmoid(og_ref[0, :, cols]) * normed).astype(BF16))
    sw = ys_ref.shape[-1]
    y = _dot(s.astype(BF16), wo_ref[0:sw, :]) + _dot(jnp.concatenate(gated, axis=1), wo_ref[sw:, :])
    o_ref[0] = _ffn_tail(x_ref[0], y, gm_ref[0], g2_ref[...], sh_ref[0], sc_ref[0], gf_ref[0],
                         w1_ref, w3_ref, w2_ref, hidden_chunk)


def _outproj_odd_kernel(x_ref, a_ref, gate_ref, gm_ref, sh_ref, sc_ref, gf_ref, g2_ref, ng_ref, wo_ref, w1_ref,
                        w3_ref, w2_ref, fg_ref, o_ref, *, hidden_chunk, heads):
    hv = a_ref.shape[-1] // heads
    y = jnp.zeros(x_ref.shape[1:], F32)
    for h in range(heads):
        cols = slice(h * hv, (h + 1) * hv)
        o = a_ref[0, :, cols].astype(F32)
        mu = jnp.mean(o, axis=1, keepdims=True)
        cen = o - mu
        var = jnp.mean(cen * cen, axis=1, keepdims=True)
        normed = cen * lax.rsqrt(var + EPS) * ng_ref[:, cols]
        gated = (_silu(gate_ref[0, :, cols].astype(F32)) * normed).astype(BF16)
        y = y + _dot(gated, wo_ref[cols, :])
    out = _ffn_tail(x_ref[0], y, gm_ref[0], g2_ref[...], sh_ref[0], sc_ref[0], gf_ref[0],
                    w1_ref, w3_ref, w2_ref, hidden_chunk)
    o_ref[0] = _final_norm(out, fg_ref[...])


MXU_WIDTH = 256


def _hidden_chunk(hidden):
    return -(-hidden // (2 * MXU_WIDTH)) * MXU_WIDTH


def _mod_specs(d, row_of_batch, slots):
    return [pl.BlockSpec((1, 1, d), lambda b, i, s=s: (row_of_batch(b) * N_MOD + s, 0, 0)) for s in slots]


def _layer_resident(w, layer):
    return _resident((None,) + w.shape[1:], lambda b, i: (layer,) + (0,) * (w.ndim - 1))


def _outproj_even(x, ys, m, p, mods, row_of_batch, g2, glu_w, glu_b, norm_g, w_out, ffn, layer):
    bn, ln, d = x.shape
    sw, mw = ys.shape[-1], m.shape[-1]
    tm = _token_tile(ln)
    res = lambda a: _resident(a.shape, lambda b, i: (0,) * a.ndim)
    tok = lambda w: pl.BlockSpec((1, tm, w), lambda b, i: (b, i, 0))
    row = lambda w: pl.BlockSpec((1, w), lambda b, i: (0, 0))
    return pl.pallas_call(
        functools.partial(_outproj_even_kernel, hidden_chunk=_hidden_chunk(ffn[0].shape[-1]), heads=ML_HEADS),
        grid=(bn, ln // tm),
        in_specs=[tok(d), tok(sw), tok(mw), pl.BlockSpec((1, tm, mw), lambda b, i: (b, i, 1))]
        + _mod_specs(d, row_of_batch, (2, 3, 4, 5))
        + [row(d), res(glu_w), row(sw), row(mw), res(w_out)] + [_layer_resident(w, layer) for w in ffn],
        out_specs=tok(d),
        out_shape=jax.ShapeDtypeStruct((bn, ln, d), F32),
        compiler_params=_cparams(("arbitrary", "arbitrary")),
        name="outproj_ffn_even",
    )(x, ys, m, p, mods, mods, mods, mods, g2, glu_w, glu_b.reshape(1, sw), norm_g.reshape(1, mw), w_out, *ffn)


def _outproj_odd(x, a, gate, mods, row_of_batch, g2, norm_g, w_out, ffn, layer, final_g):
    bn, ln, d = x.shape
    vw = a.shape[-1]
    tm = _token_tile(ln)
    res = lambda a_: _resident(a_.shape, lambda b, i: (0,) * a_.ndim)
    tok = lambda w: pl.BlockSpec((1, tm, w), lambda b, i: (b, i, 0))
    row = lambda w: pl.BlockSpec((1, w), lambda b, i: (0, 0))
    return pl.pallas_call(
        functools.partial(_outproj_odd_kernel, hidden_chunk=_hidden_chunk(ffn[0].shape[-1]), heads=RET_HEADS),
        grid=(bn, ln // tm),
        in_specs=[tok(d), tok(vw), tok(vw)] + _mod_specs(d, row_of_batch, (2, 3, 4, 5))
        + [row(d), row(vw), res(w_out)] + [_layer_resident(w, layer) for w in ffn] + [row(d)],
        out_specs=tok(d),
        out_shape=jax.ShapeDtypeStruct((bn, ln, d), F32),
        compiler_params=_cparams(("arbitrary", "arbitrary")),
        name="outproj_ffn_odd",
    )(x, a, gate, mods, mods, mods, mods, g2, norm_g.reshape(1, vw), w_out, *ffn, final_g)


def _rope_heads(t, cos, sin, heads, hk):
    half = hk // 2
    out = []
    for h in range(heads):
        t1 = t[:, h * hk:h * hk + half]
        t2 = t[:, h * hk + half:(h + 1) * hk]
        out.append(t1 * cos - t2 * sin)
        out.append(t2 * cos + t1 * sin)
    return jnp.concatenate(out, axis=-1)


def _inproj_odd_kernel(x_ref, sh_ref, sc_ref, g_ref, cos_ref, sin_ref, w_ref, q_ref, k_ref, v_ref, gate_ref,
                       *, qk, vw, heads):
    h = _norm_mod(x_ref[0], g_ref[...], sh_ref[0], sc_ref[0]).astype(BF16)
    hk = qk // heads
    cos, sin = cos_ref[...], sin_ref[...]
    q = _rope_heads(_dot(h, w_ref[:, 0:qk]), cos, sin, heads, hk)
    q_ref[0] = (q * (hk ** -0.5)).astype(BF16)
    k = _rope_heads(_dot(h, w_ref[:, qk:2 * qk]), cos, sin, heads, hk)
    k_ref[0] = k.astype(BF16)
    v_ref[0] = _dot(h, w_ref[:, 2 * qk:2 * qk + vw]).astype(BF16)
    gate_ref[0] = _dot(h, w_ref[:, 2 * qk + vw:2 * qk + 2 * vw]).astype(BF16)


def _inproj_odd_ctx_kernel(x_ref, sh_ref, sc_ref, g_ref, w_ref, k_ref, v_ref, *, qk, vw):
    h = _norm_mod(x_ref[0], g_ref[...], sh_ref[0], sc_ref[0]).astype(BF16)
    k_ref[0] = _dot(h, w_ref[:, qk:2 * qk]).astype(BF16)
    v_ref[0] = _dot(h, w_ref[:, 2 * qk:2 * qk + vw]).astype(BF16)


def _inproj_odd(x, mods, row_of_batch, g, cos, sin, w, qk, vw):
    bn, ln, d = x.shape
    tm = _token_tile(ln)
    tok = lambda w_: pl.BlockSpec((1, tm, w_), lambda b, i: (b, i, 0))
    half = cos.shape[1]
    return pl.pallas_call(
        functools.partial(_inproj_odd_kernel, qk=qk, vw=vw, heads=RET_HEADS),
        grid=(bn, ln // tm),
        in_specs=[tok(d)] + _mod_specs(d, row_of_batch, (0, 1))
        + [pl.BlockSpec((1, d), lambda b, i: (0, 0)),
           pl.BlockSpec((tm, half), lambda b, i: (i, 0)), pl.BlockSpec((tm, half), lambda b, i: (i, 0)),
           _resident(w.shape, lambda b, i: (0, 0))],
        out_specs=[tok(qk), tok(qk), tok(vw), tok(vw)],
        out_shape=[jax.ShapeDtypeStruct((bn, ln, qk), BF16), jax.ShapeDtypeStruct((bn, ln, qk), BF16),
                   jax.ShapeDtypeStruct((bn, ln, vw), BF16), jax.ShapeDtypeStruct((bn, ln, vw), BF16)],
        compiler_params=_cparams(("arbitrary", "arbitrary")),
        name="inproj_odd",
    )(x, mods, mods, g, cos, sin, w)


def _inproj_odd_ctx(x, mods, row_of_batch, g, w_kv, qk, vw):
    bn, ln, d = x.shape
    tm = _token_tile(ln)
    tok = lambda w_: pl.BlockSpec((1, tm, w_), lambda b, i: (b, i, 0))
    return pl.pallas_call(
        functools.partial(_inproj_odd_ctx_kernel, qk=qk, vw=vw),
        grid=(bn, ln // tm),
        in_specs=[tok(d)] + _mod_specs(d, row_of_batch, (0, 1))
        + [pl.BlockSpec((1, d), lambda b, i: (0, 0)), _resident(w_kv.shape, lambda b, i: (0, 0))],
        out_specs=[tok(qk), tok(vw)],
        out_shape=[jax.ShapeDtypeStruct((bn, ln, qk), BF16), jax.ShapeDtypeStruct((bn, ln, vw), BF16)],
        compiler_params=_cparams(("arbitrary", "arbitrary")),
        name="inproj_odd_ctx",
    )(x, mods, mods, g, w_kv)


def _retention_kernel(lg_ref, q_ref, k_ref, v_ref, kc_ref, vc_ref, o_ref, acc_s, sf_s, sb_s,
                      *, chunk, n_ctx_rows, n_lat_rows):
    t_n = chunk
    h = pl.program_id(1)
    lgf = jnp.full((1, 1), lg_ref[0, h], F32)
    lgb = jnp.full((1, 1), lg_ref[1, h], F32)
    nc_ctx, nc_lat = n_ctx_rows // t_n, n_lat_rows // t_n
    ti = lax.broadcasted_iota(jnp.int32, (t_n, t_n), 0)
    si = lax.broadcasted_iota(jnp.int32, (t_n, t_n), 1)
    diff = (ti - si).astype(F32)
    decay = jnp.where(diff >= 0, jnp.exp(lgf * jnp.maximum(diff, 0.0)), jnp.exp(lgb * jnp.maximum(-diff, 0.0)))
    pos = lax.broadcasted_iota(jnp.int32, (t_n, 1), 0).astype(F32)
    inter_f = jnp.exp(lgf * (pos + 1.0))
    inter_b = jnp.exp(lgb * (t_n - pos))
    wend_f = jnp.exp(lgf * (t_n - 1.0 - pos))
    wend_b = jnp.exp(lgb * pos)
    cd_f = jnp.exp(lgf * t_n)
    cd_b = jnp.exp(lgb * t_n)

    def rows_of(c):
        return pl.ds(pl.multiple_of(c * t_n, t_n), t_n)

    def bump(s_ref, k, v, wend, cd):
        kw = (k.astype(F32) * wend).astype(BF16)
        s_ref[...] = cd * s_ref[...] + _dot_tn(kw, v)

    sf_s[...] = jnp.zeros_like(sf_s)
    sb_s[...] = jnp.zeros_like(sb_s)

    def ctx_f(c, _):
        bump(sf_s, kc_ref[0, rows_of(c), :], vc_ref[0, rows_of(c), :], wend_f, cd_f)
        return 0
    lax.fori_loop(0, nc_ctx, ctx_f, 0)

    def ctx_b(i, _):
        c = nc_ctx - 1 - i
        bump(sb_s, kc_ref[0, rows_of(c), :], vc_ref[0, rows_of(c), :], wend_b, cd_b)
        return 0
    lax.fori_loop(0, nc_ctx, ctx_b, 0)

    def forward_part(rows):
        q, k, v = q_ref[0, rows, :], k_ref[0, rows, :], v_ref[0, rows, :]
        scores = (_dot_nt(q, k) * decay).astype(BF16)
        part = _dot(scores, v) + inter_f * _dot(q, sf_s[...].astype(BF16))
        bump(sf_s, k, v, wend_f, cd_f)
        return part

    def backward_part(rows):
        q, k, v = q_ref[0, rows, :], k_ref[0, rows, :], v_ref[0, rows, :]
        part = inter_b * _dot(q, sb_s[...].astype(BF16))
        bump(sb_s, k, v, wend_b, cd_b)
        return part

    def first_half(i, _):
        rf, rb = rows_of(i), rows_of(nc_lat - 1 - i)
        acc_s[rf, :] = forward_part(rf)
        acc_s[rb, :] = backward_part(rb)
        return 0
    lax.fori_loop(0, nc_lat // 2, first_half, 0)

    def second_half(i, _):
        rf, rb = rows_of(i), rows_of(nc_lat - 1 - i)
        o_ref[0, rf, :] = (acc_s[rf, :] + forward_part(rf)).astype(BF16)
        o_ref[0, rb, :] = (acc_s[rb, :] + backward_part(rb)).astype(BF16)
        return 0
    lax.fori_loop(nc_lat // 2, nc_lat, second_half, 0)


def _retention_mix(q, k, v, k_ctx, v_ctx, log_gamma):
    bn, ll, qk = q.shape
    lc = k_ctx.shape[1]
    vw = v.shape[-1]
    hk, hv = qk // RET_HEADS, vw // RET_HEADS
    chunk = 256 if (ll % 512 == 0 and lc % 256 == 0) else 128
    assert (ll // chunk) % 2 == 0, "latent chunks are visited in forward/backward pairs"
    head = lambda n_rows, w: pl.BlockSpec((1, n_rows, w), lambda b, h: (b, 0, h))
    return pl.pallas_call(
        functools.partial(_retention_kernel, chunk=chunk, n_ctx_rows=lc, n_lat_rows=ll),
        grid=(bn, RET_HEADS),
        in_specs=[
            pl.BlockSpec(memory_space=pltpu.SMEM),
            head(ll, hk), head(ll, hk), head(ll, hv), head(lc, hk), head(lc, hv),
        ],
        out_specs=head(ll, hv),
        out_shape=jax.ShapeDtypeStruct((bn, ll, vw), BF16),
        scratch_shapes=[pltpu.VMEM((ll, hv), F32), pltpu.VMEM((hk, hv), F32), pltpu.VMEM((hk, hv), F32)],
        compiler_params=_cparams(("arbitrary", "arbitrary")),
        name="retention_mix",
    )(log_gamma.astype(F32), q, k, v, k_ctx, v_ctx)


def _grid_rope(n_pos, hk):
    rows = n_pos // GRID_W
    row = jnp.repeat(jnp.arange(rows, dtype=F32), GRID_W)
    col = jnp.tile(jnp.arange(GRID_W, dtype=F32), rows)
    n_freq = hk // 4
    inv = ROPE_BASE ** (-jnp.arange(n_freq, dtype=F32) / n_freq)
    ang = jnp.concatenate([row[:, None] * inv, col[:, None] * inv], -1)
    return jnp.cos(ang), jnp.sin(ang)


def kernel(x, c, ctx, c_ctx, ada_w, ada_b, norm1_g, norm2_g, ab_w_in, ab_w_out, s5_a_re, s5_a_im, s5_log_dt,
           s5_b_re, s5_b_im, s5_c_re, s5_c_im, s5_d, s5_glu_w, s5_glu_b, ml_conv_w, ml_conv_b, ml_wq, ml_wk,
           ml_wv, ml_gate_b, ml_norm_g, ret_w_in, ret_w_out, ret_log_gamma, ret_norm_g, ffn_w1, ffn_w3, ffn_w2,
           final_g):
    bn, ln, d = x.shape
    depth = ada_w.shape[0]
    assert depth == 2, "one S5 || mLSTM layer followed by one retention layer"
    s5_width = s5_d.shape[-1]
    ml_width = ml_norm_g.shape[-1]
    assert ml_width == ML_HEADS * LANES and s5_width % S5_GROUP == 0

    r_pad = -(-(bn + 1) // SUBLANES) * SUBLANES
    vec = jnp.zeros((r_pad, d), F32).at[:bn].set(c).at[bn].set(c_ctx)
    mods_all = _modulation(vec, ada_w, ada_b).reshape(depth, r_pad * N_MOD, 1, d)
    lat_row = lambda b: b
    ctx_row = lambda b: bn

    mods = mods_all[0]
    g1 = norm1_g[0].reshape(1, d)
    g2 = norm2_g[0].reshape(1, d)
    w_in = ab_w_in[0]
    n_gate = w_in.shape[1] - s5_width - 2 * ml_width
    w_in = jnp.pad(w_in, ((0, 0), (0, LANES - n_gate))).astype(BF16)
    u_lat, p_lat = _inproj_even(x, mods, lat_row, g1, w_in, s5_width)
    u_ctx, p_ctx = _inproj_even(ctx, mods, ctx_row, g1, w_in, s5_width)
    mats = _s5_matrices(s5_a_re[0], s5_a_im[0], s5_log_dt[0], s5_b_re[0], s5_b_im[0], s5_c_re[0], s5_c_im[0],
                        s5_d[0])
    ys_ctx, ys_lat = _s5_mix(u_ctx, u_lat, mats)
    m_ctx, m_lat = _mlstm_mix(p_ctx, p_lat, ml_conv_w[0], ml_conv_b[0], ml_wq[0], ml_wk[0], ml_wv[0],
                              ml_gate_b[0])
    glu_w = s5_glu_w[0].astype(BF16)
    w_out = ab_w_out[0].astype(BF16)
    ffn = (ffn_w1.astype(BF16), ffn_w3.astype(BF16), ffn_w2.astype(BF16))
    x = _outproj_even(x, ys_lat, m_lat, p_lat, mods, lat_row, g2, glu_w, s5_glu_b[0], ml_norm_g[0],
                      w_out, ffn, 0)
    ctx = _outproj_even(ctx, ys_ctx, m_ctx, p_ctx, mods, ctx_row, g2, glu_w, s5_glu_b[0], ml_norm_g[0],
                        w_out, ffn, 0)

    mods = mods_all[1]
    g1 = norm1_g[1].reshape(1, d)
    g2 = norm2_g[1].reshape(1, d)
    vw = ret_norm_g.shape[-1]
    qk = (ret_w_in.shape[-1] - 2 * vw) // 2
    w_in = ret_w_in[0].astype(BF16)
    cos, sin = _grid_rope(ln, qk // RET_HEADS)
    q, k, v, gate = _inproj_odd(x, mods, lat_row, g1, cos, sin, w_in, qk, vw)
    k_ctx, v_ctx = _inproj_odd_ctx(ctx, mods, ctx_row, g1, w_in, qk, vw)
    a = _retention_mix(q, k, v, k_ctx, v_ctx, ret_log_gamma[0])
    w_out = ret_w_out[0].astype(BF16)
    return _outproj_odd(x, a, gate, mods, lat_row, g2, ret_norm_g[0], w_out, ffn, 1, final_g.reshape(1, d))
```

```python
import functools
import math

import jax
import jax.numpy as jnp
from jax import lax
from jax.experimental import pallas as pl
from jax.experimental.pallas import tpu as pltpu

F32 = jnp.float32
BF16 = jnp.bfloat16

EPS = 1e-6
N_MOD = 6
GRID_W = 64
ROPE_BASE = 10000.0

S5_GROUP = 16
S5_STATE = 64
S5_STEP = 16
ML_HEADS = 4
ML_CHUNK = 128
RET_HEADS = 4
SUBLANES = 8
LANES = 128
NEG_BIG = -1e30
VMEM_LIMIT = 56 * 1024 * 1024


def _cparams(sem):
    return pltpu.CompilerParams(dimension_semantics=sem, vmem_limit_bytes=VMEM_LIMIT)


def _resident(shape, index_map):
    return pl.BlockSpec(shape, index_map, pipeline_mode=pl.Buffered(1))


def _token_tile(n):
    for t in (512, 256, 128):
        if n % t == 0:
            return t
    raise ValueError(f"sequence length {n} must be a multiple of 128")


def _silu(v):
    return v * jax.nn.sigmoid(v)


def _norm_mod(x, g, shift, scale):
    y = x * lax.rsqrt(jnp.mean(x * x, axis=-1, keepdims=True) + EPS)
    return (y * g) * (1.0 + scale) + shift


def _dot(a, b):
    return jnp.dot(a, b, preferred_element_type=F32)


def _dot_nt(a, b):
    return lax.dot_general(a, b, (((1,), (1,)), ((), ())), preferred_element_type=F32)


def _dot_tn(a, b):
    return lax.dot_general(a, b, (((0,), (0,)), ((), ())), preferred_element_type=F32)


def _mod_kernel(v_ref, w_ref, b_ref, o_ref):
    s = _silu(v_ref[...]).astype(BF16)
    o_ref[0] = _dot(s, w_ref[0].astype(BF16)) + b_ref[0]


def _modulation(vec, ada_w, ada_b):
    depth, d, n = ada_w.shape
    r = vec.shape[0]
    tn = 1024
    return pl.pallas_call(
        _mod_kernel,
        grid=(depth, n // tn),
        in_specs=[
            pl.BlockSpec((r, d), lambda l, j: (0, 0)),
            pl.BlockSpec((1, d, tn), lambda l, j: (l, 0, j)),
            pl.BlockSpec((1, 1, tn), lambda l, j: (l, 0, j)),
        ],
        out_specs=pl.BlockSpec((1, r, tn), lambda l, j: (l, 0, j)),
        out_shape=jax.ShapeDtypeStruct((depth, r, n), F32),
        compiler_params=_cparams(("arbitrary", "arbitrary")),
        name="adaln_modulation",
    )(vec, ada_w, ada_b.reshape(depth, 1, n))


def _inproj_even_kernel(x_ref, sh_ref, sc_ref, g_ref, w_ref, u_ref, p_ref, *, s5_width):
    h = _norm_mod(x_ref[0], g_ref[...], sh_ref[0], sc_ref[0]).astype(BF16)
    p = _dot(h, w_ref[...])
    u_ref[0] = p[:, :s5_width]
    p_ref[0] = p[:, s5_width:]


def _inproj_even(x, mods, row_of_batch, g, w, s5_width):
    bn, ln, d = x.shape
    n = w.shape[1]
    tm = _token_tile(ln)
    return pl.pallas_call(
        functools.partial(_inproj_even_kernel, s5_width=s5_width),
        grid=(bn, ln // tm),
        in_specs=[
            pl.BlockSpec((1, tm, d), lambda b, i: (b, i, 0)),
            pl.BlockSpec((1, 1, d), lambda b, i: (row_of_batch(b) * N_MOD + 0, 0, 0)),
            pl.BlockSpec((1, 1, d), lambda b, i: (row_of_batch(b) * N_MOD + 1, 0, 0)),
            pl.BlockSpec((1, d), lambda b, i: (0, 0)),
            _resident((d, n), lambda b, i: (0, 0)),
        ],
        out_specs=[
            pl.BlockSpec((1, tm, s5_width), lambda b, i: (b, i, 0)),
            pl.BlockSpec((1, tm, n - s5_width), lambda b, i: (b, i, 0)),
        ],
        out_shape=[
            jax.ShapeDtypeStruct((bn, ln, s5_width), F32),
            jax.ShapeDtypeStruct((bn, ln, n - s5_width), F32),
        ],
        compiler_params=_cparams(("arbitrary", "arbitrary")),
        name="inproj_even",
    )(x, mods, mods, g, w)


def _s5_toeplitz_kernel(c_ref, w_ref, d_ref, o_ref, *, steps, chans):
    width = steps * chans
    lag0 = (steps - 1) * chans
    slab = lag0 // LANES
    row = lax.broadcasted_iota(jnp.int32, (chans, LANES), 0)
    lane = lax.broadcasted_iota(jnp.int32, (chans, LANES), 1)
    diag = lane == row + lag0 % LANES
    for n in range(c_ref.shape[0]):
        gen = lax.dot_general(c_ref[n], w_ref[n], (((1,), (1,)), ((), ())),
                              precision=lax.Precision.HIGHEST, preferred_element_type=F32)
        pieces = [gen[:, j * LANES:(j + 1) * LANES] for j in range(2 * width // LANES)]
        pieces[slab] = pieces[slab] + jnp.where(diag, d_ref[n], 0.0)
        gen = jnp.concatenate(pieces, axis=1)
        for t in range(steps):
            off = (steps - 1 - t) * chans
            o_ref[n, t * chans:(t + 1) * chans, :] = gen[:, off:off + width].astype(BF16)


def _s5_toeplitz(c2, w_gen, d_rep, steps, chans):
    n, co, kk = c2.shape
    r = w_gen.shape[1]
    nb = 8
    return pl.pallas_call(
        functools.partial(_s5_toeplitz_kernel, steps=steps, chans=chans),
        grid=(n // nb,),
        in_specs=[pl.BlockSpec((nb, co, kk), lambda i: (i, 0, 0)), pl.BlockSpec((nb, r, kk), lambda i: (i, 0, 0)),
                  pl.BlockSpec((nb, co, LANES), lambda i: (i, 0, 0))],
        out_specs=pl.BlockSpec((nb, steps * chans, steps * chans), lambda i: (i, 0, 0)),
        out_shape=jax.ShapeDtypeStruct((n, steps * chans, steps * chans), BF16),
        compiler_params=_cparams(("arbitrary",)),
        name="s5_toeplitz",
    )(c2, w_gen, d_rep)


def _s5_matrices(a_re, a_im, log_dt, b_re, b_im, c_re, c_im, d_skip):
    k = S5_STEP
    a_re, a_im = a_re.astype(F32), a_im.astype(F32)
    dt = jnp.exp(log_dt.astype(F32))[..., None]
    lam_re, lam_im = a_re * dt, a_im * dt
    steps = jnp.arange(k + 1, dtype=F32)[:, None, None, None]
    mag = jnp.exp(steps * lam_re)
    pr, pi = mag * jnp.cos(steps * lam_im), mag * jnp.sin(steps * lam_im)
    den = a_re * a_re + a_im * a_im
    nr = pr[1] - 1.0
    coef_re = (nr * a_re + pi[1] * a_im) / den
    coef_im = (pi[1] * a_re - nr * a_im) / den
    bt_re = jnp.swapaxes(b_re.astype(F32), -1, -2)
    bt_im = jnp.swapaxes(b_im.astype(F32), -1, -2)
    bb_re = coef_re[:, :, None] * bt_re - coef_im[:, :, None] * bt_im
    bb_im = coef_re[:, :, None] * bt_im + coef_im[:, :, None] * bt_re
    wr = pr[:, :, :, None] * bb_re - pi[:, :, :, None] * bb_im
    wi = pr[:, :, :, None] * bb_im + pi[:, :, :, None] * bb_re
    c_re, c_im = c_re.astype(F32), c_im.astype(F32)
    g_n, c_n, p_n = a_re.shape[1], b_re.shape[-1], a_re.shape[-1]
    def lag_rows(w_re, w_im):
        return jnp.transpose(jnp.concatenate([w_re, -w_im], -1), (1, 0, 2, 3)).reshape(g_n, k * c_n, 2 * p_n)

    w_gen = jnp.concatenate(
        [jnp.pad(lag_rows(wr[k - 1::-1, 0], wi[k - 1::-1, 0]), ((0, 0), (0, k * c_n), (0, 0))),
         jnp.pad(lag_rows(wr[:k, 1], wi[:k, 1]), ((0, 0), ((k - 1) * c_n, c_n), (0, 0)))], axis=-1)
    c2 = jnp.concatenate([c_re[0], c_im[0], c_re[1], c_im[1]], -1)
    d_rep = jnp.broadcast_to(d_skip.astype(F32).reshape(g_n, c_n, 1), (g_n, c_n, LANES))
    toep = _s5_toeplitz(c2, w_gen, d_rep, k, c_n)
    endw = jnp.concatenate([wr[k - 1::-1, 0], wr[:k, 1], wi[k - 1::-1, 0], wi[:k, 1]], -1)
    endw = jnp.transpose(endw, (1, 0, 2, 3)).reshape(g_n, k * c_n, 4 * p_n)

    def out_cols(d, powers_re, powers_im):
        cr, ci = c_re[d][None], c_im[d][None]
        pre, pim = powers_re[:, :, None, :], powers_im[:, :, None, :]
        return cr * pre - ci * pim, -(cr * pim + ci * pre)

    of_re, of_im = out_cols(0, pr[1:, 0], pi[1:, 0])
    ob_re, ob_im = out_cols(1, pr[k:0:-1, 1], pi[k:0:-1, 1])
    outw_t = jnp.concatenate([of_re, ob_re, of_im, ob_im], -1)
    outw_t = jnp.transpose(outw_t, (1, 0, 2, 3)).reshape(g_n, k * c_n, 4 * p_n)
    a0 = jnp.concatenate([pr[k, 0], pr[k, 1]], -1)
    a1 = jnp.concatenate([pi[k, 0], pi[k, 1]], -1)
    return toep.astype(BF16), endw.astype(BF16), outw_t.astype(BF16), a0, a1


S5_LANE_GROUPS = LANES // S5_GROUP


def _chunk_transpose(xs):
    n_arr = len(xs)
    lane_chunk = lax.broadcasted_iota(jnp.int32, xs[0].shape, 1) // S5_GROUP
    d = 1
    while d < n_arr:
        keep = (lane_chunk & d) == 0
        out = list(xs)
        for i in range(n_arr):
            if i & d:
                continue
            lo, hi = xs[i], xs[i + d]
            out[i] = jnp.where(keep, lo, pltpu.roll(hi, d * S5_GROUP, 1))
            out[i + d] = jnp.where(keep, pltpu.roll(lo, LANES - d * S5_GROUP, 1), hi)
        xs = out
        d *= 2
    return xs


def _s5_kernel(uc_ref, ul_ref, toep_ref, endw_ref, outw_ref, a0_ref, a1_ref, yc_ref, yl_ref,
               uc_s, ul_s, e_s, yc_s, yl_s, *, batch, n_ctx, n_lat):
    p = S5_STATE
    gl = S5_LANE_GROUPS
    k = S5_STEP

    def stack(u_ref, us_ref, n_blk):
        for b in range(batch):
            for half in range(k // gl):
                xs = [u_ref[b, pl.ds(half * gl + s, n_blk, stride=k), :] for s in range(gl)]
                ys = _chunk_transpose(xs)
                for g in range(gl):
                    us_ref[g, b * n_blk:(b + 1) * n_blk, half * LANES:(half + 1) * LANES] = ys[g].astype(BF16)

    stack(uc_ref, uc_s, n_ctx)
    stack(ul_ref, ul_s, n_lat)

    def increments(us_ref, n_blk, first_blk):
        for g in range(gl):
            e = _dot(us_ref[g], endw_ref[g])
            for b in range(batch):
                for part in range(2):
                    e_s[part, b, pl.ds(first_blk * gl + g, n_blk, stride=gl), :] = (
                        e[b * n_blk:(b + 1) * n_blk, part * LANES:(part + 1) * LANES])

    increments(uc_s, n_ctx, 0)
    increments(ul_s, n_lat, n_ctx)

    a0 = a0_ref[...]
    a1 = a1_ref[...]
    fwd_lane = lax.broadcasted_iota(jnp.int32, (batch, gl, 2 * p), 2) < p

    def scan(first_blk, n, carry):
        def body(i, c):
            s0, s1 = c
            rf = pl.ds(pl.multiple_of((first_blk + i) * gl, gl), gl)
            rb = pl.ds(pl.multiple_of((first_blk + n - 1 - i) * gl, gl), gl)
            e0 = jnp.where(fwd_lane, e_s[0, :, rf, :], e_s[0, :, rb, :])
            e1 = jnp.where(fwd_lane, e_s[1, :, rf, :], e_s[1, :, rb, :])
            e_s[0, :, rf, 0:p] = s0[:, :, 0:p]
            e_s[0, :, rb, p:2 * p] = s0[:, :, p:2 * p]
            e_s[1, :, rf, 0:p] = s1[:, :, 0:p]
            e_s[1, :, rb, p:2 * p] = s1[:, :, p:2 * p]
            return a0 * s0 - a1 * s1 + e0, a0 * s1 + a1 * s0 + e1
        return lax.fori_loop(0, n, body, carry)

    zero = jnp.zeros((batch, gl, 2 * p), F32)
    carry = scan(0, n_ctx, (zero, zero))
    scan(n_ctx, n_lat, carry)

    def outputs(us_ref, ys_ref, y_ref, n_blk, first_blk):
        for g in range(gl):
            state = jnp.concatenate(
                [jnp.concatenate([e_s[part, b, pl.ds(first_blk * gl + g, n_blk, stride=gl), :]
                                  for b in range(batch)], axis=0) for part in range(2)], axis=1)
            ys_ref[g] = _dot_nt(us_ref[g], toep_ref[g]) + _dot_nt(state.astype(BF16), outw_ref[g])
        for b in range(batch):
            for half in range(k // gl):
                ys = [ys_ref[g, b * n_blk:(b + 1) * n_blk, half * LANES:(half + 1) * LANES] for g in range(gl)]
                xs = _chunk_transpose(ys)
                for s in range(gl):
                    y_ref[b, pl.ds(half * gl + s, n_blk, stride=k), :] = xs[s]

    outputs(uc_s, yc_s, yc_ref, n_ctx, 0)
    outputs(ul_s, yl_s, yl_ref, n_lat, n_ctx)


def _s5_mix(u_ctx, u_lat, mats):
    toep, endw, outw, a0, a1 = mats
    bn, lc, w = u_ctx.shape
    ll = u_lat.shape[1]
    gl = S5_LANE_GROUPS
    kc = S5_STEP * S5_GROUP
    n_ctx, n_lat = lc // S5_STEP, ll // S5_STEP
    bb = 4 if bn % 4 == 0 else bn
    seq = lambda n_rows: pl.BlockSpec((bb, n_rows, LANES), lambda i, j: (j, 0, i))
    grp = lambda r, c: pl.BlockSpec((gl, r, c), lambda i, j: (i, 0, 0))
    return pl.pallas_call(
        functools.partial(_s5_kernel, batch=bb, n_ctx=n_ctx, n_lat=n_lat),
        grid=(w // LANES, bn // bb),
        in_specs=[seq(lc), seq(ll), grp(kc, kc), grp(kc, kc), grp(kc, kc),
                  pl.BlockSpec((gl, 2 * S5_STATE), lambda i, j: (i, 0)),
                  pl.BlockSpec((gl, 2 * S5_STATE), lambda i, j: (i, 0))],
        out_specs=[seq(lc), seq(ll)],
        out_shape=[jax.ShapeDtypeStruct((bn, lc, w), F32), jax.ShapeDtypeStruct((bn, ll, w), F32)],
        scratch_shapes=[
            pltpu.VMEM((gl, bb * n_ctx, kc), BF16), pltpu.VMEM((gl, bb * n_lat, kc), BF16),
            pltpu.VMEM((2, bb, (n_ctx + n_lat) * gl, LANES), F32),
            pltpu.VMEM((gl, bb * n_ctx, kc), F32), pltpu.VMEM((gl, bb * n_lat, kc), F32),
        ],
        compiler_params=_cparams(("arbitrary", "arbitrary")),
        name="s5_scan",
    )(u_ctx, u_lat, toep, endw, outw, a0, a1)


def _mlstm_kernel(pc_ref, pl_ref, cw_ref, cb_ref, wq_ref, wkt_ref, wv_ref, gb_ref, mc_ref, ml_ref,
                  q_s, kt_s, vx_s, h_s, st_s, bcol_s, mcol_s, rrow_s, xpad_s, *, n_ctx_rows, n_lat_rows):
    t_n = ML_CHUNK
    width = ML_HEADS * LANES
    nc_ctx, nc_lat = n_ctx_rows // t_n, n_lat_rows // t_n
    nc = nc_ctx + nc_lat
    scale_k = LANES ** -0.5

    def project(p_ref, n_rows, base):
        ones = jnp.ones((n_rows, LANES), BF16)
        pad = SUBLANES
        xpad_s[0:pad, :] = jnp.zeros((pad, LANES), F32)
        xpad_s[pad + n_rows:2 * pad + n_rows, :] = jnp.zeros((pad, LANES), F32)
        for h in range(ML_HEADS):
            cols = slice(h * LANES, (h + 1) * LANES)
            xm = p_ref[0, :, cols]
            xpad_s[pad:pad + n_rows, :] = xm
            taps = cw_ref[:, cols]
            n_tap = taps.shape[0]
            assert n_tap // 2 <= pad
            acc = jnp.zeros_like(xm) + cb_ref[:, cols]
            for j in range(n_tap):
                d = j - n_tap // 2
                sh = xm if d == 0 else xpad_s[pad + d:pad + d + n_rows, :]
                acc = acc + sh * taps[j:j + 1, :]
            xc = _silu(acc).astype(BF16)
            q_s[base:base + n_rows, cols] = _dot(xc, wq_ref[h]).astype(BF16)
            kt = (_dot_nt(wkt_ref[h], xc) * scale_k).astype(BF16)
            for c in range(n_rows // t_n):
                kt_s[base // t_n + c, cols, :] = kt[:, c * t_n:(c + 1) * t_n]
            vx_s[base:base + n_rows, 2 * h * LANES:(2 * h + 1) * LANES] = (
                _dot(xm.astype(BF16), wv_ref[h]).astype(BF16))
            vx_s[base:base + n_rows, (2 * h + 1) * LANES:(2 * h + 2) * LANES] = ones

    project(pc_ref, n_ctx_rows, 0)
    project(pl_ref, n_lat_rows, n_ctx_rows)

    ti = lax.broadcasted_iota(jnp.int32, (t_n, t_n), 0)
    si = lax.broadcasted_iota(jnp.int32, (t_n, t_n), 1)
    tri_f = (si <= ti).astype(BF16)
    tri_b = (si >= ti).astype(BF16)
    lane = lax.broadcasted_iota(jnp.int32, (t_n, LANES), 1)
    trow = lax.broadcasted_iota(jnp.int32, (t_n, LANES), 0)
    gate_off = 2 * width

    def gate_prep(p_ref, n_chunks, base_chunk):
        def body(c, _):
            rows = pl.ds(pl.multiple_of(c * t_n, t_n), t_n)
            gcol = p_ref[0, rows, gate_off:gate_off + LANES] + gb_ref[...]
            lf = jax.nn.log_sigmoid(gcol)
            hi = lf.astype(BF16)
            lo = (lf - hi.astype(F32)).astype(BF16)
            pre = _dot(tri_f, hi) + _dot(tri_f, lo)
            suf = _dot(tri_b, hi) + _dot(tri_b, lo)
            bsum = jnp.where(lane < 2 * ML_HEADS, pre, suf)
            rcol = gcol - pltpu.roll(bsum, LANES - ML_HEADS, 1)
            pmax, smax = rcol, rcol
            step = 1
            while step < t_n:
                pmax = jnp.maximum(pmax, jnp.where(trow >= step, pltpu.roll(pmax, step, 0), NEG_BIG))
                smax = jnp.maximum(smax, jnp.where(trow < t_n - step, pltpu.roll(smax, t_n - step, 0), NEG_BIG))
                step *= 2
            bcol_s[base_chunk + c] = bsum
            mcol_s[base_chunk + c] = jnp.where(lane < 2 * ML_HEADS, pmax, smax)
            rrow_s[base_chunk + c] = rcol.T[0:4 * ML_HEADS, :]
            return 0
        lax.fori_loop(0, n_chunks, body, 0)

    gate_prep(pc_ref, nc_ctx, 0)
    gate_prep(pl_ref, nc_lat, nc_ctx)

    def one_dir(c, h, d, m_in):
        cols = slice(h * LANES, (h + 1) * LANES)
        xcols = slice(2 * h * LANES, (2 * h + 2) * LANES)
        li = 2 * d * ML_HEADS + h
        mask = (si <= ti) if d == 0 else (si >= ti)
        last = t_n - 1 if d == 0 else 0
        rows = pl.ds(pl.multiple_of(c * t_n, t_n), t_n)
        q = q_s[rows, cols]
        kt = kt_s[c, cols, :]
        vx = vx_s[rows, xcols]
        r_row = rrow_s[c][li:li + 1, :]
        run_max = jnp.broadcast_to(mcol_s[c][:, li:li + 1], (t_n, LANES))
        b_rep = jnp.broadcast_to(bcol_s[c][:, li + ML_HEADS:li + ML_HEADS + 1], (t_n, LANES))
        mm = jnp.maximum(m_in, run_max)
        dmat = jnp.exp(jnp.where(mask, r_row - mm, NEG_BIG))
        sm = (dmat * _dot(q, kt)).astype(BF16)
        intra = _dot(sm, vx)
        cross = _dot(q, st_s[2 * h + d].astype(BF16))
        inter = jnp.exp(m_in - mm)
        num = intra[:, :LANES] + inter * cross[:, :LANES]
        den = intra[:, LANES:] + inter * cross[:, LANES:]
        h_s[rows, cols] = h_s[rows, cols] + num / jnp.maximum(jnp.abs(den), jnp.exp(-b_rep - mm))
        r_top = run_max[last:last + 1, :]
        b_last = b_rep[last:last + 1, :]
        g_max = b_last + r_top
        m_new = jnp.maximum(b_last + m_in, g_max)
        dec = jnp.exp(b_last + m_in - m_new)
        inj = jnp.exp(g_max - m_new)
        kw = (kt.astype(F32) * (jnp.exp(r_row - r_top) * inj)).astype(BF16)
        inc = _dot(kw, vx)
        st_s[2 * h + d] = jnp.concatenate([dec, dec], axis=1) * st_s[2 * h + d] + inc
        return m_new

    st_s[...] = jnp.zeros_like(st_s)
    h_s[...] = jnp.zeros_like(h_s)

    def body(i, carry):
        cb = jnp.where(i < nc_ctx, nc_ctx - 1 - i, nc + nc_ctx - 1 - i)
        return tuple(one_dir(i if d == 0 else cb, h, d, carry[2 * h + d])
                     for h in range(ML_HEADS) for d in range(2))

    zero = jnp.zeros((1, LANES), F32)
    lax.fori_loop(0, nc, body, (zero,) * (2 * ML_HEADS))

    mc_ref[0] = h_s[0:n_ctx_rows, :].astype(BF16)
    ml_ref[0] = h_s[n_ctx_rows:n_ctx_rows + n_lat_rows, :].astype(BF16)


def _mlstm_mix(p_ctx, p_lat, conv_w, conv_b, wq, wk, wv, gate_b):
    bn, lc, pw = p_ctx.shape
    ll = p_lat.shape[1]
    width = ML_HEADS * LANES
    lt = lc + ll
    nct = lt // ML_CHUNK
    gb = jnp.zeros((1, LANES), F32).at[0, :4 * ML_HEADS].set(gate_b.astype(F32))
    full2 = lambda a: pl.BlockSpec(a.shape, lambda b: (0,) * a.ndim)
    conv_b2 = conv_b.reshape(1, width)
    assert ML_CHUNK == LANES
    wq, wkt, wv = wq.astype(BF16), jnp.swapaxes(wk, 1, 2).astype(BF16), wv.astype(BF16)
    return pl.pallas_call(
        functools.partial(_mlstm_kernel, n_ctx_rows=lc, n_lat_rows=ll),
        grid=(bn,),
        in_specs=[
            pl.BlockSpec((1, lc, pw), lambda b: (b, 0, 0)),
            pl.BlockSpec((1, ll, pw), lambda b: (b, 0, 0)),
            full2(conv_w), full2(conv_b2), full2(wq), full2(wkt), full2(wv), full2(gb),
        ],
        out_specs=[
            pl.BlockSpec((1, lc, width), lambda b: (b, 0, 0)),
            pl.BlockSpec((1, ll, width), lambda b: (b, 0, 0)),
        ],
        out_shape=[jax.ShapeDtypeStruct((bn, lc, width), BF16), jax.ShapeDtypeStruct((bn, ll, width), BF16)],
        scratch_shapes=[
            pltpu.VMEM((lt, width), BF16),
            pltpu.VMEM((nct, width, ML_CHUNK), BF16),
            pltpu.VMEM((lt, 2 * width), BF16),
            pltpu.VMEM((lt, width), F32),
            pltpu.VMEM((2 * ML_HEADS, LANES, 2 * LANES), F32),
            pltpu.VMEM((nct, ML_CHUNK, LANES), F32), pltpu.VMEM((nct, ML_CHUNK, LANES), F32),
            pltpu.VMEM((nct, 4 * ML_HEADS, ML_CHUNK), F32),
            pltpu.VMEM((max(lc, ll) + 2 * SUBLANES, LANES), F32),
        ],
        compiler_params=_cparams(("arbitrary",)),
        name="mlstm_mix",
    )(p_ctx, p_lat, conv_w, conv_b2, wq, wkt, wv, gb)


def _ffn_tail(x, y, gate_mix, g2, shift, scale, gate_ffn, w1_ref, w3_ref, w2_ref, hidden_chunk):
    x1 = x + gate_mix * y
    h2 = _norm_mod(x1, g2, shift, scale).astype(BF16)
    hidden = w1_ref.shape[1]
    acc = jnp.zeros_like(x1)
    start = 0
    while start < hidden:
        cs = slice(start, min(start + hidden_chunk, hidden))
        z = (_silu(_dot(h2, w1_ref[:, cs])) * _dot(h2, w3_ref[:, cs])).astype(BF16)
        acc = acc + _dot(z, w2_ref[cs, :])
        start += hidden_chunk
    return x1 + gate_ffn * acc


def _final_norm(x, g):
    return x * lax.rsqrt(jnp.mean(x * x, axis=-1, keepdims=True) + EPS) * g


def _outproj_even_kernel(x_ref, ys_ref, m_ref, og_ref, gm_ref, sh_ref, sc_ref, gf_ref, g2_ref, gw_ref, gbias_ref,
                         ng_ref, wo_ref, w1_ref, w3_ref, w2_ref, o_ref, *, hidden_chunk, heads):
    ys = jax.nn.gelu(ys_ref[0])
    s = ys * jax.nn.sigmoid(_dot(ys.astype(BF16), gw_ref[...]) + gbias_ref[...])
    dh = m_ref.shape[-1] // heads
    gated = []
    for h in range(heads):
        cols = slice(h * dh, (h + 1) * dh)
        hh = m_ref[0, :, cols].astype(F32)
        mu = jnp.mean(hh, axis=1, keepdims=True)
        cen = hh - mu
        var = jnp.mean(cen * cen, axis=1, keepdims=True)
        normed = cen * lax.rsqrt(var + EPS) * ng_ref[:, cols]
        gated.append((jax.nn.sigmoid(og_ref[0, :, cols]) * normed).astype(BF16))
    sw = ys_ref.shape[-1]
    y = _dot(s.astype(BF16), wo_ref[0:sw, :]) + _dot(jnp.concatenate(gated, axis=1), wo_ref[sw:, :])
    o_ref[0] = _ffn_tail(x_ref[0], y, gm_ref[0], g2_ref[...], sh_ref[0], sc_ref[0], gf_ref[0],
                         w1_ref, w3_ref, w2_ref, hidden_chunk)


def _outproj_odd_kernel(x_ref, a_ref, gate_ref, gm_ref, sh_ref, sc_ref, gf_ref, g2_ref, ng_ref, wo_ref, w1_ref,
                        w3_ref, w2_ref, fg_ref, o_ref, *, hidden_chunk, heads):
    hv = a_ref.shape[-1] // heads
    y = jnp.zeros(x_ref.shape[1:], F32)
    for h in range(heads):
        cols = slice(h * hv, (h + 1) * hv)
        o = a_ref[0, :, cols].astype(F32)
        mu = jnp.mean(o, axis=1, keepdims=True)
        cen = o - mu
        var = jnp.mean(cen * cen, axis=1, keepdims=True)
        normed = cen * lax.rsqrt(var + EPS) * ng_ref[:, cols]
        gated = (_silu(gate_ref[0, :, cols].astype(F32)) * normed).astype(BF16)
        y = y + _dot(gated, wo_ref[cols, :])
    out = _ffn_tail(x_ref[0], y, gm_ref[0], g2_ref[...], sh_ref[0], sc_ref[0], gf_ref[0],
                    w1_ref, w3_ref, w2_ref, hidden_chunk)
    o_ref[0] = _final_norm(out, fg_ref[...])


MXU_WIDTH = 256


def _hidden_chunk(hidden):
    return -(-hidden // (2 * MXU_WIDTH)) * MXU_WIDTH


def _mod_specs(d, row_of_batch, slots):
    return [pl.BlockSpec((1, 1, d), lambda b, i, s=s: (row_of_batch(b) * N_MOD + s, 0, 0)) for s in slots]


def _layer_resident(w, layer):
    return _resident((None,) + w.shape[1:], lambda b, i: (layer,) + (0,) * (w.ndim - 1))


def _outproj_even(x, ys, m, p, mods, row_of_batch, g2, glu_w, glu_b, norm_g, w_out, ffn, layer):
    bn, ln, d = x.shape
    sw, mw = ys.shape[-1], m.shape[-1]
    tm = _token_tile(ln)
    res = lambda a: _resident(a.shape, lambda b, i: (0,) * a.ndim)
    tok = lambda w: pl.BlockSpec((1, tm, w), lambda b, i: (b, i, 0))
    row = lambda w: pl.BlockSpec((1, w), lambda b, i: (0, 0))
    return pl.pallas_call(
        functools.partial(_outproj_even_kernel, hidden_chunk=_hidden_chunk(ffn[0].shape[-1]), heads=ML_HEADS),
        grid=(bn, ln // tm),
        in_specs=[tok(d), tok(sw), tok(mw), pl.BlockSpec((1, tm, mw), lambda b, i: (b, i, 1))]
        + _mod_specs(d, row_of_batch, (2, 3, 4, 5))
        + [row(d), res(glu_w), row(sw), row(mw), res(w_out)] + [_layer_resident(w, layer) for w in ffn],
        out_specs=tok(d),
        out_shape=jax.ShapeDtypeStruct((bn, ln, d), F32),
        compiler_params=_cparams(("arbitrary", "arbitrary")),
        name="outproj_ffn_even",
    )(x, ys, m, p, mods, mods, mods, mods, g2, glu_w, glu_b.reshape(1, sw), norm_g.reshape(1, mw), w_out, *ffn)


def _outproj_odd(x, a, gate, mods, row_of_batch, g2, norm_g, w_out, ffn, layer, final_g):
    bn, ln, d = x.shape
    vw = a.shape[-1]
    tm = _token_tile(ln)
    res = lambda a_: _resident(a_.shape, lambda b, i: (0,) * a_.ndim)
    tok = lambda w: pl.BlockSpec((1, tm, w), lambda b, i: (b, i, 0))
    row = lambda w: pl.BlockSpec((1, w), lambda b, i: (0, 0))
    return pl.pallas_call(
        functools.partial(_outproj_odd_kernel, hidden_chunk=_hidden_chunk(ffn[0].shape[-1]), heads=RET_HEADS),
        grid=(bn, ln // tm),
        in_specs=[tok(d), tok(vw), tok(vw)] + _mod_specs(d, row_of_batch, (2, 3, 4, 5))
        + [row(d), row(vw), res(w_out)] + [_layer_resident(w, layer) for w in ffn] + [row(d)],
        out_specs=tok(d),
        out_shape=jax.ShapeDtypeStruct((bn, ln, d), F32),
        compiler_params=_cparams(("arbitrary", "arbitrary")),
        name="outproj_ffn_odd",
    )(x, a, gate, mods, mods, mods, mods, g2, norm_g.reshape(1, vw), w_out, *ffn, final_g)


def _rope_heads(t, cos, sin, heads, hk):
    half = hk // 2
    out = []
    for h in range(heads):
        t1 = t[:, h * hk:h * hk + half]
        t2 = t[:, h * hk + half:(h + 1) * hk]
        out.append(t1 * cos - t2 * sin)
        out.append(t2 * cos + t1 * sin)
    return jnp.concatenate(out, axis=-1)


def _inproj_odd_kernel(x_ref, sh_ref, sc_ref, g_ref, cos_ref, sin_ref, w_ref, q_ref, k_ref, v_ref, gate_ref,
                       *, qk, vw, heads):
    h = _norm_mod(x_ref[0], g_ref[...], sh_ref[0], sc_ref[0]).astype(BF16)
    hk = qk // heads
    cos, sin = cos_ref[...], sin_ref[...]
    q = _rope_heads(_dot(h, w_ref[:, 0:qk]), cos, sin, heads, hk)
    q_ref[0] = (q * (hk ** -0.5)).astype(BF16)
    k = _rope_heads(_dot(h, w_ref[:, qk:2 * qk]), cos, sin, heads, hk)
    k_ref[0] = k.astype(BF16)
    v_ref[0] = _dot(h, w_ref[:, 2 * qk:2 * qk + vw]).astype(BF16)
    gate_ref[0] = _dot(h, w_ref[:, 2 * qk + vw:2 * qk + 2 * vw]).astype(BF16)


def _inproj_odd_ctx_kernel(x_ref, sh_ref, sc_ref, g_ref, w_ref, k_ref, v_ref, *, qk, vw):
    h = _norm_mod(x_ref[0], g_ref[...], sh_ref[0], sc_ref[0]).astype(BF16)
    k_ref[0] = _dot(h, w_ref[:, qk:2 * qk]).astype(BF16)
    v_ref[0] = _dot(h, w_ref[:, 2 * qk:2 * qk + vw]).astype(BF16)


def _inproj_odd(x, mods, row_of_batch, g, cos, sin, w, qk, vw):
    bn, ln, d = x.shape
    tm = _token_tile(ln)
    tok = lambda w_: pl.BlockSpec((1, tm, w_), lambda b, i: (b, i, 0))
    half = cos.shape[1]
    return pl.pallas_call(
        functools.partial(_inproj_odd_kernel, qk=qk, vw=vw, heads=RET_HEADS),
        grid=(bn, ln // tm),
        in_specs=[tok(d)] + _mod_specs(d, row_of_batch, (0, 1))
        + [pl.BlockSpec((1, d), lambda b, i: (0, 0)),
           pl.BlockSpec((tm, half), lambda b, i: (i, 0)), pl.BlockSpec((tm, half), lambda b, i: (i, 0)),
           _resident(w.shape, lambda b, i: (0, 0))],
        out_specs=[tok(qk), tok(qk), tok(vw), tok(vw)],
        out_shape=[jax.ShapeDtypeStruct((bn, ln, qk), BF16), jax.ShapeDtypeStruct((bn, ln, qk), BF16),
                   jax.ShapeDtypeStruct((bn, ln, vw), BF16), jax.ShapeDtypeStruct((bn, ln, vw), BF16)],
        compiler_params=_cparams(("arbitrary", "arbitrary")),
        name="inproj_odd",
    )(x, mods, mods, g, cos, sin, w)


def _inproj_odd_ctx(x, mods, row_of_batch, g, w_kv, qk, vw):
    bn, ln, d = x.shape
    tm = _token_tile(ln)
    tok = lambda w_: pl.BlockSpec((1, tm, w_), lambda b, i: (b, i, 0))
    return pl.pallas_call(
        functools.partial(_inproj_odd_ctx_kernel, qk=qk, vw=vw),
        grid=(bn, ln // tm),
        in_specs=[tok(d)] + _mod_specs(d, row_of_batch, (0, 1))
        + [pl.BlockSpec((1, d), lambda b, i: (0, 0)), _resident(w_kv.shape, lambda b, i: (0, 0))],
        out_specs=[tok(qk), tok(vw)],
        out_shape=[jax.ShapeDtypeStruct((bn, ln, qk), BF16), jax.ShapeDtypeStruct((bn, ln, vw), BF16)],
        compiler_params=_cparams(("arbitrary", "arbitrary")),
        name="inproj_odd_ctx",
    )(x, mods, mods, g, w_kv)


def _retention_kernel(lg_ref, q_ref, k_ref, v_ref, kc_ref, vc_ref, o_ref, acc_s, sf_s, sb_s,
                      *, chunk, n_ctx_rows, n_lat_rows):
    t_n = chunk
    h = pl.program_id(1)
    lgf = jnp.full((1, 1), lg_ref[0, h], F32)
    lgb = jnp.full((1, 1), lg_ref[1, h], F32)
    nc_ctx, nc_lat = n_ctx_rows // t_n, n_lat_rows // t_n
    ti = lax.broadcasted_iota(jnp.int32, (t_n, t_n), 0)
    si = lax.broadcasted_iota(jnp.int32, (t_n, t_n), 1)
    diff = (ti - si).astype(F32)
    decay = jnp.where(diff >= 0, jnp.exp(lgf * jnp.maximum(diff, 0.0)), jnp.exp(lgb * jnp.maximum(-diff, 0.0)))
    pos = lax.broadcasted_iota(jnp.int32, (t_n, 1), 0).astype(F32)
    inter_f = jnp.exp(lgf * (pos + 1.0))
    inter_b = jnp.exp(lgb * (t_n - pos))
    wend_f = jnp.exp(lgf * (t_n - 1.0 - pos))
    wend_b = jnp.exp(lgb * pos)
    cd_f = jnp.exp(lgf * t_n)
    cd_b = jnp.exp(lgb * t_n)

    def rows_of(c):
        return pl.ds(pl.multiple_of(c * t_n, t_n), t_n)

    def bump(s_ref, k, v, wend, cd):
        kw = (k.astype(F32) * wend).astype(BF16)
        s_ref[...] = cd * s_ref[...] + _dot_tn(kw, v)

    sf_s[...] = jnp.zeros_like(sf_s)
    sb_s[...] = jnp.zeros_like(sb_s)

    def ctx_f(c, _):
        bump(sf_s, kc_ref[0, rows_of(c), :], vc_ref[0, rows_of(c), :], wend_f, cd_f)
        return 0
    lax.fori_loop(0, nc_ctx, ctx_f, 0)

    def ctx_b(i, _):
        c = nc_ctx - 1 - i
        bump(sb_s, kc_ref[0, rows_of(c), :], vc_ref[0, rows_of(c), :], wend_b, cd_b)
        return 0
    lax.fori_loop(0, nc_ctx, ctx_b, 0)

    def forward_part(rows):
        q, k, v = q_ref[0, rows, :], k_ref[0, rows, :], v_ref[0, rows, :]
        scores = (_dot_nt(q, k) * decay).astype(BF16)
        part = _dot(scores, v) + inter_f * _dot(q, sf_s[...].astype(BF16))
        bump(sf_s, k, v, wend_f, cd_f)
        return part

    def backward_part(rows):
        q, k, v = q_ref[0, rows, :], k_ref[0, rows, :], v_ref[0, rows, :]
        part = inter_b * _dot(q, sb_s[...].astype(BF16))
        bump(sb_s, k, v, wend_b, cd_b)
        return part

    def first_half(i, _):
        rf, rb = rows_of(i), rows_of(nc_lat - 1 - i)
        acc_s[rf, :] = forward_part(rf)
        acc_s[rb, :] = backward_part(rb)
        return 0
    lax.fori_loop(0, nc_lat // 2, first_half, 0)

    def second_half(i, _):
        rf, rb = rows_of(i), rows_of(nc_lat - 1 - i)
        o_ref[0, rf, :] = (acc_s[rf, :] + forward_part(rf)).astype(BF16)
        o_ref[0, rb, :] = (acc_s[rb, :] + backward_part(rb)).astype(BF16)
        return 0
    lax.fori_loop(nc_lat // 2, nc_lat, second_half, 0)


def _retention_mix(q, k, v, k_ctx, v_ctx, log_gamma):
    bn, ll, qk = q.shape
    lc = k_ctx.shape[1]
    vw = v.shape[-1]
    hk, hv = qk // RET_HEADS, vw // RET_HEADS
    chunk = 256 if (ll % 512 == 0 and lc % 256 == 0) else 128
    assert (ll // chunk) % 2 == 0, "latent chunks are visited in forward/backward pairs"
    head = lambda n_rows, w: pl.BlockSpec((1, n_rows, w), lambda b, h: (b, 0, h))
    return pl.pallas_call(
        functools.partial(_retention_kernel, chunk=chunk, n_ctx_rows=lc, n_lat_rows=ll),
        grid=(bn, RET_HEADS),
        in_specs=[
            pl.BlockSpec(memory_space=pltpu.SMEM),
            head(ll, hk), head(ll, hk), head(ll, hv), head(lc, hk), head(lc, hv),
        ],
        out_specs=head(ll, hv),
        out_shape=jax.ShapeDtypeStruct((bn, ll, vw), BF16),
        scratch_shapes=[pltpu.VMEM((ll, hv), F32), pltpu.VMEM((hk, hv), F32), pltpu.VMEM((hk, hv), F32)],
        compiler_params=_cparams(("arbitrary", "arbitrary")),
        name="retention_mix",
    )(log_gamma.astype(F32), q, k, v, k_ctx, v_ctx)


def _grid_rope(n_pos, hk):
    rows = n_pos // GRID_W
    row = jnp.repeat(jnp.arange(rows, dtype=F32), GRID_W)
    col = jnp.tile(jnp.arange(GRID_W, dtype=F32), rows)
    n_freq = hk // 4
    inv = ROPE_BASE ** (-jnp.arange(n_freq, dtype=F32) / n_freq)
    ang = jnp.concatenate([row[:, None] * inv, col[:, None] * inv], -1)
    return jnp.cos(ang), jnp.sin(ang)


def kernel(x, c, ctx, c_ctx, ada_w, ada_b, norm1_g, norm2_g, ab_w_in, ab_w_out, s5_a_re, s5_a_im, s5_log_dt,
           s5_b_re, s5_b_im, s5_c_re, s5_c_im, s5_d, s5_glu_w, s5_glu_b, ml_conv_w, ml_conv_b, ml_wq, ml_wk,
           ml_wv, ml_gate_b, ml_norm_g, ret_w_in, ret_w_out, ret_log_gamma, ret_norm_g, ffn_w1, ffn_w3, ffn_w2,
           final_g):
    bn, ln, d = x.shape
    depth = ada_w.shape[0]
    assert depth == 2, "one S5 || mLSTM layer followed by one retention layer"
    s5_width = s5_d.shape[-1]
    ml_width = ml_norm_g.shape[-1]
    assert ml_width == ML_HEADS * LANES and s5_width % S5_GROUP == 0

    r_pad = -(-(bn + 1) // SUBLANES) * SUBLANES
    vec = jnp.zeros((r_pad, d), F32).at[:bn].set(c).at[bn].set(c_ctx)
    mods_all = _modulation(vec, ada_w, ada_b).reshape(depth, r_pad * N_MOD, 1, d)
    lat_row = lambda b: b
    ctx_row = lambda b: bn

    mods = mods_all[0]
    g1 = norm1_g[0].reshape(1, d)
    g2 = norm2_g[0].reshape(1, d)
    w_in = ab_w_in[0]
    n_gate = w_in.shape[1] - s5_width - 2 * ml_width
    w_in = jnp.pad(w_in, ((0, 0), (0, LANES - n_gate))).astype(BF16)
    u_lat, p_lat = _inproj_even(x, mods, lat_row, g1, w_in, s5_width)
    u_ctx, p_ctx = _inproj_even(ctx, mods, ctx_row, g1, w_in, s5_width)
    mats = _s5_matrices(s5_a_re[0], s5_a_im[0], s5_log_dt[0], s5_b_re[0], s5_b_im[0], s5_c_re[0], s5_c_im[0],
                        s5_d[0])
    ys_ctx, ys_lat = _s5_mix(u_ctx, u_lat, mats)
    m_ctx, m_lat = _mlstm_mix(p_ctx, p_lat, ml_conv_w[0], ml_conv_b[0], ml_wq[0], ml_wk[0], ml_wv[0],
                              ml_gate_b[0])
    glu_w = s5_glu_w[0].astype(BF16)
    w_out = ab_w_out[0].astype(BF16)
    ffn = (ffn_w1.astype(BF16), ffn_w3.astype(BF16), ffn_w2.astype(BF16))
    x = _outproj_even(x, ys_lat, m_lat, p_lat, mods, lat_row, g2, glu_w, s5_glu_b[0], ml_norm_g[0],
                      w_out, ffn, 0)
    ctx = _outproj_even(ctx, ys_ctx, m_ctx, p_ctx, mods, ctx_row, g2, glu_w, s5_glu_b[0], ml_norm_g[0],
                        w_out, ffn, 0)

    mods = mods_all[1]
    g1 = norm1_g[1].reshape(1, d)
    g2 = norm2_g[1].reshape(1, d)
    vw = ret_norm_g.shape[-1]
    qk = (ret_w_in.shape[-1] - 2 * vw) // 2
    w_in = ret_w_in[0].astype(BF16)
    cos, sin = _grid_rope(ln, qk // RET_HEADS)
    q, k, v, gate = _inproj_odd(x, mods, lat_row, g1, cos, sin, w_in, qk, vw)
    k_ctx, v_ctx = _inproj_odd_ctx(ctx, mods, ctx_row, g1, w_in, qk, vw)
    a = _retention_mix(q, k, v, k_ctx, v_ctx, ret_log_gamma[0])
    w_out = ret_w_out[0].astype(BF16)
    return _outproj_odd(x, a, gate, mods, lat_row, g2, ret_norm_g[0], w_out, ffn, 1, final_g.reshape(1, d))
```

```python
import functools
import math

import jax
import jax.numpy as jnp
from jax import lax
from jax.experimental import pallas as pl
from jax.experimental.pallas import tpu as pltpu

F32 = jnp.float32
BF16 = jnp.bfloat16

EPS = 1e-6
N_MOD = 6
GRID_W = 64
ROPE_BASE = 10000.0

S5_GROUP = 16
S5_STATE = 64
S5_STEP = 16
ML_HEADS = 4
ML_CHUNK = 128
RET_HEADS = 4
SUBLANES = 8
LANES = 128
NEG_BIG = -1e30
VMEM_LIMIT = 56 * 1024 * 1024


def _cparams(sem):
    return pltpu.CompilerParams(dimension_semantics=sem, vmem_limit_bytes=VMEM_LIMIT)


def _resident(shape, index_map):
    return pl.BlockSpec(shape, index_map, pipeline_mode=pl.Buffered(1))


def _token_tile(n):
    for t in (512, 256, 128):
        if n % t == 0:
            return t
    raise ValueError(f"sequence length {n} must be a multiple of 128")


def _silu(v):
    return v * jax.nn.sigmoid(v)


def _norm_mod(x, g, shift, scale):
    y = x * lax.rsqrt(jnp.mean(x * x, axis=-1, keepdims=True) + EPS)
    return (y * g) * (1.0 + scale) + shift


def _dot(a, b):
    return jnp.dot(a, b, preferred_element_type=F32)


def _dot_nt(a, b):
    return lax.dot_general(a, b, (((1,), (1,)), ((), ())), preferred_element_type=F32)


def _dot_tn(a, b):
    return lax.dot_general(a, b, (((0,), (0,)), ((), ())), preferred_element_type=F32)


def _mod_kernel(v_ref, w_ref, b_ref, o_ref):
    s = _silu(v_ref[...]).astype(BF16)
    o_ref[0] = _dot(s, w_ref[0].astype(BF16)) + b_ref[0]


def _modulation(vec, ada_w, ada_b):
    depth, d, n = ada_w.shape
    r = vec.shape[0]
    tn = 1024
    return pl.pallas_call(
        _mod_kernel,
        grid=(depth, n // tn),
        in_specs=[
            pl.BlockSpec((r, d), lambda l, j: (0, 0)),
            pl.BlockSpec((1, d, tn), lambda l, j: (l, 0, j)),
            pl.BlockSpec((1, 1, tn), lambda l, j: (l, 0, j)),
        ],
        out_specs=pl.BlockSpec((1, r, tn), lambda l, j: (l, 0, j)),
        out_shape=jax.ShapeDtypeStruct((depth, r, n), F32),
        compiler_params=_cparams(("arbitrary", "arbitrary")),
        name="adaln_modulation",
    )(vec, ada_w, ada_b.reshape(depth, 1, n))


def _inproj_even_kernel(x_ref, sh_ref, sc_ref, g_ref, w_ref, u_ref, p_ref, gate_ref, *, s5_width):
    h = _norm_mod(x_ref[0], g_ref[...], sh_ref[0], sc_ref[0]).astype(BF16)
    p = _dot(h, w_ref[...])
    n_mid = p_ref.shape[-1]
    u_ref[0] = p[:, :s5_width]
    p_ref[0] = p[:, s5_width:s5_width + n_mid].astype(BF16)
    gate_ref[0] = p[:, s5_width + n_mid:]


def _inproj_even(x, mods, row_of_batch, g, w, s5_width):
    bn, ln, d = x.shape
    n = w.shape[1]
    n_mid = n - s5_width - LANES
    tm = _token_tile(ln)
    return pl.pallas_call(
        functools.partial(_inproj_even_kernel, s5_width=s5_width),
        grid=(bn, ln // tm),
        in_specs=[
            pl.BlockSpec((1, tm, d), lambda b, i: (b, i, 0)),
            pl.BlockSpec((1, 1, d), lambda b, i: (row_of_batch(b) * N_MOD + 0, 0, 0)),
            pl.BlockSpec((1, 1, d), lambda b, i: (row_of_batch(b) * N_MOD + 1, 0, 0)),
            pl.BlockSpec((1, d), lambda b, i: (0, 0)),
            _resident((d, n), lambda b, i: (0, 0)),
        ],
        out_specs=[
            pl.BlockSpec((1, tm, s5_width), lambda b, i: (b, i, 0)),
            pl.BlockSpec((1, tm, n_mid), lambda b, i: (b, i, 0)),
            pl.BlockSpec((1, tm, LANES), lambda b, i: (b, i, 0)),
        ],
        out_shape=[
            jax.ShapeDtypeStruct((bn, ln, s5_width), F32),
            jax.ShapeDtypeStruct((bn, ln, n_mid), BF16),
            jax.ShapeDtypeStruct((bn, ln, LANES), F32),
        ],
        compiler_params=_cparams(("arbitrary", "arbitrary")),
        name="inproj_even",
    )(x, mods, mods, g, w)


def _s5_toeplitz_kernel(c_ref, w_ref, d_ref, o_ref, *, steps, chans):
    width = steps * chans
    lag0 = (steps - 1) * chans
    slab = lag0 // LANES
    row = lax.broadcasted_iota(jnp.int32, (chans, LANES), 0)
    lane = lax.broadcasted_iota(jnp.int32, (chans, LANES), 1)
    diag = lane == row + lag0 % LANES
    for n in range(c_ref.shape[0]):
        gen = lax.dot_general(c_ref[n], w_ref[n], (((1,), (1,)), ((), ())),
                              precision=lax.Precision.HIGHEST, preferred_element_type=F32)
        pieces = [gen[:, j * LANES:(j + 1) * LANES] for j in range(2 * width // LANES)]
        pieces[slab] = pieces[slab] + jnp.where(diag, d_ref[n], 0.0)
        gen = jnp.concatenate(pieces, axis=1)
        for t in range(steps):
            off = (steps - 1 - t) * chans
            o_ref[n, t * chans:(t + 1) * chans, :] = gen[:, off:off + width].astype(BF16)


def _s5_toeplitz(c2, w_gen, d_rep, steps, chans):
    n, co, kk = c2.shape
    r = w_gen.shape[1]
    nb = 8
    return pl.pallas_call(
        functools.partial(_s5_toeplitz_kernel, steps=steps, chans=chans),
        grid=(n // nb,),
        in_specs=[pl.BlockSpec((nb, co, kk), lambda i: (i, 0, 0)), pl.BlockSpec((nb, r, kk), lambda i: (i, 0, 0)),
                  pl.BlockSpec((nb, co, LANES), lambda i: (i, 0, 0))],
        out_specs=pl.BlockSpec((nb, steps * chans, steps * chans), lambda i: (i, 0, 0)),
        out_shape=jax.ShapeDtypeStruct((n, steps * chans, steps * chans), BF16),
        compiler_params=_cparams(("arbitrary",)),
        name="s5_toeplitz",
    )(c2, w_gen, d_rep)


def _s5_matrices(a_re, a_im, log_dt, b_re, b_im, c_re, c_im, d_skip):
    k = S5_STEP
    a_re, a_im = a_re.astype(F32), a_im.astype(F32)
    dt = jnp.exp(log_dt.astype(F32))[..., None]
    lam_re, lam_im = a_re * dt, a_im * dt
    steps = jnp.arange(k + 1, dtype=F32)[:, None, None, None]
    mag = jnp.exp(steps * lam_re)
    pr, pi = mag * jnp.cos(steps * lam_im), mag * jnp.sin(steps * lam_im)
    den = a_re * a_re + a_im * a_im
    nr = pr[1] - 1.0
    coef_re = (nr * a_re + pi[1] * a_im) / den
    coef_im = (pi[1] * a_re - nr * a_im) / den
    bt_re = jnp.swapaxes(b_re.astype(F32), -1, -2)
    bt_im = jnp.swapaxes(b_im.astype(F32), -1, -2)
    bb_re = coef_re[:, :, None] * bt_re - coef_im[:, :, None] * bt_im
    bb_im = coef_re[:, :, None] * bt_im + coef_im[:, :, None] * bt_re
    wr = pr[:, :, :, None] * bb_re - pi[:, :, :, None] * bb_im
    wi = pr[:, :, :, None] * bb_im + pi[:, :, :, None] * bb_re
    c_re, c_im = c_re.astype(F32), c_im.astype(F32)
    g_n, c_n, p_n = a_re.shape[1], b_re.shape[-1], a_re.shape[-1]
    def lag_rows(w_re, w_im):
        return jnp.transpose(jnp.concatenate([w_re, -w_im], -1), (1, 0, 2, 3)).reshape(g_n, k * c_n, 2 * p_n)

    w_gen = jnp.concatenate(
        [jnp.pad(lag_rows(wr[k - 1::-1, 0], wi[k - 1::-1, 0]), ((0, 0), (0, k * c_n), (0, 0))),
         jnp.pad(lag_rows(wr[:k, 1], wi[:k, 1]), ((0, 0), ((k - 1) * c_n, c_n), (0, 0)))], axis=-1)
    c2 = jnp.concatenate([c_re[0], c_im[0], c_re[1], c_im[1]], -1)
    d_rep = jnp.broadcast_to(d_skip.astype(F32).reshape(g_n, c_n, 1), (g_n, c_n, LANES))
    toep = _s5_toeplitz(c2, w_gen, d_rep, k, c_n)
    endw = jnp.concatenate([wr[k - 1::-1, 0], wr[:k, 1], wi[k - 1::-1, 0], wi[:k, 1]], -1)
    endw = jnp.transpose(endw, (1, 0, 2, 3)).reshape(g_n, k * c_n, 4 * p_n)

    def out_cols(d, powers_re, powers_im):
        cr, ci = c_re[d][None], c_im[d][None]
        pre, pim = powers_re[:, :, None, :], powers_im[:, :, None, :]
        return cr * pre - ci * pim, -(cr * pim + ci * pre)

    of_re, of_im = out_cols(0, pr[1:, 0], pi[1:, 0])
    ob_re, ob_im = out_cols(1, pr[k:0:-1, 1], pi[k:0:-1, 1])
    outw_t = jnp.concatenate([of_re, ob_re, of_im, ob_im], -1)
    outw_t = jnp.transpose(outw_t, (1, 0, 2, 3)).reshape(g_n, k * c_n, 4 * p_n)
    a0 = jnp.concatenate([pr[k, 0], pr[k, 1]], -1)
    a1 = jnp.concatenate([pi[k, 0], pi[k, 1]], -1)
    return toep.astype(BF16), endw.astype(BF16), outw_t.astype(BF16), a0, a1


S5_LANE_GROUPS = LANES // S5_GROUP


def _chunk_transpose(xs):
    n_arr = len(xs)
    lane_chunk = lax.broadcasted_iota(jnp.int32, xs[0].shape, 1) // S5_GROUP
    d = 1
    while d < n_arr:
        keep = (lane_chunk & d) == 0
        out = list(xs)
        for i in range(n_arr):
            if i & d:
                continue
            lo, hi = xs[i], xs[i + d]
            out[i] = jnp.where(keep, lo, pltpu.roll(hi, d * S5_GROUP, 1))
            out[i + d] = jnp.where(keep, pltpu.roll(lo, LANES - d * S5_GROUP, 1), hi)
        xs = out
        d *= 2
    return xs


def _s5_kernel(uc_ref, ul_ref, toep_ref, endw_ref, outw_ref, a0_ref, a1_ref, yc_ref, yl_ref,
               uc_s, ul_s, e_s, yc_s, yl_s, *, batch, n_ctx, n_lat):
    p = S5_STATE
    gl = S5_LANE_GROUPS
    k = S5_STEP

    def stack(u_ref, us_ref, n_blk):
        for b in range(batch):
            for half in range(k // gl):
                xs = [u_ref[b, pl.ds(half * gl + s, n_blk, stride=k), :] for s in range(gl)]
                ys = _chunk_transpose(xs)
                for g in range(gl):
                    us_ref[g, b * n_blk:(b + 1) * n_blk, half * LANES:(half + 1) * LANES] = ys[g].astype(BF16)

    stack(uc_ref, uc_s, n_ctx)
    stack(ul_ref, ul_s, n_lat)

    def increments(us_ref, n_blk, first_blk):
        for g in range(gl):
            e = _dot(us_ref[g], endw_ref[g])
            for b in range(batch):
                for part in range(2):
                    e_s[part, b, pl.ds(first_blk * gl + g, n_blk, stride=gl), :] = (
                        e[b * n_blk:(b + 1) * n_blk, part * LANES:(part + 1) * LANES])

    increments(uc_s, n_ctx, 0)
    increments(ul_s, n_lat, n_ctx)

    a0 = a0_ref[...]
    a1 = a1_ref[...]
    fwd_lane = lax.broadcasted_iota(jnp.int32, (batch, gl, 2 * p), 2) < p

    def scan(first_blk, n, carry):
        def body(i, c):
            s0, s1 = c
            rf = pl.ds(pl.multiple_of((first_blk + i) * gl, gl), gl)
            rb = pl.ds(pl.multiple_of((first_blk + n - 1 - i) * gl, gl), gl)
            e0 = jnp.where(fwd_lane, e_s[0, :, rf, :], e_s[0, :, rb, :])
            e1 = jnp.where(fwd_lane, e_s[1, :, rf, :], e_s[1, :, rb, :])
            e_s[0, :, rf, 0:p] = s0[:, :, 0:p]
            e_s[0, :, rb, p:2 * p] = s0[:, :, p:2 * p]
            e_s[1, :, rf, 0:p] = s1[:, :, 0:p]
            e_s[1, :, rb, p:2 * p] = s1[:, :, p:2 * p]
            return a0 * s0 - a1 * s1 + e0, a0 * s1 + a1 * s0 + e1
        return lax.fori_loop(0, n, body, carry)

    zero = jnp.zeros((batch, gl, 2 * p), F32)
    carry = scan(0, n_ctx, (zero, zero))
    scan(n_ctx, n_lat, carry)

    def outputs(us_ref, ys_ref, y_ref, n_blk, first_blk):
        for g in range(gl):
            state = jnp.concatenate(
                [jnp.concatenate([e_s[part, b, pl.ds(first_blk * gl + g, n_blk, stride=gl), :]
                                  for b in range(batch)], axis=0) for part in range(2)], axis=1)
            ys_ref[g] = _dot_nt(us_ref[g], toep_ref[g]) + _dot_nt(state.astype(BF16), outw_ref[g])
        for b in range(batch):
            for half in range(k // gl):
                ys = [ys_ref[g, b * n_blk:(b + 1) * n_blk, half * LANES:(half + 1) * LANES] for g in range(gl)]
                xs = _chunk_transpose(ys)
                for s in range(gl):
                    y_ref[b, pl.ds(half * gl + s, n_blk, stride=k), :] = xs[s]

    outputs(uc_s, yc_s, yc_ref, n_ctx, 0)
    outputs(ul_s, yl_s, yl_ref, n_lat, n_ctx)


def _s5_mix(u_ctx, u_lat, mats):
    toep, endw, outw, a0, a1 = mats
    bn, lc, w = u_ctx.shape
    ll = u_lat.shape[1]
    gl = S5_LANE_GROUPS
    kc = S5_STEP * S5_GROUP
    n_ctx, n_lat = lc // S5_STEP, ll // S5_STEP
    bb = 4 if bn % 4 == 0 else bn
    seq = lambda n_rows: pl.BlockSpec((bb, n_rows, LANES), lambda i, j: (j, 0, i))
    grp = lambda r, c: pl.BlockSpec((gl, r, c), lambda i, j: (i, 0, 0))
    return pl.pallas_call(
        functools.partial(_s5_kernel, batch=bb, n_ctx=n_ctx, n_lat=n_lat),
        grid=(w // LANES, bn // bb),
        in_specs=[seq(lc), seq(ll), grp(kc, kc), grp(kc, kc), grp(kc, kc),
                  pl.BlockSpec((gl, 2 * S5_STATE), lambda i, j: (i, 0)),
                  pl.BlockSpec((gl, 2 * S5_STATE), lambda i, j: (i, 0))],
        out_specs=[seq(lc), seq(ll)],
        out_shape=[jax.ShapeDtypeStruct((bn, lc, w), F32), jax.ShapeDtypeStruct((bn, ll, w), F32)],
        scratch_shapes=[
            pltpu.VMEM((gl, bb * n_ctx, kc), BF16), pltpu.VMEM((gl, bb * n_lat, kc), BF16),
            pltpu.VMEM((2, bb, (n_ctx + n_lat) * gl, LANES), F32),
            pltpu.VMEM((gl, bb * n_ctx, kc), F32), pltpu.VMEM((gl, bb * n_lat, kc), F32),
        ],
        compiler_params=_cparams(("arbitrary", "arbitrary")),
        name="s5_scan",
    )(u_ctx, u_lat, toep, endw, outw, a0, a1)


def _mlstm_kernel(pc_ref, pl_ref, gc_ref, gl_ref, cw_ref, cb_ref, wq_ref, wkt_ref, wv_ref, gb_ref, mc_ref, ml_ref,
                  q_s, kt_s, vx_s, h_s, st_s, bcol_s, mcol_s, rrow_s, xpad_s, *, n_ctx_rows, n_lat_rows):
    t_n = ML_CHUNK
    width = ML_HEADS * LANES
    nc_ctx, nc_lat = n_ctx_rows // t_n, n_lat_rows // t_n
    nc = nc_ctx + nc_lat
    scale_k = LANES ** -0.5

    def project(p_ref, n_rows, base):
        ones = jnp.ones((n_rows, LANES), BF16)
        pad = SUBLANES
        xpad_s[0:pad, :] = jnp.zeros((pad, LANES), F32)
        xpad_s[pad + n_rows:2 * pad + n_rows, :] = jnp.zeros((pad, LANES), F32)
        for h in range(ML_HEADS):
            cols = slice(h * LANES, (h + 1) * LANES)
            xm_lo = p_ref[0, :, cols]
            xm = xm_lo.astype(F32)
            xpad_s[pad:pad + n_rows, :] = xm
            taps = cw_ref[:, cols]
            n_tap = taps.shape[0]
            assert n_tap // 2 <= pad
            acc = jnp.zeros_like(xm) + cb_ref[:, cols]
            for j in range(n_tap):
                d = j - n_tap // 2
                sh = xm if d == 0 else xpad_s[pad + d:pad + d + n_rows, :]
                acc = acc + sh * taps[j:j + 1, :]
            xc = _silu(acc).astype(BF16)
            q_s[base:base + n_rows, cols] = _dot(xc, wq_ref[h]).astype(BF16)
            kt = (_dot_nt(wkt_ref[h], xc) * scale_k).astype(BF16)
            for c in range(n_rows // t_n):
                kt_s[base // t_n + c, cols, :] = kt[:, c * t_n:(c + 1) * t_n]
            vx_s[base:base + n_rows, 2 * h * LANES:(2 * h + 1) * LANES] = (
                _dot(xm_lo, wv_ref[h]).astype(BF16))
            vx_s[base:base + n_rows, (2 * h + 1) * LANES:(2 * h + 2) * LANES] = ones

    project(pc_ref, n_ctx_rows, 0)
    project(pl_ref, n_lat_rows, n_ctx_rows)

    ti = lax.broadcasted_iota(jnp.int32, (t_n, t_n), 0)
    si = lax.broadcasted_iota(jnp.int32, (t_n, t_n), 1)
    tri_f = (si <= ti).astype(BF16)
    tri_b = (si >= ti).astype(BF16)
    lane = lax.broadcasted_iota(jnp.int32, (t_n, LANES), 1)
    trow = lax.broadcasted_iota(jnp.int32, (t_n, LANES), 0)

    def gate_prep(g_ref, n_chunks, base_chunk):
        def body(c, _):
            rows = pl.ds(pl.multiple_of(c * t_n, t_n), t_n)
            gcol = g_ref[0, rows, :] + gb_ref[...]
            lf = jax.nn.log_sigmoid(gcol)
            hi = lf.astype(BF16)
            lo = (lf - hi.astype(F32)).astype(BF16)
            pre = _dot(tri_f, hi) + _dot(tri_f, lo)
            suf = _dot(tri_b, hi) + _dot(tri_b, lo)
            bsum = jnp.where(lane < 2 * ML_HEADS, pre, suf)
            rcol = gcol - pltpu.roll(bsum, LANES - ML_HEADS, 1)
            pmax, smax = rcol, rcol
            step = 1
            while step < t_n:
                pmax = jnp.maximum(pmax, jnp.where(trow >= step, pltpu.roll(pmax, step, 0), NEG_BIG))
                smax = jnp.maximum(smax, jnp.where(trow < t_n - step, pltpu.roll(smax, t_n - step, 0), NEG_BIG))
                step *= 2
            bcol_s[base_chunk + c] = bsum
            mcol_s[base_chunk + c] = jnp.where(lane < 2 * ML_HEADS, pmax, smax)
            rrow_s[base_chunk + c] = rcol.T[0:4 * ML_HEADS, :]
            return 0
        lax.fori_loop(0, n_chunks, body, 0, unroll=2 if n_chunks % 2 == 0 else 1)

    gate_prep(gc_ref, nc_ctx, 0)
    gate_prep(gl_ref, nc_lat, nc_ctx)

    def one_dir(c, h, d, m_in):
        cols = slice(h * LANES, (h + 1) * LANES)
        xcols = slice(2 * h * LANES, (2 * h + 2) * LANES)
        li = 2 * d * ML_HEADS + h
        mask = (si <= ti) if d == 0 else (si >= ti)
        last = t_n - 1 if d == 0 else 0
        rows = pl.ds(pl.multiple_of(c * t_n, t_n), t_n)
        q = q_s[rows, cols]
        kt = kt_s[c, cols, :]
        vx = vx_s[rows, xcols]
        r_row = rrow_s[c][li:li + 1, :]
        run_max = jnp.broadcast_to(mcol_s[c][:, li:li + 1], (t_n, LANES))
        b_rep = jnp.broadcast_to(bcol_s[c][:, li + ML_HEADS:li + ML_HEADS + 1], (t_n, LANES))
        mm = jnp.maximum(m_in, run_max)
        dmat = jnp.exp(jnp.where(mask, r_row - mm, NEG_BIG))
        sm = (dmat * _dot(q, kt)).astype(BF16)
        intra = _dot(sm, vx)
        cross = _dot(q, st_s[2 * h + d].astype(BF16))
        inter = jnp.exp(m_in - mm)
        num = intra[:, :LANES] + inter * cross[:, :LANES]
        den = intra[:, LANES:] + inter * cross[:, LANES:]
        h_s[rows, cols] = h_s[rows, cols] + num / jnp.maximum(jnp.abs(den), jnp.exp(-b_rep - mm))
        r_top = run_max[last:last + 1, :]
        b_last = b_rep[last:last + 1, :]
        g_max = b_last + r_top
        m_new = jnp.maximum(b_last + m_in, g_max)
        dec = jnp.exp(b_last + m_in - m_new)
        inj = jnp.exp(g_max - m_new)
        kw = (kt.astype(F32) * (jnp.exp(r_row - r_top) * inj)).astype(BF16)
        inc = _dot(kw, vx)
        st_s[2 * h + d] = jnp.concatenate([dec, dec], axis=1) * st_s[2 * h + d] + inc
        return m_new

    st_s[...] = jnp.zeros_like(st_s)
    h_s[...] = jnp.zeros_like(h_s)

    def body(i, carry):
        cb = jnp.where(i < nc_ctx, nc_ctx - 1 - i, nc + nc_ctx - 1 - i)
        return tuple(one_dir(i if d == 0 else cb, h, d, carry[2 * h + d])
                     for h in range(ML_HEADS) for d in range(2))

    zero = jnp.zeros((1, LANES), F32)
    lax.fori_loop(0, nc, body, (zero,) * (2 * ML_HEADS))

    mc_ref[0] = h_s[0:n_ctx_rows, :].astype(BF16)
    ml_ref[0] = h_s[n_ctx_rows:n_ctx_rows + n_lat_rows, :].astype(BF16)


def _mlstm_mix(p_ctx, p_lat, g_ctx, g_lat, conv_w, conv_b, wq, wk, wv, gate_b):
    bn, lc, pw = p_ctx.shape
    ll = p_lat.shape[1]
    width = ML_HEADS * LANES
    lt = lc + ll
    nct = lt // ML_CHUNK
    gb = jnp.zeros((1, LANES), F32).at[0, :4 * ML_HEADS].set(gate_b.astype(F32))
    full2 = lambda a: pl.BlockSpec(a.shape, lambda b: (0,) * a.ndim)
    conv_b2 = conv_b.reshape(1, width)
    assert ML_CHUNK == LANES
    wq, wkt, wv = wq.astype(BF16), jnp.swapaxes(wk, 1, 2).astype(BF16), wv.astype(BF16)
    return pl.pallas_call(
        functools.partial(_mlstm_kernel, n_ctx_rows=lc, n_lat_rows=ll),
        grid=(bn,),
        in_specs=[
            pl.BlockSpec((1, lc, pw), lambda b: (b, 0, 0)),
            pl.BlockSpec((1, ll, pw), lambda b: (b, 0, 0)),
            pl.BlockSpec((1, lc, LANES), lambda b: (b, 0, 0)),
            pl.BlockSpec((1, ll, LANES), lambda b: (b, 0, 0)),
            full2(conv_w), full2(conv_b2), full2(wq), full2(wkt), full2(wv), full2(gb),
        ],
        out_specs=[
            pl.BlockSpec((1, lc, width), lambda b: (b, 0, 0)),
            pl.BlockSpec((1, ll, width), lambda b: (b, 0, 0)),
        ],
        out_shape=[jax.ShapeDtypeStruct((bn, lc, width), BF16), jax.ShapeDtypeStruct((bn, ll, width), BF16)],
        scratch_shapes=[
            pltpu.VMEM((lt, width), BF16),
            pltpu.VMEM((nct, width, ML_CHUNK), BF16),
            pltpu.VMEM((lt, 2 * width), BF16),
            pltpu.VMEM((lt, width), F32),
            pltpu.VMEM((2 * ML_HEADS, LANES, 2 * LANES), F32),
            pltpu.VMEM((nct, ML_CHUNK, LANES), F32), pltpu.VMEM((nct, ML_CHUNK, LANES), F32),
            pltpu.VMEM((nct, 4 * ML_HEADS, ML_CHUNK), F32),
            pltpu.VMEM((max(lc, ll) + 2 * SUBLANES, LANES), F32),
        ],
        compiler_params=_cparams(("arbitrary",)),
        name="mlstm_mix",
    )(p_ctx, p_lat, g_ctx, g_lat, conv_w, conv_b2, wq, wkt, wv, gb)


def _ffn_tail(x, y, gate_mix, g2, shift, scale, gate_ffn, w1_ref, w3_ref, w2_ref, hidden_chunk):
    x1 = x + gate_mix * y
    h2 = _norm_mod(x1, g2, shift, scale).astype(BF16)
    hidden = w1_ref.shape[1]
    acc = jnp.zeros_like(x1)
    start = 0
    while start < hidden:
        cs = slice(start, min(start + hidden_chunk, hidden))
        z = (_silu(_dot(h2, w1_ref[:, cs])) * _dot(h2, w3_ref[:, cs])).astype(BF16)
        acc = acc + _dot(z, w2_ref[cs, :])
        start += hidden_chunk
    return x1 + gate_ffn * acc


def _final_norm(x, g):
    return x * lax.rsqrt(jnp.mean(x * x, axis=-1, keepdims=True) + EPS) * g


def _outproj_even_kernel(x_ref, ys_ref, m_ref, og_ref, gm_ref, sh_ref, sc_ref, gf_ref, g2_ref, gw_ref, gbias_ref,
                         ng_ref, wo_ref, w1_ref, w3_ref, w2_ref, o_ref, *, hidden_chunk, heads):
    ys = jax.nn.gelu(ys_ref[0])
    s = ys * jax.nn.sigmoid(_dot(ys.astype(BF16), gw_ref[...]) + gbias_ref[...])
    dh = m_ref.shape[-1] // heads
    gated = []
    for h in range(heads):
        cols = slice(h * dh, (h + 1) * dh)
        hh = m_ref[0, :, cols].astype(F32)
        mu = jnp.mean(hh, axis=1, keepdims=True)
        cen = hh - mu
        var = jnp.mean(cen * cen, axis=1, keepdims=True)
        normed = cen * lax.rsqrt(var + EPS) * ng_ref[:, cols]
        gated.append((jax.nn.sigmoid(og_ref[0, :, cols].astype(F32)) * normed).astype(BF16))
    sw = ys_ref.shape[-1]
    y = _dot(s.astype(BF16), wo_ref[0:sw, :]) + _dot(jnp.concatenate(gated, axis=1), wo_ref[sw:, :])
    o_ref[0] = _ffn_tail(x_ref[0], y, gm_ref[0], g2_ref[...], sh_ref[0], sc_ref[0], gf_ref[0],
                         w1_ref, w3_ref, w2_ref, hidden_chunk)


def _outproj_odd_kernel(x_ref, a_ref, gate_ref, gm_ref, sh_ref, sc_ref, gf_ref, g2_ref, ng_ref, wo_ref, w1_ref,
                        w3_ref, w2_ref, fg_ref, o_ref, *, hidden_chunk, heads):
    hv = a_ref.shape[-1] // heads
    y = jnp.zeros(x_ref.shape[1:], F32)
    for h in range(heads):
        cols = slice(h * hv, (h + 1) * hv)
        o = a_ref[0, :, cols].astype(F32)
        mu = jnp.mean(o, axis=1, keepdims=True)
        cen = o - mu
        var = jnp.mean(cen * cen, axis=1, keepdims=True)
        normed = cen * lax.rsqrt(var + EPS) * ng_ref[:, cols]
        gated = (_silu(gate_ref[0, :, cols].astype(F32)) * normed).astype(BF16)
        y = y + _dot(gated, wo_ref[cols, :])
    out = _ffn_tail(x_ref[0], y, gm_ref[0], g2_ref[...], sh_ref[0], sc_ref[0], gf_ref[0],
                    w1_ref, w3_ref, w2_ref, hidden_chunk)
    o_ref[0] = _final_norm(out, fg_ref[...])


MXU_WIDTH = 256


def _hidden_chunk(hidden):
    return -(-hidden // (2 * MXU_WIDTH)) * MXU_WIDTH


def _mod_specs(d, row_of_batch, slots):
    return [pl.BlockSpec((1, 1, d), lambda b, i, s=s: (row_of_batch(b) * N_MOD + s, 0, 0)) for s in slots]


def _layer_resident(w, layer):
    return _resident((None,) + w.shape[1:], lambda b, i: (layer,) + (0,) * (w.ndim - 1))


def _outproj_even(x, ys, m, p, mods, row_of_batch, g2, glu_w, glu_b, norm_g, w_out, ffn, layer):
    bn, ln, d = x.shape
    sw, mw = ys.shape[-1], m.shape[-1]
    tm = _token_tile(ln)
    res = lambda a: _resident(a.shape, lambda b, i: (0,) * a.ndim)
    tok = lambda w: pl.BlockSpec((1, tm, w), lambda b, i: (b, i, 0))
    row = lambda w: pl.BlockSpec((1, w), lambda b, i: (0, 0))
    return pl.pallas_call(
        functools.partial(_outproj_even_kernel, hidden_chunk=_hidden_chunk(ffn[0].shape[-1]), heads=ML_HEADS),
        grid=(bn, ln // tm),
        in_specs=[tok(d), tok(sw), tok(mw), pl.BlockSpec((1, tm, mw), lambda b, i: (b, i, 1))]
        + _mod_specs(d, row_of_batch, (2, 3, 4, 5))
        + [row(d), res(glu_w), row(sw), row(mw), res(w_out)] + [_layer_resident(w, layer) for w in ffn],
        out_specs=tok(d),
        out_shape=jax.ShapeDtypeStruct((bn, ln, d), F32),
        compiler_params=_cparams(("arbitrary", "arbitrary")),
        name="outproj_ffn_even",
    )(x, ys, m, p, mods, mods, mods, mods, g2, glu_w, glu_b.reshape(1, sw), norm_g.reshape(1, mw), w_out, *ffn)


def _outproj_odd(x, a, gate, mods, row_of_batch, g2, norm_g, w_out, ffn, layer, final_g):
    bn, ln, d = x.shape
    vw = a.shape[-1]
    tm = _token_tile(ln)
    res = lambda a_: _resident(a_.shape, lambda b, i: (0,) * a_.ndim)
    tok = lambda w: pl.BlockSpec((1, tm, w), lambda b, i: (b, i, 0))
    row = lambda w: pl.BlockSpec((1, w), lambda b, i: (0, 0))
    return pl.pallas_call(
        functools.partial(_outproj_odd_kernel, hidden_chunk=_hidden_chunk(ffn[0].shape[-1]), heads=RET_HEADS),
        grid=(bn, ln // tm),
        in_specs=[tok(d), tok(vw), tok(vw)] + _mod_specs(d, row_of_batch, (2, 3, 4, 5))
        + [row(d), row(vw), res(w_out)] + [_layer_resident(w, layer) for w in ffn] + [row(d)],
        out_specs=tok(d),
        out_shape=jax.ShapeDtypeStruct((bn, ln, d), F32),
        compiler_params=_cparams(("arbitrary", "arbitrary")),
        name="outproj_ffn_odd",
    )(x, a, gate, mods, mods, mods, mods, g2, norm_g.reshape(1, vw), w_out, *ffn, final_g)


def _rope_heads(t, cos, sin, heads, hk):
    half = hk // 2
    out = []
    for h in range(heads):
        t1 = t[:, h * hk:h * hk + half]
        t2 = t[:, h * hk + half:(h + 1) * hk]
        out.append(t1 * cos - t2 * sin)
        out.append(t2 * cos + t1 * sin)
    return jnp.concatenate(out, axis=-1)


def _inproj_odd_kernel(x_ref, sh_ref, sc_ref, g_ref, cos_ref, sin_ref, w_ref, q_ref, k_ref, v_ref, gate_ref,
                       *, qk, vw, heads):
    h = _norm_mod(x_ref[0], g_ref[...], sh_ref[0], sc_ref[0]).astype(BF16)
    hk = qk // heads
    cos, sin = cos_ref[...], sin_ref[...]
    q = _rope_heads(_dot(h, w_ref[:, 0:qk]), cos, sin, heads, hk)
    q_ref[0] = (q * (hk ** -0.5)).astype(BF16)
    k = _rope_heads(_dot(h, w_ref[:, qk:2 * qk]), cos, sin, heads, hk)
    k_ref[0] = k.astype(BF16)
    v_ref[0] = _dot(h, w_ref[:, 2 * qk:2 * qk + vw]).astype(BF16)
    gate_ref[0] = _dot(h, w_ref[:, 2 * qk + vw:2 * qk + 2 * vw]).astype(BF16)


def _inproj_odd_ctx_kernel(x_ref, sh_ref, sc_ref, g_ref, w_ref, k_ref, v_ref, *, qk, vw):
    h = _norm_mod(x_ref[0], g_ref[...], sh_ref[0], sc_ref[0]).astype(BF16)
    k_ref[0] = _dot(h, w_ref[:, qk:2 * qk]).astype(BF16)
    v_ref[0] = _dot(h, w_ref[:, 2 * qk:2 * qk + vw]).astype(BF16)


def _inproj_odd(x, mods, row_of_batch, g, cos, sin, w, qk, vw):
    bn, ln, d = x.shape
    tm = _token_tile(ln)
    tok = lambda w_: pl.BlockSpec((1, tm, w_), lambda b, i: (b, i, 0))
    half = cos.shape[1]
    return pl.pallas_call(
        functools.partial(_inproj_odd_kernel, qk=qk, vw=vw, heads=RET_HEADS),
        grid=(bn, ln // tm),
        in_specs=[tok(d)] + _mod_specs(d, row_of_batch, (0, 1))
        + [pl.BlockSpec((1, d), lambda b, i: (0, 0)),
           pl.BlockSpec((tm, half), lambda b, i: (i, 0)), pl.BlockSpec((tm, half), lambda b, i: (i, 0)),
           _resident(w.shape, lambda b, i: (0, 0))],
        out_specs=[tok(qk), tok(qk), tok(vw), tok(vw)],
        out_shape=[jax.ShapeDtypeStruct((bn, ln, qk), BF16), jax.ShapeDtypeStruct((bn, ln, qk), BF16),
                   jax.ShapeDtypeStruct((bn, ln, vw), BF16), jax.ShapeDtypeStruct((bn, ln, vw), BF16)],
        compiler_params=_cparams(("arbitrary", "arbitrary")),
        name="inproj_odd",
    )(x, mods, mods, g, cos, sin, w)


def _inproj_odd_ctx(x, mods, row_of_batch, g, w_kv, qk, vw):
    bn, ln, d = x.shape
    tm = _token_tile(ln)
    tok = lambda w_: pl.BlockSpec((1, tm, w_), lambda b, i: (b, i, 0))
    return pl.pallas_call(
        functools.partial(_inproj_odd_ctx_kernel, qk=qk, vw=vw),
        grid=(bn, ln // tm),
        in_specs=[tok(d)] + _mod_specs(d, row_of_batch, (0, 1))
        + [pl.BlockSpec((1, d), lambda b, i: (0, 0)), _resident(w_kv.shape, lambda b, i: (0, 0))],
        out_specs=[tok(qk), tok(vw)],
        out_shape=[jax.ShapeDtypeStruct((bn, ln, qk), BF16), jax.ShapeDtypeStruct((bn, ln, vw), BF16)],
        compiler_params=_cparams(("arbitrary", "arbitrary")),
        name="inproj_odd_ctx",
    )(x, mods, mods, g, w_kv)


def _retention_kernel(lg_ref, q_ref, k_ref, v_ref, kc_ref, vc_ref, o_ref, acc_s, sf_s, sb_s,
                      *, chunk, n_ctx_rows, n_lat_rows):
    t_n = chunk
    h = pl.program_id(1)
    lgf = jnp.full((1, 1), lg_ref[0, h], F32)
    lgb = jnp.full((1, 1), lg_ref[1, h], F32)
    nc_ctx, nc_lat = n_ctx_rows // t_n, n_lat_rows // t_n
    ti = lax.broadcasted_iota(jnp.int32, (t_n, t_n), 0)
    si = lax.broadcasted_iota(jnp.int32, (t_n, t_n), 1)
    diff = (ti - si).astype(F32)
    decay = jnp.where(diff >= 0, jnp.exp(lgf * jnp.maximum(diff, 0.0)), jnp.exp(lgb * jnp.maximum(-diff, 0.0)))
    pos = lax.broadcasted_iota(jnp.int32, (t_n, 1), 0).astype(F32)
    inter_f = jnp.exp(lgf * (pos + 1.0))
    inter_b = jnp.exp(lgb * (t_n - pos))
    wend_f = jnp.exp(lgf * (t_n - 1.0 - pos))
    wend_b = jnp.exp(lgb * pos)
    cd_f = jnp.exp(lgf * t_n)
    cd_b = jnp.exp(lgb * t_n)

    def rows_of(c):
        return pl.ds(pl.multiple_of(c * t_n, t_n), t_n)

    def bump(s_ref, k, v, wend, cd):
        kw = (k.astype(F32) * wend).astype(BF16)
        s_ref[...] = cd * s_ref[...] + _dot_tn(kw, v)

    sf_s[...] = jnp.zeros_like(sf_s)
    sb_s[...] = jnp.zeros_like(sb_s)

    def ctx_f(c, _):
        bump(sf_s, kc_ref[0, rows_of(c), :], vc_ref[0, rows_of(c), :], wend_f, cd_f)
        return 0
    lax.fori_loop(0, nc_ctx, ctx_f, 0)

    def ctx_b(i, _):
        c = nc_ctx - 1 - i
        bump(sb_s, kc_ref[0, rows_of(c), :], vc_ref[0, rows_of(c), :], wend_b, cd_b)
        return 0
    lax.fori_loop(0, nc_ctx, ctx_b, 0)

    def forward_part(rows):
        q, k, v = q_ref[0, rows, :], k_ref[0, rows, :], v_ref[0, rows, :]
        scores = (_dot_nt(q, k) * decay).astype(BF16)
        kw_t = (k.astype(F32) * wend_f).T.astype(BF16)
        both = _dot(jnp.concatenate([scores, kw_t], axis=0), v)
        part = both[0:t_n] + inter_f * _dot(q, sf_s[...].astype(BF16))
        sf_s[...] = cd_f * sf_s[...] + both[t_n:]
        return part

    def backward_part(rows):
        q, k, v = q_ref[0, rows, :], k_ref[0, rows, :], v_ref[0, rows, :]
        part = inter_b * _dot(q, sb_s[...].astype(BF16))
        bump(sb_s, k, v, wend_b, cd_b)
        return part

    def first_half(i, _):
        rf, rb = rows_of(i), rows_of(nc_lat - 1 - i)
        acc_s[rf, :] = forward_part(rf)
        acc_s[rb, :] = backward_part(rb)
        return 0
    lax.fori_loop(0, nc_lat // 2, first_half, 0)

    def second_half(i, _):
        rf, rb = rows_of(i), rows_of(nc_lat - 1 - i)
        o_ref[0, rf, :] = (acc_s[rf, :] + forward_part(rf)).astype(BF16)
        o_ref[0, rb, :] = (acc_s[rb, :] + backward_part(rb)).astype(BF16)
        return 0
    lax.fori_loop(nc_lat // 2, nc_lat, second_half, 0)


def _retention_mix(q, k, v, k_ctx, v_ctx, log_gamma):
    bn, ll, qk = q.shape
    lc = k_ctx.shape[1]
    vw = v.shape[-1]
    hk, hv = qk // RET_HEADS, vw // RET_HEADS
    chunk = 256 if (ll % 512 == 0 and lc % 256 == 0) else 128
    assert (ll // chunk) % 2 == 0, "latent chunks are visited in forward/backward pairs"
    head = lambda n_rows, w: pl.BlockSpec((1, n_rows, w), lambda b, h: (b, 0, h))
    return pl.pallas_call(
        functools.partial(_retention_kernel, chunk=chunk, n_ctx_rows=lc, n_lat_rows=ll),
        grid=(bn, RET_HEADS),
        in_specs=[
            pl.BlockSpec(memory_space=pltpu.SMEM),
            head(ll, hk), head(ll, hk), head(ll, hv), head(lc, hk), head(lc, hv),
        ],
        out_specs=head(ll, hv),
        out_shape=jax.ShapeDtypeStruct((bn, ll, vw), BF16),
        scratch_shapes=[pltpu.VMEM((ll, hv), F32), pltpu.VMEM((hk, hv), F32), pltpu.VMEM((hk, hv), F32)],
        compiler_params=_cparams(("arbitrary", "arbitrary")),
        name="retention_mix",
    )(log_gamma.astype(F32), q, k, v, k_ctx, v_ctx)


def _grid_rope(n_pos, hk):
    rows = n_pos // GRID_W
    row = jnp.repeat(jnp.arange(rows, dtype=F32), GRID_W)
    col = jnp.tile(jnp.arange(GRID_W, dtype=F32), rows)
    n_freq = hk // 4
    inv = ROPE_BASE ** (-jnp.arange(n_freq, dtype=F32) / n_freq)
    ang = jnp.concatenate([row[:, None] * inv, col[:, None] * inv], -1)
    return jnp.cos(ang), jnp.sin(ang)


def kernel(x, c, ctx, c_ctx, ada_w, ada_b, norm1_g, norm2_g, ab_w_in, ab_w_out, s5_a_re, s5_a_im, s5_log_dt,
           s5_b_re, s5_b_im, s5_c_re, s5_c_im, s5_d, s5_glu_w, s5_glu_b, ml_conv_w, ml_conv_b, ml_wq, ml_wk,
           ml_wv, ml_gate_b, ml_norm_g, ret_w_in, ret_w_out, ret_log_gamma, ret_norm_g, ffn_w1, ffn_w3, ffn_w2,
           final_g):
    bn, ln, d = x.shape
    depth = ada_w.shape[0]
    assert depth == 2, "one S5 || mLSTM layer followed by one retention layer"
    s5_width = s5_d.shape[-1]
    ml_width = ml_norm_g.shape[-1]
    assert ml_width == ML_HEADS * LANES and s5_width % S5_GROUP == 0

    r_pad = -(-(bn + 1) // SUBLANES) * SUBLANES
    vec = jnp.zeros((r_pad, d), F32).at[:bn].set(c).at[bn].set(c_ctx)
    mods_all = _modulation(vec, ada_w, ada_b).reshape(depth, r_pad * N_MOD, 1, d)
    lat_row = lambda b: b
    ctx_row = lambda b: bn

    mods = mods_all[0]
    g1 = norm1_g[0].reshape(1, d)
    g2 = norm2_g[0].reshape(1, d)
    w_in = ab_w_in[0]
    n_gate = w_in.shape[1] - s5_width - 2 * ml_width
    w_in = jnp.pad(w_in, ((0, 0), (0, LANES - n_gate))).astype(BF16)
    u_lat, p_lat, g_lat = _inproj_even(x, mods, lat_row, g1, w_in, s5_width)
    u_ctx, p_ctx, g_ctx = _inproj_even(ctx, mods, ctx_row, g1, w_in, s5_width)
    mats = _s5_matrices(s5_a_re[0], s5_a_im[0], s5_log_dt[0], s5_b_re[0], s5_b_im[0], s5_c_re[0], s5_c_im[0],
                        s5_d[0])
    ys_ctx, ys_lat = _s5_mix(u_ctx, u_lat, mats)
    m_ctx, m_lat = _mlstm_mix(p_ctx, p_lat, g_ctx, g_lat, ml_conv_w[0], ml_conv_b[0], ml_wq[0], ml_wk[0],
                              ml_wv[0], ml_gate_b[0])
    glu_w = s5_glu_w[0].astype(BF16)
    w_out = ab_w_out[0].astype(BF16)
    ffn = (ffn_w1.astype(BF16), ffn_w3.astype(BF16), ffn_w2.astype(BF16))
    x = _outproj_even(x, ys_lat, m_lat, p_lat, mods, lat_row, g2, glu_w, s5_glu_b[0], ml_norm_g[0],
                      w_out, ffn, 0)
    ctx = _outproj_even(ctx, ys_ctx, m_ctx, p_ctx, mods, ctx_row, g2, glu_w, s5_glu_b[0], ml_norm_g[0],
                        w_out, ffn, 0)

    mods = mods_all[1]
    g1 = norm1_g[1].reshape(1, d)
    g2 = norm2_g[1].reshape(1, d)
    vw = ret_norm_g.shape[-1]
    qk = (ret_w_in.shape[-1] - 2 * vw) // 2
    w_in = ret_w_in[0].astype(BF16)
    cos, sin = _grid_rope(ln, qk // RET_HEADS)
    q, k, v, gate = _inproj_odd(x, mods, lat_row, g1, cos, sin, w_in, qk, vw)
    k_ctx, v_ctx = _inproj_odd_ctx(ctx, mods, ctx_row, g1, w_in, qk, vw)
    a = _retention_mix(q, k, v, k_ctx, v_ctx, ret_log_gamma[0])
    w_out = ret_w_out[0].astype(BF16)
    return _outproj_odd(x, a, gate, mods, lat_row, g2, ret_norm_g[0], w_out, ffn, 1, final_g.reshape(1, d))
```

```python
import functools
import math

import jax
import jax.numpy as jnp
from jax import lax
from jax.experimental import pallas as pl
from jax.experimental.pallas import tpu as pltpu

F32 = jnp.float32
BF16 = jnp.bfloat16

EPS = 1e-6
N_MOD = 6
GRID_W = 64
ROPE_BASE = 10000.0

S5_GROUP = 16
S5_STATE = 64
S5_STEP = 16
ML_HEADS = 4
ML_CHUNK = 128
RET_HEADS = 4
SUBLANES = 8
LANES = 128
NEG_BIG = -1e30
VMEM_LIMIT = 56 * 1024 * 1024


def _cparams(sem):
    return pltpu.CompilerParams(dimension_semantics=sem, vmem_limit_bytes=VMEM_LIMIT)


def _resident(shape, index_map):
    return pl.BlockSpec(shape, index_map, pipeline_mode=pl.Buffered(1))


def _token_tile(n):
    for t in (512, 256, 128):
        if n % t == 0:
            return t
    raise ValueError(f"sequence length {n} must be a multiple of 128")


def _silu(v):
    return v * jax.nn.sigmoid(v)


def _norm_mod(x, g, shift, scale):
    y = x * lax.rsqrt(jnp.mean(x * x, axis=-1, keepdims=True) + EPS)
    return (y * g) * (1.0 + scale) + shift


def _dot(a, b):
    return jnp.dot(a, b, preferred_element_type=F32)


def _dot_nt(a, b):
    return lax.dot_general(a, b, (((1,), (1,)), ((), ())), preferred_element_type=F32)


def _dot_tn(a, b):
    return lax.dot_general(a, b, (((0,), (0,)), ((), ())), preferred_element_type=F32)


def _mod_kernel(v_ref, w_ref, b_ref, o_ref):
    s = _silu(v_ref[...]).astype(BF16)
    o_ref[0] = _dot(s, w_ref[0].astype(BF16)) + b_ref[0]


def _modulation(vec, ada_w, ada_b):
    depth, d, n = ada_w.shape
    r = vec.shape[0]
    tn = 1024
    return pl.pallas_call(
        _mod_kernel,
        grid=(depth, n // tn),
        in_specs=[
            pl.BlockSpec((r, d), lambda l, j: (0, 0)),
            pl.BlockSpec((1, d, tn), lambda l, j: (l, 0, j)),
            pl.BlockSpec((1, 1, tn), lambda l, j: (l, 0, j)),
        ],
        out_specs=pl.BlockSpec((1, r, tn), lambda l, j: (l, 0, j)),
        out_shape=jax.ShapeDtypeStruct((depth, r, n), F32),
        compiler_params=_cparams(("arbitrary", "arbitrary")),
        name="adaln_modulation",
    )(vec, ada_w, ada_b.reshape(depth, 1, n))


def _inproj_even_kernel(x_ref, sh_ref, sc_ref, g_ref, w_ref, u_ref, p_ref, gate_ref, *, s5_width):
    h = _norm_mod(x_ref[0], g_ref[...], sh_ref[0], sc_ref[0]).astype(BF16)
    p = _dot(h, w_ref[...])
    n_mid = p_ref.shape[-1]
    u_ref[0] = p[:, :s5_width]
    p_ref[0] = p[:, s5_width:s5_width + n_mid].astype(BF16)
    gate_ref[0] = p[:, s5_width + n_mid:]


def _inproj_even(x, mods, row_of_batch, g, w, s5_width):
    bn, ln, d = x.shape
    n = w.shape[1]
    n_mid = n - s5_width - LANES
    tm = _token_tile(ln)
    return pl.pallas_call(
        functools.partial(_inproj_even_kernel, s5_width=s5_width),
        grid=(bn, ln // tm),
        in_specs=[
            pl.BlockSpec((1, tm, d), lambda b, i: (b, i, 0)),
            pl.BlockSpec((1, 1, d), lambda b, i: (row_of_batch(b) * N_MOD + 0, 0, 0)),
            pl.BlockSpec((1, 1, d), lambda b, i: (row_of_batch(b) * N_MOD + 1, 0, 0)),
            pl.BlockSpec((1, d), lambda b, i: (0, 0)),
            _resident((d, n), lambda b, i: (0, 0)),
        ],
        out_specs=[
            pl.BlockSpec((1, tm, s5_width), lambda b, i: (b, i, 0)),
            pl.BlockSpec((1, tm, n_mid), lambda b, i: (b, i, 0)),
            pl.BlockSpec((1, tm, LANES), lambda b, i: (b, i, 0)),
        ],
        out_shape=[
            jax.ShapeDtypeStruct((bn, ln, s5_width), F32),
            jax.ShapeDtypeStruct((bn, ln, n_mid), BF16),
            jax.ShapeDtypeStruct((bn, ln, LANES), F32),
        ],
        compiler_params=_cparams(("arbitrary", "arbitrary")),
        name="inproj_even",
    )(x, mods, mods, g, w)


def _s5_toeplitz_kernel(c_ref, w_ref, d_ref, o_ref, *, steps, chans):
    width = steps * chans
    lag0 = (steps - 1) * chans
    slab = lag0 // LANES
    row = lax.broadcasted_iota(jnp.int32, (chans, LANES), 0)
    lane = lax.broadcasted_iota(jnp.int32, (chans, LANES), 1)
    diag = lane == row + lag0 % LANES
    for n in range(c_ref.shape[0]):
        gen = lax.dot_general(c_ref[n], w_ref[n], (((1,), (1,)), ((), ())),
                              precision=lax.Precision.HIGHEST, preferred_element_type=F32)
        pieces = [gen[:, j * LANES:(j + 1) * LANES] for j in range(2 * width // LANES)]
        pieces[slab] = pieces[slab] + jnp.where(diag, d_ref[n], 0.0)
        gen = jnp.concatenate(pieces, axis=1)
        for t in range(steps):
            off = (steps - 1 - t) * chans
            o_ref[n, t * chans:(t + 1) * chans, :] = gen[:, off:off + width].astype(BF16)


def _s5_toeplitz(c2, w_gen, d_rep, steps, chans):
    n, co, kk = c2.shape
    r = w_gen.shape[1]
    nb = 8
    return pl.pallas_call(
        functools.partial(_s5_toeplitz_kernel, steps=steps, chans=chans),
        grid=(n // nb,),
        in_specs=[pl.BlockSpec((nb, co, kk), lambda i: (i, 0, 0)), pl.BlockSpec((nb, r, kk), lambda i: (i, 0, 0)),
                  pl.BlockSpec((nb, co, LANES), lambda i: (i, 0, 0))],
        out_specs=pl.BlockSpec((nb, steps * chans, steps * chans), lambda i: (i, 0, 0)),
        out_shape=jax.ShapeDtypeStruct((n, steps * chans, steps * chans), BF16),
        compiler_params=_cparams(("arbitrary",)),
        name="s5_toeplitz",
    )(c2, w_gen, d_rep)


def _s5_matrices(a_re, a_im, log_dt, b_re, b_im, c_re, c_im, d_skip):
    k = S5_STEP
    a_re, a_im = a_re.astype(F32), a_im.astype(F32)
    dt = jnp.exp(log_dt.astype(F32))[..., None]
    lam_re, lam_im = a_re * dt, a_im * dt
    steps = jnp.arange(k + 1, dtype=F32)[:, None, None, None]
    mag = jnp.exp(steps * lam_re)
    pr, pi = mag * jnp.cos(steps * lam_im), mag * jnp.sin(steps * lam_im)
    den = a_re * a_re + a_im * a_im
    nr = pr[1] - 1.0
    coef_re = (nr * a_re + pi[1] * a_im) / den
    coef_im = (pi[1] * a_re - nr * a_im) / den
    bt_re = jnp.swapaxes(b_re.astype(F32), -1, -2)
    bt_im = jnp.swapaxes(b_im.astype(F32), -1, -2)
    bb_re = coef_re[:, :, None] * bt_re - coef_im[:, :, None] * bt_im
    bb_im = coef_re[:, :, None] * bt_im + coef_im[:, :, None] * bt_re
    wr = pr[:, :, :, None] * bb_re - pi[:, :, :, None] * bb_im
    wi = pr[:, :, :, None] * bb_im + pi[:, :, :, None] * bb_re
    c_re, c_im = c_re.astype(F32), c_im.astype(F32)
    g_n, c_n, p_n = a_re.shape[1], b_re.shape[-1], a_re.shape[-1]
    def lag_rows(w_re, w_im):
        return jnp.transpose(jnp.concatenate([w_re, -w_im], -1), (1, 0, 2, 3)).reshape(g_n, k * c_n, 2 * p_n)

    w_gen = jnp.concatenate(
        [jnp.pad(lag_rows(wr[k - 1::-1, 0], wi[k - 1::-1, 0]), ((0, 0), (0, k * c_n), (0, 0))),
         jnp.pad(lag_rows(wr[:k, 1], wi[:k, 1]), ((0, 0), ((k - 1) * c_n, c_n), (0, 0)))], axis=-1)
    c2 = jnp.concatenate([c_re[0], c_im[0], c_re[1], c_im[1]], -1)
    d_rep = jnp.broadcast_to(d_skip.astype(F32).reshape(g_n, c_n, 1), (g_n, c_n, LANES))
    toep = _s5_toeplitz(c2, w_gen, d_rep, k, c_n)
    endw = jnp.concatenate([wr[k - 1::-1, 0], wr[:k, 1], wi[k - 1::-1, 0], wi[:k, 1]], -1)
    endw = jnp.transpose(endw, (1, 0, 2, 3)).reshape(g_n, k * c_n, 4 * p_n)

    def out_cols(d, powers_re, powers_im):
        cr, ci = c_re[d][None], c_im[d][None]
        pre, pim = powers_re[:, :, None, :], powers_im[:, :, None, :]
        return cr * pre - ci * pim, -(cr * pim + ci * pre)

    of_re, of_im = out_cols(0, pr[1:, 0], pi[1:, 0])
    ob_re, ob_im = out_cols(1, pr[k:0:-1, 1], pi[k:0:-1, 1])
    outw_t = jnp.concatenate([of_re, ob_re, of_im, ob_im], -1)
    outw_t = jnp.transpose(outw_t, (1, 0, 2, 3)).reshape(g_n, k * c_n, 4 * p_n)
    a0 = jnp.concatenate([pr[k, 0], pr[k, 1]], -1)
    a1 = jnp.concatenate([pi[k, 0], pi[k, 1]], -1)
    return toep.astype(BF16), endw.astype(BF16), outw_t.astype(BF16), a0, a1


S5_LANE_GROUPS = LANES // S5_GROUP


def _chunk_transpose(xs):
    n_arr = len(xs)
    lane_chunk = lax.broadcasted_iota(jnp.int32, xs[0].shape, 1) // S5_GROUP
    d = 1
    while d < n_arr:
        keep = (lane_chunk & d) == 0
        out = list(xs)
        for i in range(n_arr):
            if i & d:
                continue
            lo, hi = xs[i], xs[i + d]
            out[i] = jnp.where(keep, lo, pltpu.roll(hi, d * S5_GROUP, 1))
            out[i + d] = jnp.where(keep, pltpu.roll(lo, LANES - d * S5_GROUP, 1), hi)
        xs = out
        d *= 2
    return xs


def _s5_kernel(uc_ref, ul_ref, toep_ref, endw_ref, outw_ref, a0_ref, a1_ref, yc_ref, yl_ref,
               uc_s, ul_s, e_s, yc_s, yl_s, *, batch, n_ctx, n_lat):
    p = S5_STATE
    gl = S5_LANE_GROUPS
    k = S5_STEP

    def stack(u_ref, us_ref, n_blk):
        for b in range(batch):
            for half in range(k // gl):
                xs = [u_ref[b, pl.ds(half * gl + s, n_blk, stride=k), :] for s in range(gl)]
                ys = _chunk_transpose(xs)
                for g in range(gl):
                    us_ref[g, b * n_blk:(b + 1) * n_blk, half * LANES:(half + 1) * LANES] = ys[g].astype(BF16)

    stack(uc_ref, uc_s, n_ctx)
    stack(ul_ref, ul_s, n_lat)

    def increments(us_ref, n_blk, first_blk):
        for g in range(gl):
            e = _dot(us_ref[g], endw_ref[g])
            for b in range(batch):
                for part in range(2):
                    e_s[part, b, pl.ds(first_blk * gl + g, n_blk, stride=gl), :] = (
                        e[b * n_blk:(b + 1) * n_blk, part * LANES:(part + 1) * LANES])

    increments(uc_s, n_ctx, 0)
    increments(ul_s, n_lat, n_ctx)

    a0 = a0_ref[...]
    a1 = a1_ref[...]
    fwd_lane = lax.broadcasted_iota(jnp.int32, (batch, gl, 2 * p), 2) < p

    def scan(first_blk, n, carry):
        def body(i, c):
            s0, s1 = c
            rf = pl.ds(pl.multiple_of((first_blk + i) * gl, gl), gl)
            rb = pl.ds(pl.multiple_of((first_blk + n - 1 - i) * gl, gl), gl)
            e0 = jnp.where(fwd_lane, e_s[0, :, rf, :], e_s[0, :, rb, :])
            e1 = jnp.where(fwd_lane, e_s[1, :, rf, :], e_s[1, :, rb, :])
            e_s[0, :, rf, 0:p] = s0[:, :, 0:p]
            e_s[0, :, rb, p:2 * p] = s0[:, :, p:2 * p]
            e_s[1, :, rf, 0:p] = s1[:, :, 0:p]
            e_s[1, :, rb, p:2 * p] = s1[:, :, p:2 * p]
            return a0 * s0 - a1 * s1 + e0, a0 * s1 + a1 * s0 + e1
        return lax.fori_loop(0, n, body, carry)

    zero = jnp.zeros((batch, gl, 2 * p), F32)
    carry = scan(0, n_ctx, (zero, zero))
    scan(n_ctx, n_lat, carry)

    def outputs(us_ref, ys_ref, y_ref, n_blk, first_blk):
        for g in range(gl):
            state = jnp.concatenate(
                [jnp.concatenate([e_s[part, b, pl.ds(first_blk * gl + g, n_blk, stride=gl), :]
                                  for b in range(batch)], axis=0) for part in range(2)], axis=1)
            ys_ref[g] = _dot_nt(us_ref[g], toep_ref[g]) + _dot_nt(state.astype(BF16), outw_ref[g])
        for b in range(batch):
            for half in range(k // gl):
                ys = [ys_ref[g, b * n_blk:(b + 1) * n_blk, half * LANES:(half + 1) * LANES] for g in range(gl)]
                xs = _chunk_transpose(ys)
                for s in range(gl):
                    y_ref[b, pl.ds(half * gl + s, n_blk, stride=k), :] = xs[s]

    outputs(uc_s, yc_s, yc_ref, n_ctx, 0)
    outputs(ul_s, yl_s, yl_ref, n_lat, n_ctx)


def _s5_mix(u_ctx, u_lat, mats):
    toep, endw, outw, a0, a1 = mats
    bn, lc, w = u_ctx.shape
    ll = u_lat.shape[1]
    gl = S5_LANE_GROUPS
    kc = S5_STEP * S5_GROUP
    n_ctx, n_lat = lc // S5_STEP, ll // S5_STEP
    bb = 4 if bn % 4 == 0 else bn
    seq = lambda n_rows: pl.BlockSpec((bb, n_rows, LANES), lambda i, j: (j, 0, i))
    grp = lambda r, c: pl.BlockSpec((gl, r, c), lambda i, j: (i, 0, 0))
    return pl.pallas_call(
        functools.partial(_s5_kernel, batch=bb, n_ctx=n_ctx, n_lat=n_lat),
        grid=(w // LANES, bn // bb),
        in_specs=[seq(lc), seq(ll), grp(kc, kc), grp(kc, kc), grp(kc, kc),
                  pl.BlockSpec((gl, 2 * S5_STATE), lambda i, j: (i, 0)),
                  pl.BlockSpec((gl, 2 * S5_STATE), lambda i, j: (i, 0))],
        out_specs=[seq(lc), seq(ll)],
        out_shape=[jax.ShapeDtypeStruct((bn, lc, w), F32), jax.ShapeDtypeStruct((bn, ll, w), F32)],
        scratch_shapes=[
            pltpu.VMEM((gl, bb * n_ctx, kc), BF16), pltpu.VMEM((gl, bb * n_lat, kc), BF16),
            pltpu.VMEM((2, bb, (n_ctx + n_lat) * gl, LANES), F32),
            pltpu.VMEM((gl, bb * n_ctx, kc), F32), pltpu.VMEM((gl, bb * n_lat, kc), F32),
        ],
        compiler_params=_cparams(("arbitrary", "arbitrary")),
        name="s5_scan",
    )(u_ctx, u_lat, toep, endw, outw, a0, a1)


def _mlstm_kernel(pc_ref, pl_ref, gc_ref, gl_ref, cw_ref, cb_ref, wq_ref, wkt_ref, wv_ref, gb_ref, mc_ref, ml_ref,
                  q_s, kt_s, vx_s, h_s, st_s, bcol_s, mcol_s, rrow_s, xpad_s, *, n_ctx_rows, n_lat_rows):
    t_n = ML_CHUNK
    width = ML_HEADS * LANES
    nc_ctx, nc_lat = n_ctx_rows // t_n, n_lat_rows // t_n
    nc = nc_ctx + nc_lat
    scale_k = LANES ** -0.5

    def project(p_ref, n_rows, base):
        ones = jnp.ones((n_rows, LANES), BF16)
        pad = SUBLANES
        xpad_s[0:pad, :] = jnp.zeros((pad, LANES), F32)
        xpad_s[pad + n_rows:2 * pad + n_rows, :] = jnp.zeros((pad, LANES), F32)
        for h in range(ML_HEADS):
            cols = slice(h * LANES, (h + 1) * LANES)
            xm_lo = p_ref[0, :, cols]
            xm = xm_lo.astype(F32)
            xpad_s[pad:pad + n_rows, :] = xm
            taps = cw_ref[:, cols]
            n_tap = taps.shape[0]
            assert n_tap // 2 <= pad
            acc = jnp.zeros_like(xm) + cb_ref[:, cols]
            for j in range(n_tap):
                d = j - n_tap // 2
                sh = xm if d == 0 else xpad_s[pad + d:pad + d + n_rows, :]
                acc = acc + sh * taps[j:j + 1, :]
            xc = _silu(acc).astype(BF16)
            q_s[base:base + n_rows, cols] = _dot(xc, wq_ref[h]).astype(BF16)
            kt = (_dot_nt(wkt_ref[h], xc) * scale_k).astype(BF16)
            for c in range(n_rows // t_n):
                kt_s[base // t_n + c, cols, :] = kt[:, c * t_n:(c + 1) * t_n]
            vx_s[base:base + n_rows, 2 * h * LANES:(2 * h + 1) * LANES] = (
                _dot(xm_lo, wv_ref[h]).astype(BF16))
            vx_s[base:base + n_rows, (2 * h + 1) * LANES:(2 * h + 2) * LANES] = ones

    project(pc_ref, n_ctx_rows, 0)
    project(pl_ref, n_lat_rows, n_ctx_rows)

    ti = lax.broadcasted_iota(jnp.int32, (t_n, t_n), 0)
    si = lax.broadcasted_iota(jnp.int32, (t_n, t_n), 1)
    tri_f = (si <= ti).astype(BF16)
    tri_b = (si >= ti).astype(BF16)
    lane = lax.broadcasted_iota(jnp.int32, (t_n, LANES), 1)
    trow = lax.broadcasted_iota(jnp.int32, (t_n, LANES), 0)

    def gate_prep(g_ref, n_chunks, base_chunk):
        def body(c, _):
            rows = pl.ds(pl.multiple_of(c * t_n, t_n), t_n)
            gcol = g_ref[0, rows, :] + gb_ref[...]
            lf = jax.nn.log_sigmoid(gcol)
            hi = lf.astype(BF16)
            lo = (lf - hi.astype(F32)).astype(BF16)
            pre = _dot(tri_f, hi) + _dot(tri_f, lo)
            suf = _dot(tri_b, hi) + _dot(tri_b, lo)
            bsum = jnp.where(lane < 2 * ML_HEADS, pre, suf)
            rcol = gcol - pltpu.roll(bsum, LANES - ML_HEADS, 1)
            pmax, smax = rcol, rcol
            step = 1
            while step < t_n:
                pmax = jnp.maximum(pmax, jnp.where(trow >= step, pltpu.roll(pmax, step, 0), NEG_BIG))
                smax = jnp.maximum(smax, jnp.where(trow < t_n - step, pltpu.roll(smax, t_n - step, 0), NEG_BIG))
                step *= 2
            bcol_s[base_chunk + c] = bsum
            mcol_s[base_chunk + c] = jnp.where(lane < 2 * ML_HEADS, pmax, smax)
            rrow_s[base_chunk + c] = rcol.T[0:4 * ML_HEADS, :]
            return 0
        lax.fori_loop(0, n_chunks, body, 0, unroll=2 if n_chunks % 2 == 0 else 1)

    gate_prep(gc_ref, nc_ctx, 0)
    gate_prep(gl_ref, nc_lat, nc_ctx)

    def one_dir(c, h, d, m_in):
        cols = slice(h * LANES, (h + 1) * LANES)
        xcols = slice(2 * h * LANES, (2 * h + 2) * LANES)
        li = 2 * d * ML_HEADS + h
        mask = (si <= ti) if d == 0 else (si >= ti)
        last = t_n - 1 if d == 0 else 0
        rows = pl.ds(pl.multiple_of(c * t_n, t_n), t_n)
        q = q_s[rows, cols]
        kt = kt_s[c, cols, :]
        vx = vx_s[rows, xcols]
        r_row = rrow_s[c][li:li + 1, :]
        run_max = jnp.broadcast_to(mcol_s[c][:, li:li + 1], (t_n, LANES))
        b_rep = jnp.broadcast_to(bcol_s[c][:, li + ML_HEADS:li + ML_HEADS + 1], (t_n, LANES))
        mm = jnp.maximum(m_in, run_max)
        dmat = jnp.exp(jnp.where(mask, r_row - mm, NEG_BIG))
        sm = (dmat * _dot(q, kt)).astype(BF16)
        intra = _dot(sm, vx)
        cross = _dot(q, st_s[2 * h + d].astype(BF16))
        inter = jnp.exp(m_in - mm)
        num = intra[:, :LANES] + inter * cross[:, :LANES]
        den = intra[:, LANES:] + inter * cross[:, LANES:]
        h_s[rows, cols] = h_s[rows, cols] + num / jnp.maximum(jnp.abs(den), jnp.exp(-b_rep - mm))
        r_top = run_max[last:last + 1, :]
        b_last = b_rep[last:last + 1, :]
        g_max = b_last + r_top
        m_new = jnp.maximum(b_last + m_in, g_max)
        dec = jnp.exp(b_last + m_in - m_new)
        inj = jnp.exp(g_max - m_new)
        kw = (kt.astype(F32) * (jnp.exp(r_row - r_top) * inj)).astype(BF16)
        inc = _dot(kw, vx)
        st_s[2 * h + d] = jnp.concatenate([dec, dec], axis=1) * st_s[2 * h + d] + inc
        return m_new

    st_s[...] = jnp.zeros_like(st_s)
    h_s[...] = jnp.zeros_like(h_s)

    def body(i, carry):
        cb = jnp.where(i < nc_ctx, nc_ctx - 1 - i, nc + nc_ctx - 1 - i)
        return tuple(one_dir(i if d == 0 else cb, h, d, carry[2 * h + d])
                     for h in range(ML_HEADS) for d in range(2))

    zero = jnp.zeros((1, LANES), F32)
    lax.fori_loop(0, nc, body, (zero,) * (2 * ML_HEADS))

    mc_ref[0] = h_s[0:n_ctx_rows, :].astype(BF16)
    ml_ref[0] = h_s[n_ctx_rows:n_ctx_rows + n_lat_rows, :].astype(BF16)


def _mlstm_mix(p_ctx, p_lat, g_ctx, g_lat, conv_w, conv_b, wq, wk, wv, gate_b):
    bn, lc, pw = p_ctx.shape
    ll = p_lat.shape[1]
    width = ML_HEADS * LANES
    lt = lc + ll
    nct = lt // ML_CHUNK
    gb = jnp.zeros((1, LANES), F32).at[0, :4 * ML_HEADS].set(gate_b.astype(F32))
    full2 = lambda a: pl.BlockSpec(a.shape, lambda b: (0,) * a.ndim)
    conv_b2 = conv_b.reshape(1, width)
    assert ML_CHUNK == LANES
    wq, wkt, wv = wq.astype(BF16), jnp.swapaxes(wk, 1, 2).astype(BF16), wv.astype(BF16)
    return pl.pallas_call(
        functools.partial(_mlstm_kernel, n_ctx_rows=lc, n_lat_rows=ll),
        grid=(bn,),
        in_specs=[
            pl.BlockSpec((1, lc, pw), lambda b: (b, 0, 0)),
            pl.BlockSpec((1, ll, pw), lambda b: (b, 0, 0)),
            pl.BlockSpec((1, lc, LANES), lambda b: (b, 0, 0)),
            pl.BlockSpec((1, ll, LANES), lambda b: (b, 0, 0)),
            full2(conv_w), full2(conv_b2), full2(wq), full2(wkt), full2(wv), full2(gb),
        ],
        out_specs=[
            pl.BlockSpec((1, lc, width), lambda b: (b, 0, 0)),
            pl.BlockSpec((1, ll, width), lambda b: (b, 0, 0)),
        ],
        out_shape=[jax.ShapeDtypeStruct((bn, lc, width), BF16), jax.ShapeDtypeStruct((bn, ll, width), BF16)],
        scratch_shapes=[
            pltpu.VMEM((lt, width), BF16),
            pltpu.VMEM((nct, width, ML_CHUNK), BF16),
            pltpu.VMEM((lt, 2 * width), BF16),
            pltpu.VMEM((lt, width), F32),
            pltpu.VMEM((2 * ML_HEADS, LANES, 2 * LANES), F32),
            pltpu.VMEM((nct, ML_CHUNK, LANES), F32), pltpu.VMEM((nct, ML_CHUNK, LANES), F32),
            pltpu.VMEM((nct, 4 * ML_HEADS, ML_CHUNK), F32),
            pltpu.VMEM((max(lc, ll) + 2 * SUBLANES, LANES), F32),
        ],
        compiler_params=_cparams(("arbitrary",)),
        name="mlstm_mix",
    )(p_ctx, p_lat, g_ctx, g_lat, conv_w, conv_b2, wq, wkt, wv, gb)


def _ffn_tail(x, y, gate_mix, g2, shift, scale, gate_ffn, w1_ref, w3_ref, w2_ref, hidden_chunk):
    x1 = x + gate_mix * y
    h2 = _norm_mod(x1, g2, shift, scale).astype(BF16)
    hidden = w1_ref.shape[1]
    acc = jnp.zeros_like(x1)
    start = 0
    while start < hidden:
        cs = slice(start, min(start + hidden_chunk, hidden))
        z = (_silu(_dot(h2, w1_ref[:, cs])) * _dot(h2, w3_ref[:, cs])).astype(BF16)
        acc = acc + _dot(z, w2_ref[cs, :])
        start += hidden_chunk
    return x1 + gate_ffn * acc


def _final_norm(x, g):
    return x * lax.rsqrt(jnp.mean(x * x, axis=-1, keepdims=True) + EPS) * g


def _outproj_even_kernel(x_ref, ys_ref, m_ref, og_ref, gm_ref, sh_ref, sc_ref, gf_ref, g2_ref, gw_ref, gbias_ref,
                         ng_ref, wo_ref, w1_ref, w3_ref, w2_ref, o_ref, *, hidden_chunk, heads):
    ys = jax.nn.gelu(ys_ref[0])
    s = ys * jax.nn.sigmoid(_dot(ys.astype(BF16), gw_ref[...]) + gbias_ref[...])
    dh = m_ref.shape[-1] // heads
    gated = []
    for h in range(heads):
        cols = slice(h * dh, (h + 1) * dh)
        hh = m_ref[0, :, cols].astype(F32)
        mu = jnp.mean(hh, axis=1, keepdims=True)
        cen = hh - mu
        var = jnp.mean(cen * cen, axis=1, keepdims=True)
        normed = cen * lax.rsqrt(var + EPS) * ng_ref[:, cols]
        gated.append((jax.nn.sigmoid(og_ref[0, :, cols].astype(F32)) * normed).astype(BF16))
    sw = ys_ref.shape[-1]
    y = _dot(s.astype(BF16), wo_ref[0:sw, :]) + _dot(jnp.concatenate(gated, axis=1), wo_ref[sw:, :])
    o_ref[0] = _ffn_tail(x_ref[0], y, gm_ref[0], g2_ref[...], sh_ref[0], sc_ref[0], gf_ref[0],
                         w1_ref, w3_ref, w2_ref, hidden_chunk)


def _outproj_odd_kernel(x_ref, a_ref, gate_ref, gm_ref, sh_ref, sc_ref, gf_ref, g2_ref, ng_ref, wo_ref, w1_ref,
                        w3_ref, w2_ref, fg_ref, o_ref, *, hidden_chunk, heads):
    hv = a_ref.shape[-1] // heads
    y = jnp.zeros(x_ref.shape[1:], F32)
    for h in range(heads):
        cols = slice(h * hv, (h + 1) * hv)
        o = a_ref[0, :, cols].astype(F32)
        mu = jnp.mean(o, axis=1, keepdims=True)
        cen = o - mu
        var = jnp.mean(cen * cen, axis=1, keepdims=True)
        normed = cen * lax.rsqrt(var + EPS) * ng_ref[:, cols]
        gated = (_silu(gate_ref[0, :, cols].astype(F32)) * normed).astype(BF16)
        y = y + _dot(gated, wo_ref[cols, :])
    out = _ffn_tail(x_ref[0], y, gm_ref[0], g2_ref[...], sh_ref[0], sc_ref[0], gf_ref[0],
                    w1_ref, w3_ref, w2_ref, hidden_chunk)
    o_ref[0] = _final_norm(out, fg_ref[...])


MXU_WIDTH = 256


def _hidden_chunk(hidden):
    return -(-hidden // (2 * MXU_WIDTH)) * MXU_WIDTH


def _mod_specs(d, row_of_batch, slots):
    return [pl.BlockSpec((1, 1, d), lambda b, i, s=s: (row_of_batch(b) * N_MOD + s, 0, 0)) for s in slots]


def _layer_resident(w, layer):
    return _resident((None,) + w.shape[1:], lambda b, i: (layer,) + (0,) * (w.ndim - 1))


def _outproj_even(x, ys, m, p, mods, row_of_batch, g2, glu_w, glu_b, norm_g, w_out, ffn, layer):
    bn, ln, d = x.shape
    sw, mw = ys.shape[-1], m.shape[-1]
    tm = _token_tile(ln)
    res = lambda a: _resident(a.shape, lambda b, i: (0,) * a.ndim)
    tok = lambda w: pl.BlockSpec((1, tm, w), lambda b, i: (b, i, 0))
    row = lambda w: pl.BlockSpec((1, w), lambda b, i: (0, 0))
    return pl.pallas_call(
        functools.partial(_outproj_even_kernel, hidden_chunk=_hidden_chunk(ffn[0].shape[-1]), heads=ML_HEADS),
        grid=(bn, ln // tm),
        in_specs=[tok(d), tok(sw), tok(mw), pl.BlockSpec((1, tm, mw), lambda b, i: (b, i, 1))]
        + _mod_specs(d, row_of_batch, (2, 3, 4, 5))
        + [row(d), res(glu_w), row(sw), row(mw), res(w_out)] + [_layer_resident(w, layer) for w in ffn],
        out_specs=tok(d),
        out_shape=jax.ShapeDtypeStruct((bn, ln, d), F32),
        compiler_params=_cparams(("arbitrary", "arbitrary")),
        name="outproj_ffn_even",
    )(x, ys, m, p, mods, mods, mods, mods, g2, glu_w, glu_b.reshape(1, sw), norm_g.reshape(1, mw), w_out, *ffn)


def _outproj_odd(x, a, gate, mods, row_of_batch, g2, norm_g, w_out, ffn, layer, final_g):
    bn, ln, d = x.shape
    vw = a.shape[-1]
    tm = _token_tile(ln)
    res = lambda a_: _resident(a_.shape, lambda b, i: (0,) * a_.ndim)
    tok = lambda w: pl.BlockSpec((1, tm, w), lambda b, i: (b, i, 0))
    row = lambda w: pl.BlockSpec((1, w), lambda b, i: (0, 0))
    return pl.pallas_call(
        functools.partial(_outproj_odd_kernel, hidden_chunk=_hidden_chunk(ffn[0].shape[-1]), heads=RET_HEADS),
        grid=(bn, ln // tm),
        in_specs=[tok(d), tok(vw), tok(vw)] + _mod_specs(d, row_of_batch, (2, 3, 4, 5))
        + [row(d), row(vw), res(w_out)] + [_layer_resident(w, layer) for w in ffn] + [row(d)],
        out_specs=tok(d),
        out_shape=jax.ShapeDtypeStruct((bn, ln, d), F32),
        compiler_params=_cparams(("arbitrary", "arbitrary")),
        name="outproj_ffn_odd",
    )(x, a, gate, mods, mods, mods, mods, g2, norm_g.reshape(1, vw), w_out, *ffn, final_g)


def _rope_heads(t, cos, sin, heads, hk):
    half = hk // 2
    out = []
    for h in range(heads):
        t1 = t[:, h * hk:h * hk + half]
        t2 = t[:, h * hk + half:(h + 1) * hk]
        out.append(t1 * cos - t2 * sin)
        out.append(t2 * cos + t1 * sin)
    return jnp.concatenate(out, axis=-1)


def _inproj_odd_kernel(x_ref, sh_ref, sc_ref, g_ref, cos_ref, sin_ref, w_ref, q_ref, k_ref, v_ref, gate_ref,
                       *, qk, vw, heads):
    h = _norm_mod(x_ref[0], g_ref[...], sh_ref[0], sc_ref[0]).astype(BF16)
    hk = qk // heads
    cos, sin = cos_ref[...], sin_ref[...]
    q = _rope_heads(_dot(h, w_ref[:, 0:qk]), cos, sin, heads, hk)
    q_ref[0] = (q * (hk ** -0.5)).astype(BF16)
    k = _rope_heads(_dot(h, w_ref[:, qk:2 * qk]), cos, sin, heads, hk)
    k_ref[0] = k.astype(BF16)
    v_ref[0] = _dot(h, w_ref[:, 2 * qk:2 * qk + vw]).astype(BF16)
    gate_ref[0] = _dot(h, w_ref[:, 2 * qk + vw:2 * qk + 2 * vw]).astype(BF16)


def _inproj_odd_ctx_kernel(x_ref, sh_ref, sc_ref, g_ref, w_ref, k_ref, v_ref, *, qk, vw):
    h = _norm_mod(x_ref[0], g_ref[...], sh_ref[0], sc_ref[0]).astype(BF16)
    k_ref[0] = _dot(h, w_ref[:, qk:2 * qk]).astype(BF16)
    v_ref[0] = _dot(h, w_ref[:, 2 * qk:2 * qk + vw]).astype(BF16)


def _inproj_odd(x, mods, row_of_batch, g, cos, sin, w, qk, vw):
    bn, ln, d = x.shape
    tm = _token_tile(ln)
    tok = lambda w_: pl.BlockSpec((1, tm, w_), lambda b, i: (b, i, 0))
    half = cos.shape[1]
    return pl.pallas_call(
        functools.partial(_inproj_odd_kernel, qk=qk, vw=vw, heads=RET_HEADS),
        grid=(bn, ln // tm),
        in_specs=[tok(d)] + _mod_specs(d, row_of_batch, (0, 1))
        + [pl.BlockSpec((1, d), lambda b, i: (0, 0)),
           pl.BlockSpec((tm, half), lambda b, i: (i, 0)), pl.BlockSpec((tm, half), lambda b, i: (i, 0)),
           _resident(w.shape, lambda b, i: (0, 0))],
        out_specs=[tok(qk), tok(qk), tok(vw), tok(vw)],
        out_shape=[jax.ShapeDtypeStruct((bn, ln, qk), BF16), jax.ShapeDtypeStruct((bn, ln, qk), BF16),
                   jax.ShapeDtypeStruct((bn, ln, vw), BF16), jax.ShapeDtypeStruct((bn, ln, vw), BF16)],
        compiler_params=_cparams(("arbitrary", "arbitrary")),
        name="inproj_odd",
    )(x, mods, mods, g, cos, sin, w)


def _inproj_odd_ctx(x, mods, row_of_batch, g, w_kv, qk, vw):
    bn, ln, d = x.shape
    tm = _token_tile(ln)
    tok = lambda w_: pl.BlockSpec((1, tm, w_), lambda b, i: (b, i, 0))
    return pl.pallas_call(
        functools.partial(_inproj_odd_ctx_kernel, qk=qk, vw=vw),
        grid=(bn, ln // tm),
        in_specs=[tok(d)] + _mod_specs(d, row_of_batch, (0, 1))
        + [pl.BlockSpec((1, d), lambda b, i: (0, 0)), _resident(w_kv.shape, lambda b, i: (0, 0))],
        out_specs=[tok(qk), tok(vw)],
        out_shape=[jax.ShapeDtypeStruct((bn, ln, qk), BF16), jax.ShapeDtypeStruct((bn, ln, vw), BF16)],
        compiler_params=_cparams(("arbitrary", "arbitrary")),
        name="inproj_odd_ctx",
    )(x, mods, mods, g, w_kv)


def _retention_kernel(lg_ref, q_ref, k_ref, v_ref, kc_ref, vc_ref, o_ref, acc_s, sf_s, sb_s,
                      *, chunk, n_ctx_rows, n_lat_rows):
    t_n = chunk
    h = pl.program_id(1)
    lgf = jnp.full((1, 1), lg_ref[0, h], F32)
    lgb = jnp.full((1, 1), lg_ref[1, h], F32)
    nc_ctx, nc_lat = n_ctx_rows // t_n, n_lat_rows // t_n
    ti = lax.broadcasted_iota(jnp.int32, (t_n, t_n), 0)
    si = lax.broadcasted_iota(jnp.int32, (t_n, t_n), 1)
    diff = (ti - si).astype(F32)
    decay = jnp.where(diff >= 0, jnp.exp(lgf * jnp.maximum(diff, 0.0)), jnp.exp(lgb * jnp.maximum(-diff, 0.0)))
    pos = lax.broadcasted_iota(jnp.int32, (t_n, 1), 0).astype(F32)
    inter_f = jnp.exp(lgf * (pos + 1.0))
    inter_b = jnp.exp(lgb * (t_n - pos))
    wend_f = jnp.exp(lgf * (t_n - 1.0 - pos))
    wend_b = jnp.exp(lgb * pos)
    cd_f = jnp.exp(lgf * t_n)
    cd_b = jnp.exp(lgb * t_n)

    def rows_of(c):
        return pl.ds(pl.multiple_of(c * t_n, t_n), t_n)

    def bump(s_ref, k, v, wend, cd):
        kw = (k.astype(F32) * wend).astype(BF16)
        s_ref[...] = cd * s_ref[...] + _dot_tn(kw, v)

    sf_s[...] = jnp.zeros_like(sf_s)
    sb_s[...] = jnp.zeros_like(sb_s)

    def ctx_f(c, _):
        bump(sf_s, kc_ref[0, rows_of(c), :], vc_ref[0, rows_of(c), :], wend_f, cd_f)
        return 0
    lax.fori_loop(0, nc_ctx, ctx_f, 0)

    def ctx_b(i, _):
        c = nc_ctx - 1 - i
        bump(sb_s, kc_ref[0, rows_of(c), :], vc_ref[0, rows_of(c), :], wend_b, cd_b)
        return 0
    lax.fori_loop(0, nc_ctx, ctx_b, 0)

    def forward_part(rows):
        q, k, v = q_ref[0, rows, :], k_ref[0, rows, :], v_ref[0, rows, :]
        scores = (_dot_nt(q, k) * decay).astype(BF16)
        part = _dot(scores, v) + inter_f * _dot(q, sf_s[...].astype(BF16))
        bump(sf_s, k, v, wend_f, cd_f)
        return part

    def backward_part(rows):
        q, k, v = q_ref[0, rows, :], k_ref[0, rows, :], v_ref[0, rows, :]
        part = inter_b * _dot(q, sb_s[...].astype(BF16))
        bump(sb_s, k, v, wend_b, cd_b)
        return part

    def first_half(i, _):
        rf, rb = rows_of(i), rows_of(nc_lat - 1 - i)
        acc_s[rf, :] = forward_part(rf)
        acc_s[rb, :] = backward_part(rb)
        return 0
    lax.fori_loop(0, nc_lat // 2, first_half, 0)

    def second_half(i, _):
        rf, rb = rows_of(i), rows_of(nc_lat - 1 - i)
        o_ref[0, rf, :] = (acc_s[rf, :] + forward_part(rf)).astype(BF16)
        o_ref[0, rb, :] = (acc_s[rb, :] + backward_part(rb)).astype(BF16)
        return 0
    lax.fori_loop(nc_lat // 2, nc_lat, second_half, 0)


def _retention_mix(q, k, v, k_ctx, v_ctx, log_gamma):
    bn, ll, qk = q.shape
    lc = k_ctx.shape[1]
    vw = v.shape[-1]
    hk, hv = qk // RET_HEADS, vw // RET_HEADS
    chunk = 256 if (ll % 512 == 0 and lc % 256 == 0) else 128
    assert (ll // chunk) % 2 == 0, "latent chunks are visited in forward/backward pairs"
    head = lambda n_rows, w: pl.BlockSpec((1, n_rows, w), lambda b, h: (b, 0, h))
    return pl.pallas_call(
        functools.partial(_retention_kernel, chunk=chunk, n_ctx_rows=lc, n_lat_rows=ll),
        grid=(bn, RET_HEADS),
        in_specs=[
            pl.BlockSpec(memory_space=pltpu.SMEM),
            head(ll, hk), head(ll, hk), head(ll, hv), head(lc, hk), head(lc, hv),
        ],
        out_specs=head(ll, hv),
        out_shape=jax.ShapeDtypeStruct((bn, ll, vw), BF16),
        scratch_shapes=[pltpu.VMEM((ll, hv), F32), pltpu.VMEM((hk, hv), F32), pltpu.VMEM((hk, hv), F32)],
        compiler_params=_cparams(("arbitrary", "arbitrary")),
        name="retention_mix",
    )(log_gamma.astype(F32), q, k, v, k_ctx, v_ctx)


def _grid_rope(n_pos, hk):
    rows = n_pos // GRID_W
    row = jnp.repeat(jnp.arange(rows, dtype=F32), GRID_W)
    col = jnp.tile(jnp.arange(GRID_W, dtype=F32), rows)
    n_freq = hk // 4
    inv = ROPE_BASE ** (-jnp.arange(n_freq, dtype=F32) / n_freq)
    ang = jnp.concatenate([row[:, None] * inv, col[:, None] * inv], -1)
    return jnp.cos(ang), jnp.sin(ang)


def kernel(x, c, ctx, c_ctx, ada_w, ada_b, norm1_g, norm2_g, ab_w_in, ab_w_out, s5_a_re, s5_a_im, s5_log_dt,
           s5_b_re, s5_b_im, s5_c_re, s5_c_im, s5_d, s5_glu_w, s5_glu_b, ml_conv_w, ml_conv_b, ml_wq, ml_wk,
           ml_wv, ml_gate_b, ml_norm_g, ret_w_in, ret_w_out, ret_log_gamma, ret_norm_g, ffn_w1, ffn_w3, ffn_w2,
           final_g):
    bn, ln, d = x.shape
    depth = ada_w.shape[0]
    assert depth == 2, "one S5 || mLSTM layer followed by one retention layer"
    s5_width = s5_d.shape[-1]
    ml_width = ml_norm_g.shape[-1]
    assert ml_width == ML_HEADS * LANES and s5_width % S5_GROUP == 0

    r_pad = -(-(bn + 1) // SUBLANES) * SUBLANES
    vec = jnp.zeros((r_pad, d), F32).at[:bn].set(c).at[bn].set(c_ctx)
    mods_all = _modulation(vec, ada_w, ada_b).reshape(depth, r_pad * N_MOD, 1, d)
    lat_row = lambda b: b
    ctx_row = lambda b: bn

    mods = mods_all[0]
    g1 = norm1_g[0].reshape(1, d)
    g2 = norm2_g[0].reshape(1, d)
    w_in = ab_w_in[0]
    n_gate = w_in.shape[1] - s5_width - 2 * ml_width
    w_in = jnp.pad(w_in, ((0, 0), (0, LANES - n_gate))).astype(BF16)
    u_lat, p_lat, g_lat = _inproj_even(x, mods, lat_row, g1, w_in, s5_width)
    lc = ctx.shape[1]
    flat = lambda t: t.reshape(1, bn * lc, t.shape[-1])
    unflat = lambda t: t.reshape(bn, lc, t.shape[-1])
    u_ctx, p_ctx, g_ctx = (unflat(t) for t in _inproj_even(flat(ctx), mods, ctx_row, g1, w_in, s5_width))
    mats = _s5_matrices(s5_a_re[0], s5_a_im[0], s5_log_dt[0], s5_b_re[0], s5_b_im[0], s5_c_re[0], s5_c_im[0],
                        s5_d[0])
    ys_ctx, ys_lat = _s5_mix(u_ctx, u_lat, mats)
    m_ctx, m_lat = _mlstm_mix(p_ctx, p_lat, g_ctx, g_lat, ml_conv_w[0], ml_conv_b[0], ml_wq[0], ml_wk[0],
                              ml_wv[0], ml_gate_b[0])
    glu_w = s5_glu_w[0].astype(BF16)
    w_out = ab_w_out[0].astype(BF16)
    ffn = (ffn_w1.astype(BF16), ffn_w3.astype(BF16), ffn_w2.astype(BF16))
    x = _outproj_even(x, ys_lat, m_lat, p_lat, mods, lat_row, g2, glu_w, s5_glu_b[0], ml_norm_g[0],
                      w_out, ffn, 0)
    ctx = _outproj_even(flat(ctx), flat(ys_ctx), flat(m_ctx), flat(p_ctx), mods, ctx_row, g2, glu_w, s5_glu_b[0],
                        ml_norm_g[0], w_out, ffn, 0)

    mods = mods_all[1]
    g1 = norm1_g[1].reshape(1, d)
    g2 = norm2_g[1].reshape(1, d)
    vw = ret_norm_g.shape[-1]
    qk = (ret_w_in.shape[-1] - 2 * vw) // 2
    w_in = ret_w_in[0].astype(BF16)
    cos, sin = _grid_rope(ln, qk // RET_HEADS)
    q, k, v, gate = _inproj_odd(x, mods, lat_row, g1, cos, sin, w_in, qk, vw)
    k_ctx, v_ctx = (unflat(t) for t in _inproj_odd_ctx(ctx, mods, ctx_row, g1, w_in, qk, vw))
    a = _retention_mix(q, k, v, k_ctx, v_ctx, ret_log_gamma[0])
    w_out = ret_w_out[0].astype(BF16)
    return _outproj_odd(x, a, gate, mods, lat_row, g2, ret_norm_g[0], w_out, ffn, 1, final_g.reshape(1, d))
```

```python
import functools
import math

import jax
import jax.numpy as jnp
from jax import lax
from jax.experimental import pallas as pl
from jax.experimental.pallas import tpu as pltpu

F32 = jnp.float32
BF16 = jnp.bfloat16

EPS = 1e-6
N_MOD = 6
GRID_W = 64
ROPE_BASE = 10000.0

S5_GROUP = 16
S5_STATE = 64
S5_STEP = 16
ML_HEADS = 4
ML_CHUNK = 128
RET_HEADS = 4
SUBLANES = 8
LANES = 128
NEG_BIG = -1e30
VMEM_LIMIT = 56 * 1024 * 1024


def _cparams(sem):
    return pltpu.CompilerParams(dimension_semantics=sem, vmem_limit_bytes=VMEM_LIMIT)


def _resident(shape, index_map):
    return pl.BlockSpec(shape, index_map, pipeline_mode=pl.Buffered(1))


def _token_tile(n):
    for t in (512, 256, 128):
        if n % t == 0:
            return t
    raise ValueError(f"sequence length {n} must be a multiple of 128")


def _silu(v):
    return v * jax.nn.sigmoid(v)


def _norm_mod(x, g, shift, scale):
    y = x * lax.rsqrt(jnp.mean(x * x, axis=-1, keepdims=True) + EPS)
    return (y * g) * (1.0 + scale) + shift


def _dot(a, b):
    return jnp.dot(a, b, preferred_element_type=F32)


def _dot_nt(a, b):
    return lax.dot_general(a, b, (((1,), (1,)), ((), ())), preferred_element_type=F32)


def _dot_tn(a, b):
    return lax.dot_general(a, b, (((0,), (0,)), ((), ())), preferred_element_type=F32)


def _mod_kernel(v_ref, w_ref, b_ref, o_ref):
    s = _silu(v_ref[...]).astype(BF16)
    o_ref[0] = _dot(s, w_ref[0].astype(BF16)) + b_ref[0]


def _modulation(vec, ada_w, ada_b):
    depth, d, n = ada_w.shape
    r = vec.shape[0]
    tn = 1024
    return pl.pallas_call(
        _mod_kernel,
        grid=(depth, n // tn),
        in_specs=[
            pl.BlockSpec((r, d), lambda l, j: (0, 0)),
            pl.BlockSpec((1, d, tn), lambda l, j: (l, 0, j)),
            pl.BlockSpec((1, 1, tn), lambda l, j: (l, 0, j)),
        ],
        out_specs=pl.BlockSpec((1, r, tn), lambda l, j: (l, 0, j)),
        out_shape=jax.ShapeDtypeStruct((depth, r, n), F32),
        compiler_params=_cparams(("arbitrary", "arbitrary")),
        name="adaln_modulation",
    )(vec, ada_w, ada_b.reshape(depth, 1, n))


def _inproj_even_kernel(x_ref, sh_ref, sc_ref, g_ref, w_ref, u_ref, p_ref, gate_ref, *, s5_width):
    h = _norm_mod(x_ref[0], g_ref[...], sh_ref[0], sc_ref[0]).astype(BF16)
    p = _dot(h, w_ref[...])
    n_mid = p_ref.shape[-1]
    u_ref[0] = p[:, :s5_width]
    p_ref[0] = p[:, s5_width:s5_width + n_mid].astype(BF16)
    gate_ref[0] = p[:, s5_width + n_mid:]


def _inproj_even(x, mods, row_of_batch, g, w, s5_width):
    bn, ln, d = x.shape
    n = w.shape[1]
    n_mid = n - s5_width - LANES
    tm = _token_tile(ln)
    return pl.pallas_call(
        functools.partial(_inproj_even_kernel, s5_width=s5_width),
        grid=(bn, ln // tm),
        in_specs=[
            pl.BlockSpec((1, tm, d), lambda b, i: (b, i, 0)),
            pl.BlockSpec((1, 1, d), lambda b, i: (row_of_batch(b) * N_MOD + 0, 0, 0)),
            pl.BlockSpec((1, 1, d), lambda b, i: (row_of_batch(b) * N_MOD + 1, 0, 0)),
            pl.BlockSpec((1, d), lambda b, i: (0, 0)),
            _resident((d, n), lambda b, i: (0, 0)),
        ],
        out_specs=[
            pl.BlockSpec((1, tm, s5_width), lambda b, i: (b, i, 0)),
            pl.BlockSpec((1, tm, n_mid), lambda b, i: (b, i, 0)),
            pl.BlockSpec((1, tm, LANES), lambda b, i: (b, i, 0)),
        ],
        out_shape=[
            jax.ShapeDtypeStruct((bn, ln, s5_width), F32),
            jax.ShapeDtypeStruct((bn, ln, n_mid), BF16),
            jax.ShapeDtypeStruct((bn, ln, LANES), F32),
        ],
        compiler_params=_cparams(("arbitrary", "arbitrary")),
        name="inproj_even",
    )(x, mods, mods, g, w)


def _s5_toeplitz_kernel(c_ref, w_ref, d_ref, o_ref, *, steps, chans):
    width = steps * chans
    lag0 = (steps - 1) * chans
    slab = lag0 // LANES
    row = lax.broadcasted_iota(jnp.int32, (chans, LANES), 0)
    lane = lax.broadcasted_iota(jnp.int32, (chans, LANES), 1)
    diag = lane == row + lag0 % LANES
    for n in range(c_ref.shape[0]):
        gen = lax.dot_general(c_ref[n], w_ref[n], (((1,), (1,)), ((), ())),
                              precision=lax.Precision.HIGHEST, preferred_element_type=F32)
        pieces = [gen[:, j * LANES:(j + 1) * LANES] for j in range(2 * width // LANES)]
        pieces[slab] = pieces[slab] + jnp.where(diag, d_ref[n], 0.0)
        gen = jnp.concatenate(pieces, axis=1)
        for t in range(steps):
            off = (steps - 1 - t) * chans
            o_ref[n, t * chans:(t + 1) * chans, :] = gen[:, off:off + width].astype(BF16)


def _s5_toeplitz(c2, w_gen, d_rep, steps, chans):
    n, co, kk = c2.shape
    r = w_gen.shape[1]
    nb = 8
    return pl.pallas_call(
        functools.partial(_s5_toeplitz_kernel, steps=steps, chans=chans),
        grid=(n // nb,),
        in_specs=[pl.BlockSpec((nb, co, kk), lambda i: (i, 0, 0)), pl.BlockSpec((nb, r, kk), lambda i: (i, 0, 0)),
                  pl.BlockSpec((nb, co, LANES), lambda i: (i, 0, 0))],
        out_specs=pl.BlockSpec((nb, steps * chans, steps * chans), lambda i: (i, 0, 0)),
        out_shape=jax.ShapeDtypeStruct((n, steps * chans, steps * chans), BF16),
        compiler_params=_cparams(("arbitrary",)),
        name="s5_toeplitz",
    )(c2, w_gen, d_rep)


def _s5_matrices(a_re, a_im, log_dt, b_re, b_im, c_re, c_im, d_skip):
    k = S5_STEP
    a_re, a_im = a_re.astype(F32), a_im.astype(F32)
    dt = jnp.exp(log_dt.astype(F32))[..., None]
    lam_re, lam_im = a_re * dt, a_im * dt
    steps = jnp.arange(k + 1, dtype=F32)[:, None, None, None]
    mag = jnp.exp(steps * lam_re)
    pr, pi = mag * jnp.cos(steps * lam_im), mag * jnp.sin(steps * lam_im)
    den = a_re * a_re + a_im * a_im
    nr = pr[1] - 1.0
    coef_re = (nr * a_re + pi[1] * a_im) / den
    coef_im = (pi[1] * a_re - nr * a_im) / den
    bt_re = jnp.swapaxes(b_re.astype(F32), -1, -2)
    bt_im = jnp.swapaxes(b_im.astype(F32), -1, -2)
    bb_re = coef_re[:, :, None] * bt_re - coef_im[:, :, None] * bt_im
    bb_im = coef_re[:, :, None] * bt_im + coef_im[:, :, None] * bt_re
    wr = pr[:, :, :, None] * bb_re - pi[:, :, :, None] * bb_im
    wi = pr[:, :, :, None] * bb_im + pi[:, :, :, None] * bb_re
    c_re, c_im = c_re.astype(F32), c_im.astype(F32)
    g_n, c_n, p_n = a_re.shape[1], b_re.shape[-1], a_re.shape[-1]
    def lag_rows(w_re, w_im):
        return jnp.transpose(jnp.concatenate([w_re, -w_im], -1), (1, 0, 2, 3)).reshape(g_n, k * c_n, 2 * p_n)

    w_gen = jnp.concatenate(
        [jnp.pad(lag_rows(wr[k - 1::-1, 0], wi[k - 1::-1, 0]), ((0, 0), (0, k * c_n), (0, 0))),
         jnp.pad(lag_rows(wr[:k, 1], wi[:k, 1]), ((0, 0), ((k - 1) * c_n, c_n), (0, 0)))], axis=-1)
    c2 = jnp.concatenate([c_re[0], c_im[0], c_re[1], c_im[1]], -1)
    d_rep = jnp.broadcast_to(d_skip.astype(F32).reshape(g_n, c_n, 1), (g_n, c_n, LANES))
    toep = _s5_toeplitz(c2, w_gen, d_rep, k, c_n)
    endw = jnp.concatenate([wr[k - 1::-1, 0], wr[:k, 1], wi[k - 1::-1, 0], wi[:k, 1]], -1)
    endw = jnp.transpose(endw, (1, 0, 2, 3)).reshape(g_n, k * c_n, 4 * p_n)

    def out_cols(d, powers_re, powers_im):
        cr, ci = c_re[d][None], c_im[d][None]
        pre, pim = powers_re[:, :, None, :], powers_im[:, :, None, :]
        return cr * pre - ci * pim, -(cr * pim + ci * pre)

    of_re, of_im = out_cols(0, pr[1:, 0], pi[1:, 0])
    ob_re, ob_im = out_cols(1, pr[k:0:-1, 1], pi[k:0:-1, 1])
    outw_t = jnp.concatenate([of_re, ob_re, of_im, ob_im], -1)
    outw_t = jnp.transpose(outw_t, (1, 0, 2, 3)).reshape(g_n, k * c_n, 4 * p_n)
    a0 = jnp.concatenate([pr[k, 0], pr[k, 1]], -1)
    a1 = jnp.concatenate([pi[k, 0], pi[k, 1]], -1)
    return toep.astype(BF16), endw.astype(BF16), outw_t.astype(BF16), a0, a1


S5_LANE_GROUPS = LANES // S5_GROUP


def _chunk_transpose(xs):
    n_arr = len(xs)
    lane_chunk = lax.broadcasted_iota(jnp.int32, xs[0].shape, 1) // S5_GROUP
    d = 1
    while d < n_arr:
        keep = (lane_chunk & d) == 0
        out = list(xs)
        for i in range(n_arr):
            if i & d:
                continue
            lo, hi = xs[i], xs[i + d]
            out[i] = jnp.where(keep, lo, pltpu.roll(hi, d * S5_GROUP, 1))
            out[i + d] = jnp.where(keep, pltpu.roll(lo, LANES - d * S5_GROUP, 1), hi)
        xs = out
        d *= 2
    return xs


def _s5_kernel(uc_ref, ul_ref, toep_ref, endw_ref, outw_ref, a0_ref, a1_ref, yc_ref, yl_ref,
               uc_s, ul_s, e_s, yc_s, yl_s, *, batch, n_ctx, n_lat):
    p = S5_STATE
    gl = S5_LANE_GROUPS
    k = S5_STEP

    def stack(u_ref, us_ref, n_blk):
        for b in range(batch):
            for half in range(k // gl):
                xs = [u_ref[b, pl.ds(half * gl + s, n_blk, stride=k), :] for s in range(gl)]
                ys = _chunk_transpose(xs)
                for g in range(gl):
                    us_ref[g, b * n_blk:(b + 1) * n_blk, half * LANES:(half + 1) * LANES] = ys[g].astype(BF16)

    stack(uc_ref, uc_s, n_ctx)
    stack(ul_ref, ul_s, n_lat)

    def increments(us_ref, n_blk, first_blk):
        for g in range(gl):
            e = _dot(us_ref[g], endw_ref[g])
            for b in range(batch):
                for part in range(2):
                    e_s[part, b, pl.ds(first_blk * gl + g, n_blk, stride=gl), :] = (
                        e[b * n_blk:(b + 1) * n_blk, part * LANES:(part + 1) * LANES])

    increments(uc_s, n_ctx, 0)
    increments(ul_s, n_lat, n_ctx)

    a0 = a0_ref[...]
    a1 = a1_ref[...]
    fwd_lane = lax.broadcasted_iota(jnp.int32, (batch, gl, 2 * p), 2) < p

    def scan(first_blk, n, carry):
        def body(i, c):
            s0, s1 = c
            rf = pl.ds(pl.multiple_of((first_blk + i) * gl, gl), gl)
            rb = pl.ds(pl.multiple_of((first_blk + n - 1 - i) * gl, gl), gl)
            e0 = jnp.where(fwd_lane, e_s[0, :, rf, :], e_s[0, :, rb, :])
            e1 = jnp.where(fwd_lane, e_s[1, :, rf, :], e_s[1, :, rb, :])
            e_s[0, :, rf, 0:p] = s0[:, :, 0:p]
            e_s[0, :, rb, p:2 * p] = s0[:, :, p:2 * p]
            e_s[1, :, rf, 0:p] = s1[:, :, 0:p]
            e_s[1, :, rb, p:2 * p] = s1[:, :, p:2 * p]
            return a0 * s0 - a1 * s1 + e0, a0 * s1 + a1 * s0 + e1
        return lax.fori_loop(0, n, body, carry)

    zero = jnp.zeros((batch, gl, 2 * p), F32)
    carry = scan(0, n_ctx, (zero, zero))
    scan(n_ctx, n_lat, carry)

    def outputs(us_ref, ys_ref, y_ref, n_blk, first_blk):
        for g in range(gl):
            state = jnp.concatenate(
                [jnp.concatenate([e_s[part, b, pl.ds(first_blk * gl + g, n_blk, stride=gl), :]
                                  for b in range(batch)], axis=0) for part in range(2)], axis=1)
            ys_ref[g] = _dot_nt(us_ref[g], toep_ref[g]) + _dot_nt(state.astype(BF16), outw_ref[g])
        for b in range(batch):
            for half in range(k // gl):
                ys = [ys_ref[g, b * n_blk:(b + 1) * n_blk, half * LANES:(half + 1) * LANES] for g in range(gl)]
                xs = _chunk_transpose(ys)
                for s in range(gl):
                    y_ref[b, pl.ds(half * gl + s, n_blk, stride=k), :] = xs[s]

    outputs(uc_s, yc_s, yc_ref, n_ctx, 0)
    outputs(ul_s, yl_s, yl_ref, n_lat, n_ctx)


def _s5_mix(u_ctx, u_lat, mats):
    toep, endw, outw, a0, a1 = mats
    bn, lc, w = u_ctx.shape
    ll = u_lat.shape[1]
    gl = S5_LANE_GROUPS
    kc = S5_STEP * S5_GROUP
    n_ctx, n_lat = lc // S5_STEP, ll // S5_STEP
    bb = 4 if bn % 4 == 0 else bn
    seq = lambda n_rows: pl.BlockSpec((bb, n_rows, LANES), lambda i, j: (j, 0, i))
    grp = lambda r, c: pl.BlockSpec((gl, r, c), lambda i, j: (i, 0, 0))
    return pl.pallas_call(
        functools.partial(_s5_kernel, batch=bb, n_ctx=n_ctx, n_lat=n_lat),
        grid=(w // LANES, bn // bb),
        in_specs=[seq(lc), seq(ll), grp(kc, kc), grp(kc, kc), grp(kc, kc),
                  pl.BlockSpec((gl, 2 * S5_STATE), lambda i, j: (i, 0)),
                  pl.BlockSpec((gl, 2 * S5_STATE), lambda i, j: (i, 0))],
        out_specs=[seq(lc), seq(ll)],
        out_shape=[jax.ShapeDtypeStruct((bn, lc, w), F32), jax.ShapeDtypeStruct((bn, ll, w), F32)],
        scratch_shapes=[
            pltpu.VMEM((gl, bb * n_ctx, kc), BF16), pltpu.VMEM((gl, bb * n_lat, kc), BF16),
            pltpu.VMEM((2, bb, (n_ctx + n_lat) * gl, LANES), F32),
            pltpu.VMEM((gl, bb * n_ctx, kc), F32), pltpu.VMEM((gl, bb * n_lat, kc), F32),
        ],
        compiler_params=_cparams(("arbitrary", "arbitrary")),
        name="s5_scan",
    )(u_ctx, u_lat, toep, endw, outw, a0, a1)


def _mlstm_kernel(pc_ref, pl_ref, gc_ref, gl_ref, cw_ref, cb_ref, wq_ref, wkt_ref, wv_ref, gb_ref, mc_ref, ml_ref,
                  q_s, kt_s, vx_s, h_s, st_s, bcol_s, mcol_s, rrow_s, xpad_s, *, n_ctx_rows, n_lat_rows):
    t_n = ML_CHUNK
    width = ML_HEADS * LANES
    nc_ctx, nc_lat = n_ctx_rows // t_n, n_lat_rows // t_n
    nc = nc_ctx + nc_lat
    scale_k = LANES ** -0.5

    def project(p_ref, n_rows, base):
        ones = jnp.ones((n_rows, LANES), BF16)
        pad = SUBLANES
        xpad_s[0:pad, :] = jnp.zeros((pad, LANES), F32)
        xpad_s[pad + n_rows:2 * pad + n_rows, :] = jnp.zeros((pad, LANES), F32)
        for h in range(ML_HEADS):
            cols = slice(h * LANES, (h + 1) * LANES)
            xm_lo = p_ref[0, :, cols]
            xm = xm_lo.astype(F32)
            xpad_s[pad:pad + n_rows, :] = xm
            taps = cw_ref[:, cols]
            n_tap = taps.shape[0]
            assert n_tap // 2 <= pad
            acc = jnp.zeros_like(xm) + cb_ref[:, cols]
            for j in range(n_tap):
                d = j - n_tap // 2
                sh = xm if d == 0 else xpad_s[pad + d:pad + d + n_rows, :]
                acc = acc + sh * taps[j:j + 1, :]
            xc = _silu(acc).astype(BF16)
            q_s[base:base + n_rows, cols] = _dot(xc, wq_ref[h]).astype(BF16)
            kt = (_dot_nt(wkt_ref[h], xc) * scale_k).astype(BF16)
            for c in range(n_rows // t_n):
                kt_s[base // t_n + c, cols, :] = kt[:, c * t_n:(c + 1) * t_n]
            vx_s[base:base + n_rows, 2 * h * LANES:(2 * h + 1) * LANES] = (
                _dot(xm_lo, wv_ref[h]).astype(BF16))
            vx_s[base:base + n_rows, (2 * h + 1) * LANES:(2 * h + 2) * LANES] = ones

    project(pc_ref, n_ctx_rows, 0)
    project(pl_ref, n_lat_rows, n_ctx_rows)

    ti = lax.broadcasted_iota(jnp.int32, (t_n, t_n), 0)
    si = lax.broadcasted_iota(jnp.int32, (t_n, t_n), 1)
    tri_f = (si <= ti).astype(BF16)
    tri_b = (si >= ti).astype(BF16)
    lane = lax.broadcasted_iota(jnp.int32, (t_n, LANES), 1)
    trow = lax.broadcasted_iota(jnp.int32, (t_n, LANES), 0)

    def gate_prep(g_ref, n_chunks, base_chunk):
        def body(c, _):
            rows = pl.ds(pl.multiple_of(c * t_n, t_n), t_n)
            gcol = g_ref[0, rows, :] + gb_ref[...]
            lf = jax.nn.log_sigmoid(gcol)
            hi = lf.astype(BF16)
            lo = (lf - hi.astype(F32)).astype(BF16)
            pre = _dot(tri_f, hi) + _dot(tri_f, lo)
            suf = _dot(tri_b, hi) + _dot(tri_b, lo)
            bsum = jnp.where(lane < 2 * ML_HEADS, pre, suf)
            rcol = gcol - pltpu.roll(bsum, LANES - ML_HEADS, 1)
            pmax, smax = rcol, rcol
            step = 1
            while step < t_n:
                pmax = jnp.maximum(pmax, jnp.where(trow >= step, pltpu.roll(pmax, step, 0), NEG_BIG))
                smax = jnp.maximum(smax, jnp.where(trow < t_n - step, pltpu.roll(smax, t_n - step, 0), NEG_BIG))
                step *= 2
            bcol_s[base_chunk + c] = bsum
            mcol_s[base_chunk + c] = jnp.where(lane < 2 * ML_HEADS, pmax, smax)
            rrow_s[base_chunk + c] = rcol.T[0:4 * ML_HEADS, :]
            return 0
        lax.fori_loop(0, n_chunks, body, 0, unroll=2 if n_chunks % 2 == 0 else 1)

    gate_prep(gc_ref, nc_ctx, 0)
    gate_prep(gl_ref, nc_lat, nc_ctx)

    def one_dir(c, h, d, m_in):
        cols = slice(h * LANES, (h + 1) * LANES)
        xcols = slice(2 * h * LANES, (2 * h + 2) * LANES)
        li = 2 * d * ML_HEADS + h
        mask = (si <= ti) if d == 0 else (si >= ti)
        last = t_n - 1 if d == 0 else 0
        rows = pl.ds(pl.multiple_of(c * t_n, t_n), t_n)
        q = q_s[rows, cols]
        kt = kt_s[c, cols, :]
        vx = vx_s[rows, xcols]
        r_row = rrow_s[c][li:li + 1, :]
        run_max = jnp.broadcast_to(mcol_s[c][:, li:li + 1], (t_n, LANES))
        b_rep = jnp.broadcast_to(bcol_s[c][:, li + ML_HEADS:li + ML_HEADS + 1], (t_n, LANES))
        mm = jnp.maximum(m_in, run_max)
        dmat = jnp.exp(jnp.where(mask, r_row - mm, NEG_BIG))
        sm = (dmat * _dot(q, kt)).astype(BF16)
        intra = _dot(sm, vx)
        cross = _dot(q, st_s[2 * h + d].astype(BF16))
        inter = jnp.exp(m_in - mm)
        num = intra[:, :LANES] + inter * cross[:, :LANES]
        den = intra[:, LANES:] + inter * cross[:, LANES:]
        h_s[rows, cols] = h_s[rows, cols] + num / jnp.maximum(jnp.abs(den), jnp.exp(-b_rep - mm))
        r_top = run_max[last:last + 1, :]
        b_last = b_rep[last:last + 1, :]
        g_max = b_last + r_top
        m_new = jnp.maximum(b_last + m_in, g_max)
        dec = jnp.exp(b_last + m_in - m_new)
        inj = jnp.exp(g_max - m_new)
        kw = (kt.astype(F32) * (jnp.exp(r_row - r_top) * inj)).astype(BF16)
        inc = _dot(kw, vx)
        st_s[2 * h + d] = jnp.concatenate([dec, dec], axis=1) * st_s[2 * h + d] + inc
        return m_new

    st_s[...] = jnp.zeros_like(st_s)
    h_s[...] = jnp.zeros_like(h_s)

    def body(i, carry):
        cb = jnp.where(i < nc_ctx, nc_ctx - 1 - i, nc + nc_ctx - 1 - i)
        return tuple(one_dir(i if d == 0 else cb, h, d, carry[2 * h + d])
                     for h in range(ML_HEADS) for d in range(2))

    zero = jnp.zeros((1, LANES), F32)
    lax.fori_loop(0, nc, body, (zero,) * (2 * ML_HEADS))

    mc_ref[0] = h_s[0:n_ctx_rows, :].astype(BF16)
    ml_ref[0] = h_s[n_ctx_rows:n_ctx_rows + n_lat_rows, :].astype(BF16)


def _mlstm_mix(p_ctx, p_lat, g_ctx, g_lat, conv_w, conv_b, wq, wk, wv, gate_b):
    bn, lc, pw = p_ctx.shape
    ll = p_lat.shape[1]
    width = ML_HEADS * LANES
    lt = lc + ll
    nct = lt // ML_CHUNK
    gb = jnp.zeros((1, LANES), F32).at[0, :4 * ML_HEADS].set(gate_b.astype(F32))
    full2 = lambda a: pl.BlockSpec(a.shape, lambda b: (0,) * a.ndim)
    conv_b2 = conv_b.reshape(1, width)
    assert ML_CHUNK == LANES
    wq, wkt, wv = wq.astype(BF16), jnp.swapaxes(wk, 1, 2).astype(BF16), wv.astype(BF16)
    return pl.pallas_call(
        functools.partial(_mlstm_kernel, n_ctx_rows=lc, n_lat_rows=ll),
        grid=(bn,),
        in_specs=[
            pl.BlockSpec((1, lc, pw), lambda b: (b, 0, 0)),
            pl.BlockSpec((1, ll, pw), lambda b: (b, 0, 0)),
            pl.BlockSpec((1, lc, LANES), lambda b: (b, 0, 0)),
            pl.BlockSpec((1, ll, LANES), lambda b: (b, 0, 0)),
            full2(conv_w), full2(conv_b2), full2(wq), full2(wkt), full2(wv), full2(gb),
        ],
        out_specs=[
            pl.BlockSpec((1, lc, width), lambda b: (b, 0, 0)),
            pl.BlockSpec((1, ll, width), lambda b: (b, 0, 0)),
        ],
        out_shape=[jax.ShapeDtypeStruct((bn, lc, width), BF16), jax.ShapeDtypeStruct((bn, ll, width), BF16)],
        scratch_shapes=[
            pltpu.VMEM((lt, width), BF16),
            pltpu.VMEM((nct, width, ML_CHUNK), BF16),
            pltpu.VMEM((lt, 2 * width), BF16),
            pltpu.VMEM((lt, width), F32),
            pltpu.VMEM((2 * ML_HEADS, LANES, 2 * LANES), F32),
            pltpu.VMEM((nct, ML_CHUNK, LANES), F32), pltpu.VMEM((nct, ML_CHUNK, LANES), F32),
            pltpu.VMEM((nct, 4 * ML_HEADS, ML_CHUNK), F32),
            pltpu.VMEM((max(lc, ll) + 2 * SUBLANES, LANES), F32),
        ],
        compiler_params=_cparams(("arbitrary",)),
        name="mlstm_mix",
    )(p_ctx, p_lat, g_ctx, g_lat, conv_w, conv_b2, wq, wkt, wv, gb)


def _ffn_tail(x, y, gate_mix, g2, shift, scale, gate_ffn, w1_ref, w3_ref, w2_ref, hidden_chunk):
    x1 = x + gate_mix * y
    h2 = _norm_mod(x1, g2, shift, scale).astype(BF16)
    hidden = w1_ref.shape[1]
    acc = jnp.zeros_like(x1)
    start = 0
    while start < hidden:
        cs = slice(start, min(start + hidden_chunk, hidden))
        z = (_silu(_dot(h2, w1_ref[:, cs])) * _dot(h2, w3_ref[:, cs])).astype(BF16)
        acc = acc + _dot(z, w2_ref[cs, :])
        start += hidden_chunk
    return x1 + gate_ffn * acc


def _final_norm(x, g):
    return x * lax.rsqrt(jnp.mean(x * x, axis=-1, keepdims=True) + EPS) * g


def _outproj_even_kernel(x_ref, ys_ref, m_ref, og_ref, gm_ref, sh_ref, sc_ref, gf_ref, g2_ref, gw_ref, gbias_ref,
                         ng_ref, wo_ref, w1_ref, w3_ref, w2_ref, o_ref, *, hidden_chunk, heads):
    ys = jax.nn.gelu(ys_ref[0])
    s = ys * jax.nn.sigmoid(_dot(ys.astype(BF16), gw_ref[...]) + gbias_ref[...])
    dh = m_ref.shape[-1] // heads
    gated = []
    for h in range(heads):
        cols = slice(h * dh, (h + 1) * dh)
        hh = m_ref[0, :, cols].astype(F32)
        mu = jnp.mean(hh, axis=1, keepdims=True)
        cen = hh - mu
        var = jnp.mean(cen * cen, axis=1, keepdims=True)
        normed = cen * lax.rsqrt(var + EPS) * ng_ref[:, cols]
        gated.append((jax.nn.sigmoid(og_ref[0, :, cols].astype(F32)) * normed).astype(BF16))
    sw = ys_ref.shape[-1]
    y = _dot(s.astype(BF16), wo_ref[0:sw, :]) + _dot(jnp.concatenate(gated, axis=1), wo_ref[sw:, :])
    o_ref[0] = _ffn_tail(x_ref[0], y, gm_ref[0], g2_ref[...], sh_ref[0], sc_ref[0], gf_ref[0],
                         w1_ref, w3_ref, w2_ref, hidden_chunk)


def _outproj_odd_kernel(x_ref, a_ref, gate_ref, gm_ref, sh_ref, sc_ref, gf_ref, g2_ref, ng_ref, wo_ref, w1_ref,
                        w3_ref, w2_ref, fg_ref, o_ref, *, hidden_chunk, heads):
    hv = a_ref.shape[-1] // heads
    y = jnp.zeros(x_ref.shape[1:], F32)
    for h in range(heads):
        cols = slice(h * hv, (h + 1) * hv)
        o = a_ref[0, :, cols].astype(F32)
        mu = jnp.mean(o, axis=1, keepdims=True)
        cen = o - mu
        var = jnp.mean(cen * cen, axis=1, keepdims=True)
        normed = cen * lax.rsqrt(var + EPS) * ng_ref[:, cols]
        gated = (_silu(gate_ref[0, :, cols].astype(F32)) * normed).astype(BF16)
        y = y + _dot(gated, wo_ref[cols, :])
    out = _ffn_tail(x_ref[0], y, gm_ref[0], g2_ref[...], sh_ref[0], sc_ref[0], gf_ref[0],
                    w1_ref, w3_ref, w2_ref, hidden_chunk)
    o_ref[0] = _final_norm(out, fg_ref[...])


MXU_WIDTH = 256


def _hidden_chunk(hidden):
    return -(-hidden // (2 * MXU_WIDTH)) * MXU_WIDTH


def _mod_specs(d, row_of_batch, slots):
    return [pl.BlockSpec((1, 1, d), lambda b, i, s=s: (row_of_batch(b) * N_MOD + s, 0, 0)) for s in slots]


def _layer_resident(w, layer):
    return _resident((None,) + w.shape[1:], lambda b, i: (layer,) + (0,) * (w.ndim - 1))


def _outproj_even(x, ys, m, p, mods, row_of_batch, g2, glu_w, glu_b, norm_g, w_out, ffn, layer):
    bn, ln, d = x.shape
    sw, mw = ys.shape[-1], m.shape[-1]
    tm = _token_tile(ln)
    res = lambda a: _resident(a.shape, lambda b, i: (0,) * a.ndim)
    tok = lambda w: pl.BlockSpec((1, tm, w), lambda b, i: (b, i, 0))
    row = lambda w: pl.BlockSpec((1, w), lambda b, i: (0, 0))
    return pl.pallas_call(
        functools.partial(_outproj_even_kernel, hidden_chunk=_hidden_chunk(ffn[0].shape[-1]), heads=ML_HEADS),
        grid=(bn, ln // tm),
        in_specs=[tok(d), tok(sw), tok(mw), pl.BlockSpec((1, tm, mw), lambda b, i: (b, i, 1))]
        + _mod_specs(d, row_of_batch, (2, 3, 4, 5))
        + [row(d), res(glu_w), row(sw), row(mw), res(w_out)] + [_layer_resident(w, layer) for w in ffn],
        out_specs=tok(d),
        out_shape=jax.ShapeDtypeStruct((bn, ln, d), F32),
        compiler_params=_cparams(("arbitrary", "arbitrary")),
        name="outproj_ffn_even",
    )(x, ys, m, p, mods, mods, mods, mods, g2, glu_w, glu_b.reshape(1, sw), norm_g.reshape(1, mw), w_out, *ffn)


def _outproj_odd(x, a, gate, mods, row_of_batch, g2, norm_g, w_out, ffn, layer, final_g):
    bn, ln, d = x.shape
    vw = a.shape[-1]
    tm = _token_tile(ln)
    res = lambda a_: _resident(a_.shape, lambda b, i: (0,) * a_.ndim)
    tok = lambda w: pl.BlockSpec((1, tm, w), lambda b, i: (b, i, 0))
    row = lambda w: pl.BlockSpec((1, w), lambda b, i: (0, 0))
    return pl.pallas_call(
        functools.partial(_outproj_odd_kernel, hidden_chunk=_hidden_chunk(ffn[0].shape[-1]), heads=RET_HEADS),
        grid=(bn, ln // tm),
        in_specs=[tok(d), tok(vw), tok(vw)] + _mod_specs(d, row_of_batch, (2, 3, 4, 5))
        + [row(d), row(vw), res(w_out)] + [_layer_resident(w, layer) for w in ffn] + [row(d)],
        out_specs=tok(d),
        out_shape=jax.ShapeDtypeStruct((bn, ln, d), F32),
        compiler_params=_cparams(("arbitrary", "arbitrary")),
        name="outproj_ffn_odd",
    )(x, a, gate, mods, mods, mods, mods, g2, norm_g.reshape(1, vw), w_out, *ffn, final_g)


def _rope_heads(t, cos, sin, heads, hk):
    half = hk // 2
    out = []
    for h in range(heads):
        t1 = t[:, h * hk:h * hk + half]
        t2 = t[:, h * hk + half:(h + 1) * hk]
        out.append(t1 * cos - t2 * sin)
        out.append(t2 * cos + t1 * sin)
    return jnp.concatenate(out, axis=-1)


def _inproj_odd_kernel(x_ref, sh_ref, sc_ref, g_ref, cos_ref, sin_ref, w_ref, q_ref, k_ref, v_ref, gate_ref,
                       *, qk, vw, heads):
    h = _norm_mod(x_ref[0], g_ref[...], sh_ref[0], sc_ref[0]).astype(BF16)
    hk = qk // heads
    cos, sin = cos_ref[...], sin_ref[...]
    q = _rope_heads(_dot(h, w_ref[:, 0:qk]), cos, sin, heads, hk)
    q_ref[0] = (q * (hk ** -0.5)).astype(BF16)
    k = _rope_heads(_dot(h, w_ref[:, qk:2 * qk]), cos, sin, heads, hk)
    k_ref[0] = k.astype(BF16)
    v_ref[0] = _dot(h, w_ref[:, 2 * qk:2 * qk + vw]).astype(BF16)
    gate_ref[0] = _dot(h, w_ref[:, 2 * qk + vw:2 * qk + 2 * vw]).astype(BF16)


def _inproj_odd_ctx_kernel(x_ref, sh_ref, sc_ref, g_ref, w_ref, k_ref, v_ref, *, qk, vw):
    h = _norm_mod(x_ref[0], g_ref[...], sh_ref[0], sc_ref[0]).astype(BF16)
    k_ref[0] = _dot(h, w_ref[:, qk:2 * qk]).astype(BF16)
    v_ref[0] = _dot(h, w_ref[:, 2 * qk:2 * qk + vw]).astype(BF16)


def _inproj_odd(x, mods, row_of_batch, g, cos, sin, w, qk, vw):
    bn, ln, d = x.shape
    tm = _token_tile(ln)
    tok = lambda w_: pl.BlockSpec((1, tm, w_), lambda b, i: (b, i, 0))
    half = cos.shape[1]
    return pl.pallas_call(
        functools.partial(_inproj_odd_kernel, qk=qk, vw=vw, heads=RET_HEADS),
        grid=(bn, ln // tm),
        in_specs=[tok(d)] + _mod_specs(d, row_of_batch, (0, 1))
        + [pl.BlockSpec((1, d), lambda b, i: (0, 0)),
           pl.BlockSpec((tm, half), lambda b, i: (i, 0)), pl.BlockSpec((tm, half), lambda b, i: (i, 0)),
           _resident(w.shape, lambda b, i: (0, 0))],
        out_specs=[tok(qk), tok(qk), tok(vw), tok(vw)],
        out_shape=[jax.ShapeDtypeStruct((bn, ln, qk), BF16), jax.ShapeDtypeStruct((bn, ln, qk), BF16),
                   jax.ShapeDtypeStruct((bn, ln, vw), BF16), jax.ShapeDtypeStruct((bn, ln, vw), BF16)],
        compiler_params=_cparams(("arbitrary", "arbitrary")),
        name="inproj_odd",
    )(x, mods, mods, g, cos, sin, w)


def _inproj_odd_ctx(x, mods, row_of_batch, g, w_kv, qk, vw):
    bn, ln, d = x.shape
    tm = _token_tile(ln)
    tok = lambda w_: pl.BlockSpec((1, tm, w_), lambda b, i: (b, i, 0))
    return pl.pallas_call(
        functools.partial(_inproj_odd_ctx_kernel, qk=qk, vw=vw),
        grid=(bn, ln // tm),
        in_specs=[tok(d)] + _mod_specs(d, row_of_batch, (0, 1))
        + [pl.BlockSpec((1, d), lambda b, i: (0, 0)), _resident(w_kv.shape, lambda b, i: (0, 0))],
        out_specs=[tok(qk), tok(vw)],
        out_shape=[jax.ShapeDtypeStruct((bn, ln, qk), BF16), jax.ShapeDtypeStruct((bn, ln, vw), BF16)],
        compiler_params=_cparams(("arbitrary", "arbitrary")),
        name="inproj_odd_ctx",
    )(x, mods, mods, g, w_kv)


def _retention_kernel(lg_ref, q_ref, k_ref, v_ref, kc_ref, vc_ref, o_ref, acc_s, sf_s, sb_s,
                      *, chunk, n_ctx_rows, n_lat_rows):
    t_n = chunk
    h = pl.program_id(1)
    lgf = jnp.full((1, 1), lg_ref[0, h], F32)
    lgb = jnp.full((1, 1), lg_ref[1, h], F32)
    nc_ctx, nc_lat = n_ctx_rows // t_n, n_lat_rows // t_n
    ti = lax.broadcasted_iota(jnp.int32, (t_n, t_n), 0)
    si = lax.broadcasted_iota(jnp.int32, (t_n, t_n), 1)
    diff = (ti - si).astype(F32)
    decay = jnp.where(diff >= 0, jnp.exp(lgf * jnp.maximum(diff, 0.0)), jnp.exp(lgb * jnp.maximum(-diff, 0.0)))
    pos = lax.broadcasted_iota(jnp.int32, (t_n, 1), 0).astype(F32)
    inter_f = jnp.exp(lgf * (pos + 1.0))
    inter_b = jnp.exp(lgb * (t_n - pos))
    wend_f = jnp.exp(lgf * (t_n - 1.0 - pos))
    wend_b = jnp.exp(lgb * pos)
    cd_f = jnp.exp(lgf * t_n)
    cd_b = jnp.exp(lgb * t_n)

    def rows_of(c):
        return pl.ds(pl.multiple_of(c * t_n, t_n), t_n)

    def bump(s_ref, k, v, wend, cd):
        kw = (k.astype(F32) * wend).astype(BF16)
        s_ref[...] = cd * s_ref[...] + _dot_tn(kw, v)

    sf_s[...] = jnp.zeros_like(sf_s)
    sb_s[...] = jnp.zeros_like(sb_s)

    def ctx_f(c, _):
        bump(sf_s, kc_ref[0, rows_of(c), :], vc_ref[0, rows_of(c), :], wend_f, cd_f)
        return 0
    lax.fori_loop(0, nc_ctx, ctx_f, 0)

    def ctx_b(i, _):
        c = nc_ctx - 1 - i
        bump(sb_s, kc_ref[0, rows_of(c), :], vc_ref[0, rows_of(c), :], wend_b, cd_b)
        return 0
    lax.fori_loop(0, nc_ctx, ctx_b, 0)

    def forward_part(rows):
        q, k, v = q_ref[0, rows, :], k_ref[0, rows, :], v_ref[0, rows, :]
        scores = (_dot_nt(q, k) * decay).astype(BF16)
        part = _dot(scores, v) + inter_f * _dot(q, sf_s[...].astype(BF16))
        bump(sf_s, k, v, wend_f, cd_f)
        return part

    def backward_part(rows):
        q, k, v = q_ref[0, rows, :], k_ref[0, rows, :], v_ref[0, rows, :]
        part = inter_b * _dot(q, sb_s[...].astype(BF16))
        bump(sb_s, k, v, wend_b, cd_b)
        return part

    def first_half(i, _):
        rf, rb = rows_of(i), rows_of(nc_lat - 1 - i)
        acc_s[rf, :] = forward_part(rf)
        acc_s[rb, :] = backward_part(rb)
        return 0
    pair_unroll = 4 if (nc_lat // 2) % 4 == 0 else 1
    lax.fori_loop(0, nc_lat // 2, first_half, 0, unroll=pair_unroll)

    def second_half(i, _):
        rf, rb = rows_of(i), rows_of(nc_lat - 1 - i)
        o_ref[0, rf, :] = (acc_s[rf, :] + forward_part(rf)).astype(BF16)
        o_ref[0, rb, :] = (acc_s[rb, :] + backward_part(rb)).astype(BF16)
        return 0
    lax.fori_loop(nc_lat // 2, nc_lat, second_half, 0, unroll=pair_unroll)


def _retention_mix(q, k, v, k_ctx, v_ctx, log_gamma):
    bn, ll, qk = q.shape
    lc = k_ctx.shape[1]
    vw = v.shape[-1]
    hk, hv = qk // RET_HEADS, vw // RET_HEADS
    chunk = 256 if (ll % 512 == 0 and lc % 256 == 0) else 128
    assert (ll // chunk) % 2 == 0, "latent chunks are visited in forward/backward pairs"
    head = lambda n_rows, w: pl.BlockSpec((1, n_rows, w), lambda b, h: (b, 0, h))
    return pl.pallas_call(
        functools.partial(_retention_kernel, chunk=chunk, n_ctx_rows=lc, n_lat_rows=ll),
        grid=(bn, RET_HEADS),
        in_specs=[
            pl.BlockSpec(memory_space=pltpu.SMEM),
            head(ll, hk), head(ll, hk), head(ll, hv), head(lc, hk), head(lc, hv),
        ],
        out_specs=head(ll, hv),
        out_shape=jax.ShapeDtypeStruct((bn, ll, vw), BF16),
        scratch_shapes=[pltpu.VMEM((ll, hv), F32), pltpu.VMEM((hk, hv), F32), pltpu.VMEM((hk, hv), F32)],
        compiler_params=_cparams(("arbitrary", "arbitrary")),
        name="retention_mix",
    )(log_gamma.astype(F32), q, k, v, k_ctx, v_ctx)


def _grid_rope(n_pos, hk):
    rows = n_pos // GRID_W
    row = jnp.repeat(jnp.arange(rows, dtype=F32), GRID_W)
    col = jnp.tile(jnp.arange(GRID_W, dtype=F32), rows)
    n_freq = hk // 4
    inv = ROPE_BASE ** (-jnp.arange(n_freq, dtype=F32) / n_freq)
    ang = jnp.concatenate([row[:, None] * inv, col[:, None] * inv], -1)
    return jnp.cos(ang), jnp.sin(ang)


def kernel(x, c, ctx, c_ctx, ada_w, ada_b, norm1_g, norm2_g, ab_w_in, ab_w_out, s5_a_re, s5_a_im, s5_log_dt,
           s5_b_re, s5_b_im, s5_c_re, s5_c_im, s5_d, s5_glu_w, s5_glu_b, ml_conv_w, ml_conv_b, ml_wq, ml_wk,
           ml_wv, ml_gate_b, ml_norm_g, ret_w_in, ret_w_out, ret_log_gamma, ret_norm_g, ffn_w1, ffn_w3, ffn_w2,
           final_g):
    bn, ln, d = x.shape
    depth = ada_w.shape[0]
    assert depth == 2, "one S5 || mLSTM layer followed by one retention layer"
    s5_width = s5_d.shape[-1]
    ml_width = ml_norm_g.shape[-1]
    assert ml_width == ML_HEADS * LANES and s5_width % S5_GROUP == 0

    r_pad = -(-(bn + 1) // SUBLANES) * SUBLANES
    vec = jnp.zeros((r_pad, d), F32).at[:bn].set(c).at[bn].set(c_ctx)
    mods_all = _modulation(vec, ada_w, ada_b).reshape(depth, r_pad * N_MOD, 1, d)
    lat_row = lambda b: b
    ctx_row = lambda b: bn

    mods = mods_all[0]
    g1 = norm1_g[0].reshape(1, d)
    g2 = norm2_g[0].reshape(1, d)
    w_in = ab_w_in[0]
    n_gate = w_in.shape[1] - s5_width - 2 * ml_width
    w_in = jnp.pad(w_in, ((0, 0), (0, LANES - n_gate))).astype(BF16)
    u_lat, p_lat, g_lat = _inproj_even(x, mods, lat_row, g1, w_in, s5_width)
    lc = ctx.shape[1]
    flat = lambda t: t.reshape(1, bn * lc, t.shape[-1])
    unflat = lambda t: t.reshape(bn, lc, t.shape[-1])
    u_ctx, p_ctx, g_ctx = (unflat(t) for t in _inproj_even(flat(ctx), mods, ctx_row, g1, w_in, s5_width))
    mats = _s5_matrices(s5_a_re[0], s5_a_im[0], s5_log_dt[0], s5_b_re[0], s5_b_im[0], s5_c_re[0], s5_c_im[0],
                        s5_d[0])
    ys_ctx, ys_lat = _s5_mix(u_ctx, u_lat, mats)
    m_ctx, m_lat = _mlstm_mix(p_ctx, p_lat, g_ctx, g_lat, ml_conv_w[0], ml_conv_b[0], ml_wq[0], ml_wk[0],
                              ml_wv[0], ml_gate_b[0])
    glu_w = s5_glu_w[0].astype(BF16)
    w_out = ab_w_out[0].astype(BF16)
    ffn = (ffn_w1.astype(BF16), ffn_w3.astype(BF16), ffn_w2.astype(BF16))
    x = _outproj_even(x, ys_lat, m_lat, p_lat, mods, lat_row, g2, glu_w, s5_glu_b[0], ml_norm_g[0],
                      w_out, ffn, 0)
    ctx = _outproj_even(flat(ctx), flat(ys_ctx), flat(m_ctx), flat(p_ctx), mods, ctx_row, g2, glu_w, s5_glu_b[0],
                        ml_norm_g[0], w_out, ffn, 0)

    mods = mods_all[1]
    g1 = norm1_g[1].reshape(1, d)
    g2 = norm2_g[1].reshape(1, d)
    vw = ret_norm_g.shape[-1]
    qk = (ret_w_in.shape[-1] - 2 * vw) // 2
    w_in = ret_w_in[0].astype(BF16)
    cos, sin = _grid_rope(ln, qk // RET_HEADS)
    q, k, v, gate = _inproj_odd(x, mods, lat_row, g1, cos, sin, w_in, qk, vw)
    k_ctx, v_ctx = (unflat(t) for t in _inproj_odd_ctx(ctx, mods, ctx_row, g1, w_in, qk, vw))
    a = _retention_mix(q, k, v, k_ctx, v_ctx, ret_log_gamma[0])
    w_out = ret_w_out[0].astype(BF16)
    return _outproj_odd(x, a, gate, mods, lat_row, g2, ret_norm_g[0], w_out, ffn, 1, final_g.reshape(1, d))
```

```python
import functools
import math

import jax
import jax.numpy as jnp
from jax import lax
from jax.experimental import pallas as pl
from jax.experimental.pallas import tpu as pltpu

F32 = jnp.float32
BF16 = jnp.bfloat16

EPS = 1e-6
N_MOD = 6
GRID_W = 64
ROPE_BASE = 10000.0

S5_GROUP = 16
S5_STATE = 64
S5_STEP = 16
ML_HEADS = 4
ML_CHUNK = 128
RET_HEADS = 4
SUBLANES = 8
LANES = 128
NEG_BIG = -1e30
VMEM_LIMIT = 56 * 1024 * 1024


def _cparams(sem):
    return pltpu.CompilerParams(dimension_semantics=sem, vmem_limit_bytes=VMEM_LIMIT)


def _resident(shape, index_map):
    return pl.BlockSpec(shape, index_map, pipeline_mode=pl.Buffered(1))


def _token_tile(n):
    for t in (512, 256, 128):
        if n % t == 0:
            return t
    raise ValueError(f"sequence length {n} must be a multiple of 128")


def _silu(v):
    return v * jax.nn.sigmoid(v)


def _norm_mod(x, g, shift, scale):
    y = x * lax.rsqrt(jnp.mean(x * x, axis=-1, keepdims=True) + EPS)
    return (y * g) * (1.0 + scale) + shift


def _dot(a, b):
    return jnp.dot(a, b, preferred_element_type=F32)


def _dot_nt(a, b):
    return lax.dot_general(a, b, (((1,), (1,)), ((), ())), preferred_element_type=F32)


def _dot_tn(a, b):
    return lax.dot_general(a, b, (((0,), (0,)), ((), ())), preferred_element_type=F32)


def _mod_kernel(v_ref, w_ref, b_ref, o_ref):
    s = _silu(v_ref[...]).astype(BF16)
    o_ref[0] = _dot(s, w_ref[0].astype(BF16)) + b_ref[0]


def _modulation(vec, ada_w, ada_b):
    depth, d, n = ada_w.shape
    r = vec.shape[0]
    tn = 1024
    return pl.pallas_call(
        _mod_kernel,
        grid=(depth, n // tn),
        in_specs=[
            pl.BlockSpec((r, d), lambda l, j: (0, 0)),
            pl.BlockSpec((1, d, tn), lambda l, j: (l, 0, j)),
            pl.BlockSpec((1, 1, tn), lambda l, j: (l, 0, j)),
        ],
        out_specs=pl.BlockSpec((1, r, tn), lambda l, j: (l, 0, j)),
        out_shape=jax.ShapeDtypeStruct((depth, r, n), F32),
        compiler_params=_cparams(("arbitrary", "arbitrary")),
        name="adaln_modulation",
    )(vec, ada_w, ada_b.reshape(depth, 1, n))


def _inproj_even_kernel(x_ref, sh_ref, sc_ref, g_ref, w_ref, u_ref, p_ref, gate_ref, *, s5_width):
    h = _norm_mod(x_ref[0], g_ref[...], sh_ref[0], sc_ref[0]).astype(BF16)
    p = _dot(h, w_ref[...])
    n_mid = p_ref.shape[-1]
    u_ref[0] = p[:, :s5_width]
    p_ref[0] = p[:, s5_width:s5_width + n_mid].astype(BF16)
    gate_ref[0] = p[:, s5_width + n_mid:]


def _inproj_even(x, mods, row_of_batch, g, w, s5_width):
    bn, ln, d = x.shape
    n = w.shape[1]
    n_mid = n - s5_width - LANES
    tm = _token_tile(ln)
    return pl.pallas_call(
        functools.partial(_inproj_even_kernel, s5_width=s5_width),
        grid=(bn, ln // tm),
        in_specs=[
            pl.BlockSpec((1, tm, d), lambda b, i: (b, i, 0)),
            pl.BlockSpec((1, 1, d), lambda b, i: (row_of_batch(b) * N_MOD + 0, 0, 0)),
            pl.BlockSpec((1, 1, d), lambda b, i: (row_of_batch(b) * N_MOD + 1, 0, 0)),
            pl.BlockSpec((1, d), lambda b, i: (0, 0)),
            _resident((d, n), lambda b, i: (0, 0)),
        ],
        out_specs=[
            pl.BlockSpec((1, tm, s5_width), lambda b, i: (b, i, 0)),
            pl.BlockSpec((1, tm, n_mid), lambda b, i: (b, i, 0)),
            pl.BlockSpec((1, tm, LANES), lambda b, i: (b, i, 0)),
        ],
        out_shape=[
            jax.ShapeDtypeStruct((bn, ln, s5_width), F32),
            jax.ShapeDtypeStruct((bn, ln, n_mid), BF16),
            jax.ShapeDtypeStruct((bn, ln, LANES), F32),
        ],
        compiler_params=_cparams(("arbitrary", "arbitrary")),
        name="inproj_even",
    )(x, mods, mods, g, w)


def _s5_toeplitz_kernel(c_ref, w_ref, d_ref, o_ref, *, steps, chans):
    width = steps * chans
    lag0 = (steps - 1) * chans
    slab = lag0 // LANES
    row = lax.broadcasted_iota(jnp.int32, (chans, LANES), 0)
    lane = lax.broadcasted_iota(jnp.int32, (chans, LANES), 1)
    diag = lane == row + lag0 % LANES
    for n in range(c_ref.shape[0]):
        gen = lax.dot_general(c_ref[n], w_ref[n], (((1,), (1,)), ((), ())),
                              precision=lax.Precision.HIGHEST, preferred_element_type=F32)
        pieces = [gen[:, j * LANES:(j + 1) * LANES] for j in range(2 * width // LANES)]
        pieces[slab] = pieces[slab] + jnp.where(diag, d_ref[n], 0.0)
        gen = jnp.concatenate(pieces, axis=1)
        for t in range(steps):
            off = (steps - 1 - t) * chans
            o_ref[n, t * chans:(t + 1) * chans, :] = gen[:, off:off + width].astype(BF16)


def _s5_toeplitz(c2, w_gen, d_rep, steps, chans):
    n, co, kk = c2.shape
    r = w_gen.shape[1]
    nb = 8
    return pl.pallas_call(
        functools.partial(_s5_toeplitz_kernel, steps=steps, chans=chans),
        grid=(n // nb,),
        in_specs=[pl.BlockSpec((nb, co, kk), lambda i: (i, 0, 0)), pl.BlockSpec((nb, r, kk), lambda i: (i, 0, 0)),
                  pl.BlockSpec((nb, co, LANES), lambda i: (i, 0, 0))],
        out_specs=pl.BlockSpec((nb, steps * chans, steps * chans), lambda i: (i, 0, 0)),
        out_shape=jax.ShapeDtypeStruct((n, steps * chans, steps * chans), BF16),
        compiler_params=_cparams(("arbitrary",)),
        name="s5_toeplitz",
    )(c2, w_gen, d_rep)


def _s5_matrices(a_re, a_im, log_dt, b_re, b_im, c_re, c_im, d_skip):
    k = S5_STEP
    a_re, a_im = a_re.astype(F32), a_im.astype(F32)
    dt = jnp.exp(log_dt.astype(F32))[..., None]
    lam_re, lam_im = a_re * dt, a_im * dt
    steps = jnp.arange(k + 1, dtype=F32)[:, None, None, None]
    mag = jnp.exp(steps * lam_re)
    pr, pi = mag * jnp.cos(steps * lam_im), mag * jnp.sin(steps * lam_im)
    den = a_re * a_re + a_im * a_im
    nr = pr[1] - 1.0
    coef_re = (nr * a_re + pi[1] * a_im) / den
    coef_im = (pi[1] * a_re - nr * a_im) / den
    bt_re = jnp.swapaxes(b_re.astype(F32), -1, -2)
    bt_im = jnp.swapaxes(b_im.astype(F32), -1, -2)
    bb_re = coef_re[:, :, None] * bt_re - coef_im[:, :, None] * bt_im
    bb_im = coef_re[:, :, None] * bt_im + coef_im[:, :, None] * bt_re
    wr = pr[:, :, :, None] * bb_re - pi[:, :, :, None] * bb_im
    wi = pr[:, :, :, None] * bb_im + pi[:, :, :, None] * bb_re
    c_re, c_im = c_re.astype(F32), c_im.astype(F32)
    g_n, c_n, p_n = a_re.shape[1], b_re.shape[-1], a_re.shape[-1]
    def lag_rows(w_re, w_im):
        return jnp.transpose(jnp.concatenate([w_re, -w_im], -1), (1, 0, 2, 3)).reshape(g_n, k * c_n, 2 * p_n)

    w_gen = jnp.concatenate(
        [jnp.pad(lag_rows(wr[k - 1::-1, 0], wi[k - 1::-1, 0]), ((0, 0), (0, k * c_n), (0, 0))),
         jnp.pad(lag_rows(wr[:k, 1], wi[:k, 1]), ((0, 0), ((k - 1) * c_n, c_n), (0, 0)))], axis=-1)
    c2 = jnp.concatenate([c_re[0], c_im[0], c_re[1], c_im[1]], -1)
    d_rep = jnp.broadcast_to(d_skip.astype(F32).reshape(g_n, c_n, 1), (g_n, c_n, LANES))
    toep = _s5_toeplitz(c2, w_gen, d_rep, k, c_n)
    endw = jnp.concatenate([wr[k - 1::-1, 0], wr[:k, 1], wi[k - 1::-1, 0], wi[:k, 1]], -1)
    endw = jnp.transpose(endw, (1, 0, 2, 3)).reshape(g_n, k * c_n, 4 * p_n)

    def out_cols(d, powers_re, powers_im):
        cr, ci = c_re[d][None], c_im[d][None]
        pre, pim = powers_re[:, :, None, :], powers_im[:, :, None, :]
        return cr * pre - ci * pim, -(cr * pim + ci * pre)

    of_re, of_im = out_cols(0, pr[1:, 0], pi[1:, 0])
    ob_re, ob_im = out_cols(1, pr[k:0:-1, 1], pi[k:0:-1, 1])
    outw_t = jnp.concatenate([of_re, ob_re, of_im, ob_im], -1)
    outw_t = jnp.transpose(outw_t, (1, 0, 2, 3)).reshape(g_n, k * c_n, 4 * p_n)
    a0 = jnp.concatenate([pr[k, 0], pr[k, 1]], -1)
    a1 = jnp.concatenate([pi[k, 0], pi[k, 1]], -1)
    return toep.astype(BF16), endw.astype(BF16), outw_t.astype(BF16), a0, a1


S5_LANE_GROUPS = LANES // S5_GROUP


def _chunk_transpose(xs):
    n_arr = len(xs)
    lane_chunk = lax.broadcasted_iota(jnp.int32, xs[0].shape, 1) // S5_GROUP
    d = 1
    while d < n_arr:
        keep = (lane_chunk & d) == 0
        out = list(xs)
        for i in range(n_arr):
            if i & d:
                continue
            lo, hi = xs[i], xs[i + d]
            out[i] = jnp.where(keep, lo, pltpu.roll(hi, d * S5_GROUP, 1))
            out[i + d] = jnp.where(keep, pltpu.roll(lo, LANES - d * S5_GROUP, 1), hi)
        xs = out
        d *= 2
    return xs


def _s5_kernel(uc_ref, ul_ref, toep_ref, endw_ref, outw_ref, a0_ref, a1_ref, yc_ref, yl_ref,
               uc_s, ul_s, e_s, yc_s, yl_s, *, batch, n_ctx, n_lat):
    p = S5_STATE
    gl = S5_LANE_GROUPS
    k = S5_STEP

    def stack(u_ref, us_ref, n_blk):
        for b in range(batch):
            for half in range(k // gl):
                xs = [u_ref[b, pl.ds(half * gl + s, n_blk, stride=k), :] for s in range(gl)]
                ys = _chunk_transpose(xs)
                for g in range(gl):
                    us_ref[g, b * n_blk:(b + 1) * n_blk, half * LANES:(half + 1) * LANES] = ys[g].astype(BF16)

    stack(uc_ref, uc_s, n_ctx)
    stack(ul_ref, ul_s, n_lat)

    def increments(us_ref, n_blk, first_blk):
        for g in range(gl):
            e = _dot(us_ref[g], endw_ref[g])
            for b in range(batch):
                for part in range(2):
                    e_s[part, b, pl.ds(first_blk * gl + g, n_blk, stride=gl), :] = (
                        e[b * n_blk:(b + 1) * n_blk, part * LANES:(part + 1) * LANES])

    increments(uc_s, n_ctx, 0)
    increments(ul_s, n_lat, n_ctx)

    a0 = a0_ref[...]
    a1 = a1_ref[...]
    fwd_lane = lax.broadcasted_iota(jnp.int32, (batch, gl, 2 * p), 2) < p

    def scan(first_blk, n, carry):
        def body(i, c):
            s0, s1 = c
            rf = pl.ds(pl.multiple_of((first_blk + i) * gl, gl), gl)
            rb = pl.ds(pl.multiple_of((first_blk + n - 1 - i) * gl, gl), gl)
            e0 = jnp.where(fwd_lane, e_s[0, :, rf, :], e_s[0, :, rb, :])
            e1 = jnp.where(fwd_lane, e_s[1, :, rf, :], e_s[1, :, rb, :])
            e_s[0, :, rf, 0:p] = s0[:, :, 0:p]
            e_s[0, :, rb, p:2 * p] = s0[:, :, p:2 * p]
            e_s[1, :, rf, 0:p] = s1[:, :, 0:p]
            e_s[1, :, rb, p:2 * p] = s1[:, :, p:2 * p]
            return a0 * s0 - a1 * s1 + e0, a0 * s1 + a1 * s0 + e1
        return lax.fori_loop(0, n, body, carry)

    zero = jnp.zeros((batch, gl, 2 * p), F32)
    carry = scan(0, n_ctx, (zero, zero))
    scan(n_ctx, n_lat, carry)

    def outputs(us_ref, ys_ref, y_ref, n_blk, first_blk):
        for g in range(gl):
            state = jnp.concatenate(
                [jnp.concatenate([e_s[part, b, pl.ds(first_blk * gl + g, n_blk, stride=gl), :]
                                  for b in range(batch)], axis=0) for part in range(2)], axis=1)
            ys_ref[g] = _dot_nt(us_ref[g], toep_ref[g]) + _dot_nt(state.astype(BF16), outw_ref[g])
        for b in range(batch):
            for half in range(k // gl):
                ys = [ys_ref[g, b * n_blk:(b + 1) * n_blk, half * LANES:(half + 1) * LANES] for g in range(gl)]
                xs = _chunk_transpose(ys)
                for s in range(gl):
                    y_ref[b, pl.ds(half * gl + s, n_blk, stride=k), :] = xs[s]

    outputs(uc_s, yc_s, yc_ref, n_ctx, 0)
    outputs(ul_s, yl_s, yl_ref, n_lat, n_ctx)


def _s5_mix(u_ctx, u_lat, mats):
    toep, endw, outw, a0, a1 = mats
    bn, lc, w = u_ctx.shape
    ll = u_lat.shape[1]
    gl = S5_LANE_GROUPS
    kc = S5_STEP * S5_GROUP
    n_ctx, n_lat = lc // S5_STEP, ll // S5_STEP
    bb = 4 if bn % 4 == 0 else bn
    seq = lambda n_rows: pl.BlockSpec((bb, n_rows, LANES), lambda i, j: (j, 0, i))
    grp = lambda r, c: pl.BlockSpec((gl, r, c), lambda i, j: (i, 0, 0))
    return pl.pallas_call(
        functools.partial(_s5_kernel, batch=bb, n_ctx=n_ctx, n_lat=n_lat),
        grid=(w // LANES, bn // bb),
        in_specs=[seq(lc), seq(ll), grp(kc, kc), grp(kc, kc), grp(kc, kc),
                  pl.BlockSpec((gl, 2 * S5_STATE), lambda i, j: (i, 0)),
                  pl.BlockSpec((gl, 2 * S5_STATE), lambda i, j: (i, 0))],
        out_specs=[seq(lc), seq(ll)],
        out_shape=[jax.ShapeDtypeStruct((bn, lc, w), F32), jax.ShapeDtypeStruct((bn, ll, w), F32)],
        scratch_shapes=[
            pltpu.VMEM((gl, bb * n_ctx, kc), BF16), pltpu.VMEM((gl, bb * n_lat, kc), BF16),
            pltpu.VMEM((2, bb, (n_ctx + n_lat) * gl, LANES), F32),
            pltpu.VMEM((gl, bb * n_ctx, kc), F32), pltpu.VMEM((gl, bb * n_lat, kc), F32),
        ],
        compiler_params=_cparams(("arbitrary", "arbitrary")),
        name="s5_scan",
    )(u_ctx, u_lat, toep, endw, outw, a0, a1)


def _mlstm_kernel(pc_ref, pl_ref, gc_ref, gl_ref, cw_ref, cb_ref, wq_ref, wkt_ref, wv_ref, gb_ref, mc_ref, ml_ref,
                  q_s, kt_s, vx_s, h_s, st_s, bcol_s, mcol_s, rrow_s, xpad_s, *, n_ctx_rows, n_lat_rows):
    t_n = ML_CHUNK
    width = ML_HEADS * LANES
    nc_ctx, nc_lat = n_ctx_rows // t_n, n_lat_rows // t_n
    nc = nc_ctx + nc_lat
    scale_k = LANES ** -0.5

    def project(p_ref, n_rows, base):
        ones = jnp.ones((n_rows, LANES), BF16)
        pad = SUBLANES
        xpad_s[0:pad, :] = jnp.zeros((pad, LANES), F32)
        xpad_s[pad + n_rows:2 * pad + n_rows, :] = jnp.zeros((pad, LANES), F32)
        for h in range(ML_HEADS):
            cols = slice(h * LANES, (h + 1) * LANES)
            xm_lo = p_ref[0, :, cols]
            xm = xm_lo.astype(F32)
            xpad_s[pad:pad + n_rows, :] = xm
            taps = cw_ref[:, cols]
            n_tap = taps.shape[0]
            assert n_tap // 2 <= pad
            acc = jnp.zeros_like(xm) + cb_ref[:, cols]
            for j in range(n_tap):
                d = j - n_tap // 2
                sh = xm if d == 0 else xpad_s[pad + d:pad + d + n_rows, :]
                acc = acc + sh * taps[j:j + 1, :]
            xc = _silu(acc).astype(BF16)
            q_s[base:base + n_rows, cols] = _dot(xc, wq_ref[h]).astype(BF16)
            kt = (_dot_nt(wkt_ref[h], xc) * scale_k).astype(BF16)
            for c in range(n_rows // t_n):
                kt_s[base // t_n + c, cols, :] = kt[:, c * t_n:(c + 1) * t_n]
            vx_s[base:base + n_rows, 2 * h * LANES:(2 * h + 1) * LANES] = (
                _dot(xm_lo, wv_ref[h]).astype(BF16))
            vx_s[base:base + n_rows, (2 * h + 1) * LANES:(2 * h + 2) * LANES] = ones

    project(pc_ref, n_ctx_rows, 0)
    project(pl_ref, n_lat_rows, n_ctx_rows)

    ti = lax.broadcasted_iota(jnp.int32, (t_n, t_n), 0)
    si = lax.broadcasted_iota(jnp.int32, (t_n, t_n), 1)
    tri_f = (si <= ti).astype(BF16)
    tri_b = (si >= ti).astype(BF16)
    lane = lax.broadcasted_iota(jnp.int32, (t_n, LANES), 1)
    trow = lax.broadcasted_iota(jnp.int32, (t_n, LANES), 0)

    def gate_prep(g_ref, n_chunks, base_chunk):
        def body(c, _):
            rows = pl.ds(pl.multiple_of(c * t_n, t_n), t_n)
            gcol = g_ref[0, rows, :] + gb_ref[...]
            lf = jax.nn.log_sigmoid(gcol)
            hi = lf.astype(BF16)
            lo = (lf - hi.astype(F32)).astype(BF16)
            pre = _dot(tri_f, hi) + _dot(tri_f, lo)
            suf = _dot(tri_b, hi) + _dot(tri_b, lo)
            bsum = jnp.where(lane < 2 * ML_HEADS, pre, suf)
            rcol = gcol - pltpu.roll(bsum, LANES - ML_HEADS, 1)
            pmax, smax = rcol, rcol
            step = 1
            while step < t_n:
                pmax = jnp.maximum(pmax, jnp.where(trow >= step, pltpu.roll(pmax, step, 0), NEG_BIG))
                smax = jnp.maximum(smax, jnp.where(trow < t_n - step, pltpu.roll(smax, t_n - step, 0), NEG_BIG))
                step *= 2
            bcol_s[base_chunk + c] = bsum
            mcol_s[base_chunk + c] = jnp.where(lane < 2 * ML_HEADS, pmax, smax)
            rrow_s[base_chunk + c] = rcol.T[0:4 * ML_HEADS, :]
            return 0
        lax.fori_loop(0, n_chunks, body, 0, unroll=4 if n_chunks % 4 == 0 else (2 if n_chunks % 2 == 0 else 1))

    gate_prep(gc_ref, nc_ctx, 0)
    gate_prep(gl_ref, nc_lat, nc_ctx)

    def one_dir(c, h, d, m_in):
        cols = slice(h * LANES, (h + 1) * LANES)
        xcols = slice(2 * h * LANES, (2 * h + 2) * LANES)
        li = 2 * d * ML_HEADS + h
        mask = (si <= ti) if d == 0 else (si >= ti)
        last = t_n - 1 if d == 0 else 0
        rows = pl.ds(pl.multiple_of(c * t_n, t_n), t_n)
        q = q_s[rows, cols]
        kt = kt_s[c, cols, :]
        vx = vx_s[rows, xcols]
        r_row = rrow_s[c][li:li + 1, :]
        run_max = jnp.broadcast_to(mcol_s[c][:, li:li + 1], (t_n, LANES))
        b_rep = jnp.broadcast_to(bcol_s[c][:, li + ML_HEADS:li + ML_HEADS + 1], (t_n, LANES))
        mm = jnp.maximum(m_in, run_max)
        dmat = jnp.exp(jnp.where(mask, r_row - mm, NEG_BIG))
        sm = (dmat * _dot(q, kt)).astype(BF16)
        intra = _dot(sm, vx)
        cross = _dot(q, st_s[2 * h + d].astype(BF16))
        inter = jnp.exp(m_in - mm)
        num = intra[:, :LANES] + inter * cross[:, :LANES]
        den = intra[:, LANES:] + inter * cross[:, LANES:]
        h_s[rows, cols] = h_s[rows, cols] + num / jnp.maximum(jnp.abs(den), jnp.exp(-b_rep - mm))
        r_top = run_max[last:last + 1, :]
        b_last = b_rep[last:last + 1, :]
        g_max = b_last + r_top
        m_new = jnp.maximum(b_last + m_in, g_max)
        dec = jnp.exp(b_last + m_in - m_new)
        inj = jnp.exp(g_max - m_new)
        kw = (kt.astype(F32) * (jnp.exp(r_row - r_top) * inj)).astype(BF16)
        inc = _dot(kw, vx)
        st_s[2 * h + d] = jnp.concatenate([dec, dec], axis=1) * st_s[2 * h + d] + inc
        return m_new

    st_s[...] = jnp.zeros_like(st_s)
    h_s[...] = jnp.zeros_like(h_s)

    def body(i, carry):
        cb = jnp.where(i < nc_ctx, nc_ctx - 1 - i, nc + nc_ctx - 1 - i)
        return tuple(one_dir(i if d == 0 else cb, h, d, carry[2 * h + d])
                     for h in range(ML_HEADS) for d in range(2))

    zero = jnp.zeros((1, LANES), F32)
    lax.fori_loop(0, nc, body, (zero,) * (2 * ML_HEADS), unroll=2 if nc % 2 == 0 else 1)

    mc_ref[0] = h_s[0:n_ctx_rows, :].astype(BF16)
    ml_ref[0] = h_s[n_ctx_rows:n_ctx_rows + n_lat_rows, :].astype(BF16)


def _mlstm_mix(p_ctx, p_lat, g_ctx, g_lat, conv_w, conv_b, wq, wk, wv, gate_b):
    bn, lc, pw = p_ctx.shape
    ll = p_lat.shape[1]
    width = ML_HEADS * LANES
    lt = lc + ll
    nct = lt // ML_CHUNK
    gb = jnp.zeros((1, LANES), F32).at[0, :4 * ML_HEADS].set(gate_b.astype(F32))
    full2 = lambda a: pl.BlockSpec(a.shape, lambda b: (0,) * a.ndim)
    conv_b2 = conv_b.reshape(1, width)
    assert ML_CHUNK == LANES
    wq, wkt, wv = wq.astype(BF16), jnp.swapaxes(wk, 1, 2).astype(BF16), wv.astype(BF16)
    return pl.pallas_call(
        functools.partial(_mlstm_kernel, n_ctx_rows=lc, n_lat_rows=ll),
        grid=(bn,),
        in_specs=[
            pl.BlockSpec((1, lc, pw), lambda b: (b, 0, 0)),
            pl.BlockSpec((1, ll, pw), lambda b: (b, 0, 0)),
            pl.BlockSpec((1, lc, LANES), lambda b: (b, 0, 0)),
            pl.BlockSpec((1, ll, LANES), lambda b: (b, 0, 0)),
            full2(conv_w), full2(conv_b2), full2(wq), full2(wkt), full2(wv), full2(gb),
        ],
        out_specs=[
            pl.BlockSpec((1, lc, width), lambda b: (b, 0, 0)),
            pl.BlockSpec((1, ll, width), lambda b: (b, 0, 0)),
        ],
        out_shape=[jax.ShapeDtypeStruct((bn, lc, width), BF16), jax.ShapeDtypeStruct((bn, ll, width), BF16)],
        scratch_shapes=[
            pltpu.VMEM((lt, width), BF16),
            pltpu.VMEM((nct, width, ML_CHUNK), BF16),
            pltpu.VMEM((lt, 2 * width), BF16),
            pltpu.VMEM((lt, width), F32),
            pltpu.VMEM((2 * ML_HEADS, LANES, 2 * LANES), F32),
            pltpu.VMEM((nct, ML_CHUNK, LANES), F32), pltpu.VMEM((nct, ML_CHUNK, LANES), F32),
            pltpu.VMEM((nct, 4 * ML_HEADS, ML_CHUNK), F32),
            pltpu.VMEM((max(lc, ll) + 2 * SUBLANES, LANES), F32),
        ],
        compiler_params=_cparams(("arbitrary",)),
        name="mlstm_mix",
    )(p_ctx, p_lat, g_ctx, g_lat, conv_w, conv_b2, wq, wkt, wv, gb)


def _ffn_tail(x, y, gate_mix, g2, shift, scale, gate_ffn, w1_ref, w3_ref, w2_ref, hidden_chunk):
    x1 = x + gate_mix * y
    h2 = _norm_mod(x1, g2, shift, scale).astype(BF16)
    hidden = w1_ref.shape[1]
    acc = jnp.zeros_like(x1)
    start = 0
    while start < hidden:
        cs = slice(start, min(start + hidden_chunk, hidden))
        z = (_silu(_dot(h2, w1_ref[:, cs])) * _dot(h2, w3_ref[:, cs])).astype(BF16)
        acc = acc + _dot(z, w2_ref[cs, :])
        start += hidden_chunk
    return x1 + gate_ffn * acc


def _final_norm(x, g):
    return x * lax.rsqrt(jnp.mean(x * x, axis=-1, keepdims=True) + EPS) * g


def _outproj_even_kernel(x_ref, ys_ref, m_ref, og_ref, gm_ref, sh_ref, sc_ref, gf_ref, g2_ref, gw_ref, gbias_ref,
                         ng_ref, wo_ref, w1_ref, w3_ref, w2_ref, o_ref, *, hidden_chunk, heads):
    ys = jax.nn.gelu(ys_ref[0])
    s = ys * jax.nn.sigmoid(_dot(ys.astype(BF16), gw_ref[...]) + gbias_ref[...])
    dh = m_ref.shape[-1] // heads
    gated = []
    for h in range(heads):
        cols = slice(h * dh, (h + 1) * dh)
        hh = m_ref[0, :, cols].astype(F32)
        mu = jnp.mean(hh, axis=1, keepdims=True)
        cen = hh - mu
        var = jnp.mean(cen * cen, axis=1, keepdims=True)
        normed = cen * lax.rsqrt(var + EPS) * ng_ref[:, cols]
        gated.append((jax.nn.sigmoid(og_ref[0, :, cols].astype(F32)) * normed).astype(BF16))
    sw = ys_ref.shape[-1]
    y = _dot(s.astype(BF16), wo_ref[0:sw, :]) + _dot(jnp.concatenate(gated, axis=1), wo_ref[sw:, :])
    o_ref[0] = _ffn_tail(x_ref[0], y, gm_ref[0], g2_ref[...], sh_ref[0], sc_ref[0], gf_ref[0],
                         w1_ref, w3_ref, w2_ref, hidden_chunk)


def _outproj_odd_kernel(x_ref, a_ref, gate_ref, gm_ref, sh_ref, sc_ref, gf_ref, g2_ref, ng_ref, wo_ref, w1_ref,
                        w3_ref, w2_ref, fg_ref, o_ref, *, hidden_chunk, heads):
    hv = a_ref.shape[-1] // heads
    y = jnp.zeros(x_ref.shape[1:], F32)
    for h in range(heads):
        cols = slice(h * hv, (h + 1) * hv)
        o = a_ref[0, :, cols].astype(F32)
        mu = jnp.mean(o, axis=1, keepdims=True)
        cen = o - mu
        var = jnp.mean(cen * cen, axis=1, keepdims=True)
        normed = cen * lax.rsqrt(var + EPS) * ng_ref[:, cols]
        gated = (_silu(gate_ref[0, :, cols].astype(F32)) * normed).astype(BF16)
        y = y + _dot(gated, wo_ref[cols, :])
    out = _ffn_tail(x_ref[0], y, gm_ref[0], g2_ref[...], sh_ref[0], sc_ref[0], gf_ref[0],
                    w1_ref, w3_ref, w2_ref, hidden_chunk)
    o_ref[0] = _final_norm(out, fg_ref[...])


MXU_WIDTH = 256


def _hidden_chunk(hidden):
    return -(-hidden // (2 * MXU_WIDTH)) * MXU_WIDTH


def _mod_specs(d, row_of_batch, slots):
    return [pl.BlockSpec((1, 1, d), lambda b, i, s=s: (row_of_batch(b) * N_MOD + s, 0, 0)) for s in slots]


def _layer_resident(w, layer):
    return _resident((None,) + w.shape[1:], lambda b, i: (layer,) + (0,) * (w.ndim - 1))


def _outproj_even(x, ys, m, p, mods, row_of_batch, g2, glu_w, glu_b, norm_g, w_out, ffn, layer):
    bn, ln, d = x.shape
    sw, mw = ys.shape[-1], m.shape[-1]
    tm = _token_tile(ln)
    res = lambda a: _resident(a.shape, lambda b, i: (0,) * a.ndim)
    tok = lambda w: pl.BlockSpec((1, tm, w), lambda b, i: (b, i, 0))
    row = lambda w: pl.BlockSpec((1, w), lambda b, i: (0, 0))
    return pl.pallas_call(
        functools.partial(_outproj_even_kernel, hidden_chunk=_hidden_chunk(ffn[0].shape[-1]), heads=ML_HEADS),
        grid=(bn, ln // tm),
        in_specs=[tok(d), tok(sw), tok(mw), pl.BlockSpec((1, tm, mw), lambda b, i: (b, i, 1))]
        + _mod_specs(d, row_of_batch, (2, 3, 4, 5))
        + [row(d), res(glu_w), row(sw), row(mw), res(w_out)] + [_layer_resident(w, layer) for w in ffn],
        out_specs=tok(d),
        out_shape=jax.ShapeDtypeStruct((bn, ln, d), F32),
        compiler_params=_cparams(("arbitrary", "arbitrary")),
        name="outproj_ffn_even",
    )(x, ys, m, p, mods, mods, mods, mods, g2, glu_w, glu_b.reshape(1, sw), norm_g.reshape(1, mw), w_out, *ffn)


def _outproj_odd(x, a, gate, mods, row_of_batch, g2, norm_g, w_out, ffn, layer, final_g):
    bn, ln, d = x.shape
    vw = a.shape[-1]
    tm = _token_tile(ln)
    res = lambda a_: _resident(a_.shape, lambda b, i: (0,) * a_.ndim)
    tok = lambda w: pl.BlockSpec((1, tm, w), lambda b, i: (b, i, 0))
    row = lambda w: pl.BlockSpec((1, w), lambda b, i: (0, 0))
    return pl.pallas_call(
        functools.partial(_outproj_odd_kernel, hidden_chunk=_hidden_chunk(ffn[0].shape[-1]), heads=RET_HEADS),
        grid=(bn, ln // tm),
        in_specs=[tok(d), tok(vw), tok(vw)] + _mod_specs(d, row_of_batch, (2, 3, 4, 5))
        + [row(d), row(vw), res(w_out)] + [_layer_resident(w, layer) for w in ffn] + [row(d)],
        out_specs=tok(d),
        out_shape=jax.ShapeDtypeStruct((bn, ln, d), F32),
        compiler_params=_cparams(("arbitrary", "arbitrary")),
        name="outproj_ffn_odd",
    )(x, a, gate, mods, mods, mods, mods, g2, norm_g.reshape(1, vw), w_out, *ffn, final_g)


def _rope_heads(t, cos, sin, heads, hk):
    half = hk // 2
    out = []
    for h in range(heads):
        t1 = t[:, h * hk:h * hk + half]
        t2 = t[:, h * hk + half:(h + 1) * hk]
        out.append(t1 * cos - t2 * sin)
        out.append(t2 * cos + t1 * sin)
    return jnp.concatenate(out, axis=-1)


def _inproj_odd_kernel(x_ref, sh_ref, sc_ref, g_ref, cos_ref, sin_ref, w_ref, q_ref, k_ref, v_ref, gate_ref,
                       *, qk, vw, heads):
    h = _norm_mod(x_ref[0], g_ref[...], sh_ref[0], sc_ref[0]).astype(BF16)
    hk = qk // heads
    cos, sin = cos_ref[...], sin_ref[...]
    q = _rope_heads(_dot(h, w_ref[:, 0:qk]), cos, sin, heads, hk)
    q_ref[0] = (q * (hk ** -0.5)).astype(BF16)
    k = _rope_heads(_dot(h, w_ref[:, qk:2 * qk]), cos, sin, heads, hk)
    k_ref[0] = k.astype(BF16)
    v_ref[0] = _dot(h, w_ref[:, 2 * qk:2 * qk + vw]).astype(BF16)
    gate_ref[0] = _dot(h, w_ref[:, 2 * qk + vw:2 * qk + 2 * vw]).astype(BF16)


def _inproj_odd_ctx_kernel(x_ref, sh_ref, sc_ref, g_ref, w_ref, k_ref, v_ref, *, qk, vw):
    h = _norm_mod(x_ref[0], g_ref[...], sh_ref[0], sc_ref[0]).astype(BF16)
    k_ref[0] = _dot(h, w_ref[:, qk:2 * qk]).astype(BF16)
    v_ref[0] = _dot(h, w_ref[:, 2 * qk:2 * qk + vw]).astype(BF16)


def _inproj_odd(x, mods, row_of_batch, g, cos, sin, w, qk, vw):
    bn, ln, d = x.shape
    tm = _token_tile(ln)
    tok = lambda w_: pl.BlockSpec((1, tm, w_), lambda b, i: (b, i, 0))
    half = cos.shape[1]
    return pl.pallas_call(
        functools.partial(_inproj_odd_kernel, qk=qk, vw=vw, heads=RET_HEADS),
        grid=(bn, ln // tm),
        in_specs=[tok(d)] + _mod_specs(d, row_of_batch, (0, 1))
        + [pl.BlockSpec((1, d), lambda b, i: (0, 0)),
           pl.BlockSpec((tm, half), lambda b, i: (i, 0)), pl.BlockSpec((tm, half), lambda b, i: (i, 0)),
           _resident(w.shape, lambda b, i: (0, 0))],
        out_specs=[tok(qk), tok(qk), tok(vw), tok(vw)],
        out_shape=[jax.ShapeDtypeStruct((bn, ln, qk), BF16), jax.ShapeDtypeStruct((bn, ln, qk), BF16),
                   jax.ShapeDtypeStruct((bn, ln, vw), BF16), jax.ShapeDtypeStruct((bn, ln, vw), BF16)],
        compiler_params=_cparams(("arbitrary", "arbitrary")),
        name="inproj_odd",
    )(x, mods, mods, g, cos, sin, w)


def _inproj_odd_ctx(x, mods, row_of_batch, g, w_kv, qk, vw):
    bn, ln, d = x.shape
    tm = _token_tile(ln)
    tok = lambda w_: pl.BlockSpec((1, tm, w_), lambda b, i: (b, i, 0))
    return pl.pallas_call(
        functools.partial(_inproj_odd_ctx_kernel, qk=qk, vw=vw),
        grid=(bn, ln // tm),
        in_specs=[tok(d)] + _mod_specs(d, row_of_batch, (0, 1))
        + [pl.BlockSpec((1, d), lambda b, i: (0, 0)), _resident(w_kv.shape, lambda b, i: (0, 0))],
        out_specs=[tok(qk), tok(vw)],
        out_shape=[jax.ShapeDtypeStruct((bn, ln, qk), BF16), jax.ShapeDtypeStruct((bn, ln, vw), BF16)],
        compiler_params=_cparams(("arbitrary", "arbitrary")),
        name="inproj_odd_ctx",
    )(x, mods, mods, g, w_kv)


def _retention_kernel(lg_ref, q_ref, k_ref, v_ref, kc_ref, vc_ref, o_ref, acc_s, sf_s, sb_s,
                      *, chunk, n_ctx_rows, n_lat_rows):
    t_n = chunk
    h = pl.program_id(1)
    lgf = jnp.full((1, 1), lg_ref[0, h], F32)
    lgb = jnp.full((1, 1), lg_ref[1, h], F32)
    nc_ctx, nc_lat = n_ctx_rows // t_n, n_lat_rows // t_n
    ti = lax.broadcasted_iota(jnp.int32, (t_n, t_n), 0)
    si = lax.broadcasted_iota(jnp.int32, (t_n, t_n), 1)
    diff = (ti - si).astype(F32)
    decay = jnp.where(diff >= 0, jnp.exp(lgf * jnp.maximum(diff, 0.0)), jnp.exp(lgb * jnp.maximum(-diff, 0.0)))
    pos = lax.broadcasted_iota(jnp.int32, (t_n, 1), 0).astype(F32)
    inter_f = jnp.exp(lgf * (pos + 1.0))
    inter_b = jnp.exp(lgb * (t_n - pos))
    wend_f = jnp.exp(lgf * (t_n - 1.0 - pos))
    wend_b = jnp.exp(lgb * pos)
    cd_f = jnp.exp(lgf * t_n)
    cd_b = jnp.exp(lgb * t_n)

    def rows_of(c):
        return pl.ds(pl.multiple_of(c * t_n, t_n), t_n)

    def bump(s_ref, k, v, wend, cd):
        kw = (k.astype(F32) * wend).astype(BF16)
        s_ref[...] = cd * s_ref[...] + _dot_tn(kw, v)

    sf_s[...] = jnp.zeros_like(sf_s)
    sb_s[...] = jnp.zeros_like(sb_s)

    def ctx_f(c, _):
        bump(sf_s, kc_ref[0, rows_of(c), :], vc_ref[0, rows_of(c), :], wend_f, cd_f)
        return 0
    lax.fori_loop(0, nc_ctx, ctx_f, 0)

    def ctx_b(i, _):
        c = nc_ctx - 1 - i
        bump(sb_s, kc_ref[0, rows_of(c), :], vc_ref[0, rows_of(c), :], wend_b, cd_b)
        return 0
    lax.fori_loop(0, nc_ctx, ctx_b, 0)

    def forward_part(rows):
        q, k, v = q_ref[0, rows, :], k_ref[0, rows, :], v_ref[0, rows, :]
        scores = (_dot_nt(q, k) * decay).astype(BF16)
        part = _dot(scores, v) + inter_f * _dot(q, sf_s[...].astype(BF16))
        bump(sf_s, k, v, wend_f, cd_f)
        return part

    def backward_part(rows):
        q, k, v = q_ref[0, rows, :], k_ref[0, rows, :], v_ref[0, rows, :]
        part = inter_b * _dot(q, sb_s[...].astype(BF16))
        bump(sb_s, k, v, wend_b, cd_b)
        return part

    def first_half(i, _):
        rf, rb = rows_of(i), rows_of(nc_lat - 1 - i)
        acc_s[rf, :] = forward_part(rf)
        acc_s[rb, :] = backward_part(rb)
        return 0
    pair_unroll = 4 if (nc_lat // 2) % 4 == 0 else 1
    lax.fori_loop(0, nc_lat // 2, first_half, 0, unroll=pair_unroll)

    def second_half(i, _):
        rf, rb = rows_of(i), rows_of(nc_lat - 1 - i)
        o_ref[0, rf, :] = (acc_s[rf, :] + forward_part(rf)).astype(BF16)
        o_ref[0, rb, :] = (acc_s[rb, :] + backward_part(rb)).astype(BF16)
        return 0
    lax.fori_loop(nc_lat // 2, nc_lat, second_half, 0, unroll=pair_unroll)


def _retention_mix(q, k, v, k_ctx, v_ctx, log_gamma):
    bn, ll, qk = q.shape
    lc = k_ctx.shape[1]
    vw = v.shape[-1]
    hk, hv = qk // RET_HEADS, vw // RET_HEADS
    chunk = 256 if (ll % 512 == 0 and lc % 256 == 0) else 128
    assert (ll // chunk) % 2 == 0, "latent chunks are visited in forward/backward pairs"
    head = lambda n_rows, w: pl.BlockSpec((1, n_rows, w), lambda b, h: (b, 0, h))
    return pl.pallas_call(
        functools.partial(_retention_kernel, chunk=chunk, n_ctx_rows=lc, n_lat_rows=ll),
        grid=(bn, RET_HEADS),
        in_specs=[
            pl.BlockSpec(memory_space=pltpu.SMEM),
            head(ll, hk), head(ll, hk), head(ll, hv), head(lc, hk), head(lc, hv),
        ],
        out_specs=head(ll, hv),
        out_shape=jax.ShapeDtypeStruct((bn, ll, vw), BF16),
        scratch_shapes=[pltpu.VMEM((ll, hv), F32), pltpu.VMEM((hk, hv), F32), pltpu.VMEM((hk, hv), F32)],
        compiler_params=_cparams(("arbitrary", "arbitrary")),
        name="retention_mix",
    )(log_gamma.astype(F32), q, k, v, k_ctx, v_ctx)


def _grid_rope(n_pos, hk):
    rows = n_pos // GRID_W
    row = jnp.repeat(jnp.arange(rows, dtype=F32), GRID_W)
    col = jnp.tile(jnp.arange(GRID_W, dtype=F32), rows)
    n_freq = hk // 4
    inv = ROPE_BASE ** (-jnp.arange(n_freq, dtype=F32) / n_freq)
    ang = jnp.concatenate([row[:, None] * inv, col[:, None] * inv], -1)
    return jnp.cos(ang), jnp.sin(ang)


def kernel(x, c, ctx, c_ctx, ada_w, ada_b, norm1_g, norm2_g, ab_w_in, ab_w_out, s5_a_re, s5_a_im, s5_log_dt,
           s5_b_re, s5_b_im, s5_c_re, s5_c_im, s5_d, s5_glu_w, s5_glu_b, ml_conv_w, ml_conv_b, ml_wq, ml_wk,
           ml_wv, ml_gate_b, ml_norm_g, ret_w_in, ret_w_out, ret_log_gamma, ret_norm_g, ffn_w1, ffn_w3, ffn_w2,
           final_g):
    bn, ln, d = x.shape
    depth = ada_w.shape[0]
    assert depth == 2, "one S5 || mLSTM layer followed by one retention layer"
    s5_width = s5_d.shape[-1]
    ml_width = ml_norm_g.shape[-1]
    assert ml_width == ML_HEADS * LANES and s5_width % S5_GROUP == 0

    r_pad = -(-(bn + 1) // SUBLANES) * SUBLANES
    vec = jnp.zeros((r_pad, d), F32).at[:bn].set(c).at[bn].set(c_ctx)
    mods_all = _modulation(vec, ada_w, ada_b).reshape(depth, r_pad * N_MOD, 1, d)
    lat_row = lambda b: b
    ctx_row = lambda b: bn

    mods = mods_all[0]
    g1 = norm1_g[0].reshape(1, d)
    g2 = norm2_g[0].reshape(1, d)
    w_in = ab_w_in[0]
    n_gate = w_in.shape[1] - s5_width - 2 * ml_width
    w_in = jnp.pad(w_in, ((0, 0), (0, LANES - n_gate))).astype(BF16)
    u_lat, p_lat, g_lat = _inproj_even(x, mods, lat_row, g1, w_in, s5_width)
    lc = ctx.shape[1]
    flat = lambda t: t.reshape(1, bn * lc, t.shape[-1])
    unflat = lambda t: t.reshape(bn, lc, t.shape[-1])
    u_ctx, p_ctx, g_ctx = (unflat(t) for t in _inproj_even(flat(ctx), mods, ctx_row, g1, w_in, s5_width))
    mats = _s5_matrices(s5_a_re[0], s5_a_im[0], s5_log_dt[0], s5_b_re[0], s5_b_im[0], s5_c_re[0], s5_c_im[0],
                        s5_d[0])
    ys_ctx, ys_lat = _s5_mix(u_ctx, u_lat, mats)
    m_ctx, m_lat = _mlstm_mix(p_ctx, p_lat, g_ctx, g_lat, ml_conv_w[0], ml_conv_b[0], ml_wq[0], ml_wk[0],
                              ml_wv[0], ml_gate_b[0])
    glu_w = s5_glu_w[0].astype(BF16)
    w_out = ab_w_out[0].astype(BF16)
    ffn = (ffn_w1.astype(BF16), ffn_w3.astype(BF16), ffn_w2.astype(BF16))
    x = _outproj_even(x, ys_lat, m_lat, p_lat, mods, lat_row, g2, glu_w, s5_glu_b[0], ml_norm_g[0],
                      w_out, ffn, 0)
    ctx = _outproj_even(flat(ctx), flat(ys_ctx), flat(m_ctx), flat(p_ctx), mods, ctx_row, g2, glu_w, s5_glu_b[0],
                        ml_norm_g[0], w_out, ffn, 0)

    mods = mods_all[1]
    g1 = norm1_g[1].reshape(1, d)
    g2 = norm2_g[1].reshape(1, d)
    vw = ret_norm_g.shape[-1]
    qk = (ret_w_in.shape[-1] - 2 * vw) // 2
    w_in = ret_w_in[0].astype(BF16)
    cos, sin = _grid_rope(ln, qk // RET_HEADS)
    q, k, v, gate = _inproj_odd(x, mods, lat_row, g1, cos, sin, w_in, qk, vw)
    k_ctx, v_ctx = (unflat(t) for t in _inproj_odd_ctx(ctx, mods, ctx_row, g1, w_in, qk, vw))
    a = _retention_mix(q, k, v, k_ctx, v_ctx, ret_log_gamma[0])
    w_out = ret_w_out[0].astype(BF16)
    return _outproj_odd(x, a, gate, mods, lat_row, g2, ret_norm_g[0], w_out, ffn, 1, final_g.reshape(1, d))
```

```python
import functools
import math

import jax
import jax.numpy as jnp
from jax import lax
from jax.experimental import pallas as pl
from jax.experimental.pallas import tpu as pltpu

F32 = jnp.float32
BF16 = jnp.bfloat16

EPS = 1e-6
N_MOD = 6
GRID_W = 64
ROPE_BASE = 10000.0

S5_GROUP = 16
S5_STATE = 64
S5_STEP = 16
ML_HEADS = 4
ML_CHUNK = 128
RET_HEADS = 4
SUBLANES = 8
LANES = 128
NEG_BIG = -1e30
VMEM_LIMIT = 56 * 1024 * 1024


def _cparams(sem):
    return pltpu.CompilerParams(dimension_semantics=sem, vmem_limit_bytes=VMEM_LIMIT)


def _resident(shape, index_map):
    return pl.BlockSpec(shape, index_map, pipeline_mode=pl.Buffered(1))


BF16_ROWS = 16


class _CastAlong:
    def __init__(self, weights_and_layers, grid):
        self.items = list(weights_and_layers)
        steps = math.prod(grid)
        strides = [math.prod(grid[i + 1:]) for i in range(len(grid))]
        self.step = lambda idx: sum(i * s for i, s in zip(idx, strides))
        self.plans = []
        for w, _ in self.items:
            _, r, c = w.shape
            if r % (steps * BF16_ROWS) == 0:
                self.plans.append(((r // steps, c), lambda s: (s, 0)))
            else:
                assert steps % 2 == 0 and r % (steps // 2 * BF16_ROWS) == 0 and c % (2 * LANES) == 0
                self.plans.append(((2 * r // steps, c // 2), lambda s: (s // 2, s % 2)))

    def __len__(self):
        return len(self.items)

    def in_specs(self):
        return [pl.BlockSpec((None,) + blk, lambda *idx, f=f, l=l: (l,) + f(self.step(idx)))
                for (blk, f), (_, l) in zip(self.plans, self.items)]

    def out_specs(self):
        return [pl.BlockSpec(blk, lambda *idx, f=f: f(self.step(idx))) for blk, f in self.plans]

    def out_shapes(self):
        return [jax.ShapeDtypeStruct(w.shape[1:], BF16) for w, _ in self.items]

    def operands(self):
        return [w for w, _ in self.items]

    def wrap(self, kernel_fn, n_in, n_out):
        n = len(self)

        def wrapped(*refs):
            ins, cast_in = refs[:n_in], refs[n_in:n_in + n]
            outs, cast_out = refs[n_in + n:n_in + n + n_out], refs[n_in + n + n_out:n_in + 2 * n + n_out]
            for src, dst in zip(cast_in, cast_out):
                dst[...] = src[...].astype(BF16)
            kernel_fn(*ins, *outs, *refs[n_in + 2 * n + n_out:])
        return wrapped


def _token_tile(n):
    for t in (512, 256, 128):
        if n % t == 0:
            return t
    raise ValueError(f"sequence length {n} must be a multiple of 128")


def _silu(v):
    return v * jax.nn.sigmoid(v)


def _norm_mod(x, g, shift, scale):
    y = x * lax.rsqrt(jnp.mean(x * x, axis=-1, keepdims=True) + EPS)
    return (y * g) * (1.0 + scale) + shift


def _dot(a, b):
    return jnp.dot(a, b, preferred_element_type=F32)


def _dot_nt(a, b):
    return lax.dot_general(a, b, (((1,), (1,)), ((), ())), preferred_element_type=F32)


def _dot_tn(a, b):
    return lax.dot_general(a, b, (((0,), (0,)), ((), ())), preferred_element_type=F32)


def _mod_kernel(v_ref, w_ref, b_ref, o_ref):
    s = _silu(v_ref[...]).astype(BF16)
    o_ref[0] = _dot(s, w_ref[0].astype(BF16)) + b_ref[0]


def _modulation(vec, ada_w, ada_b):
    depth, d, n = ada_w.shape
    r = vec.shape[0]
    tn = 1024
    return pl.pallas_call(
        _mod_kernel,
        grid=(depth, n // tn),
        in_specs=[
            pl.BlockSpec((r, d), lambda l, j: (0, 0)),
            pl.BlockSpec((1, d, tn), lambda l, j: (l, 0, j)),
            pl.BlockSpec((1, 1, tn), lambda l, j: (l, 0, j)),
        ],
        out_specs=pl.BlockSpec((1, r, tn), lambda l, j: (l, 0, j)),
        out_shape=jax.ShapeDtypeStruct((depth, r, n), F32),
        compiler_params=_cparams(("arbitrary", "arbitrary")),
        name="adaln_modulation",
    )(vec, ada_w, ada_b.reshape(depth, 1, n))


def _inproj_even_kernel(x_ref, sh_ref, sc_ref, g_ref, w_ref, u_ref, p_ref, gate_ref, *, s5_width):
    h = _norm_mod(x_ref[0], g_ref[...], sh_ref[0], sc_ref[0]).astype(BF16)
    p = _dot(h, w_ref[...])
    n_mid = p_ref.shape[-1]
    u_ref[0] = p[:, :s5_width]
    p_ref[0] = p[:, s5_width:s5_width + n_mid].astype(BF16)
    gate_ref[0] = p[:, s5_width + n_mid:]


def _inproj_even(x, mods, row_of_batch, g, w, s5_width):
    bn, ln, d = x.shape
    n = w.shape[1]
    n_mid = n - s5_width - LANES
    tm = _token_tile(ln)
    return pl.pallas_call(
        functools.partial(_inproj_even_kernel, s5_width=s5_width),
        grid=(bn, ln // tm),
        in_specs=[
            pl.BlockSpec((1, tm, d), lambda b, i: (b, i, 0)),
            pl.BlockSpec((1, 1, d), lambda b, i: (row_of_batch(b) * N_MOD + 0, 0, 0)),
            pl.BlockSpec((1, 1, d), lambda b, i: (row_of_batch(b) * N_MOD + 1, 0, 0)),
            pl.BlockSpec((1, d), lambda b, i: (0, 0)),
            _resident((d, n), lambda b, i: (0, 0)),
        ],
        out_specs=[
            pl.BlockSpec((1, tm, s5_width), lambda b, i: (b, i, 0)),
            pl.BlockSpec((1, tm, n_mid), lambda b, i: (b, i, 0)),
            pl.BlockSpec((1, tm, LANES), lambda b, i: (b, i, 0)),
        ],
        out_shape=[
            jax.ShapeDtypeStruct((bn, ln, s5_width), F32),
            jax.ShapeDtypeStruct((bn, ln, n_mid), BF16),
            jax.ShapeDtypeStruct((bn, ln, LANES), F32),
        ],
        compiler_params=_cparams(("arbitrary", "arbitrary")),
        name="inproj_even",
    )(x, mods, mods, g, w)


def _s5_toeplitz_kernel(c_ref, w_ref, d_ref, o_ref, *, steps, chans):
    width = steps * chans
    lag0 = (steps - 1) * chans
    slab = lag0 // LANES
    row = lax.broadcasted_iota(jnp.int32, (chans, LANES), 0)
    lane = lax.broadcasted_iota(jnp.int32, (chans, LANES), 1)
    diag = lane == row + lag0 % LANES
    for n in range(c_ref.shape[0]):
        gen = lax.dot_general(c_ref[n], w_ref[n], (((1,), (1,)), ((), ())),
                              precision=lax.Precision.HIGHEST, preferred_element_type=F32)
        pieces = [gen[:, j * LANES:(j + 1) * LANES] for j in range(2 * width // LANES)]
        pieces[slab] = pieces[slab] + jnp.where(diag, d_ref[n], 0.0)
        gen = jnp.concatenate(pieces, axis=1)
        for t in range(steps):
            off = (steps - 1 - t) * chans
            o_ref[n, t * chans:(t + 1) * chans, :] = gen[:, off:off + width].astype(BF16)


def _s5_toeplitz(c2, w_gen, d_rep, steps, chans):
    n, co, kk = c2.shape
    r = w_gen.shape[1]
    nb = 8
    return pl.pallas_call(
        functools.partial(_s5_toeplitz_kernel, steps=steps, chans=chans),
        grid=(n // nb,),
        in_specs=[pl.BlockSpec((nb, co, kk), lambda i: (i, 0, 0)), pl.BlockSpec((nb, r, kk), lambda i: (i, 0, 0)),
                  pl.BlockSpec((nb, co, LANES), lambda i: (i, 0, 0))],
        out_specs=pl.BlockSpec((nb, steps * chans, steps * chans), lambda i: (i, 0, 0)),
        out_shape=jax.ShapeDtypeStruct((n, steps * chans, steps * chans), BF16),
        compiler_params=_cparams(("arbitrary",)),
        name="s5_toeplitz",
    )(c2, w_gen, d_rep)


def _s5_matrices(a_re, a_im, log_dt, b_re, b_im, c_re, c_im, d_skip):
    k = S5_STEP
    a_re, a_im = a_re.astype(F32), a_im.astype(F32)
    dt = jnp.exp(log_dt.astype(F32))[..., None]
    lam_re, lam_im = a_re * dt, a_im * dt
    steps = jnp.arange(k + 1, dtype=F32)[:, None, None, None]
    mag = jnp.exp(steps * lam_re)
    pr, pi = mag * jnp.cos(steps * lam_im), mag * jnp.sin(steps * lam_im)
    den = a_re * a_re + a_im * a_im
    nr = pr[1] - 1.0
    coef_re = (nr * a_re + pi[1] * a_im) / den
    coef_im = (pi[1] * a_re - nr * a_im) / den
    bt_re = jnp.swapaxes(b_re.astype(F32), -1, -2)
    bt_im = jnp.swapaxes(b_im.astype(F32), -1, -2)
    bb_re = coef_re[:, :, None] * bt_re - coef_im[:, :, None] * bt_im
    bb_im = coef_re[:, :, None] * bt_im + coef_im[:, :, None] * bt_re
    wr = pr[:, :, :, None] * bb_re - pi[:, :, :, None] * bb_im
    wi = pr[:, :, :, None] * bb_im + pi[:, :, :, None] * bb_re
    c_re, c_im = c_re.astype(F32), c_im.astype(F32)
    g_n, c_n, p_n = a_re.shape[1], b_re.shape[-1], a_re.shape[-1]
    def lag_rows(w_re, w_im):
        return jnp.transpose(jnp.concatenate([w_re, -w_im], -1), (1, 0, 2, 3)).reshape(g_n, k * c_n, 2 * p_n)

    w_gen = jnp.concatenate(
        [jnp.pad(lag_rows(wr[k - 1::-1, 0], wi[k - 1::-1, 0]), ((0, 0), (0, k * c_n), (0, 0))),
         jnp.pad(lag_rows(wr[:k, 1], wi[:k, 1]), ((0, 0), ((k - 1) * c_n, c_n), (0, 0)))], axis=-1)
    c2 = jnp.concatenate([c_re[0], c_im[0], c_re[1], c_im[1]], -1)
    d_rep = jnp.broadcast_to(d_skip.astype(F32).reshape(g_n, c_n, 1), (g_n, c_n, LANES))
    toep = _s5_toeplitz(c2, w_gen, d_rep, k, c_n)
    endw = jnp.concatenate([wr[k - 1::-1, 0], wr[:k, 1], wi[k - 1::-1, 0], wi[:k, 1]], -1)
    endw = jnp.transpose(endw, (1, 0, 2, 3)).reshape(g_n, k * c_n, 4 * p_n)

    def out_cols(d, powers_re, powers_im):
        cr, ci = c_re[d][None], c_im[d][None]
        pre, pim = powers_re[:, :, None, :], powers_im[:, :, None, :]
        return cr * pre - ci * pim, -(cr * pim + ci * pre)

    of_re, of_im = out_cols(0, pr[1:, 0], pi[1:, 0])
    ob_re, ob_im = out_cols(1, pr[k:0:-1, 1], pi[k:0:-1, 1])
    outw_t = jnp.concatenate([of_re, ob_re, of_im, ob_im], -1)
    outw_t = jnp.transpose(outw_t, (1, 0, 2, 3)).reshape(g_n, k * c_n, 4 * p_n)
    a0 = jnp.concatenate([pr[k, 0], pr[k, 1]], -1)
    a1 = jnp.concatenate([pi[k, 0], pi[k, 1]], -1)
    return toep.astype(BF16), endw.astype(BF16), outw_t.astype(BF16), a0, a1


S5_LANE_GROUPS = LANES // S5_GROUP


def _chunk_transpose(xs):
    n_arr = len(xs)
    lane_chunk = lax.broadcasted_iota(jnp.int32, xs[0].shape, 1) // S5_GROUP
    d = 1
    while d < n_arr:
        keep = (lane_chunk & d) == 0
        out = list(xs)
        for i in range(n_arr):
            if i & d:
                continue
            lo, hi = xs[i], xs[i + d]
            out[i] = jnp.where(keep, lo, pltpu.roll(hi, d * S5_GROUP, 1))
            out[i + d] = jnp.where(keep, pltpu.roll(lo, LANES - d * S5_GROUP, 1), hi)
        xs = out
        d *= 2
    return xs


def _s5_kernel(uc_ref, ul_ref, toep_ref, endw_ref, outw_ref, a0_ref, a1_ref, yc_ref, yl_ref,
               uc_s, ul_s, e_s, yc_s, yl_s, *, batch, n_ctx, n_lat):
    p = S5_STATE
    gl = S5_LANE_GROUPS
    k = S5_STEP

    def stack(u_ref, us_ref, n_blk):
        for b in range(batch):
            for half in range(k // gl):
                xs = [u_ref[b, pl.ds(half * gl + s, n_blk, stride=k), :] for s in range(gl)]
                ys = _chunk_transpose(xs)
                for g in range(gl):
                    us_ref[g, b * n_blk:(b + 1) * n_blk, half * LANES:(half + 1) * LANES] = ys[g].astype(BF16)

    stack(uc_ref, uc_s, n_ctx)
    stack(ul_ref, ul_s, n_lat)

    def increments(us_ref, n_blk, first_blk):
        for g in range(gl):
            e = _dot(us_ref[g], endw_ref[g])
            for b in range(batch):
                for part in range(2):
                    e_s[part, b, pl.ds(first_blk * gl + g, n_blk, stride=gl), :] = (
                        e[b * n_blk:(b + 1) * n_blk, part * LANES:(part + 1) * LANES])

    increments(uc_s, n_ctx, 0)
    increments(ul_s, n_lat, n_ctx)

    a0 = a0_ref[...]
    a1 = a1_ref[...]
    fwd_lane = lax.broadcasted_iota(jnp.int32, (batch, gl, 2 * p), 2) < p

    def scan(first_blk, n, carry):
        def body(i, c):
            s0, s1 = c
            rf = pl.ds(pl.multiple_of((first_blk + i) * gl, gl), gl)
            rb = pl.ds(pl.multiple_of((first_blk + n - 1 - i) * gl, gl), gl)
            e0 = jnp.where(fwd_lane, e_s[0, :, rf, :], e_s[0, :, rb, :])
            e1 = jnp.where(fwd_lane, e_s[1, :, rf, :], e_s[1, :, rb, :])
            e_s[0, :, rf, 0:p] = s0[:, :, 0:p]
            e_s[0, :, rb, p:2 * p] = s0[:, :, p:2 * p]
            e_s[1, :, rf, 0:p] = s1[:, :, 0:p]
            e_s[1, :, rb, p:2 * p] = s1[:, :, p:2 * p]
            return a0 * s0 - a1 * s1 + e0, a0 * s1 + a1 * s0 + e1
        return lax.fori_loop(0, n, body, carry)

    zero = jnp.zeros((batch, gl, 2 * p), F32)
    carry = scan(0, n_ctx, (zero, zero))
    scan(n_ctx, n_lat, carry)

    def outputs(us_ref, ys_ref, y_ref, n_blk, first_blk):
        for g in range(gl):
            state = jnp.concatenate(
                [jnp.concatenate([e_s[part, b, pl.ds(first_blk * gl + g, n_blk, stride=gl), :]
                                  for b in range(batch)], axis=0) for part in range(2)], axis=1)
            ys_ref[g] = _dot_nt(us_ref[g], toep_ref[g]) + _dot_nt(state.astype(BF16), outw_ref[g])
        for b in range(batch):
            for half in range(k // gl):
                ys = [ys_ref[g, b * n_blk:(b + 1) * n_blk, half * LANES:(half + 1) * LANES] for g in range(gl)]
                xs = _chunk_transpose(ys)
                for s in range(gl):
                    y_ref[b, pl.ds(half * gl + s, n_blk, stride=k), :] = xs[s]

    outputs(uc_s, yc_s, yc_ref, n_ctx, 0)
    outputs(ul_s, yl_s, yl_ref, n_lat, n_ctx)


def _s5_mix(u_ctx, u_lat, mats):
    toep, endw, outw, a0, a1 = mats
    bn, lc, w = u_ctx.shape
    ll = u_lat.shape[1]
    gl = S5_LANE_GROUPS
    kc = S5_STEP * S5_GROUP
    n_ctx, n_lat = lc // S5_STEP, ll // S5_STEP
    bb = 4 if bn % 4 == 0 else bn
    seq = lambda n_rows: pl.BlockSpec((bb, n_rows, LANES), lambda i, j: (j, 0, i))
    grp = lambda r, c: pl.BlockSpec((gl, r, c), lambda i, j: (i, 0, 0))
    return pl.pallas_call(
        functools.partial(_s5_kernel, batch=bb, n_ctx=n_ctx, n_lat=n_lat),
        grid=(w // LANES, bn // bb),
        in_specs=[seq(lc), seq(ll), grp(kc, kc), grp(kc, kc), grp(kc, kc),
                  pl.BlockSpec((gl, 2 * S5_STATE), lambda i, j: (i, 0)),
                  pl.BlockSpec((gl, 2 * S5_STATE), lambda i, j: (i, 0))],
        out_specs=[seq(lc), seq(ll)],
        out_shape=[jax.ShapeDtypeStruct((bn, lc, w), F32), jax.ShapeDtypeStruct((bn, ll, w), F32)],
        scratch_shapes=[
            pltpu.VMEM((gl, bb * n_ctx, kc), BF16), pltpu.VMEM((gl, bb * n_lat, kc), BF16),
            pltpu.VMEM((2, bb, (n_ctx + n_lat) * gl, LANES), F32),
            pltpu.VMEM((gl, bb * n_ctx, kc), F32), pltpu.VMEM((gl, bb * n_lat, kc), F32),
        ],
        compiler_params=_cparams(("arbitrary", "arbitrary")),
        name="s5_scan",
    )(u_ctx, u_lat, toep, endw, outw, a0, a1)


def _mlstm_kernel(pc_ref, pl_ref, gc_ref, gl_ref, cw_ref, cb_ref, wq_ref, wkt_ref, wv_ref, gb_ref, mc_ref, ml_ref,
                  q_s, kt_s, vx_s, h_s, st_s, bcol_s, mcol_s, rrow_s, xpad_s, *, n_ctx_rows, n_lat_rows):
    t_n = ML_CHUNK
    width = ML_HEADS * LANES
    nc_ctx, nc_lat = n_ctx_rows // t_n, n_lat_rows // t_n
    nc = nc_ctx + nc_lat
    scale_k = LANES ** -0.5

    def project(p_ref, n_rows, base):
        ones = jnp.ones((n_rows, LANES), BF16)
        pad = SUBLANES
        xpad_s[0:pad, :] = jnp.zeros((pad, LANES), F32)
        xpad_s[pad + n_rows:2 * pad + n_rows, :] = jnp.zeros((pad, LANES), F32)
        for h in range(ML_HEADS):
            cols = slice(h * LANES, (h + 1) * LANES)
            xm_lo = p_ref[0, :, cols]
            xm = xm_lo.astype(F32)
            xpad_s[pad:pad + n_rows, :] = xm
            taps = cw_ref[:, cols]
            n_tap = taps.shape[0]
            assert n_tap // 2 <= pad
            acc = jnp.zeros_like(xm) + cb_ref[:, cols]
            for j in range(n_tap):
                d = j - n_tap // 2
                sh = xm if d == 0 else xpad_s[pad + d:pad + d + n_rows, :]
                acc = acc + sh * taps[j:j + 1, :]
            xc = _silu(acc).astype(BF16)
            q_s[base:base + n_rows, cols] = _dot(xc, wq_ref[h]).astype(BF16)
            kt = (_dot_nt(wkt_ref[h], xc) * scale_k).astype(BF16)
            for c in range(n_rows // t_n):
                kt_s[base // t_n + c, cols, :] = kt[:, c * t_n:(c + 1) * t_n]
            vx_s[base:base + n_rows, 2 * h * LANES:(2 * h + 1) * LANES] = (
                _dot(xm_lo, wv_ref[h]).astype(BF16))
            vx_s[base:base + n_rows, (2 * h + 1) * LANES:(2 * h + 2) * LANES] = ones

    project(pc_ref, n_ctx_rows, 0)
    project(pl_ref, n_lat_rows, n_ctx_rows)

    ti = lax.broadcasted_iota(jnp.int32, (t_n, t_n), 0)
    si = lax.broadcasted_iota(jnp.int32, (t_n, t_n), 1)
    tri_f = (si <= ti).astype(BF16)
    tri_b = (si >= ti).astype(BF16)
    lane = lax.broadcasted_iota(jnp.int32, (t_n, LANES), 1)
    trow = lax.broadcasted_iota(jnp.int32, (t_n, LANES), 0)

    def gate_prep(g_ref, n_chunks, base_chunk):
        def body(c, _):
            rows = pl.ds(pl.multiple_of(c * t_n, t_n), t_n)
            gcol = g_ref[0, rows, :] + gb_ref[...]
            lf = jax.nn.log_sigmoid(gcol)
            hi = lf.astype(BF16)
            lo = (lf - hi.astype(F32)).astype(BF16)
            pre = _dot(tri_f, hi) + _dot(tri_f, lo)
            suf = _dot(tri_b, hi) + _dot(tri_b, lo)
            bsum = jnp.where(lane < 2 * ML_HEADS, pre, suf)
            rcol = gcol - pltpu.roll(bsum, LANES - ML_HEADS, 1)
            pmax, smax = rcol, rcol
            step = 1
            while step < t_n:
                pmax = jnp.maximum(pmax, jnp.where(trow >= step, pltpu.roll(pmax, step, 0), NEG_BIG))
                smax = jnp.maximum(smax, jnp.where(trow < t_n - step, pltpu.roll(smax, t_n - step, 0), NEG_BIG))
                step *= 2
            bcol_s[base_chunk + c] = bsum
            mcol_s[base_chunk + c] = jnp.where(lane < 2 * ML_HEADS, pmax, smax)
            rrow_s[base_chunk + c] = rcol.T[0:4 * ML_HEADS, :]
            return 0
        lax.fori_loop(0, n_chunks, body, 0, unroll=4 if n_chunks % 4 == 0 else (2 if n_chunks % 2 == 0 else 1))

    gate_prep(gc_ref, nc_ctx, 0)
    gate_prep(gl_ref, nc_lat, nc_ctx)

    def one_dir(c, h, d, m_in):
        cols = slice(h * LANES, (h + 1) * LANES)
        xcols = slice(2 * h * LANES, (2 * h + 2) * LANES)
        li = 2 * d * ML_HEADS + h
        mask = (si <= ti) if d == 0 else (si >= ti)
        last = t_n - 1 if d == 0 else 0
        rows = pl.ds(pl.multiple_of(c * t_n, t_n), t_n)
        q = q_s[rows, cols]
        kt = kt_s[c, cols, :]
        vx = vx_s[rows, xcols]
        r_row = rrow_s[c][li:li + 1, :]
        run_max = jnp.broadcast_to(mcol_s[c][:, li:li + 1], (t_n, LANES))
        b_rep = jnp.broadcast_to(bcol_s[c][:, li + ML_HEADS:li + ML_HEADS + 1], (t_n, LANES))
        mm = jnp.maximum(m_in, run_max)
        dmat = jnp.exp(jnp.where(mask, r_row - mm, NEG_BIG))
        sm = (dmat * _dot(q, kt)).astype(BF16)
        intra = _dot(sm, vx)
        cross = _dot(q, st_s[2 * h + d].astype(BF16))
        inter = jnp.exp(m_in - mm)
        num = intra[:, :LANES] + inter * cross[:, :LANES]
        den = intra[:, LANES:] + inter * cross[:, LANES:]
        h_s[rows, cols] = h_s[rows, cols] + num / jnp.maximum(jnp.abs(den), jnp.exp(-b_rep - mm))
        r_top = run_max[last:last + 1, :]
        b_last = b_rep[last:last + 1, :]
        g_max = b_last + r_top
        m_new = jnp.maximum(b_last + m_in, g_max)
        dec = jnp.exp(b_last + m_in - m_new)
        inj = jnp.exp(g_max - m_new)
        kw = (kt.astype(F32) * (jnp.exp(r_row - r_top) * inj)).astype(BF16)
        inc = _dot(kw, vx)
        st_s[2 * h + d] = jnp.concatenate([dec, dec], axis=1) * st_s[2 * h + d] + inc
        return m_new

    st_s[...] = jnp.zeros_like(st_s)
    h_s[...] = jnp.zeros_like(h_s)

    def body(i, carry):
        cb = jnp.where(i < nc_ctx, nc_ctx - 1 - i, nc + nc_ctx - 1 - i)
        return tuple(one_dir(i if d == 0 else cb, h, d, carry[2 * h + d])
                     for h in range(ML_HEADS) for d in range(2))

    zero = jnp.zeros((1, LANES), F32)
    lax.fori_loop(0, nc, body, (zero,) * (2 * ML_HEADS), unroll=2 if nc % 2 == 0 else 1)

    mc_ref[0] = h_s[0:n_ctx_rows, :].astype(BF16)
    ml_ref[0] = h_s[n_ctx_rows:n_ctx_rows + n_lat_rows, :].astype(BF16)


def _mlstm_mix(p_ctx, p_lat, g_ctx, g_lat, conv_w, conv_b, wq, wk, wv, gate_b, cast_items):
    bn, lc, pw = p_ctx.shape
    cast = _CastAlong(cast_items, (bn,))
    ll = p_lat.shape[1]
    width = ML_HEADS * LANES
    lt = lc + ll
    nct = lt // ML_CHUNK
    gb = jnp.zeros((1, LANES), F32).at[0, :4 * ML_HEADS].set(gate_b.astype(F32))
    full2 = lambda a: pl.BlockSpec(a.shape, lambda b: (0,) * a.ndim)
    conv_b2 = conv_b.reshape(1, width)
    assert ML_CHUNK == LANES
    wq, wkt, wv = wq.astype(BF16), jnp.swapaxes(wk, 1, 2).astype(BF16), wv.astype(BF16)
    outs = pl.pallas_call(
        cast.wrap(functools.partial(_mlstm_kernel, n_ctx_rows=lc, n_lat_rows=ll), n_in=10, n_out=2),
        grid=(bn,),
        in_specs=[
            pl.BlockSpec((1, lc, pw), lambda b: (b, 0, 0)),
            pl.BlockSpec((1, ll, pw), lambda b: (b, 0, 0)),
            pl.BlockSpec((1, lc, LANES), lambda b: (b, 0, 0)),
            pl.BlockSpec((1, ll, LANES), lambda b: (b, 0, 0)),
            full2(conv_w), full2(conv_b2), full2(wq), full2(wkt), full2(wv), full2(gb),
        ] + cast.in_specs(),
        out_specs=[
            pl.BlockSpec((1, lc, width), lambda b: (b, 0, 0)),
            pl.BlockSpec((1, ll, width), lambda b: (b, 0, 0)),
        ] + cast.out_specs(),
        out_shape=[jax.ShapeDtypeStruct((bn, lc, width), BF16), jax.ShapeDtypeStruct((bn, ll, width), BF16)]
        + cast.out_shapes(),
        scratch_shapes=[
            pltpu.VMEM((lt, width), BF16),
            pltpu.VMEM((nct, width, ML_CHUNK), BF16),
            pltpu.VMEM((lt, 2 * width), BF16),
            pltpu.VMEM((lt, width), F32),
            pltpu.VMEM((2 * ML_HEADS, LANES, 2 * LANES), F32),
            pltpu.VMEM((nct, ML_CHUNK, LANES), F32), pltpu.VMEM((nct, ML_CHUNK, LANES), F32),
            pltpu.VMEM((nct, 4 * ML_HEADS, ML_CHUNK), F32),
            pltpu.VMEM((max(lc, ll) + 2 * SUBLANES, LANES), F32),
        ],
        compiler_params=_cparams(("arbitrary",)),
        name="mlstm_mix",
    )(p_ctx, p_lat, g_ctx, g_lat, conv_w, conv_b2, wq, wkt, wv, gb, *cast.operands())
    return outs[0], outs[1], outs[2:]


def _ffn_tail(x, y, gate_mix, g2, shift, scale, gate_ffn, w1_ref, w3_ref, w2_ref, hidden_chunk):
    x1 = x + gate_mix * y
    h2 = _norm_mod(x1, g2, shift, scale).astype(BF16)
    hidden = w1_ref.shape[1]
    acc = jnp.zeros_like(x1)
    start = 0
    while start < hidden:
        cs = slice(start, min(start + hidden_chunk, hidden))
        z = (_silu(_dot(h2, w1_ref[:, cs])) * _dot(h2, w3_ref[:, cs])).astype(BF16)
        acc = acc + _dot(z, w2_ref[cs, :])
        start += hidden_chunk
    return x1 + gate_ffn * acc


def _final_norm(x, g):
    return x * lax.rsqrt(jnp.mean(x * x, axis=-1, keepdims=True) + EPS) * g


def _outproj_even_kernel(x_ref, ys_ref, m_ref, og_ref, gm_ref, sh_ref, sc_ref, gf_ref, g2_ref, gw_ref, gbias_ref,
                         ng_ref, wo_ref, w1_ref, w3_ref, w2_ref, o_ref, *, hidden_chunk, heads):
    ys = jax.nn.gelu(ys_ref[0])
    s = ys * jax.nn.sigmoid(_dot(ys.astype(BF16), gw_ref[...]) + gbias_ref[...])
    dh = m_ref.shape[-1] // heads
    gated = []
    for h in range(heads):
        cols = slice(h * dh, (h + 1) * dh)
        hh = m_ref[0, :, cols].astype(F32)
        mu = jnp.mean(hh, axis=1, keepdims=True)
        cen = hh - mu
        var = jnp.mean(cen * cen, axis=1, keepdims=True)
        normed = cen * lax.rsqrt(var + EPS) * ng_ref[:, cols]
        gated.append((jax.nn.sigmoid(og_ref[0, :, cols].astype(F32)) * normed).astype(BF16))
    sw = ys_ref.shape[-1]
    y = _dot(s.astype(BF16), wo_ref[0:sw, :]) + _dot(jnp.concatenate(gated, axis=1), wo_ref[sw:, :])
    o_ref[0] = _ffn_tail(x_ref[0], y, gm_ref[0], g2_ref[...], sh_ref[0], sc_ref[0], gf_ref[0],
                         w1_ref, w3_ref, w2_ref, hidden_chunk)


def _outproj_odd_kernel(x_ref, a_ref, gate_ref, gm_ref, sh_ref, sc_ref, gf_ref, g2_ref, ng_ref, wo_ref, w1_ref,
                        w3_ref, w2_ref, fg_ref, o_ref, *, hidden_chunk, heads):
    hv = a_ref.shape[-1] // heads
    y = jnp.zeros(x_ref.shape[1:], F32)
    for h in range(heads):
        cols = slice(h * hv, (h + 1) * hv)
        o = a_ref[0, :, cols].astype(F32)
        mu = jnp.mean(o, axis=1, keepdims=True)
        cen = o - mu
        var = jnp.mean(cen * cen, axis=1, keepdims=True)
        normed = cen * lax.rsqrt(var + EPS) * ng_ref[:, cols]
        gated = (_silu(gate_ref[0, :, cols].astype(F32)) * normed).astype(BF16)
        y = y + _dot(gated, wo_ref[cols, :])
    out = _ffn_tail(x_ref[0], y, gm_ref[0], g2_ref[...], sh_ref[0], sc_ref[0], gf_ref[0],
                    w1_ref, w3_ref, w2_ref, hidden_chunk)
    o_ref[0] = _final_norm(out, fg_ref[...])


MXU_WIDTH = 256


def _hidden_chunk(hidden):
    return -(-hidden // (2 * MXU_WIDTH)) * MXU_WIDTH


def _mod_specs(d, row_of_batch, slots):
    return [pl.BlockSpec((1, 1, d), lambda b, i, s=s: (row_of_batch(b) * N_MOD + s, 0, 0)) for s in slots]


def _outproj_even(x, ys, m, p, mods, row_of_batch, g2, glu_w, glu_b, norm_g, w_out, ffn, cast_items=()):
    bn, ln, d = x.shape
    sw, mw = ys.shape[-1], m.shape[-1]
    tm = _token_tile(ln)
    grid = (bn, ln // tm)
    cast = _CastAlong(cast_items, grid)
    res = lambda a: _resident(a.shape, lambda b, i: (0,) * a.ndim)
    tok = lambda w: pl.BlockSpec((1, tm, w), lambda b, i: (b, i, 0))
    row = lambda w: pl.BlockSpec((1, w), lambda b, i: (0, 0))
    kern = functools.partial(_outproj_even_kernel, hidden_chunk=_hidden_chunk(ffn[0].shape[-1]), heads=ML_HEADS)
    outs = pl.pallas_call(
        cast.wrap(kern, n_in=16, n_out=1),
        grid=grid,
        in_specs=[tok(d), tok(sw), tok(mw), pl.BlockSpec((1, tm, mw), lambda b, i: (b, i, 1))]
        + _mod_specs(d, row_of_batch, (2, 3, 4, 5))
        + [row(d), res(glu_w), row(sw), row(mw), res(w_out)] + [res(w) for w in ffn] + cast.in_specs(),
        out_specs=[tok(d)] + cast.out_specs(),
        out_shape=[jax.ShapeDtypeStruct((bn, ln, d), F32)] + cast.out_shapes(),
        compiler_params=_cparams(("arbitrary", "arbitrary")),
        name="outproj_ffn_even",
    )(x, ys, m, p, mods, mods, mods, mods, g2, glu_w, glu_b.reshape(1, sw), norm_g.reshape(1, mw), w_out, *ffn,
      *cast.operands())
    return outs[0], outs[1:]


def _outproj_odd(x, a, gate, mods, row_of_batch, g2, norm_g, w_out, ffn, final_g):
    bn, ln, d = x.shape
    vw = a.shape[-1]
    tm = _token_tile(ln)
    res = lambda a_: _resident(a_.shape, lambda b, i: (0,) * a_.ndim)
    tok = lambda w: pl.BlockSpec((1, tm, w), lambda b, i: (b, i, 0))
    row = lambda w: pl.BlockSpec((1, w), lambda b, i: (0, 0))
    return pl.pallas_call(
        functools.partial(_outproj_odd_kernel, hidden_chunk=_hidden_chunk(ffn[0].shape[-1]), heads=RET_HEADS),
        grid=(bn, ln // tm),
        in_specs=[tok(d), tok(vw), tok(vw)] + _mod_specs(d, row_of_batch, (2, 3, 4, 5))
        + [row(d), row(vw), res(w_out)] + [res(w) for w in ffn] + [row(d)],
        out_specs=tok(d),
        out_shape=jax.ShapeDtypeStruct((bn, ln, d), F32),
        compiler_params=_cparams(("arbitrary", "arbitrary")),
        name="outproj_ffn_odd",
    )(x, a, gate, mods, mods, mods, mods, g2, norm_g.reshape(1, vw), w_out, *ffn, final_g)


def _rope_heads(t, cos, sin, heads, hk):
    half = hk // 2
    out = []
    for h in range(heads):
        t1 = t[:, h * hk:h * hk + half]
        t2 = t[:, h * hk + half:(h + 1) * hk]
        out.append(t1 * cos - t2 * sin)
        out.append(t2 * cos + t1 * sin)
    return jnp.concatenate(out, axis=-1)


def _inproj_odd_kernel(x_ref, sh_ref, sc_ref, g_ref, cos_ref, sin_ref, w_ref, q_ref, k_ref, v_ref, gate_ref,
                       *, qk, vw, heads):
    h = _norm_mod(x_ref[0], g_ref[...], sh_ref[0], sc_ref[0]).astype(BF16)
    hk = qk // heads
    cos, sin = cos_ref[...], sin_ref[...]
    q = _rope_heads(_dot(h, w_ref[:, 0:qk]), cos, sin, heads, hk)
    q_ref[0] = (q * (hk ** -0.5)).astype(BF16)
    k = _rope_heads(_dot(h, w_ref[:, qk:2 * qk]), cos, sin, heads, hk)
    k_ref[0] = k.astype(BF16)
    v_ref[0] = _dot(h, w_ref[:, 2 * qk:2 * qk + vw]).astype(BF16)
    gate_ref[0] = _dot(h, w_ref[:, 2 * qk + vw:2 * qk + 2 * vw]).astype(BF16)


def _inproj_odd_ctx_kernel(x_ref, sh_ref, sc_ref, g_ref, w_ref, k_ref, v_ref, *, qk, vw):
    h = _norm_mod(x_ref[0], g_ref[...], sh_ref[0], sc_ref[0]).astype(BF16)
    k_ref[0] = _dot(h, w_ref[:, qk:2 * qk]).astype(BF16)
    v_ref[0] = _dot(h, w_ref[:, 2 * qk:2 * qk + vw]).astype(BF16)


def _inproj_odd(x, mods, row_of_batch, g, cos, sin, w, qk, vw, cast_items=()):
    bn, ln, d = x.shape
    tm = _token_tile(ln)
    grid = (bn, ln // tm)
    cast = _CastAlong(cast_items, grid)
    tok = lambda w_: pl.BlockSpec((1, tm, w_), lambda b, i: (b, i, 0))
    half = cos.shape[1]
    outs = pl.pallas_call(
        cast.wrap(functools.partial(_inproj_odd_kernel, qk=qk, vw=vw, heads=RET_HEADS), n_in=7, n_out=4),
        grid=grid,
        in_specs=[tok(d)] + _mod_specs(d, row_of_batch, (0, 1))
        + [pl.BlockSpec((1, d), lambda b, i: (0, 0)),
           pl.BlockSpec((tm, half), lambda b, i: (i, 0)), pl.BlockSpec((tm, half), lambda b, i: (i, 0)),
           _resident(w.shape, lambda b, i: (0, 0))] + cast.in_specs(),
        out_specs=[tok(qk), tok(qk), tok(vw), tok(vw)] + cast.out_specs(),
        out_shape=[jax.ShapeDtypeStruct((bn, ln, qk), BF16), jax.ShapeDtypeStruct((bn, ln, qk), BF16),
                   jax.ShapeDtypeStruct((bn, ln, vw), BF16), jax.ShapeDtypeStruct((bn, ln, vw), BF16)]
        + cast.out_shapes(),
        compiler_params=_cparams(("arbitrary", "arbitrary")),
        name="inproj_odd",
    )(x, mods, mods, g, cos, sin, w, *cast.operands())
    return outs[:4], outs[4:]


def _inproj_odd_ctx(x, mods, row_of_batch, g, w_kv, qk, vw):
    bn, ln, d = x.shape
    tm = _token_tile(ln)
    tok = lambda w_: pl.BlockSpec((1, tm, w_), lambda b, i: (b, i, 0))
    return pl.pallas_call(
        functools.partial(_inproj_odd_ctx_kernel, qk=qk, vw=vw),
        grid=(bn, ln // tm),
        in_specs=[tok(d)] + _mod_specs(d, row_of_batch, (0, 1))
        + [pl.BlockSpec((1, d), lambda b, i: (0, 0)), _resident(w_kv.shape, lambda b, i: (0, 0))],
        out_specs=[tok(qk), tok(vw)],
        out_shape=[jax.ShapeDtypeStruct((bn, ln, qk), BF16), jax.ShapeDtypeStruct((bn, ln, vw), BF16)],
        compiler_params=_cparams(("arbitrary", "arbitrary")),
        name="inproj_odd_ctx",
    )(x, mods, mods, g, w_kv)


def _retention_kernel(lg_ref, q_ref, k_ref, v_ref, kc_ref, vc_ref, o_ref, acc_s, sf_s, sb_s,
                      *, chunk, n_ctx_rows, n_lat_rows):
    t_n = chunk
    h = pl.program_id(1)
    lgf = jnp.full((1, 1), lg_ref[0, h], F32)
    lgb = jnp.full((1, 1), lg_ref[1, h], F32)
    nc_ctx, nc_lat = n_ctx_rows // t_n, n_lat_rows // t_n
    ti = lax.broadcasted_iota(jnp.int32, (t_n, t_n), 0)
    si = lax.broadcasted_iota(jnp.int32, (t_n, t_n), 1)
    diff = (ti - si).astype(F32)
    decay = jnp.where(diff >= 0, jnp.exp(lgf * jnp.maximum(diff, 0.0)), jnp.exp(lgb * jnp.maximum(-diff, 0.0)))
    pos = lax.broadcasted_iota(jnp.int32, (t_n, 1), 0).astype(F32)
    inter_f = jnp.exp(lgf * (pos + 1.0))
    inter_b = jnp.exp(lgb * (t_n - pos))
    wend_f = jnp.exp(lgf * (t_n - 1.0 - pos))
    wend_b = jnp.exp(lgb * pos)
    cd_f = jnp.exp(lgf * t_n)
    cd_b = jnp.exp(lgb * t_n)

    def rows_of(c):
        return pl.ds(pl.multiple_of(c * t_n, t_n), t_n)

    def bump(s_ref, k, v, wend, cd):
        kw = (k.astype(F32) * wend).astype(BF16)
        s_ref[...] = cd * s_ref[...] + _dot_tn(kw, v)

    sf_s[...] = jnp.zeros_like(sf_s)
    sb_s[...] = jnp.zeros_like(sb_s)

    def ctx_f(c, _):
        bump(sf_s, kc_ref[0, rows_of(c), :], vc_ref[0, rows_of(c), :], wend_f, cd_f)
        return 0
    lax.fori_loop(0, nc_ctx, ctx_f, 0)

    def ctx_b(i, _):
        c = nc_ctx - 1 - i
        bump(sb_s, kc_ref[0, rows_of(c), :], vc_ref[0, rows_of(c), :], wend_b, cd_b)
        return 0
    lax.fori_loop(0, nc_ctx, ctx_b, 0)

    def forward_part(rows):
        q, k, v = q_ref[0, rows, :], k_ref[0, rows, :], v_ref[0, rows, :]
        scores = (_dot_nt(q, k) * decay).astype(BF16)
        part = _dot(scores, v) + inter_f * _dot(q, sf_s[...].astype(BF16))
        bump(sf_s, k, v, wend_f, cd_f)
        return part

    def backward_part(rows):
        q, k, v = q_ref[0, rows, :], k_ref[0, rows, :], v_ref[0, rows, :]
        part = inter_b * _dot(q, sb_s[...].astype(BF16))
        bump(sb_s, k, v, wend_b, cd_b)
        return part

    def first_half(i, _):
        rf, rb = rows_of(i), rows_of(nc_lat - 1 - i)
        acc_s[rf, :] = forward_part(rf)
        acc_s[rb, :] = backward_part(rb)
        return 0
    pair_unroll = 4 if (nc_lat // 2) % 4 == 0 else 1
    lax.fori_loop(0, nc_lat // 2, first_half, 0, unroll=pair_unroll)

    def second_half(i, _):
        rf, rb = rows_of(i), rows_of(nc_lat - 1 - i)
        o_ref[0, rf, :] = (acc_s[rf, :] + forward_part(rf)).astype(BF16)
        o_ref[0, rb, :] = (acc_s[rb, :] + backward_part(rb)).astype(BF16)
        return 0
    lax.fori_loop(nc_lat // 2, nc_lat, second_half, 0, unroll=pair_unroll)


def _retention_mix(q, k, v, k_ctx, v_ctx, log_gamma):
    bn, ll, qk = q.shape
    lc = k_ctx.shape[1]
    vw = v.shape[-1]
    hk, hv = qk // RET_HEADS, vw // RET_HEADS
    chunk = 256 if (ll % 512 == 0 and lc % 256 == 0) else 128
    assert (ll // chunk) % 2 == 0, "latent chunks are visited in forward/backward pairs"
    head = lambda n_rows, w: pl.BlockSpec((1, n_rows, w), lambda b, h: (b, 0, h))
    return pl.pallas_call(
        functools.partial(_retention_kernel, chunk=chunk, n_ctx_rows=lc, n_lat_rows=ll),
        grid=(bn, RET_HEADS),
        in_specs=[
            pl.BlockSpec(memory_space=pltpu.SMEM),
            head(ll, hk), head(ll, hk), head(ll, hv), head(lc, hk), head(lc, hv),
        ],
        out_specs=head(ll, hv),
        out_shape=jax.ShapeDtypeStruct((bn, ll, vw), BF16),
        scratch_shapes=[pltpu.VMEM((ll, hv), F32), pltpu.VMEM((hk, hv), F32), pltpu.VMEM((hk, hv), F32)],
        compiler_params=_cparams(("arbitrary", "arbitrary")),
        name="retention_mix",
    )(log_gamma.astype(F32), q, k, v, k_ctx, v_ctx)


def _grid_rope(n_pos, hk):
    rows = n_pos // GRID_W
    row = jnp.repeat(jnp.arange(rows, dtype=F32), GRID_W)
    col = jnp.tile(jnp.arange(GRID_W, dtype=F32), rows)
    n_freq = hk // 4
    inv = ROPE_BASE ** (-jnp.arange(n_freq, dtype=F32) / n_freq)
    ang = jnp.concatenate([row[:, None] * inv, col[:, None] * inv], -1)
    return jnp.cos(ang), jnp.sin(ang)


def kernel(x, c, ctx, c_ctx, ada_w, ada_b, norm1_g, norm2_g, ab_w_in, ab_w_out, s5_a_re, s5_a_im, s5_log_dt,
           s5_b_re, s5_b_im, s5_c_re, s5_c_im, s5_d, s5_glu_w, s5_glu_b, ml_conv_w, ml_conv_b, ml_wq, ml_wk,
           ml_wv, ml_gate_b, ml_norm_g, ret_w_in, ret_w_out, ret_log_gamma, ret_norm_g, ffn_w1, ffn_w3, ffn_w2,
           final_g):
    bn, ln, d = x.shape
    depth = ada_w.shape[0]
    assert depth == 2, "one S5 || mLSTM layer followed by one retention layer"
    s5_width = s5_d.shape[-1]
    ml_width = ml_norm_g.shape[-1]
    assert ml_width == ML_HEADS * LANES and s5_width % S5_GROUP == 0

    r_pad = -(-(bn + 1) // SUBLANES) * SUBLANES
    vec = jnp.zeros((r_pad, d), F32).at[:bn].set(c).at[bn].set(c_ctx)
    mods_all = _modulation(vec, ada_w, ada_b).reshape(depth, r_pad * N_MOD, 1, d)
    lat_row = lambda b: b
    ctx_row = lambda b: bn

    mods = mods_all[0]
    g1 = norm1_g[0].reshape(1, d)
    g2 = norm2_g[0].reshape(1, d)
    w_in = ab_w_in[0]
    n_gate = w_in.shape[1] - s5_width - 2 * ml_width
    w_in = jnp.pad(w_in, ((0, 0), (0, LANES - n_gate))).astype(BF16)
    u_lat, p_lat, g_lat = _inproj_even(x, mods, lat_row, g1, w_in, s5_width)
    lc = ctx.shape[1]
    flat = lambda t: t.reshape(1, bn * lc, t.shape[-1])
    unflat = lambda t: t.reshape(bn, lc, t.shape[-1])
    u_ctx, p_ctx, g_ctx = (unflat(t) for t in _inproj_even(flat(ctx), mods, ctx_row, g1, w_in, s5_width))
    mats = _s5_matrices(s5_a_re[0], s5_a_im[0], s5_log_dt[0], s5_b_re[0], s5_b_im[0], s5_c_re[0], s5_c_im[0],
                        s5_d[0])
    ys_ctx, ys_lat = _s5_mix(u_ctx, u_lat, mats)
    m_ctx, m_lat, ffn = _mlstm_mix(p_ctx, p_lat, g_ctx, g_lat, ml_conv_w[0], ml_conv_b[0], ml_wq[0], ml_wk[0],
                                   ml_wv[0], ml_gate_b[0], [(ffn_w1, 0), (ffn_w3, 0), (ffn_w2, 0)])
    glu_w = s5_glu_w[0].astype(BF16)
    w_out = ab_w_out[0].astype(BF16)
    x, (ret_in,) = _outproj_even(x, ys_lat, m_lat, p_lat, mods, lat_row, g2, glu_w, s5_glu_b[0], ml_norm_g[0],
                                 w_out, ffn, [(ret_w_in, 0)])
    ctx, _ = _outproj_even(flat(ctx), flat(ys_ctx), flat(m_ctx), flat(p_ctx), mods, ctx_row, g2, glu_w,
                           s5_glu_b[0], ml_norm_g[0], w_out, ffn)

    mods = mods_all[1]
    g1 = norm1_g[1].reshape(1, d)
    g2 = norm2_g[1].reshape(1, d)
    vw = ret_norm_g.shape[-1]
    qk = (ret_w_in.shape[-1] - 2 * vw) // 2
    cos, sin = _grid_rope(ln, qk // RET_HEADS)
    (q, k, v, gate), ffn = _inproj_odd(x, mods, lat_row, g1, cos, sin, ret_in, qk, vw,
                                       [(ffn_w1, 1), (ffn_w3, 1), (ffn_w2, 1)])
    k_ctx, v_ctx = (unflat(t) for t in _inproj_odd_ctx(ctx, mods, ctx_row, g1, ret_in, qk, vw))
    a = _retention_mix(q, k, v, k_ctx, v_ctx, ret_log_gamma[0])
    w_out = ret_w_out[0].astype(BF16)
    return _outproj_odd(x, a, gate, mods, lat_row, g2, ret_norm_g[0], w_out, ffn, final_g.reshape(1, d))
```

```python
import functools
import math

import jax
import jax.numpy as jnp
from jax import lax
from jax.experimental import pallas as pl
from jax.experimental.pallas import tpu as pltpu

F32 = jnp.float32
BF16 = jnp.bfloat16

EPS = 1e-6
N_MOD = 6
GRID_W = 64
ROPE_BASE = 10000.0

S5_GROUP = 16
S5_STATE = 64
S5_STEP = 16
ML_HEADS = 4
ML_CHUNK = 128
RET_HEADS = 4
SUBLANES = 8
LANES = 128
NEG_BIG = -1e30
VMEM_LIMIT = 56 * 1024 * 1024


def _cparams(sem):
    return pltpu.CompilerParams(dimension_semantics=sem, vmem_limit_bytes=VMEM_LIMIT)


def _resident(shape, index_map):
    return pl.BlockSpec(shape, index_map, pipeline_mode=pl.Buffered(1))


BF16_ROWS = 16


class _CastAlong:
    def __init__(self, weights_and_layers, grid):
        self.items = list(weights_and_layers)
        steps = math.prod(grid)
        strides = [math.prod(grid[i + 1:]) for i in range(len(grid))]
        self.step = lambda idx: sum(i * s for i, s in zip(idx, strides))
        self.plans = []
        for w, _ in self.items:
            _, r, c = w.shape
            if r % (steps * BF16_ROWS) == 0:
                self.plans.append(((r // steps, c), lambda s: (s, 0)))
            else:
                assert steps % 2 == 0 and r % (steps // 2 * BF16_ROWS) == 0 and c % (2 * LANES) == 0
                self.plans.append(((2 * r // steps, c // 2), lambda s: (s // 2, s % 2)))

    def __len__(self):
        return len(self.items)

    def in_specs(self):
        return [pl.BlockSpec((None,) + blk, lambda *idx, f=f, l=l: (l,) + f(self.step(idx)))
                for (blk, f), (_, l) in zip(self.plans, self.items)]

    def out_specs(self):
        return [pl.BlockSpec(blk, lambda *idx, f=f: f(self.step(idx))) for blk, f in self.plans]

    def out_shapes(self):
        return [jax.ShapeDtypeStruct(w.shape[1:], BF16) for w, _ in self.items]

    def operands(self):
        return [w for w, _ in self.items]

    def wrap(self, kernel_fn, n_in, n_out):
        n = len(self)

        def wrapped(*refs):
            ins, cast_in = refs[:n_in], refs[n_in:n_in + n]
            outs, cast_out = refs[n_in + n:n_in + n + n_out], refs[n_in + n + n_out:n_in + 2 * n + n_out]
            for src, dst in zip(cast_in, cast_out):
                dst[...] = src[...].astype(BF16)
            kernel_fn(*ins, *outs, *refs[n_in + 2 * n + n_out:])
        return wrapped


def _token_tile(n):
    for t in (512, 256, 128):
        if n % t == 0:
            return t
    raise ValueError(f"sequence length {n} must be a multiple of 128")


def _silu(v):
    return v * jax.nn.sigmoid(v)


def _norm_mod(x, g, shift, scale):
    y = x * lax.rsqrt(jnp.mean(x * x, axis=-1, keepdims=True) + EPS)
    return (y * g) * (1.0 + scale) + shift


def _dot(a, b):
    return jnp.dot(a, b, preferred_element_type=F32)


def _dot_nt(a, b):
    return lax.dot_general(a, b, (((1,), (1,)), ((), ())), preferred_element_type=F32)


def _dot_tn(a, b):
    return lax.dot_general(a, b, (((0,), (0,)), ((), ())), preferred_element_type=F32)


def _mod_kernel(v_ref, w_ref, b_ref, o_ref):
    s = _silu(v_ref[...]).astype(BF16)
    o_ref[0] = _dot(s, w_ref[0].astype(BF16)) + b_ref[0]


def _modulation(vec, ada_w, ada_b):
    depth, d, n = ada_w.shape
    r = vec.shape[0]
    tn = 1024
    return pl.pallas_call(
        _mod_kernel,
        grid=(depth, n // tn),
        in_specs=[
            pl.BlockSpec((r, d), lambda l, j: (0, 0)),
            pl.BlockSpec((1, d, tn), lambda l, j: (l, 0, j)),
            pl.BlockSpec((1, 1, tn), lambda l, j: (l, 0, j)),
        ],
        out_specs=pl.BlockSpec((1, r, tn), lambda l, j: (l, 0, j)),
        out_shape=jax.ShapeDtypeStruct((depth, r, n), F32),
        compiler_params=_cparams(("arbitrary", "arbitrary")),
        name="adaln_modulation",
    )(vec, ada_w, ada_b.reshape(depth, 1, n))


def _inproj_even_kernel(x_ref, sh_ref, sc_ref, g_ref, w_ref, u_ref, p_ref, gate_ref, *, s5_width):
    h = _norm_mod(x_ref[0], g_ref[...], sh_ref[0], sc_ref[0]).astype(BF16)
    p = _dot(h, w_ref[...])
    n_mid = p_ref.shape[-1]
    u_ref[0] = p[:, :s5_width]
    p_ref[0] = p[:, s5_width:s5_width + n_mid].astype(BF16)
    gate_ref[0] = p[:, s5_width + n_mid:]


def _inproj_even(x, mods, row_of_batch, g, w, s5_width):
    bn, ln, d = x.shape
    n = w.shape[1]
    n_mid = n - s5_width - LANES
    tm = _token_tile(ln)
    return pl.pallas_call(
        functools.partial(_inproj_even_kernel, s5_width=s5_width),
        grid=(bn, ln // tm),
        in_specs=[
            pl.BlockSpec((1, tm, d), lambda b, i: (b, i, 0)),
            pl.BlockSpec((1, 1, d), lambda b, i: (row_of_batch(b) * N_MOD + 0, 0, 0)),
            pl.BlockSpec((1, 1, d), lambda b, i: (row_of_batch(b) * N_MOD + 1, 0, 0)),
            pl.BlockSpec((1, d), lambda b, i: (0, 0)),
            _resident((d, n), lambda b, i: (0, 0)),
        ],
        out_specs=[
            pl.BlockSpec((1, tm, s5_width), lambda b, i: (b, i, 0)),
            pl.BlockSpec((1, tm, n_mid), lambda b, i: (b, i, 0)),
            pl.BlockSpec((1, tm, LANES), lambda b, i: (b, i, 0)),
        ],
        out_shape=[
            jax.ShapeDtypeStruct((bn, ln, s5_width), F32),
            jax.ShapeDtypeStruct((bn, ln, n_mid), BF16),
            jax.ShapeDtypeStruct((bn, ln, LANES), F32),
        ],
        compiler_params=_cparams(("arbitrary", "arbitrary")),
        name="inproj_even",
    )(x, mods, mods, g, w)


def _s5_prep_kernel(sp_ref, bt_ref, c_ref, d_ref, toep_ref, endw_ref, outw_ref, a0_ref, a1_ref, *, steps, chans):
    k, p = steps, S5_STATE
    width = k * chans
    lane = lax.broadcasted_iota(jnp.int32, (1, LANES), 1)
    low = lane < p
    lag = lax.broadcasted_iota(jnp.int32, (k + SUBLANES, LANES), 0).astype(F32)
    row = lax.broadcasted_iota(jnp.int32, (chans, LANES), 0)
    lane_c = lax.broadcasted_iota(jnp.int32, (chans, LANES), 1)
    lag0 = (k - 1) * chans
    diag = lane_c == row + lag0 % LANES
    pair = lambda x, y: x + pltpu.roll(y, p, 1)
    zeros = jnp.zeros((chans, LANES), F32)

    for n in range(sp_ref.shape[0]):
        wr, wi, pr, pi, cre, cim = [], [], [], [], [], []
        for d in range(2):
            a_re, a_im = sp_ref[n, 4 * d:4 * d + 1, :], sp_ref[n, 4 * d + 1:4 * d + 2, :]
            l_re, l_im = sp_ref[n, 4 * d + 2:4 * d + 3, :], sp_ref[n, 4 * d + 3:4 * d + 4, :]
            mag = jnp.exp(lag * l_re)
            p_re, p_im = mag * jnp.cos(lag * l_im), mag * jnp.sin(lag * l_im)
            den = jnp.where(low, a_re * a_re + a_im * a_im, 1.0)
            nr = p_re[1:2] - 1.0
            coef_re = (nr * a_re + p_im[1:2] * a_im) / den
            coef_im = (p_im[1:2] * a_re - nr * a_im) / den
            b_re, b_im = bt_ref[n, 2 * d * chans:(2 * d + 1) * chans, :], bt_ref[n, (2 * d + 1) * chans:(2 * d + 2) * chans, :]
            bb_re = coef_re * b_re - coef_im * b_im
            bb_im = coef_re * b_im + coef_im * b_re
            wr.append([p_re[l:l + 1] * bb_re - p_im[l:l + 1] * bb_im for l in range(k)])
            wi.append([p_re[l:l + 1] * bb_im + p_im[l:l + 1] * bb_re for l in range(k)])
            pr.append(p_re)
            pi.append(p_im)
            cre.append(c_ref[n, 2 * d * chans:(2 * d + 1) * chans, :])
            cim.append(c_ref[n, (2 * d + 1) * chans:(2 * d + 2) * chans, :])

        for s in range(k):
            endw_ref[n, s * chans:(s + 1) * chans, :] = jnp.concatenate(
                [pair(wr[0][k - 1 - s], wr[1][s]), pair(wi[0][k - 1 - s], wi[1][s])], axis=1).astype(BF16)
        for t in range(k):
            f_re, f_im = pr[0][t + 1:t + 2], pi[0][t + 1:t + 2]
            b_re, b_im = pr[1][k - t:k - t + 1], pi[1][k - t:k - t + 1]
            outw_ref[n, t * chans:(t + 1) * chans, :] = jnp.concatenate(
                [pair(cre[0] * f_re - cim[0] * f_im, cre[1] * b_re - cim[1] * b_im),
                 pair(-(cre[0] * f_im + cim[0] * f_re), -(cre[1] * b_im + cim[1] * b_re))], axis=1).astype(BF16)
        a0_ref[n:n + 1, :] = jnp.where(low, pr[0][k:k + 1], pltpu.roll(pr[1][k:k + 1], p, 1))
        a1_ref[n:n + 1, :] = jnp.where(low, pi[0][k:k + 1], pltpu.roll(pi[1][k:k + 1], p, 1))

        blocks = []
        for m in range(2 * k):
            fwd = pair(wr[0][k - 1 - m], -wi[0][k - 1 - m]) if m < k else zeros
            bwd = pair(wr[1][m - k + 1], -wi[1][m - k + 1]) if k - 1 <= m < 2 * k - 1 else zeros
            blocks.append(jnp.concatenate([fwd, bwd], axis=1))
        w_gen = jnp.concatenate(blocks, axis=0)
        c2 = jnp.concatenate([pair(cre[0], cim[0]), pair(cre[1], cim[1])], axis=1)
        gen = lax.dot_general(c2, w_gen, (((1,), (1,)), ((), ())),
                              precision=lax.Precision.HIGHEST, preferred_element_type=F32)
        pieces = [gen[:, j * LANES:(j + 1) * LANES] for j in range(2 * width // LANES)]
        pieces[lag0 // LANES] = pieces[lag0 // LANES] + jnp.where(diag, d_ref[n], 0.0)
        gen = jnp.concatenate(pieces, axis=1)
        for t in range(k):
            off = (k - 1 - t) * chans
            toep_ref[n, t * chans:(t + 1) * chans, :] = gen[:, off:off + width].astype(BF16)


def _s5_prep(a_re, a_im, log_dt, b_re, b_im, c_re, c_im, d_skip):
    k = S5_STEP
    g_n, p_n, c_n = a_re.shape[1], a_re.shape[2], b_re.shape[-1]
    assert 2 * p_n == LANES
    lanes = lambda t: jnp.pad(t.astype(F32), [(0, 0)] * (t.ndim - 1) + [(0, LANES - p_n)])
    dt = jnp.exp(log_dt.astype(F32))[..., None]
    a_re, a_im = a_re.astype(F32), a_im.astype(F32)
    sp = lanes(jnp.stack([a_re[0], a_im[0], a_re[0] * dt[0], a_im[0] * dt[0],
                          a_re[1], a_im[1], a_re[1] * dt[1], a_im[1] * dt[1]], axis=1))
    tr = lambda t: jnp.swapaxes(t, -1, -2)
    bt = lanes(jnp.concatenate([tr(b_re[0]), tr(b_im[0]), tr(b_re[1]), tr(b_im[1])], axis=1))
    cc = lanes(jnp.concatenate([c_re[0], c_im[0], c_re[1], c_im[1]], axis=1))
    d_rep = jnp.broadcast_to(d_skip.astype(F32).reshape(g_n, c_n, 1), (g_n, c_n, LANES))
    nb = 8
    blk = lambda r, c: pl.BlockSpec((nb, r, c), lambda i: (i, 0, 0))
    kc = k * c_n
    mat = jax.ShapeDtypeStruct((g_n, kc, kc), BF16)
    vec = jax.ShapeDtypeStruct((g_n, LANES), F32)
    return pl.pallas_call(
        functools.partial(_s5_prep_kernel, steps=k, chans=c_n),
        grid=(g_n // nb,),
        in_specs=[blk(8, LANES), blk(4 * c_n, LANES), blk(4 * c_n, LANES), blk(c_n, LANES)],
        out_specs=[blk(kc, kc), blk(kc, kc), blk(kc, kc),
                   pl.BlockSpec((nb, LANES), lambda i: (i, 0)), pl.BlockSpec((nb, LANES), lambda i: (i, 0))],
        out_shape=[mat, mat, mat, vec, vec],
        compiler_params=_cparams(("arbitrary",)),
        name="s5_prep",
    )(sp, bt, cc, d_rep)


S5_LANE_GROUPS = LANES // S5_GROUP


def _chunk_transpose(xs):
    n_arr = len(xs)
    lane_chunk = lax.broadcasted_iota(jnp.int32, xs[0].shape, 1) // S5_GROUP
    d = 1
    while d < n_arr:
        keep = (lane_chunk & d) == 0
        out = list(xs)
        for i in range(n_arr):
            if i & d:
                continue
            lo, hi = xs[i], xs[i + d]
            out[i] = jnp.where(keep, lo, pltpu.roll(hi, d * S5_GROUP, 1))
            out[i + d] = jnp.where(keep, pltpu.roll(lo, LANES - d * S5_GROUP, 1), hi)
        xs = out
        d *= 2
    return xs


def _s5_kernel(uc_ref, ul_ref, toep_ref, endw_ref, outw_ref, a0_ref, a1_ref, yc_ref, yl_ref,
               uc_s, ul_s, e_s, yc_s, yl_s, *, batch, n_ctx, n_lat):
    p = S5_STATE
    gl = S5_LANE_GROUPS
    k = S5_STEP

    def stack(u_ref, us_ref, n_blk):
        for b in range(batch):
            for half in range(k // gl):
                xs = [u_ref[b, pl.ds(half * gl + s, n_blk, stride=k), :] for s in range(gl)]
                ys = _chunk_transpose(xs)
                for g in range(gl):
                    us_ref[g, b * n_blk:(b + 1) * n_blk, half * LANES:(half + 1) * LANES] = ys[g].astype(BF16)

    stack(uc_ref, uc_s, n_ctx)
    stack(ul_ref, ul_s, n_lat)

    def increments(us_ref, n_blk, first_blk):
        for g in range(gl):
            e = _dot(us_ref[g], endw_ref[g])
            for b in range(batch):
                for part in range(2):
                    e_s[part, b, pl.ds(first_blk * gl + g, n_blk, stride=gl), :] = (
                        e[b * n_blk:(b + 1) * n_blk, part * LANES:(part + 1) * LANES])

    increments(uc_s, n_ctx, 0)
    increments(ul_s, n_lat, n_ctx)

    a0 = a0_ref[...]
    a1 = a1_ref[...]
    fwd_lane = lax.broadcasted_iota(jnp.int32, (batch, gl, 2 * p), 2) < p

    def scan(first_blk, n, carry):
        def body(i, c):
            s0, s1 = c
            rf = pl.ds(pl.multiple_of((first_blk + i) * gl, gl), gl)
            rb = pl.ds(pl.multiple_of((first_blk + n - 1 - i) * gl, gl), gl)
            e0 = jnp.where(fwd_lane, e_s[0, :, rf, :], e_s[0, :, rb, :])
            e1 = jnp.where(fwd_lane, e_s[1, :, rf, :], e_s[1, :, rb, :])
            e_s[0, :, rf, 0:p] = s0[:, :, 0:p]
            e_s[0, :, rb, p:2 * p] = s0[:, :, p:2 * p]
            e_s[1, :, rf, 0:p] = s1[:, :, 0:p]
            e_s[1, :, rb, p:2 * p] = s1[:, :, p:2 * p]
            return a0 * s0 - a1 * s1 + e0, a0 * s1 + a1 * s0 + e1
        return lax.fori_loop(0, n, body, carry)

    zero = jnp.zeros((batch, gl, 2 * p), F32)
    carry = scan(0, n_ctx, (zero, zero))
    scan(n_ctx, n_lat, carry)

    def outputs(us_ref, ys_ref, y_ref, n_blk, first_blk):
        for g in range(gl):
            state = jnp.concatenate(
                [jnp.concatenate([e_s[part, b, pl.ds(first_blk * gl + g, n_blk, stride=gl), :]
                                  for b in range(batch)], axis=0) for part in range(2)], axis=1)
            ys_ref[g] = _dot_nt(us_ref[g], toep_ref[g]) + _dot_nt(state.astype(BF16), outw_ref[g])
        for b in range(batch):
            for half in range(k // gl):
                ys = [ys_ref[g, b * n_blk:(b + 1) * n_blk, half * LANES:(half + 1) * LANES] for g in range(gl)]
                xs = _chunk_transpose(ys)
                for s in range(gl):
                    y_ref[b, pl.ds(half * gl + s, n_blk, stride=k), :] = xs[s]

    outputs(uc_s, yc_s, yc_ref, n_ctx, 0)
    outputs(ul_s, yl_s, yl_ref, n_lat, n_ctx)


def _s5_mix(u_ctx, u_lat, mats):
    toep, endw, outw, a0, a1 = mats
    bn, lc, w = u_ctx.shape
    ll = u_lat.shape[1]
    gl = S5_LANE_GROUPS
    kc = S5_STEP * S5_GROUP
    n_ctx, n_lat = lc // S5_STEP, ll // S5_STEP
    bb = 4 if bn % 4 == 0 else bn
    seq = lambda n_rows: pl.BlockSpec((bb, n_rows, LANES), lambda i, j: (j, 0, i))
    grp = lambda r, c: pl.BlockSpec((gl, r, c), lambda i, j: (i, 0, 0))
    return pl.pallas_call(
        functools.partial(_s5_kernel, batch=bb, n_ctx=n_ctx, n_lat=n_lat),
        grid=(w // LANES, bn // bb),
        in_specs=[seq(lc), seq(ll), grp(kc, kc), grp(kc, kc), grp(kc, kc),
                  pl.BlockSpec((gl, 2 * S5_STATE), lambda i, j: (i, 0)),
                  pl.BlockSpec((gl, 2 * S5_STATE), lambda i, j: (i, 0))],
        out_specs=[seq(lc), seq(ll)],
        out_shape=[jax.ShapeDtypeStruct((bn, lc, w), F32), jax.ShapeDtypeStruct((bn, ll, w), F32)],
        scratch_shapes=[
            pltpu.VMEM((gl, bb * n_ctx, kc), BF16), pltpu.VMEM((gl, bb * n_lat, kc), BF16),
            pltpu.VMEM((2, bb, (n_ctx + n_lat) * gl, LANES), F32),
            pltpu.VMEM((gl, bb * n_ctx, kc), F32), pltpu.VMEM((gl, bb * n_lat, kc), F32),
        ],
        compiler_params=_cparams(("arbitrary", "arbitrary")),
        name="s5_scan",
    )(u_ctx, u_lat, toep, endw, outw, a0, a1)


def _mlstm_kernel(pc_ref, pl_ref, gc_ref, gl_ref, cw_ref, cb_ref, wq_ref, wkt_ref, wv_ref, gb_ref, mc_ref, ml_ref,
                  q_s, kt_s, vx_s, h_s, st_s, bcol_s, mcol_s, rrow_s, xpad_s, *, n_ctx_rows, n_lat_rows):
    t_n = ML_CHUNK
    width = ML_HEADS * LANES
    nc_ctx, nc_lat = n_ctx_rows // t_n, n_lat_rows // t_n
    nc = nc_ctx + nc_lat
    scale_k = LANES ** -0.5

    def project(p_ref, n_rows, base):
        ones = jnp.ones((n_rows, LANES), BF16)
        pad = SUBLANES
        xpad_s[0:pad, :] = jnp.zeros((pad, LANES), F32)
        xpad_s[pad + n_rows:2 * pad + n_rows, :] = jnp.zeros((pad, LANES), F32)
        for h in range(ML_HEADS):
            cols = slice(h * LANES, (h + 1) * LANES)
            xm_lo = p_ref[0, :, cols]
            xm = xm_lo.astype(F32)
            xpad_s[pad:pad + n_rows, :] = xm
            taps = cw_ref[:, cols]
            n_tap = taps.shape[0]
            assert n_tap // 2 <= pad
            acc = jnp.zeros_like(xm) + cb_ref[:, cols]
            for j in range(n_tap):
                d = j - n_tap // 2
                sh = xm if d == 0 else xpad_s[pad + d:pad + d + n_rows, :]
                acc = acc + sh * taps[j:j + 1, :]
            xc = _silu(acc).astype(BF16)
            q_s[base:base + n_rows, cols] = _dot(xc, wq_ref[h]).astype(BF16)
            kt = (_dot_nt(wkt_ref[h], xc) * scale_k).astype(BF16)
            for c in range(n_rows // t_n):
                kt_s[base // t_n + c, cols, :] = kt[:, c * t_n:(c + 1) * t_n]
            vx_s[base:base + n_rows, 2 * h * LANES:(2 * h + 1) * LANES] = (
                _dot(xm_lo, wv_ref[h]).astype(BF16))
            vx_s[base:base + n_rows, (2 * h + 1) * LANES:(2 * h + 2) * LANES] = ones

    project(pc_ref, n_ctx_rows, 0)
    project(pl_ref, n_lat_rows, n_ctx_rows)

    ti = lax.broadcasted_iota(jnp.int32, (t_n, t_n), 0)
    si = lax.broadcasted_iota(jnp.int32, (t_n, t_n), 1)
    tri_f = (si <= ti).astype(BF16)
    tri_b = (si >= ti).astype(BF16)
    lane = lax.broadcasted_iota(jnp.int32, (t_n, LANES), 1)
    trow = lax.broadcasted_iota(jnp.int32, (t_n, LANES), 0)

    def gate_prep(g_ref, n_chunks, base_chunk):
        def body(c, _):
            rows = pl.ds(pl.multiple_of(c * t_n, t_n), t_n)
            gcol = g_ref[0, rows, :] + gb_ref[...]
            lf = jax.nn.log_sigmoid(gcol)
            hi = lf.astype(BF16)
            lo = (lf - hi.astype(F32)).astype(BF16)
            pre = _dot(tri_f, hi) + _dot(tri_f, lo)
            suf = _dot(tri_b, hi) + _dot(tri_b, lo)
            bsum = jnp.where(lane < 2 * ML_HEADS, pre, suf)
            rcol = gcol - pltpu.roll(bsum, LANES - ML_HEADS, 1)
            pmax, smax = rcol, rcol
            step = 1
            while step < t_n:
                pmax = jnp.maximum(pmax, jnp.where(trow >= step, pltpu.roll(pmax, step, 0), NEG_BIG))
                smax = jnp.maximum(smax, jnp.where(trow < t_n - step, pltpu.roll(smax, t_n - step, 0), NEG_BIG))
                step *= 2
            bcol_s[base_chunk + c] = bsum
            mcol_s[base_chunk + c] = jnp.where(lane < 2 * ML_HEADS, pmax, smax)
            rrow_s[base_chunk + c] = rcol.T[0:4 * ML_HEADS, :]
            return 0
        lax.fori_loop(0, n_chunks, body, 0, unroll=4 if n_chunks % 4 == 0 else (2 if n_chunks % 2 == 0 else 1))

    gate_prep(gc_ref, nc_ctx, 0)
    gate_prep(gl_ref, nc_lat, nc_ctx)

    def one_dir(c, h, d, m_in):
        cols = slice(h * LANES, (h + 1) * LANES)
        xcols = slice(2 * h * LANES, (2 * h + 2) * LANES)
        li = 2 * d * ML_HEADS + h
        mask = (si <= ti) if d == 0 else (si >= ti)
        last = t_n - 1 if d == 0 else 0
        rows = pl.ds(pl.multiple_of(c * t_n, t_n), t_n)
        q = q_s[rows, cols]
        kt = kt_s[c, cols, :]
        vx = vx_s[rows, xcols]
        r_row = rrow_s[c][li:li + 1, :]
        run_max = jnp.broadcast_to(mcol_s[c][:, li:li + 1], (t_n, LANES))
        b_rep = jnp.broadcast_to(bcol_s[c][:, li + ML_HEADS:li + ML_HEADS + 1], (t_n, LANES))
        mm = jnp.maximum(m_in, run_max)
        dmat = jnp.exp(jnp.where(mask, r_row - mm, NEG_BIG))
        sm = (dmat * _dot(q, kt)).astype(BF16)
        intra = _dot(sm, vx)
        cross = _dot(q, st_s[2 * h + d].astype(BF16))
        inter = jnp.exp(m_in - mm)
        num = intra[:, :LANES] + inter * cross[:, :LANES]
        den = intra[:, LANES:] + inter * cross[:, LANES:]
        h_s[rows, cols] = h_s[rows, cols] + num / jnp.maximum(jnp.abs(den), jnp.exp(-b_rep - mm))
        r_top = run_max[last:last + 1, :]
        b_last = b_rep[last:last + 1, :]
        g_max = b_last + r_top
        m_new = jnp.maximum(b_last + m_in, g_max)
        dec = jnp.exp(b_last + m_in - m_new)
        inj = jnp.exp(g_max - m_new)
        kw = (kt.astype(F32) * (jnp.exp(r_row - r_top) * inj)).astype(BF16)
        inc = _dot(kw, vx)
        st_s[2 * h + d] = jnp.concatenate([dec, dec], axis=1) * st_s[2 * h + d] + inc
        return m_new

    st_s[...] = jnp.zeros_like(st_s)
    h_s[...] = jnp.zeros_like(h_s)

    def body(i, carry):
        cb = jnp.where(i < nc_ctx, nc_ctx - 1 - i, nc + nc_ctx - 1 - i)
        return tuple(one_dir(i if d == 0 else cb, h, d, carry[2 * h + d])
                     for h in range(ML_HEADS) for d in range(2))

    zero = jnp.zeros((1, LANES), F32)
    lax.fori_loop(0, nc, body, (zero,) * (2 * ML_HEADS), unroll=2 if nc % 2 == 0 else 1)

    mc_ref[0] = h_s[0:n_ctx_rows, :].astype(BF16)
    ml_ref[0] = h_s[n_ctx_rows:n_ctx_rows + n_lat_rows, :].astype(BF16)


def _mlstm_mix(p_ctx, p_lat, g_ctx, g_lat, conv_w, conv_b, wq, wk, wv, gate_b, cast_items):
    bn, lc, pw = p_ctx.shape
    cast = _CastAlong(cast_items, (bn,))
    ll = p_lat.shape[1]
    width = ML_HEADS * LANES
    lt = lc + ll
    nct = lt // ML_CHUNK
    gb = jnp.zeros((1, LANES), F32).at[0, :4 * ML_HEADS].set(gate_b.astype(F32))
    full2 = lambda a: pl.BlockSpec(a.shape, lambda b: (0,) * a.ndim)
    conv_b2 = conv_b.reshape(1, width)
    assert ML_CHUNK == LANES
    wq, wkt, wv = wq.astype(BF16), jnp.swapaxes(wk, 1, 2).astype(BF16), wv.astype(BF16)
    outs = pl.pallas_call(
        cast.wrap(functools.partial(_mlstm_kernel, n_ctx_rows=lc, n_lat_rows=ll), n_in=10, n_out=2),
        grid=(bn,),
        in_specs=[
            pl.BlockSpec((1, lc, pw), lambda b: (b, 0, 0)),
            pl.BlockSpec((1, ll, pw), lambda b: (b, 0, 0)),
            pl.BlockSpec((1, lc, LANES), lambda b: (b, 0, 0)),
            pl.BlockSpec((1, ll, LANES), lambda b: (b, 0, 0)),
            full2(conv_w), full2(conv_b2), full2(wq), full2(wkt), full2(wv), full2(gb),
        ] + cast.in_specs(),
        out_specs=[
            pl.BlockSpec((1, lc, width), lambda b: (b, 0, 0)),
            pl.BlockSpec((1, ll, width), lambda b: (b, 0, 0)),
        ] + cast.out_specs(),
        out_shape=[jax.ShapeDtypeStruct((bn, lc, width), BF16), jax.ShapeDtypeStruct((bn, ll, width), BF16)]
        + cast.out_shapes(),
        scratch_shapes=[
            pltpu.VMEM((lt, width), BF16),
            pltpu.VMEM((nct, width, ML_CHUNK), BF16),
            pltpu.VMEM((lt, 2 * width), BF16),
            pltpu.VMEM((lt, width), F32),
            pltpu.VMEM((2 * ML_HEADS, LANES, 2 * LANES), F32),
            pltpu.VMEM((nct, ML_CHUNK, LANES), F32), pltpu.VMEM((nct, ML_CHUNK, LANES), F32),
            pltpu.VMEM((nct, 4 * ML_HEADS, ML_CHUNK), F32),
            pltpu.VMEM((max(lc, ll) + 2 * SUBLANES, LANES), F32),
        ],
        compiler_params=_cparams(("arbitrary",)),
        name="mlstm_mix",
    )(p_ctx, p_lat, g_ctx, g_lat, conv_w, conv_b2, wq, wkt, wv, gb, *cast.operands())
    return outs[0], outs[1], outs[2:]


def _ffn_tail(x, y, gate_mix, g2, shift, scale, gate_ffn, w1_ref, w3_ref, w2_ref, hidden_chunk):
    x1 = x + gate_mix * y
    h2 = _norm_mod(x1, g2, shift, scale).astype(BF16)
    hidden = w1_ref.shape[1]
    acc = jnp.zeros_like(x1)
    start = 0
    while start < hidden:
        cs = slice(start, min(start + hidden_chunk, hidden))
        z = (_silu(_dot(h2, w1_ref[:, cs])) * _dot(h2, w3_ref[:, cs])).astype(BF16)
        acc = acc + _dot(z, w2_ref[cs, :])
        start += hidden_chunk
    return x1 + gate_ffn * acc


def _final_norm(x, g):
    return x * lax.rsqrt(jnp.mean(x * x, axis=-1, keepdims=True) + EPS) * g


def _outproj_even_kernel(x_ref, ys_ref, m_ref, og_ref, gm_ref, sh_ref, sc_ref, gf_ref, g2_ref, gw_ref, gbias_ref,
                         ng_ref, wo_ref, w1_ref, w3_ref, w2_ref, o_ref, *, hidden_chunk, heads):
    ys = jax.nn.gelu(ys_ref[0])
    s = ys * jax.nn.sigmoid(_dot(ys.astype(BF16), gw_ref[...]) + gbias_ref[...])
    dh = m_ref.shape[-1] // heads
    gated = []
    for h in range(heads):
        cols = slice(h * dh, (h + 1) * dh)
        hh = m_ref[0, :, cols].astype(F32)
        mu = jnp.mean(hh, axis=1, keepdims=True)
        cen = hh - mu
        var = jnp.mean(cen * cen, axis=1, keepdims=True)
        normed = cen * lax.rsqrt(var + EPS) * ng_ref[:, cols]
        gated.append((jax.nn.sigmoid(og_ref[0, :, cols].astype(F32)) * normed).astype(BF16))
    sw = ys_ref.shape[-1]
    y = _dot(s.astype(BF16), wo_ref[0:sw, :]) + _dot(jnp.concatenate(gated, axis=1), wo_ref[sw:, :])
    o_ref[0] = _ffn_tail(x_ref[0], y, gm_ref[0], g2_ref[...], sh_ref[0], sc_ref[0], gf_ref[0],
                         w1_ref, w3_ref, w2_ref, hidden_chunk)


def _outproj_odd_kernel(x_ref, a_ref, gate_ref, gm_ref, sh_ref, sc_ref, gf_ref, g2_ref, ng_ref, wo_ref, w1_ref,
                        w3_ref, w2_ref, fg_ref, o_ref, *, hidden_chunk, heads):
    hv = a_ref.shape[-1] // heads
    y = jnp.zeros(x_ref.shape[1:], F32)
    for h in range(heads):
        cols = slice(h * hv, (h + 1) * hv)
        o = a_ref[0, :, cols].astype(F32)
        mu = jnp.mean(o, axis=1, keepdims=True)
        cen = o - mu
        var = jnp.mean(cen * cen, axis=1, keepdims=True)
        normed = cen * lax.rsqrt(var + EPS) * ng_ref[:, cols]
        gated = (_silu(gate_ref[0, :, cols].astype(F32)) * normed).astype(BF16)
        y = y + _dot(gated, wo_ref[cols, :])
    out = _ffn_tail(x_ref[0], y, gm_ref[0], g2_ref[...], sh_ref[0], sc_ref[0], gf_ref[0],
                    w1_ref, w3_ref, w2_ref, hidden_chunk)
    o_ref[0] = _final_norm(out, fg_ref[...])


MXU_WIDTH = 256


def _hidden_chunk(hidden):
    return -(-hidden // (2 * MXU_WIDTH)) * MXU_WIDTH


def _mod_specs(d, row_of_batch, slots):
    return [pl.BlockSpec((1, 1, d), lambda b, i, s=s: (row_of_batch(b) * N_MOD + s, 0, 0)) for s in slots]


def _outproj_even(x, ys, m, p, mods, row_of_batch, g2, glu_w, glu_b, norm_g, w_out, ffn, cast_items=()):
    bn, ln, d = x.shape
    sw, mw = ys.shape[-1], m.shape[-1]
    tm = _token_tile(ln)
    grid = (bn, ln // tm)
    cast = _CastAlong(cast_items, grid)
    res = lambda a: _resident(a.shape, lambda b, i: (0,) * a.ndim)
    tok = lambda w: pl.BlockSpec((1, tm, w), lambda b, i: (b, i, 0))
    row = lambda w: pl.BlockSpec((1, w), lambda b, i: (0, 0))
    kern = functools.partial(_outproj_even_kernel, hidden_chunk=_hidden_chunk(ffn[0].shape[-1]), heads=ML_HEADS)
    outs = pl.pallas_call(
        cast.wrap(kern, n_in=16, n_out=1),
        grid=grid,
        in_specs=[tok(d), tok(sw), tok(mw), pl.BlockSpec((1, tm, mw), lambda b, i: (b, i, 1))]
        + _mod_specs(d, row_of_batch, (2, 3, 4, 5))
        + [row(d), res(glu_w), row(sw), row(mw), res(w_out)] + [res(w) for w in ffn] + cast.in_specs(),
        out_specs=[tok(d)] + cast.out_specs(),
        out_shape=[jax.ShapeDtypeStruct((bn, ln, d), F32)] + cast.out_shapes(),
        compiler_params=_cparams(("arbitrary", "arbitrary")),
        name="outproj_ffn_even",
    )(x, ys, m, p, mods, mods, mods, mods, g2, glu_w, glu_b.reshape(1, sw), norm_g.reshape(1, mw), w_out, *ffn,
      *cast.operands())
    return outs[0], outs[1:]


def _outproj_odd(x, a, gate, mods, row_of_batch, g2, norm_g, w_out, ffn, final_g):
    bn, ln, d = x.shape
    vw = a.shape[-1]
    tm = _token_tile(ln)
    res = lambda a_: _resident(a_.shape, lambda b, i: (0,) * a_.ndim)
    tok = lambda w: pl.BlockSpec((1, tm, w), lambda b, i: (b, i, 0))
    row = lambda w: pl.BlockSpec((1, w), lambda b, i: (0, 0))
    return pl.pallas_call(
        functools.partial(_outproj_odd_kernel, hidden_chunk=_hidden_chunk(ffn[0].shape[-1]), heads=RET_HEADS),
        grid=(bn, ln // tm),
        in_specs=[tok(d), tok(vw), tok(vw)] + _mod_specs(d, row_of_batch, (2, 3, 4, 5))
        + [row(d), row(vw), res(w_out)] + [res(w) for w in ffn] + [row(d)],
        out_specs=tok(d),
        out_shape=jax.ShapeDtypeStruct((bn, ln, d), F32),
        compiler_params=_cparams(("arbitrary", "arbitrary")),
        name="outproj_ffn_odd",
    )(x, a, gate, mods, mods, mods, mods, g2, norm_g.reshape(1, vw), w_out, *ffn, final_g)


def _rope_heads(t, cos, sin, heads, hk):
    half = hk // 2
    out = []
    for h in range(heads):
        t1 = t[:, h * hk:h * hk + half]
        t2 = t[:, h * hk + half:(h + 1) * hk]
        out.append(t1 * cos - t2 * sin)
        out.append(t2 * cos + t1 * sin)
    return jnp.concatenate(out, axis=-1)


def _inproj_odd_kernel(x_ref, sh_ref, sc_ref, g_ref, cos_ref, sin_ref, w_ref, q_ref, k_ref, v_ref, gate_ref,
                       *, qk, vw, heads):
    h = _norm_mod(x_ref[0], g_ref[...], sh_ref[0], sc_ref[0]).astype(BF16)
    hk = qk // heads
    cos, sin = cos_ref[...], sin_ref[...]
    q = _rope_heads(_dot(h, w_ref[:, 0:qk]), cos, sin, heads, hk)
    q_ref[0] = (q * (hk ** -0.5)).astype(BF16)
    k = _rope_heads(_dot(h, w_ref[:, qk:2 * qk]), cos, sin, heads, hk)
    k_ref[0] = k.astype(BF16)
    v_ref[0] = _dot(h, w_ref[:, 2 * qk:2 * qk + vw]).astype(BF16)
    gate_ref[0] = _dot(h, w_ref[:, 2 * qk + vw:2 * qk + 2 * vw]).astype(BF16)


def _inproj_odd_ctx_kernel(x_ref, sh_ref, sc_ref, g_ref, w_ref, k_ref, v_ref, *, qk, vw):
    h = _norm_mod(x_ref[0], g_ref[...], sh_ref[0], sc_ref[0]).astype(BF16)
    k_ref[0] = _dot(h, w_ref[:, qk:2 * qk]).astype(BF16)
    v_ref[0] = _dot(h, w_ref[:, 2 * qk:2 * qk + vw]).astype(BF16)


def _inproj_odd(x, mods, row_of_batch, g, cos, sin, w, qk, vw, cast_items=()):
    bn, ln, d = x.shape
    tm = _token_tile(ln)
    grid = (bn, ln // tm)
    cast = _CastAlong(cast_items, grid)
    tok = lambda w_: pl.BlockSpec((1, tm, w_), lambda b, i: (b, i, 0))
    half = cos.shape[1]
    outs = pl.pallas_call(
        cast.wrap(functools.partial(_inproj_odd_kernel, qk=qk, vw=vw, heads=RET_HEADS), n_in=7, n_out=4),
        grid=grid,
        in_specs=[tok(d)] + _mod_specs(d, row_of_batch, (0, 1))
        + [pl.BlockSpec((1, d), lambda b, i: (0, 0)),
           pl.BlockSpec((tm, half), lambda b, i: (i, 0)), pl.BlockSpec((tm, half), lambda b, i: (i, 0)),
           _resident(w.shape, lambda b, i: (0, 0))] + cast.in_specs(),
        out_specs=[tok(qk), tok(qk), tok(vw), tok(vw)] + cast.out_specs(),
        out_shape=[jax.ShapeDtypeStruct((bn, ln, qk), BF16), jax.ShapeDtypeStruct((bn, ln, qk), BF16),
                   jax.ShapeDtypeStruct((bn, ln, vw), BF16), jax.ShapeDtypeStruct((bn, ln, vw), BF16)]
        + cast.out_shapes(),
        compiler_params=_cparams(("arbitrary", "arbitrary")),
        name="inproj_odd",
    )(x, mods, mods, g, cos, sin, w, *cast.operands())
    return outs[:4], outs[4:]


def _inproj_odd_ctx(x, mods, row_of_batch, g, w_kv, qk, vw):
    bn, ln, d = x.shape
    tm = _token_tile(ln)
    tok = lambda w_: pl.BlockSpec((1, tm, w_), lambda b, i: (b, i, 0))
    return pl.pallas_call(
        functools.partial(_inproj_odd_ctx_kernel, qk=qk, vw=vw),
        grid=(bn, ln // tm),
        in_specs=[tok(d)] + _mod_specs(d, row_of_batch, (0, 1))
        + [pl.BlockSpec((1, d), lambda b, i: (0, 0)), _resident(w_kv.shape, lambda b, i: (0, 0))],
        out_specs=[tok(qk), tok(vw)],
        out_shape=[jax.ShapeDtypeStruct((bn, ln, qk), BF16), jax.ShapeDtypeStruct((bn, ln, vw), BF16)],
        compiler_params=_cparams(("arbitrary", "arbitrary")),
        name="inproj_odd_ctx",
    )(x, mods, mods, g, w_kv)


def _retention_kernel(lg_ref, q_ref, k_ref, v_ref, kc_ref, vc_ref, o_ref, acc_s, sf_s, sb_s,
                      *, chunk, n_ctx_rows, n_lat_rows):
    t_n = chunk
    h = pl.program_id(1)
    lgf = jnp.full((1, 1), lg_ref[0, h], F32)
    lgb = jnp.full((1, 1), lg_ref[1, h], F32)
    nc_ctx, nc_lat = n_ctx_rows // t_n, n_lat_rows // t_n
    ti = lax.broadcasted_iota(jnp.int32, (t_n, t_n), 0)
    si = lax.broadcasted_iota(jnp.int32, (t_n, t_n), 1)
    diff = (ti - si).astype(F32)
    decay = jnp.where(diff >= 0, jnp.exp(lgf * jnp.maximum(diff, 0.0)), jnp.exp(lgb * jnp.maximum(-diff, 0.0)))
    pos = lax.broadcasted_iota(jnp.int32, (t_n, 1), 0).astype(F32)
    inter_f = jnp.exp(lgf * (pos + 1.0))
    inter_b = jnp.exp(lgb * (t_n - pos))
    wend_f = jnp.exp(lgf * (t_n - 1.0 - pos))
    wend_b = jnp.exp(lgb * pos)
    cd_f = jnp.exp(lgf * t_n)
    cd_b = jnp.exp(lgb * t_n)

    def rows_of(c):
        return pl.ds(pl.multiple_of(c * t_n, t_n), t_n)

    def bump(s_ref, k, v, wend, cd):
        kw = (k.astype(F32) * wend).astype(BF16)
        s_ref[...] = cd * s_ref[...] + _dot_tn(kw, v)

    sf_s[...] = jnp.zeros_like(sf_s)
    sb_s[...] = jnp.zeros_like(sb_s)

    def ctx_f(c, _):
        bump(sf_s, kc_ref[0, rows_of(c), :], vc_ref[0, rows_of(c), :], wend_f, cd_f)
        return 0
    lax.fori_loop(0, nc_ctx, ctx_f, 0)

    def ctx_b(i, _):
        c = nc_ctx - 1 - i
        bump(sb_s, kc_ref[0, rows_of(c), :], vc_ref[0, rows_of(c), :], wend_b, cd_b)
        return 0
    lax.fori_loop(0, nc_ctx, ctx_b, 0)

    def forward_part(rows):
        q, k, v = q_ref[0, rows, :], k_ref[0, rows, :], v_ref[0, rows, :]
        scores = (_dot_nt(q, k) * decay).astype(BF16)
        part = _dot(scores, v) + inter_f * _dot(q, sf_s[...].astype(BF16))
        bump(sf_s, k, v, wend_f, cd_f)
        return part

    def backward_part(rows):
        q, k, v = q_ref[0, rows, :], k_ref[0, rows, :], v_ref[0, rows, :]
        part = inter_b * _dot(q, sb_s[...].astype(BF16))
        bump(sb_s, k, v, wend_b, cd_b)
        return part

    def first_half(i, _):
        rf, rb = rows_of(i), rows_of(nc_lat - 1 - i)
        acc_s[rf, :] = forward_part(rf)
        acc_s[rb, :] = backward_part(rb)
        return 0
    pair_unroll = 4 if (nc_lat // 2) % 4 == 0 else 1
    lax.fori_loop(0, nc_lat // 2, first_half, 0, unroll=pair_unroll)

    def second_half(i, _):
        rf, rb = rows_of(i), rows_of(nc_lat - 1 - i)
        o_ref[0, rf, :] = (acc_s[rf, :] + forward_part(rf)).astype(BF16)
        o_ref[0, rb, :] = (acc_s[rb, :] + backward_part(rb)).astype(BF16)
        return 0
    lax.fori_loop(nc_lat // 2, nc_lat, second_half, 0, unroll=pair_unroll)


def _retention_mix(q, k, v, k_ctx, v_ctx, log_gamma):
    bn, ll, qk = q.shape
    lc = k_ctx.shape[1]
    vw = v.shape[-1]
    hk, hv = qk // RET_HEADS, vw // RET_HEADS
    chunk = 256 if (ll % 512 == 0 and lc % 256 == 0) else 128
    assert (ll // chunk) % 2 == 0, "latent chunks are visited in forward/backward pairs"
    head = lambda n_rows, w: pl.BlockSpec((1, n_rows, w), lambda b, h: (b, 0, h))
    return pl.pallas_call(
        functools.partial(_retention_kernel, chunk=chunk, n_ctx_rows=lc, n_lat_rows=ll),
        grid=(bn, RET_HEADS),
        in_specs=[
            pl.BlockSpec(memory_space=pltpu.SMEM),
            head(ll, hk), head(ll, hk), head(ll, hv), head(lc, hk), head(lc, hv),
        ],
        out_specs=head(ll, hv),
        out_shape=jax.ShapeDtypeStruct((bn, ll, vw), BF16),
        scratch_shapes=[pltpu.VMEM((ll, hv), F32), pltpu.VMEM((hk, hv), F32), pltpu.VMEM((hk, hv), F32)],
        compiler_params=_cparams(("arbitrary", "arbitrary")),
        name="retention_mix",
    )(log_gamma.astype(F32), q, k, v, k_ctx, v_ctx)


def _grid_rope(n_pos, hk):
    rows = n_pos // GRID_W
    row = jnp.repeat(jnp.arange(rows, dtype=F32), GRID_W)
    col = jnp.tile(jnp.arange(GRID_W, dtype=F32), rows)
    n_freq = hk // 4
    inv = ROPE_BASE ** (-jnp.arange(n_freq, dtype=F32) / n_freq)
    ang = jnp.concatenate([row[:, None] * inv, col[:, None] * inv], -1)
    return jnp.cos(ang), jnp.sin(ang)


def kernel(x, c, ctx, c_ctx, ada_w, ada_b, norm1_g, norm2_g, ab_w_in, ab_w_out, s5_a_re, s5_a_im, s5_log_dt,
           s5_b_re, s5_b_im, s5_c_re, s5_c_im, s5_d, s5_glu_w, s5_glu_b, ml_conv_w, ml_conv_b, ml_wq, ml_wk,
           ml_wv, ml_gate_b, ml_norm_g, ret_w_in, ret_w_out, ret_log_gamma, ret_norm_g, ffn_w1, ffn_w3, ffn_w2,
           final_g):
    bn, ln, d = x.shape
    depth = ada_w.shape[0]
    assert depth == 2, "one S5 || mLSTM layer followed by one retention layer"
    s5_width = s5_d.shape[-1]
    ml_width = ml_norm_g.shape[-1]
    assert ml_width == ML_HEADS * LANES and s5_width % S5_GROUP == 0

    r_pad = -(-(bn + 1) // SUBLANES) * SUBLANES
    vec = jnp.zeros((r_pad, d), F32).at[:bn].set(c).at[bn].set(c_ctx)
    mods_all = _modulation(vec, ada_w, ada_b).reshape(depth, r_pad * N_MOD, 1, d)
    lat_row = lambda b: b
    ctx_row = lambda b: bn

    mods = mods_all[0]
    g1 = norm1_g[0].reshape(1, d)
    g2 = norm2_g[0].reshape(1, d)
    w_in = ab_w_in[0]
    n_gate = w_in.shape[1] - s5_width - 2 * ml_width
    w_in = jnp.pad(w_in, ((0, 0), (0, LANES - n_gate))).astype(BF16)
    u_lat, p_lat, g_lat = _inproj_even(x, mods, lat_row, g1, w_in, s5_width)
    lc = ctx.shape[1]
    flat = lambda t: t.reshape(1, bn * lc, t.shape[-1])
    unflat = lambda t: t.reshape(bn, lc, t.shape[-1])
    u_ctx, p_ctx, g_ctx = (unflat(t) for t in _inproj_even(flat(ctx), mods, ctx_row, g1, w_in, s5_width))
    mats = _s5_prep(s5_a_re[0], s5_a_im[0], s5_log_dt[0], s5_b_re[0], s5_b_im[0], s5_c_re[0], s5_c_im[0],
                        s5_d[0])
    ys_ctx, ys_lat = _s5_mix(u_ctx, u_lat, mats)
    m_ctx, m_lat, ffn = _mlstm_mix(p_ctx, p_lat, g_ctx, g_lat, ml_conv_w[0], ml_conv_b[0], ml_wq[0], ml_wk[0],
                                   ml_wv[0], ml_gate_b[0], [(ffn_w1, 0), (ffn_w3, 0), (ffn_w2, 0)])
    glu_w = s5_glu_w[0].astype(BF16)
    w_out = ab_w_out[0].astype(BF16)
    x, (ret_in,) = _outproj_even(x, ys_lat, m_lat, p_lat, mods, lat_row, g2, glu_w, s5_glu_b[0], ml_norm_g[0],
                                 w_out, ffn, [(ret_w_in, 0)])
    ctx, _ = _outproj_even(flat(ctx), flat(ys_ctx), flat(m_ctx), flat(p_ctx), mods, ctx_row, g2, glu_w,
                           s5_glu_b[0], ml_norm_g[0], w_out, ffn)

    mods = mods_all[1]
    g1 = norm1_g[1].reshape(1, d)
    g2 = norm2_g[1].reshape(1, d)
    vw = ret_norm_g.shape[-1]
    qk = (ret_w_in.shape[-1] - 2 * vw) // 2
    cos, sin = _grid_rope(ln, qk // RET_HEADS)
    (q, k, v, gate), ffn = _inproj_odd(x, mods, lat_row, g1, cos, sin, ret_in, qk, vw,
                                       [(ffn_w1, 1), (ffn_w3, 1), (ffn_w2, 1)])
    k_ctx, v_ctx = (unflat(t) for t in _inproj_odd_ctx(ctx, mods, ctx_row, g1, ret_in, qk, vw))
    a = _retention_mix(q, k, v, k_ctx, v_ctx, ret_log_gamma[0])
    w_out = ret_w_out[0].astype(BF16)
    return _outproj_odd(x, a, gate, mods, lat_row, g2, ret_norm_g[0], w_out, ffn, final_g.reshape(1, d))
```

```python
import functools
import math

import jax
import jax.numpy as jnp
from jax import lax
from jax.experimental import pallas as pl
from jax.experimental.pallas import tpu as pltpu

F32 = jnp.float32
BF16 = jnp.bfloat16

EPS = 1e-6
N_MOD = 6
GRID_W = 64
ROPE_BASE = 10000.0

S5_GROUP = 16
S5_STATE = 64
S5_STEP = 16
ML_HEADS = 4
ML_CHUNK = 128
RET_HEADS = 4
SUBLANES = 8
LANES = 128
NEG_BIG = -1e30
VMEM_LIMIT = 56 * 1024 * 1024


def _cparams(sem):
    return pltpu.CompilerParams(dimension_semantics=sem, vmem_limit_bytes=VMEM_LIMIT)


def _resident(shape, index_map):
    return pl.BlockSpec(shape, index_map, pipeline_mode=pl.Buffered(1))


BF16_ROWS = 16


class _CastAlong:
    def __init__(self, weights_and_layers, grid):
        self.items = list(weights_and_layers)
        steps = math.prod(grid)
        strides = [math.prod(grid[i + 1:]) for i in range(len(grid))]
        self.step = lambda idx: sum(i * s for i, s in zip(idx, strides))
        self.plans = []
        for w, _ in self.items:
            _, r, c = w.shape
            if r % (steps * BF16_ROWS) == 0:
                self.plans.append(((r // steps, c), lambda s: (s, 0)))
            else:
                assert steps % 2 == 0 and r % (steps // 2 * BF16_ROWS) == 0 and c % (2 * LANES) == 0
                self.plans.append(((2 * r // steps, c // 2), lambda s: (s // 2, s % 2)))

    def __len__(self):
        return len(self.items)

    def in_specs(self):
        return [pl.BlockSpec((None,) + blk, lambda *idx, f=f, l=l: (l,) + f(self.step(idx)))
                for (blk, f), (_, l) in zip(self.plans, self.items)]

    def out_specs(self):
        return [pl.BlockSpec(blk, lambda *idx, f=f: f(self.step(idx))) for blk, f in self.plans]

    def out_shapes(self):
        return [jax.ShapeDtypeStruct(w.shape[1:], BF16) for w, _ in self.items]

    def operands(self):
        return [w for w, _ in self.items]

    def wrap(self, kernel_fn, n_in, n_out):
        n = len(self)

        def wrapped(*refs):
            ins, cast_in = refs[:n_in], refs[n_in:n_in + n]
            outs, cast_out = refs[n_in + n:n_in + n + n_out], refs[n_in + n + n_out:n_in + 2 * n + n_out]
            for src, dst in zip(cast_in, cast_out):
                dst[...] = src[...].astype(BF16)
            kernel_fn(*ins, *outs, *refs[n_in + 2 * n + n_out:])
        return wrapped


def _token_tile(n, largest=512):
    for t in (largest, 512, 256, 128):
        if n % t == 0:
            return t
    raise ValueError(f"sequence length {n} must be a multiple of 128")


def _silu(v):
    return v * jax.nn.sigmoid(v)


def _norm_mod(x, g, shift, scale):
    y = x * lax.rsqrt(jnp.mean(x * x, axis=-1, keepdims=True) + EPS)
    return (y * g) * (1.0 + scale) + shift


def _dot(a, b):
    return jnp.dot(a, b, preferred_element_type=F32)


def _dot_nt(a, b):
    return lax.dot_general(a, b, (((1,), (1,)), ((), ())), preferred_element_type=F32)


def _dot_tn(a, b):
    return lax.dot_general(a, b, (((0,), (0,)), ((), ())), preferred_element_type=F32)


def _mod_kernel(v_ref, w_ref, b_ref, o_ref):
    s = _silu(v_ref[...]).astype(BF16)
    o_ref[0] = _dot(s, w_ref[0].astype(BF16)) + b_ref[0]


def _modulation(vec, ada_w, ada_b):
    depth, d, n = ada_w.shape
    r = vec.shape[0]
    tn = 1024
    return pl.pallas_call(
        _mod_kernel,
        grid=(depth, n // tn),
        in_specs=[
            pl.BlockSpec((r, d), lambda l, j: (0, 0)),
            pl.BlockSpec((1, d, tn), lambda l, j: (l, 0, j)),
            pl.BlockSpec((1, 1, tn), lambda l, j: (l, 0, j)),
        ],
        out_specs=pl.BlockSpec((1, r, tn), lambda l, j: (l, 0, j)),
        out_shape=jax.ShapeDtypeStruct((depth, r, n), F32),
        compiler_params=_cparams(("arbitrary", "arbitrary")),
        name="adaln_modulation",
    )(vec, ada_w, ada_b.reshape(depth, 1, n))


def _inproj_even_kernel(x_ref, sh_ref, sc_ref, g_ref, w_ref, u_ref, p_ref, gate_ref, *, s5_width):
    h = _norm_mod(x_ref[0], g_ref[...], sh_ref[0], sc_ref[0]).astype(BF16)
    p = _dot(h, w_ref[...])
    n_mid = p_ref.shape[-1]
    u_ref[0] = p[:, :s5_width]
    p_ref[0] = p[:, s5_width:s5_width + n_mid].astype(BF16)
    gate_ref[0] = p[:, s5_width + n_mid:]


def _inproj_even(x, mods, row_of_batch, g, w, s5_width):
    bn, ln, d = x.shape
    n = w.shape[1]
    n_mid = n - s5_width - LANES
    tm = _token_tile(ln, largest=1024)
    return pl.pallas_call(
        functools.partial(_inproj_even_kernel, s5_width=s5_width),
        grid=(bn, ln // tm),
        in_specs=[
            pl.BlockSpec((1, tm, d), lambda b, i: (b, i, 0)),
            pl.BlockSpec((1, 1, d), lambda b, i: (row_of_batch(b) * N_MOD + 0, 0, 0)),
            pl.BlockSpec((1, 1, d), lambda b, i: (row_of_batch(b) * N_MOD + 1, 0, 0)),
            pl.BlockSpec((1, d), lambda b, i: (0, 0)),
            _resident((d, n), lambda b, i: (0, 0)),
        ],
        out_specs=[
            pl.BlockSpec((1, tm, s5_width), lambda b, i: (b, i, 0)),
            pl.BlockSpec((1, tm, n_mid), lambda b, i: (b, i, 0)),
            pl.BlockSpec((1, tm, LANES), lambda b, i: (b, i, 0)),
        ],
        out_shape=[
            jax.ShapeDtypeStruct((bn, ln, s5_width), F32),
            jax.ShapeDtypeStruct((bn, ln, n_mid), BF16),
            jax.ShapeDtypeStruct((bn, ln, LANES), F32),
        ],
        compiler_params=_cparams(("arbitrary", "arbitrary")),
        name="inproj_even",
    )(x, mods, mods, g, w)


def _s5_prep_kernel(sp_ref, bt_ref, c_ref, d_ref, toep_ref, endw_ref, outw_ref, a0_ref, a1_ref, *, steps, chans):
    k, p = steps, S5_STATE
    width = k * chans
    lane = lax.broadcasted_iota(jnp.int32, (1, LANES), 1)
    low = lane < p
    lag = lax.broadcasted_iota(jnp.int32, (k + SUBLANES, LANES), 0).astype(F32)
    row = lax.broadcasted_iota(jnp.int32, (chans, LANES), 0)
    lane_c = lax.broadcasted_iota(jnp.int32, (chans, LANES), 1)
    lag0 = (k - 1) * chans
    diag = lane_c == row + lag0 % LANES
    pair = lambda x, y: x + pltpu.roll(y, p, 1)
    zeros = jnp.zeros((chans, LANES), F32)

    for n in range(sp_ref.shape[0]):
        wr, wi, pr, pi, cre, cim = [], [], [], [], [], []
        for d in range(2):
            a_re, a_im = sp_ref[n, 4 * d:4 * d + 1, :], sp_ref[n, 4 * d + 1:4 * d + 2, :]
            l_re, l_im = sp_ref[n, 4 * d + 2:4 * d + 3, :], sp_ref[n, 4 * d + 3:4 * d + 4, :]
            mag = jnp.exp(lag * l_re)
            p_re, p_im = mag * jnp.cos(lag * l_im), mag * jnp.sin(lag * l_im)
            den = jnp.where(low, a_re * a_re + a_im * a_im, 1.0)
            nr = p_re[1:2] - 1.0
            coef_re = (nr * a_re + p_im[1:2] * a_im) / den
            coef_im = (p_im[1:2] * a_re - nr * a_im) / den
            b_re, b_im = bt_ref[n, 2 * d * chans:(2 * d + 1) * chans, :], bt_ref[n, (2 * d + 1) * chans:(2 * d + 2) * chans, :]
            bb_re = coef_re * b_re - coef_im * b_im
            bb_im = coef_re * b_im + coef_im * b_re
            wr.append([p_re[l:l + 1] * bb_re - p_im[l:l + 1] * bb_im for l in range(k)])
            wi.append([p_re[l:l + 1] * bb_im + p_im[l:l + 1] * bb_re for l in range(k)])
            pr.append(p_re)
            pi.append(p_im)
            cre.append(c_ref[n, 2 * d * chans:(2 * d + 1) * chans, :])
            cim.append(c_ref[n, (2 * d + 1) * chans:(2 * d + 2) * chans, :])

        for s in range(k):
            endw_ref[n, s * chans:(s + 1) * chans, :] = jnp.concatenate(
                [pair(wr[0][k - 1 - s], wr[1][s]), pair(wi[0][k - 1 - s], wi[1][s])], axis=1).astype(BF16)
        for t in range(k):
            f_re, f_im = pr[0][t + 1:t + 2], pi[0][t + 1:t + 2]
            b_re, b_im = pr[1][k - t:k - t + 1], pi[1][k - t:k - t + 1]
            outw_ref[n, t * chans:(t + 1) * chans, :] = jnp.concatenate(
                [pair(cre[0] * f_re - cim[0] * f_im, cre[1] * b_re - cim[1] * b_im),
                 pair(-(cre[0] * f_im + cim[0] * f_re), -(cre[1] * b_im + cim[1] * b_re))], axis=1).astype(BF16)
        a0_ref[n:n + 1, :] = jnp.where(low, pr[0][k:k + 1], pltpu.roll(pr[1][k:k + 1], p, 1))
        a1_ref[n:n + 1, :] = jnp.where(low, pi[0][k:k + 1], pltpu.roll(pi[1][k:k + 1], p, 1))

        blocks = []
        for m in range(2 * k):
            fwd = pair(wr[0][k - 1 - m], -wi[0][k - 1 - m]) if m < k else zeros
            bwd = pair(wr[1][m - k + 1], -wi[1][m - k + 1]) if k - 1 <= m < 2 * k - 1 else zeros
            blocks.append(jnp.concatenate([fwd, bwd], axis=1))
        w_gen = jnp.concatenate(blocks, axis=0)
        c2 = jnp.concatenate([pair(cre[0], cim[0]), pair(cre[1], cim[1])], axis=1)
        gen = lax.dot_general(c2, w_gen, (((1,), (1,)), ((), ())),
                              precision=lax.Precision.HIGHEST, preferred_element_type=F32)
        pieces = [gen[:, j * LANES:(j + 1) * LANES] for j in range(2 * width // LANES)]
        pieces[lag0 // LANES] = pieces[lag0 // LANES] + jnp.where(diag, d_ref[n], 0.0)
        gen = jnp.concatenate(pieces, axis=1)
        for t in range(k):
            off = (k - 1 - t) * chans
            toep_ref[n, t * chans:(t + 1) * chans, :] = gen[:, off:off + width].astype(BF16)


def _s5_prep(a_re, a_im, log_dt, b_re, b_im, c_re, c_im, d_skip):
    k = S5_STEP
    g_n, p_n, c_n = a_re.shape[1], a_re.shape[2], b_re.shape[-1]
    assert 2 * p_n == LANES
    lanes = lambda t: jnp.pad(t.astype(F32), [(0, 0)] * (t.ndim - 1) + [(0, LANES - p_n)])
    dt = jnp.exp(log_dt.astype(F32))[..., None]
    a_re, a_im = a_re.astype(F32), a_im.astype(F32)
    sp = lanes(jnp.stack([a_re[0], a_im[0], a_re[0] * dt[0], a_im[0] * dt[0],
                          a_re[1], a_im[1], a_re[1] * dt[1], a_im[1] * dt[1]], axis=1))
    tr = lambda t: jnp.swapaxes(t, -1, -2)
    bt = lanes(jnp.concatenate([tr(b_re[0]), tr(b_im[0]), tr(b_re[1]), tr(b_im[1])], axis=1))
    cc = lanes(jnp.concatenate([c_re[0], c_im[0], c_re[1], c_im[1]], axis=1))
    d_rep = jnp.broadcast_to(d_skip.astype(F32).reshape(g_n, c_n, 1), (g_n, c_n, LANES))
    nb = 8
    blk = lambda r, c: pl.BlockSpec((nb, r, c), lambda i: (i, 0, 0))
    kc = k * c_n
    mat = jax.ShapeDtypeStruct((g_n, kc, kc), BF16)
    vec = jax.ShapeDtypeStruct((g_n, LANES), F32)
    return pl.pallas_call(
        functools.partial(_s5_prep_kernel, steps=k, chans=c_n),
        grid=(g_n // nb,),
        in_specs=[blk(8, LANES), blk(4 * c_n, LANES), blk(4 * c_n, LANES), blk(c_n, LANES)],
        out_specs=[blk(kc, kc), blk(kc, kc), blk(kc, kc),
                   pl.BlockSpec((nb, LANES), lambda i: (i, 0)), pl.BlockSpec((nb, LANES), lambda i: (i, 0))],
        out_shape=[mat, mat, mat, vec, vec],
        compiler_params=_cparams(("arbitrary",)),
        name="s5_prep",
    )(sp, bt, cc, d_rep)


S5_LANE_GROUPS = LANES // S5_GROUP


def _chunk_transpose(xs):
    n_arr = len(xs)
    lane_chunk = lax.broadcasted_iota(jnp.int32, xs[0].shape, 1) // S5_GROUP
    d = 1
    while d < n_arr:
        keep = (lane_chunk & d) == 0
        out = list(xs)
        for i in range(n_arr):
            if i & d:
                continue
            lo, hi = xs[i], xs[i + d]
            out[i] = jnp.where(keep, lo, pltpu.roll(hi, d * S5_GROUP, 1))
            out[i + d] = jnp.where(keep, pltpu.roll(lo, LANES - d * S5_GROUP, 1), hi)
        xs = out
        d *= 2
    return xs


def _s5_kernel(uc_ref, ul_ref, toep_ref, endw_ref, outw_ref, a0_ref, a1_ref, yc_ref, yl_ref,
               uc_s, ul_s, e_s, yc_s, yl_s, *, batch, n_ctx, n_lat):
    p = S5_STATE
    gl = S5_LANE_GROUPS
    k = S5_STEP

    def stack(u_ref, us_ref, n_blk):
        for b in range(batch):
            for half in range(k // gl):
                xs = [u_ref[b, pl.ds(half * gl + s, n_blk, stride=k), :] for s in range(gl)]
                ys = _chunk_transpose(xs)
                for g in range(gl):
                    us_ref[g, b * n_blk:(b + 1) * n_blk, half * LANES:(half + 1) * LANES] = ys[g].astype(BF16)

    stack(uc_ref, uc_s, n_ctx)
    stack(ul_ref, ul_s, n_lat)

    def increments(us_ref, n_blk, first_blk):
        for g in range(gl):
            e = _dot(us_ref[g], endw_ref[g])
            for b in range(batch):
                for part in range(2):
                    e_s[part, b, pl.ds(first_blk * gl + g, n_blk, stride=gl), :] = (
                        e[b * n_blk:(b + 1) * n_blk, part * LANES:(part + 1) * LANES])

    increments(uc_s, n_ctx, 0)
    increments(ul_s, n_lat, n_ctx)

    a0 = a0_ref[...]
    a1 = a1_ref[...]
    fwd_lane = lax.broadcasted_iota(jnp.int32, (batch, gl, 2 * p), 2) < p

    def scan(first_blk, n, carry):
        def body(i, c):
            s0, s1 = c
            rf = pl.ds(pl.multiple_of((first_blk + i) * gl, gl), gl)
            rb = pl.ds(pl.multiple_of((first_blk + n - 1 - i) * gl, gl), gl)
            e0 = jnp.where(fwd_lane, e_s[0, :, rf, :], e_s[0, :, rb, :])
            e1 = jnp.where(fwd_lane, e_s[1, :, rf, :], e_s[1, :, rb, :])
            e_s[0, :, rf, 0:p] = s0[:, :, 0:p]
            e_s[0, :, rb, p:2 * p] = s0[:, :, p:2 * p]
            e_s[1, :, rf, 0:p] = s1[:, :, 0:p]
            e_s[1, :, rb, p:2 * p] = s1[:, :, p:2 * p]
            return a0 * s0 - a1 * s1 + e0, a0 * s1 + a1 * s0 + e1
        return lax.fori_loop(0, n, body, carry, unroll=4 if n % 4 == 0 else 1)

    zero = jnp.zeros((batch, gl, 2 * p), F32)
    carry = scan(0, n_ctx, (zero, zero))
    scan(n_ctx, n_lat, carry)

    def outputs(us_ref, ys_ref, y_ref, n_blk, first_blk):
        for g in range(gl):
            state = jnp.concatenate(
                [jnp.concatenate([e_s[part, b, pl.ds(first_blk * gl + g, n_blk, stride=gl), :]
                                  for b in range(batch)], axis=0) for part in range(2)], axis=1)
            ys_ref[g] = _dot_nt(us_ref[g], toep_ref[g]) + _dot_nt(state.astype(BF16), outw_ref[g])
        for b in range(batch):
            for half in range(k // gl):
                ys = [ys_ref[g, b * n_blk:(b + 1) * n_blk, half * LANES:(half + 1) * LANES] for g in range(gl)]
                xs = _chunk_transpose(ys)
                for s in range(gl):
                    y_ref[b, pl.ds(half * gl + s, n_blk, stride=k), :] = xs[s]

    outputs(uc_s, yc_s, yc_ref, n_ctx, 0)
    outputs(ul_s, yl_s, yl_ref, n_lat, n_ctx)


def _s5_mix(u_ctx, u_lat, mats):
    toep, endw, outw, a0, a1 = mats
    bn, lc, w = u_ctx.shape
    ll = u_lat.shape[1]
    gl = S5_LANE_GROUPS
    kc = S5_STEP * S5_GROUP
    n_ctx, n_lat = lc // S5_STEP, ll // S5_STEP
    bb = 4 if bn % 4 == 0 else bn
    seq = lambda n_rows: pl.BlockSpec((bb, n_rows, LANES), lambda i, j: (j, 0, i))
    grp = lambda r, c: pl.BlockSpec((gl, r, c), lambda i, j: (i, 0, 0))
    return pl.pallas_call(
        functools.partial(_s5_kernel, batch=bb, n_ctx=n_ctx, n_lat=n_lat),
        grid=(w // LANES, bn // bb),
        in_specs=[seq(lc), seq(ll), grp(kc, kc), grp(kc, kc), grp(kc, kc),
                  pl.BlockSpec((gl, 2 * S5_STATE), lambda i, j: (i, 0)),
                  pl.BlockSpec((gl, 2 * S5_STATE), lambda i, j: (i, 0))],
        out_specs=[seq(lc), seq(ll)],
        out_shape=[jax.ShapeDtypeStruct((bn, lc, w), F32), jax.ShapeDtypeStruct((bn, ll, w), F32)],
        scratch_shapes=[
            pltpu.VMEM((gl, bb * n_ctx, kc), BF16), pltpu.VMEM((gl, bb * n_lat, kc), BF16),
            pltpu.VMEM((2, bb, (n_ctx + n_lat) * gl, LANES), F32),
            pltpu.VMEM((gl, bb * n_ctx, kc), F32), pltpu.VMEM((gl, bb * n_lat, kc), F32),
        ],
        compiler_params=_cparams(("arbitrary", "arbitrary")),
        name="s5_scan",
    )(u_ctx, u_lat, toep, endw, outw, a0, a1)


def _mlstm_kernel(pc_ref, pl_ref, gc_ref, gl_ref, cw_ref, cb_ref, wq_ref, wkt_ref, wv_ref, gb_ref, mc_ref, ml_ref,
                  q_s, kt_s, vx_s, h_s, st_s, bcol_s, mcol_s, rrow_s, xpad_s, *, n_ctx_rows, n_lat_rows):
    t_n = ML_CHUNK
    width = ML_HEADS * LANES
    nc_ctx, nc_lat = n_ctx_rows // t_n, n_lat_rows // t_n
    nc = nc_ctx + nc_lat
    scale_k = LANES ** -0.5

    def project(p_ref, n_rows, base):
        ones = jnp.ones((n_rows, LANES), BF16)
        pad = SUBLANES
        xpad_s[0:pad, :] = jnp.zeros((pad, LANES), F32)
        xpad_s[pad + n_rows:2 * pad + n_rows, :] = jnp.zeros((pad, LANES), F32)
        for h in range(ML_HEADS):
            cols = slice(h * LANES, (h + 1) * LANES)
            xm_lo = p_ref[0, :, cols]
            xm = xm_lo.astype(F32)
            xpad_s[pad:pad + n_rows, :] = xm
            taps = cw_ref[:, cols]
            n_tap = taps.shape[0]
            assert n_tap // 2 <= pad
            acc = jnp.zeros_like(xm) + cb_ref[:, cols]
            for j in range(n_tap):
                d = j - n_tap // 2
                sh = xm if d == 0 else xpad_s[pad + d:pad + d + n_rows, :]
                acc = acc + sh * taps[j:j + 1, :]
            xc = _silu(acc).astype(BF16)
            q_s[base:base + n_rows, cols] = _dot(xc, wq_ref[h]).astype(BF16)
            kt = (_dot_nt(wkt_ref[h], xc) * scale_k).astype(BF16)
            for c in range(n_rows // t_n):
                kt_s[base // t_n + c, cols, :] = kt[:, c * t_n:(c + 1) * t_n]
            vx_s[base:base + n_rows, 2 * h * LANES:(2 * h + 1) * LANES] = (
                _dot(xm_lo, wv_ref[h]).astype(BF16))
            vx_s[base:base + n_rows, (2 * h + 1) * LANES:(2 * h + 2) * LANES] = ones

    project(pc_ref, n_ctx_rows, 0)
    project(pl_ref, n_lat_rows, n_ctx_rows)

    ti = lax.broadcasted_iota(jnp.int32, (t_n, t_n), 0)
    si = lax.broadcasted_iota(jnp.int32, (t_n, t_n), 1)
    tri_f = (si <= ti).astype(BF16)
    tri_b = (si >= ti).astype(BF16)
    lane = lax.broadcasted_iota(jnp.int32, (t_n, LANES), 1)
    trow = lax.broadcasted_iota(jnp.int32, (t_n, LANES), 0)

    def gate_prep(g_ref, n_chunks, base_chunk):
        def body(c, _):
            rows = pl.ds(pl.multiple_of(c * t_n, t_n), t_n)
            gcol = g_ref[0, rows, :] + gb_ref[...]
            lf = jax.nn.log_sigmoid(gcol)
            hi = lf.astype(BF16)
            lo = (lf - hi.astype(F32)).astype(BF16)
            pre = _dot(tri_f, hi) + _dot(tri_f, lo)
            suf = _dot(tri_b, hi) + _dot(tri_b, lo)
            bsum = jnp.where(lane < 2 * ML_HEADS, pre, suf)
            rcol = gcol - pltpu.roll(bsum, LANES - ML_HEADS, 1)
            pmax, smax = rcol, rcol
            step = 1
            while step < t_n:
                pmax = jnp.maximum(pmax, jnp.where(trow >= step, pltpu.roll(pmax, step, 0), NEG_BIG))
                smax = jnp.maximum(smax, jnp.where(trow < t_n - step, pltpu.roll(smax, t_n - step, 0), NEG_BIG))
                step *= 2
            bcol_s[base_chunk + c] = bsum
            mcol_s[base_chunk + c] = jnp.where(lane < 2 * ML_HEADS, pmax, smax)
            rrow_s[base_chunk + c] = rcol.T[0:4 * ML_HEADS, :]
            return 0
        lax.fori_loop(0, n_chunks, body, 0, unroll=4 if n_chunks % 4 == 0 else (2 if n_chunks % 2 == 0 else 1))

    gate_prep(gc_ref, nc_ctx, 0)
    gate_prep(gl_ref, nc_lat, nc_ctx)

    def one_dir(c, h, d, m_in):
        cols = slice(h * LANES, (h + 1) * LANES)
        xcols = slice(2 * h * LANES, (2 * h + 2) * LANES)
        li = 2 * d * ML_HEADS + h
        mask = (si <= ti) if d == 0 else (si >= ti)
        last = t_n - 1 if d == 0 else 0
        rows = pl.ds(pl.multiple_of(c * t_n, t_n), t_n)
        q = q_s[rows, cols]
        kt = kt_s[c, cols, :]
        vx = vx_s[rows, xcols]
        r_row = rrow_s[c][li:li + 1, :]
        run_max = jnp.broadcast_to(mcol_s[c][:, li:li + 1], (t_n, LANES))
        b_rep = jnp.broadcast_to(bcol_s[c][:, li + ML_HEADS:li + ML_HEADS + 1], (t_n, LANES))
        mm = jnp.maximum(m_in, run_max)
        dmat = jnp.exp(jnp.where(mask, r_row - mm, NEG_BIG))
        sm = (dmat * _dot(q, kt)).astype(BF16)
        intra = _dot(sm, vx)
        cross = _dot(q, st_s[2 * h + d].astype(BF16))
        inter = jnp.exp(m_in - mm)
        num = intra[:, :LANES] + inter * cross[:, :LANES]
        den = intra[:, LANES:] + inter * cross[:, LANES:]
        h_s[rows, cols] = h_s[rows, cols] + num / jnp.maximum(jnp.abs(den), jnp.exp(-b_rep - mm))
        r_top = run_max[last:last + 1, :]
        b_last = b_rep[last:last + 1, :]
        g_max = b_last + r_top
        m_new = jnp.maximum(b_last + m_in, g_max)
        dec = jnp.exp(b_last + m_in - m_new)
        inj = jnp.exp(g_max - m_new)
        kw = (kt.astype(F32) * (jnp.exp(r_row - r_top) * inj)).astype(BF16)
        inc = _dot(kw, vx)
        st_s[2 * h + d] = jnp.concatenate([dec, dec], axis=1) * st_s[2 * h + d] + inc
        return m_new

    st_s[...] = jnp.zeros_like(st_s)
    h_s[...] = jnp.zeros_like(h_s)

    def body(i, carry):
        cb = jnp.where(i < nc_ctx, nc_ctx - 1 - i, nc + nc_ctx - 1 - i)
        return tuple(one_dir(i if d == 0 else cb, h, d, carry[2 * h + d])
                     for h in range(ML_HEADS) for d in range(2))

    zero = jnp.zeros((1, LANES), F32)
    lax.fori_loop(0, nc, body, (zero,) * (2 * ML_HEADS), unroll=2 if nc % 2 == 0 else 1)

    mc_ref[0] = h_s[0:n_ctx_rows, :].astype(BF16)
    ml_ref[0] = h_s[n_ctx_rows:n_ctx_rows + n_lat_rows, :].astype(BF16)


def _mlstm_mix(p_ctx, p_lat, g_ctx, g_lat, conv_w, conv_b, wq, wk, wv, gate_b, cast_items):
    bn, lc, pw = p_ctx.shape
    cast = _CastAlong(cast_items, (bn,))
    ll = p_lat.shape[1]
    width = ML_HEADS * LANES
    lt = lc + ll
    nct = lt // ML_CHUNK
    gb = jnp.zeros((1, LANES), F32).at[0, :4 * ML_HEADS].set(gate_b.astype(F32))
    full2 = lambda a: pl.BlockSpec(a.shape, lambda b: (0,) * a.ndim)
    conv_b2 = conv_b.reshape(1, width)
    assert ML_CHUNK == LANES
    wq, wkt, wv = wq.astype(BF16), jnp.swapaxes(wk, 1, 2).astype(BF16), wv.astype(BF16)
    outs = pl.pallas_call(
        cast.wrap(functools.partial(_mlstm_kernel, n_ctx_rows=lc, n_lat_rows=ll), n_in=10, n_out=2),
        grid=(bn,),
        in_specs=[
            pl.BlockSpec((1, lc, pw), lambda b: (b, 0, 0)),
            pl.BlockSpec((1, ll, pw), lambda b: (b, 0, 0)),
            pl.BlockSpec((1, lc, LANES), lambda b: (b, 0, 0)),
            pl.BlockSpec((1, ll, LANES), lambda b: (b, 0, 0)),
            full2(conv_w), full2(conv_b2), full2(wq), full2(wkt), full2(wv), full2(gb),
        ] + cast.in_specs(),
        out_specs=[
            pl.BlockSpec((1, lc, width), lambda b: (b, 0, 0)),
            pl.BlockSpec((1, ll, width), lambda b: (b, 0, 0)),
        ] + cast.out_specs(),
        out_shape=[jax.ShapeDtypeStruct((bn, lc, width), BF16), jax.ShapeDtypeStruct((bn, ll, width), BF16)]
        + cast.out_shapes(),
        scratch_shapes=[
            pltpu.VMEM((lt, width), BF16),
            pltpu.VMEM((nct, width, ML_CHUNK), BF16),
            pltpu.VMEM((lt, 2 * width), BF16),
            pltpu.VMEM((lt, width), F32),
            pltpu.VMEM((2 * ML_HEADS, LANES, 2 * LANES), F32),
            pltpu.VMEM((nct, ML_CHUNK, LANES), F32), pltpu.VMEM((nct, ML_CHUNK, LANES), F32),
            pltpu.VMEM((nct, 4 * ML_HEADS, ML_CHUNK), F32),
            pltpu.VMEM((max(lc, ll) + 2 * SUBLANES, LANES), F32),
        ],
        compiler_params=_cparams(("arbitrary",)),
        name="mlstm_mix",
    )(p_ctx, p_lat, g_ctx, g_lat, conv_w, conv_b2, wq, wkt, wv, gb, *cast.operands())
    return outs[0], outs[1], outs[2:]


def _ffn_tail(x, y, gate_mix, g2, shift, scale, gate_ffn, w1_ref, w3_ref, w2_ref, hidden_chunk):
    x1 = x + gate_mix * y
    h2 = _norm_mod(x1, g2, shift, scale).astype(BF16)
    hidden = w1_ref.shape[1]
    acc = jnp.zeros_like(x1)
    start = 0
    while start < hidden:
        cs = slice(start, min(start + hidden_chunk, hidden))
        z = (_silu(_dot(h2, w1_ref[:, cs])) * _dot(h2, w3_ref[:, cs])).astype(BF16)
        acc = acc + _dot(z, w2_ref[cs, :])
        start += hidden_chunk
    return x1 + gate_ffn * acc


def _final_norm(x, g):
    return x * lax.rsqrt(jnp.mean(x * x, axis=-1, keepdims=True) + EPS) * g


def _outproj_even_kernel(x_ref, ys_ref, m_ref, og_ref, gm_ref, sh_ref, sc_ref, gf_ref, g2_ref, gw_ref, gbias_ref,
                         ng_ref, wo_ref, w1_ref, w3_ref, w2_ref, o_ref, *, hidden_chunk, heads):
    ys = jax.nn.gelu(ys_ref[0])
    s = ys * jax.nn.sigmoid(_dot(ys.astype(BF16), gw_ref[...]) + gbias_ref[...])
    dh = m_ref.shape[-1] // heads
    gated = []
    for h in range(heads):
        cols = slice(h * dh, (h + 1) * dh)
        hh = m_ref[0, :, cols].astype(F32)
        mu = jnp.mean(hh, axis=1, keepdims=True)
        cen = hh - mu
        var = jnp.mean(cen * cen, axis=1, keepdims=True)
        normed = cen * lax.rsqrt(var + EPS) * ng_ref[:, cols]
        gated.append((jax.nn.sigmoid(og_ref[0, :, cols].astype(F32)) * normed).astype(BF16))
    sw = ys_ref.shape[-1]
    y = _dot(s.astype(BF16), wo_ref[0:sw, :]) + _dot(jnp.concatenate(gated, axis=1), wo_ref[sw:, :])
    o_ref[0] = _ffn_tail(x_ref[0], y, gm_ref[0], g2_ref[...], sh_ref[0], sc_ref[0], gf_ref[0],
                         w1_ref, w3_ref, w2_ref, hidden_chunk)


def _outproj_odd_kernel(x_ref, a_ref, gate_ref, gm_ref, sh_ref, sc_ref, gf_ref, g2_ref, ng_ref, wo_ref, w1_ref,
                        w3_ref, w2_ref, fg_ref, o_ref, *, hidden_chunk, heads):
    hv = a_ref.shape[-1] // heads
    y = jnp.zeros(x_ref.shape[1:], F32)
    for h in range(heads):
        cols = slice(h * hv, (h + 1) * hv)
        o = a_ref[0, :, cols].astype(F32)
        mu = jnp.mean(o, axis=1, keepdims=True)
        cen = o - mu
        var = jnp.mean(cen * cen, axis=1, keepdims=True)
        normed = cen * lax.rsqrt(var + EPS) * ng_ref[:, cols]
        gated = (_silu(gate_ref[0, :, cols].astype(F32)) * normed).astype(BF16)
        y = y + _dot(gated, wo_ref[cols, :])
    out = _ffn_tail(x_ref[0], y, gm_ref[0], g2_ref[...], sh_ref[0], sc_ref[0], gf_ref[0],
                    w1_ref, w3_ref, w2_ref, hidden_chunk)
    o_ref[0] = _final_norm(out, fg_ref[...])


MXU_WIDTH = 256


def _hidden_chunk(hidden):
    return -(-hidden // (2 * MXU_WIDTH)) * MXU_WIDTH


def _mod_specs(d, row_of_batch, slots):
    return [pl.BlockSpec((1, 1, d), lambda b, i, s=s: (row_of_batch(b) * N_MOD + s, 0, 0)) for s in slots]


def _outproj_even(x, ys, m, p, mods, row_of_batch, g2, glu_w, glu_b, norm_g, w_out, ffn, cast_items=()):
    bn, ln, d = x.shape
    sw, mw = ys.shape[-1], m.shape[-1]
    tm = _token_tile(ln)
    grid = (bn, ln // tm)
    cast = _CastAlong(cast_items, grid)
    res = lambda a: _resident(a.shape, lambda b, i: (0,) * a.ndim)
    tok = lambda w: pl.BlockSpec((1, tm, w), lambda b, i: (b, i, 0))
    row = lambda w: pl.BlockSpec((1, w), lambda b, i: (0, 0))
    kern = functools.partial(_outproj_even_kernel, hidden_chunk=_hidden_chunk(ffn[0].shape[-1]), heads=ML_HEADS)
    outs = pl.pallas_call(
        cast.wrap(kern, n_in=16, n_out=1),
        grid=grid,
        in_specs=[tok(d), tok(sw), tok(mw), pl.BlockSpec((1, tm, mw), lambda b, i: (b, i, 1))]
        + _mod_specs(d, row_of_batch, (2, 3, 4, 5))
        + [row(d), res(glu_w), row(sw), row(mw), res(w_out)] + [res(w) for w in ffn] + cast.in_specs(),
        out_specs=[tok(d)] + cast.out_specs(),
        out_shape=[jax.ShapeDtypeStruct((bn, ln, d), F32)] + cast.out_shapes(),
        compiler_params=_cparams(("arbitrary", "arbitrary")),
        name="outproj_ffn_even",
    )(x, ys, m, p, mods, mods, mods, mods, g2, glu_w, glu_b.reshape(1, sw), norm_g.reshape(1, mw), w_out, *ffn,
      *cast.operands())
    return outs[0], outs[1:]


def _outproj_odd(x, a, gate, mods, row_of_batch, g2, norm_g, w_out, ffn, final_g):
    bn, ln, d = x.shape
    vw = a.shape[-1]
    tm = _token_tile(ln)
    res = lambda a_: _resident(a_.shape, lambda b, i: (0,) * a_.ndim)
    tok = lambda w: pl.BlockSpec((1, tm, w), lambda b, i: (b, i, 0))
    row = lambda w: pl.BlockSpec((1, w), lambda b, i: (0, 0))
    return pl.pallas_call(
        functools.partial(_outproj_odd_kernel, hidden_chunk=_hidden_chunk(ffn[0].shape[-1]), heads=RET_HEADS),
        grid=(bn, ln // tm),
        in_specs=[tok(d), tok(vw), tok(vw)] + _mod_specs(d, row_of_batch, (2, 3, 4, 5))
        + [row(d), row(vw), res(w_out)] + [res(w) for w in ffn] + [row(d)],
        out_specs=tok(d),
        out_shape=jax.ShapeDtypeStruct((bn, ln, d), F32),
        compiler_params=_cparams(("arbitrary", "arbitrary")),
        name="outproj_ffn_odd",
    )(x, a, gate, mods, mods, mods, mods, g2, norm_g.reshape(1, vw), w_out, *ffn, final_g)


def _rope_heads(t, cos, sin, heads, hk):
    half = hk // 2
    out = []
    for h in range(heads):
        t1 = t[:, h * hk:h * hk + half]
        t2 = t[:, h * hk + half:(h + 1) * hk]
        out.append(t1 * cos - t2 * sin)
        out.append(t2 * cos + t1 * sin)
    return jnp.concatenate(out, axis=-1)


def _inproj_odd_kernel(x_ref, sh_ref, sc_ref, g_ref, cos_ref, sin_ref, w_ref, q_ref, k_ref, v_ref, gate_ref,
                       *, qk, vw, heads):
    h = _norm_mod(x_ref[0], g_ref[...], sh_ref[0], sc_ref[0]).astype(BF16)
    hk = qk // heads
    cos, sin = cos_ref[...], sin_ref[...]
    q = _rope_heads(_dot(h, w_ref[:, 0:qk]), cos, sin, heads, hk)
    q_ref[0] = (q * (hk ** -0.5)).astype(BF16)
    k = _rope_heads(_dot(h, w_ref[:, qk:2 * qk]), cos, sin, heads, hk)
    k_ref[0] = k.astype(BF16)
    v_ref[0] = _dot(h, w_ref[:, 2 * qk:2 * qk + vw]).astype(BF16)
    gate_ref[0] = _dot(h, w_ref[:, 2 * qk + vw:2 * qk + 2 * vw]).astype(BF16)


def _inproj_odd_ctx_kernel(x_ref, sh_ref, sc_ref, g_ref, w_ref, k_ref, v_ref, *, qk, vw):
    h = _norm_mod(x_ref[0], g_ref[...], sh_ref[0], sc_ref[0]).astype(BF16)
    k_ref[0] = _dot(h, w_ref[:, qk:2 * qk]).astype(BF16)
    v_ref[0] = _dot(h, w_ref[:, 2 * qk:2 * qk + vw]).astype(BF16)


def _inproj_odd(x, mods, row_of_batch, g, cos, sin, w, qk, vw, cast_items=()):
    bn, ln, d = x.shape
    tm = _token_tile(ln)
    grid = (bn, ln // tm)
    cast = _CastAlong(cast_items, grid)
    tok = lambda w_: pl.BlockSpec((1, tm, w_), lambda b, i: (b, i, 0))
    half = cos.shape[1]
    outs = pl.pallas_call(
        cast.wrap(functools.partial(_inproj_odd_kernel, qk=qk, vw=vw, heads=RET_HEADS), n_in=7, n_out=4),
        grid=grid,
        in_specs=[tok(d)] + _mod_specs(d, row_of_batch, (0, 1))
        + [pl.BlockSpec((1, d), lambda b, i: (0, 0)),
           pl.BlockSpec((tm, half), lambda b, i: (i, 0)), pl.BlockSpec((tm, half), lambda b, i: (i, 0)),
           _resident(w.shape, lambda b, i: (0, 0))] + cast.in_specs(),
        out_specs=[tok(qk), tok(qk), tok(vw), tok(vw)] + cast.out_specs(),
        out_shape=[jax.ShapeDtypeStruct((bn, ln, qk), BF16), jax.ShapeDtypeStruct((bn, ln, qk), BF16),
                   jax.ShapeDtypeStruct((bn, ln, vw), BF16), jax.ShapeDtypeStruct((bn, ln, vw), BF16)]
        + cast.out_shapes(),
        compiler_params=_cparams(("arbitrary", "arbitrary")),
        name="inproj_odd",
    )(x, mods, mods, g, cos, sin, w, *cast.operands())
    return outs[:4], outs[4:]


def _inproj_odd_ctx(x, mods, row_of_batch, g, w_kv, qk, vw):
    bn, ln, d = x.shape
    tm = _token_tile(ln)
    tok = lambda w_: pl.BlockSpec((1, tm, w_), lambda b, i: (b, i, 0))
    return pl.pallas_call(
        functools.partial(_inproj_odd_ctx_kernel, qk=qk, vw=vw),
        grid=(bn, ln // tm),
        in_specs=[tok(d)] + _mod_specs(d, row_of_batch, (0, 1))
        + [pl.BlockSpec((1, d), lambda b, i: (0, 0)), _resident(w_kv.shape, lambda b, i: (0, 0))],
        out_specs=[tok(qk), tok(vw)],
        out_shape=[jax.ShapeDtypeStruct((bn, ln, qk), BF16), jax.ShapeDtypeStruct((bn, ln, vw), BF16)],
        compiler_params=_cparams(("arbitrary", "arbitrary")),
        name="inproj_odd_ctx",
    )(x, mods, mods, g, w_kv)


def _retention_kernel(lg_ref, q_ref, k_ref, v_ref, kc_ref, vc_ref, o_ref, acc_s, sf_s, sb_s,
                      *, chunk, n_ctx_rows, n_lat_rows):
    t_n = chunk
    h = pl.program_id(1)
    lgf = jnp.full((1, 1), lg_ref[0, h], F32)
    lgb = jnp.full((1, 1), lg_ref[1, h], F32)
    nc_ctx, nc_lat = n_ctx_rows // t_n, n_lat_rows // t_n
    ti = lax.broadcasted_iota(jnp.int32, (t_n, t_n), 0)
    si = lax.broadcasted_iota(jnp.int32, (t_n, t_n), 1)
    diff = (ti - si).astype(F32)
    decay = jnp.where(diff >= 0, jnp.exp(lgf * jnp.maximum(diff, 0.0)), jnp.exp(lgb * jnp.maximum(-diff, 0.0)))
    pos = lax.broadcasted_iota(jnp.int32, (t_n, 1), 0).astype(F32)
    inter_f = jnp.exp(lgf * (pos + 1.0))
    inter_b = jnp.exp(lgb * (t_n - pos))
    wend_f = jnp.exp(lgf * (t_n - 1.0 - pos))
    wend_b = jnp.exp(lgb * pos)
    cd_f = jnp.exp(lgf * t_n)
    cd_b = jnp.exp(lgb * t_n)

    def rows_of(c):
        return pl.ds(pl.multiple_of(c * t_n, t_n), t_n)

    def bump(s_ref, k, v, wend, cd):
        kw = (k.astype(F32) * wend).astype(BF16)
        s_ref[...] = cd * s_ref[...] + _dot_tn(kw, v)

    sf_s[...] = jnp.zeros_like(sf_s)
    sb_s[...] = jnp.zeros_like(sb_s)

    def ctx_f(c, _):
        bump(sf_s, kc_ref[0, rows_of(c), :], vc_ref[0, rows_of(c), :], wend_f, cd_f)
        return 0
    lax.fori_loop(0, nc_ctx, ctx_f, 0)

    def ctx_b(i, _):
        c = nc_ctx - 1 - i
        bump(sb_s, kc_ref[0, rows_of(c), :], vc_ref[0, rows_of(c), :], wend_b, cd_b)
        return 0
    lax.fori_loop(0, nc_ctx, ctx_b, 0)

    def forward_part(rows):
        q, k, v = q_ref[0, rows, :], k_ref[0, rows, :], v_ref[0, rows, :]
        scores = (_dot_nt(q, k) * decay).astype(BF16)
        part = _dot(scores, v) + inter_f * _dot(q, sf_s[...].astype(BF16))
        bump(sf_s, k, v, wend_f, cd_f)
        return part

    def backward_part(rows):
        q, k, v = q_ref[0, rows, :], k_ref[0, rows, :], v_ref[0, rows, :]
        part = inter_b * _dot(q, sb_s[...].astype(BF16))
        bump(sb_s, k, v, wend_b, cd_b)
        return part

    def first_half(i, _):
        rf, rb = rows_of(i), rows_of(nc_lat - 1 - i)
        acc_s[rf, :] = forward_part(rf)
        acc_s[rb, :] = backward_part(rb)
        return 0
    pair_unroll = 4 if (nc_lat // 2) % 4 == 0 else 1
    lax.fori_loop(0, nc_lat // 2, first_half, 0, unroll=pair_unroll)

    def second_half(i, _):
        rf, rb = rows_of(i), rows_of(nc_lat - 1 - i)
        o_ref[0, rf, :] = (acc_s[rf, :] + forward_part(rf)).astype(BF16)
        o_ref[0, rb, :] = (acc_s[rb, :] + backward_part(rb)).astype(BF16)
        return 0
    lax.fori_loop(nc_lat // 2, nc_lat, second_half, 0, unroll=pair_unroll)


def _retention_mix(q, k, v, k_ctx, v_ctx, log_gamma):
    bn, ll, qk = q.shape
    lc = k_ctx.shape[1]
    vw = v.shape[-1]
    hk, hv = qk // RET_HEADS, vw // RET_HEADS
    chunk = 256 if (ll % 512 == 0 and lc % 256 == 0) else 128
    assert (ll // chunk) % 2 == 0, "latent chunks are visited in forward/backward pairs"
    head = lambda n_rows, w: pl.BlockSpec((1, n_rows, w), lambda b, h: (b, 0, h))
    return pl.pallas_call(
        functools.partial(_retention_kernel, chunk=chunk, n_ctx_rows=lc, n_lat_rows=ll),
        grid=(bn, RET_HEADS),
        in_specs=[
            pl.BlockSpec(memory_space=pltpu.SMEM),
            head(ll, hk), head(ll, hk), head(ll, hv), head(lc, hk), head(lc, hv),
        ],
        out_specs=head(ll, hv),
        out_shape=jax.ShapeDtypeStruct((bn, ll, vw), BF16),
        scratch_shapes=[pltpu.VMEM((ll, hv), F32), pltpu.VMEM((hk, hv), F32), pltpu.VMEM((hk, hv), F32)],
        compiler_params=_cparams(("arbitrary", "arbitrary")),
        name="retention_mix",
    )(log_gamma.astype(F32), q, k, v, k_ctx, v_ctx)


def _grid_rope(n_pos, hk):
    rows = n_pos // GRID_W
    row = jnp.repeat(jnp.arange(rows, dtype=F32), GRID_W)
    col = jnp.tile(jnp.arange(GRID_W, dtype=F32), rows)
    n_freq = hk // 4
    inv = ROPE_BASE ** (-jnp.arange(n_freq, dtype=F32) / n_freq)
    ang = jnp.concatenate([row[:, None] * inv, col[:, None] * inv], -1)
    return jnp.cos(ang), jnp.sin(ang)


def kernel(x, c, ctx, c_ctx, ada_w, ada_b, norm1_g, norm2_g, ab_w_in, ab_w_out, s5_a_re, s5_a_im, s5_log_dt,
           s5_b_re, s5_b_im, s5_c_re, s5_c_im, s5_d, s5_glu_w, s5_glu_b, ml_conv_w, ml_conv_b, ml_wq, ml_wk,
           ml_wv, ml_gate_b, ml_norm_g, ret_w_in, ret_w_out, ret_log_gamma, ret_norm_g, ffn_w1, ffn_w3, ffn_w2,
           final_g):
    bn, ln, d = x.shape
    depth = ada_w.shape[0]
    assert depth == 2, "one S5 || mLSTM layer followed by one retention layer"
    s5_width = s5_d.shape[-1]
    ml_width = ml_norm_g.shape[-1]
    assert ml_width == ML_HEADS * LANES and s5_width % S5_GROUP == 0

    r_pad = -(-(bn + 1) // SUBLANES) * SUBLANES
    vec = jnp.zeros((r_pad, d), F32).at[:bn].set(c).at[bn].set(c_ctx)
    mods_all = _modulation(vec, ada_w, ada_b).reshape(depth, r_pad * N_MOD, 1, d)
    lat_row = lambda b: b
    ctx_row = lambda b: bn

    mods = mods_all[0]
    g1 = norm1_g[0].reshape(1, d)
    g2 = norm2_g[0].reshape(1, d)
    w_in = ab_w_in[0]
    n_gate = w_in.shape[1] - s5_width - 2 * ml_width
    w_in = jnp.pad(w_in, ((0, 0), (0, LANES - n_gate))).astype(BF16)
    u_lat, p_lat, g_lat = _inproj_even(x, mods, lat_row, g1, w_in, s5_width)
    lc = ctx.shape[1]
    flat = lambda t: t.reshape(1, bn * lc, t.shape[-1])
    unflat = lambda t: t.reshape(bn, lc, t.shape[-1])
    u_ctx, p_ctx, g_ctx = (unflat(t) for t in _inproj_even(flat(ctx), mods, ctx_row, g1, w_in, s5_width))
    mats = _s5_prep(s5_a_re[0], s5_a_im[0], s5_log_dt[0], s5_b_re[0], s5_b_im[0], s5_c_re[0], s5_c_im[0],
                        s5_d[0])
    ys_ctx, ys_lat = _s5_mix(u_ctx, u_lat, mats)
    m_ctx, m_lat, ffn = _mlstm_mix(p_ctx, p_lat, g_ctx, g_lat, ml_conv_w[0], ml_conv_b[0], ml_wq[0], ml_wk[0],
                                   ml_wv[0], ml_gate_b[0], [(ffn_w1, 0), (ffn_w3, 0), (ffn_w2, 0)])
    glu_w = s5_glu_w[0].astype(BF16)
    w_out = ab_w_out[0].astype(BF16)
    x, (ret_in,) = _outproj_even(x, ys_lat, m_lat, p_lat, mods, lat_row, g2, glu_w, s5_glu_b[0], ml_norm_g[0],
                                 w_out, ffn, [(ret_w_in, 0)])
    ctx, _ = _outproj_even(flat(ctx), flat(ys_ctx), flat(m_ctx), flat(p_ctx), mods, ctx_row, g2, glu_w,
                           s5_glu_b[0], ml_norm_g[0], w_out, ffn)

    mods = mods_all[1]
    g1 = norm1_g[1].reshape(1, d)
    g2 = norm2_g[1].reshape(1, d)
    vw = ret_norm_g.shape[-1]
    qk = (ret_w_in.shape[-1] - 2 * vw) // 2
    cos, sin = _grid_rope(ln, qk // RET_HEADS)
    (q, k, v, gate), ffn = _inproj_odd(x, mods, lat_row, g1, cos, sin, ret_in, qk, vw,
                                       [(ffn_w1, 1), (ffn_w3, 1), (ffn_w2, 1)])
    k_ctx, v_ctx = (unflat(t) for t in _inproj_odd_ctx(ctx, mods, ctx_row, g1, ret_in, qk, vw))
    a = _retention_mix(q, k, v, k_ctx, v_ctx, ret_log_gamma[0])
    w_out = ret_w_out[0].astype(BF16)
    return _outproj_odd(x, a, gate, mods, lat_row, g2, ret_norm_g[0], w_out, ffn, final_g.reshape(1, d))
```

```python
import functools
import math

import jax
import jax.numpy as jnp
from jax import lax
from jax.experimental import pallas as pl
from jax.experimental.pallas import tpu as pltpu

F32 = jnp.float32
BF16 = jnp.bfloat16

EPS = 1e-6
N_MOD = 6
GRID_W = 64
ROPE_BASE = 10000.0

S5_GROUP = 16
S5_STATE = 64
S5_STEP = 16
ML_HEADS = 4
ML_CHUNK = 128
RET_HEADS = 4
SUBLANES = 8
MXU_WIDTH = 256
LANES = 128
NEG_BIG = -1e30
VMEM_LIMIT = 56 * 1024 * 1024


def _cparams(sem):
    return pltpu.CompilerParams(dimension_semantics=sem, vmem_limit_bytes=VMEM_LIMIT)


def _resident(shape, index_map):
    return pl.BlockSpec(shape, index_map, pipeline_mode=pl.Buffered(1))


BF16_ROWS = 16


class _CastAlong:
    def __init__(self, weights_and_layers, grid):
        self.items = list(weights_and_layers)
        steps = math.prod(grid)
        strides = [math.prod(grid[i + 1:]) for i in range(len(grid))]
        self.step = lambda idx: sum(i * s for i, s in zip(idx, strides))
        self.plans = []
        for w, _ in self.items:
            _, r, c = w.shape
            if r % (steps * BF16_ROWS) == 0:
                self.plans.append(((r // steps, c), lambda s: (s, 0)))
            else:
                assert steps % 2 == 0 and r % (steps // 2 * BF16_ROWS) == 0 and c % (2 * LANES) == 0
                self.plans.append(((2 * r // steps, c // 2), lambda s: (s // 2, s % 2)))

    def __len__(self):
        return len(self.items)

    def in_specs(self):
        return [pl.BlockSpec((None,) + blk, lambda *idx, f=f, l=l: (l,) + f(self.step(idx)))
                for (blk, f), (_, l) in zip(self.plans, self.items)]

    def out_specs(self):
        return [pl.BlockSpec(blk, lambda *idx, f=f: f(self.step(idx))) for blk, f in self.plans]

    def out_shapes(self):
        return [jax.ShapeDtypeStruct(w.shape[1:], BF16) for w, _ in self.items]

    def operands(self):
        return [w for w, _ in self.items]

    def wrap(self, kernel_fn, n_in, n_out):
        n = len(self)

        def wrapped(*refs):
            ins, cast_in = refs[:n_in], refs[n_in:n_in + n]
            outs, cast_out = refs[n_in + n:n_in + n + n_out], refs[n_in + n + n_out:n_in + 2 * n + n_out]
            for src, dst in zip(cast_in, cast_out):
                dst[...] = src[...].astype(BF16)
            kernel_fn(*ins, *outs, *refs[n_in + 2 * n + n_out:])
        return wrapped


def _token_tile(n, largest=512):
    for t in (largest, 512, 256, 128):
        if n % t == 0:
            return t
    raise ValueError(f"sequence length {n} must be a multiple of 128")


def _silu(v):
    return v * jax.nn.sigmoid(v)


def _norm_mod(x, g, shift, scale):
    y = x * lax.rsqrt(jnp.mean(x * x, axis=-1, keepdims=True) + EPS)
    return (y * g) * (1.0 + scale) + shift


def _dot(a, b):
    return jnp.dot(a, b, preferred_element_type=F32)


def _dot_nt(a, b):
    return lax.dot_general(a, b, (((1,), (1,)), ((), ())), preferred_element_type=F32)


def _dot_tn(a, b):
    return lax.dot_general(a, b, (((0,), (0,)), ((), ())), preferred_element_type=F32)


def _mod_kernel(v_ref, w_ref, b_ref, o_ref):
    s = _silu(v_ref[...]).astype(BF16)
    o_ref[0] = _dot(s, w_ref[0].astype(BF16)) + b_ref[0]


def _modulation(vec, ada_w, ada_b):
    depth, d, n = ada_w.shape
    r = vec.shape[0]
    tn = 1024
    return pl.pallas_call(
        _mod_kernel,
        grid=(depth, n // tn),
        in_specs=[
            pl.BlockSpec((r, d), lambda l, j: (0, 0)),
            pl.BlockSpec((1, d, tn), lambda l, j: (l, 0, j)),
            pl.BlockSpec((1, 1, tn), lambda l, j: (l, 0, j)),
        ],
        out_specs=pl.BlockSpec((1, r, tn), lambda l, j: (l, 0, j)),
        out_shape=jax.ShapeDtypeStruct((depth, r, n), F32),
        compiler_params=_cparams(("arbitrary", "arbitrary")),
        name="adaln_modulation",
    )(vec, ada_w, ada_b.reshape(depth, 1, n))


def _inproj_even_kernel(x_ref, sh_ref, sc_ref, g_ref, w_ref, u_ref, p_ref, gate_ref, *, s5_width):
    h = _norm_mod(x_ref[0], g_ref[...], sh_ref[0], sc_ref[0]).astype(BF16)
    p = _dot(h, w_ref[...])
    n_mid = p_ref.shape[-1]
    u_ref[0] = p[:, :s5_width]
    p_ref[0] = p[:, s5_width:s5_width + n_mid].astype(BF16)
    gate_ref[0] = p[:, s5_width + n_mid:]


def _inproj_even(x, mods, row_of_batch, g, w, s5_width):
    bn, ln, d = x.shape
    n = w.shape[1]
    n_mid = n - s5_width - LANES
    tm = _token_tile(ln, largest=1024)
    return pl.pallas_call(
        functools.partial(_inproj_even_kernel, s5_width=s5_width),
        grid=(bn, ln // tm),
        in_specs=[
            pl.BlockSpec((1, tm, d), lambda b, i: (b, i, 0)),
            pl.BlockSpec((1, 1, d), lambda b, i: (row_of_batch(b) * N_MOD + 0, 0, 0)),
            pl.BlockSpec((1, 1, d), lambda b, i: (row_of_batch(b) * N_MOD + 1, 0, 0)),
            pl.BlockSpec((1, d), lambda b, i: (0, 0)),
            _resident((d, n), lambda b, i: (0, 0)),
        ],
        out_specs=[
            pl.BlockSpec((1, tm, s5_width), lambda b, i: (b, i, 0)),
            pl.BlockSpec((1, tm, n_mid), lambda b, i: (b, i, 0)),
            pl.BlockSpec((1, tm, LANES), lambda b, i: (b, i, 0)),
        ],
        out_shape=[
            jax.ShapeDtypeStruct((bn, ln, s5_width), F32),
            jax.ShapeDtypeStruct((bn, ln, n_mid), BF16),
            jax.ShapeDtypeStruct((bn, ln, LANES), F32),
        ],
        compiler_params=_cparams(("arbitrary", "arbitrary")),
        name="inproj_even",
    )(x, mods, mods, g, w)


def _s5_prep_kernel(sp_ref, bt_ref, c_ref, d_ref, toep_ref, endw_ref, outw_ref, a0_ref, a1_ref, *, steps, chans):
    k, p = steps, S5_STATE
    width = k * chans
    lane = lax.broadcasted_iota(jnp.int32, (1, LANES), 1)
    low = lane < p
    lag = lax.broadcasted_iota(jnp.int32, (k + SUBLANES, LANES), 0).astype(F32)
    row = lax.broadcasted_iota(jnp.int32, (chans, LANES), 0)
    lane_c = lax.broadcasted_iota(jnp.int32, (chans, LANES), 1)
    lag0 = (k - 1) * chans
    diag = lane_c == row + lag0 % LANES
    pair = lambda x, y: x + pltpu.roll(y, p, 1)
    zeros = jnp.zeros((chans, LANES), F32)

    for n in range(sp_ref.shape[0]):
        wr, wi, pr, pi, cre, cim = [], [], [], [], [], []
        for d in range(2):
            a_re, a_im = sp_ref[n, 4 * d:4 * d + 1, :], sp_ref[n, 4 * d + 1:4 * d + 2, :]
            l_re, l_im = sp_ref[n, 4 * d + 2:4 * d + 3, :], sp_ref[n, 4 * d + 3:4 * d + 4, :]
            mag = jnp.exp(lag * l_re)
            p_re, p_im = mag * jnp.cos(lag * l_im), mag * jnp.sin(lag * l_im)
            den = jnp.where(low, a_re * a_re + a_im * a_im, 1.0)
            nr = p_re[1:2] - 1.0
            coef_re = (nr * a_re + p_im[1:2] * a_im) / den
            coef_im = (p_im[1:2] * a_re - nr * a_im) / den
            b_re, b_im = bt_ref[n, 2 * d * chans:(2 * d + 1) * chans, :], bt_ref[n, (2 * d + 1) * chans:(2 * d + 2) * chans, :]
            bb_re = coef_re * b_re - coef_im * b_im
            bb_im = coef_re * b_im + coef_im * b_re
            wr.append([p_re[l:l + 1] * bb_re - p_im[l:l + 1] * bb_im for l in range(k)])
            wi.append([p_re[l:l + 1] * bb_im + p_im[l:l + 1] * bb_re for l in range(k)])
            pr.append(p_re)
            pi.append(p_im)
            cre.append(c_ref[n, 2 * d * chans:(2 * d + 1) * chans, :])
            cim.append(c_ref[n, (2 * d + 1) * chans:(2 * d + 2) * chans, :])

        for s in range(k):
            endw_ref[n, s * chans:(s + 1) * chans, :] = jnp.concatenate(
                [pair(wr[0][k - 1 - s], wr[1][s]), pair(wi[0][k - 1 - s], wi[1][s])], axis=1).astype(BF16)
        for t in range(k):
            f_re, f_im = pr[0][t + 1:t + 2], pi[0][t + 1:t + 2]
            b_re, b_im = pr[1][k - t:k - t + 1], pi[1][k - t:k - t + 1]
            outw_ref[n, t * chans:(t + 1) * chans, :] = jnp.concatenate(
                [pair(cre[0] * f_re - cim[0] * f_im, cre[1] * b_re - cim[1] * b_im),
                 pair(-(cre[0] * f_im + cim[0] * f_re), -(cre[1] * b_im + cim[1] * b_re))], axis=1).astype(BF16)
        a0_ref[n:n + 1, :] = jnp.where(low, pr[0][k:k + 1], pltpu.roll(pr[1][k:k + 1], p, 1))
        a1_ref[n:n + 1, :] = jnp.where(low, pi[0][k:k + 1], pltpu.roll(pi[1][k:k + 1], p, 1))

        blocks = []
        for m in range(2 * k):
            fwd = pair(wr[0][k - 1 - m], -wi[0][k - 1 - m]) if m < k else zeros
            bwd = pair(wr[1][m - k + 1], -wi[1][m - k + 1]) if k - 1 <= m < 2 * k - 1 else zeros
            blocks.append(jnp.concatenate([fwd, bwd], axis=1))
        w_gen = jnp.concatenate(blocks, axis=0)
        c2 = jnp.concatenate([pair(cre[0], cim[0]), pair(cre[1], cim[1])], axis=1)
        gen = lax.dot_general(c2, w_gen, (((1,), (1,)), ((), ())),
                              precision=lax.Precision.HIGHEST, preferred_element_type=F32)
        pieces = [gen[:, j * LANES:(j + 1) * LANES] for j in range(2 * width // LANES)]
        pieces[lag0 // LANES] = pieces[lag0 // LANES] + jnp.where(diag, d_ref[n], 0.0)
        gen = jnp.concatenate(pieces, axis=1)
        for t in range(k):
            off = (k - 1 - t) * chans
            toep_ref[n, t * chans:(t + 1) * chans, :] = gen[:, off:off + width].astype(BF16)


def _s5_prep(a_re, a_im, log_dt, b_re, b_im, c_re, c_im, d_skip):
    k = S5_STEP
    g_n, p_n, c_n = a_re.shape[1], a_re.shape[2], b_re.shape[-1]
    assert 2 * p_n == LANES
    lanes = lambda t: jnp.pad(t.astype(F32), [(0, 0)] * (t.ndim - 1) + [(0, LANES - p_n)])
    dt = jnp.exp(log_dt.astype(F32))[..., None]
    a_re, a_im = a_re.astype(F32), a_im.astype(F32)
    sp = lanes(jnp.stack([a_re[0], a_im[0], a_re[0] * dt[0], a_im[0] * dt[0],
                          a_re[1], a_im[1], a_re[1] * dt[1], a_im[1] * dt[1]], axis=1))
    tr = lambda t: jnp.swapaxes(t, -1, -2)
    bt = lanes(jnp.concatenate([tr(b_re[0]), tr(b_im[0]), tr(b_re[1]), tr(b_im[1])], axis=1))
    cc = lanes(jnp.concatenate([c_re[0], c_im[0], c_re[1], c_im[1]], axis=1))
    d_rep = jnp.broadcast_to(d_skip.astype(F32).reshape(g_n, c_n, 1), (g_n, c_n, LANES))
    nb = 8
    blk = lambda r, c: pl.BlockSpec((nb, r, c), lambda i: (i, 0, 0))
    kc = k * c_n
    mat = jax.ShapeDtypeStruct((g_n, kc, kc), BF16)
    vec = jax.ShapeDtypeStruct((g_n, LANES), F32)
    return pl.pallas_call(
        functools.partial(_s5_prep_kernel, steps=k, chans=c_n),
        grid=(g_n // nb,),
        in_specs=[blk(8, LANES), blk(4 * c_n, LANES), blk(4 * c_n, LANES), blk(c_n, LANES)],
        out_specs=[blk(kc, kc), blk(kc, kc), blk(kc, kc),
                   pl.BlockSpec((nb, LANES), lambda i: (i, 0)), pl.BlockSpec((nb, LANES), lambda i: (i, 0))],
        out_shape=[mat, mat, mat, vec, vec],
        compiler_params=_cparams(("arbitrary",)),
        name="s5_prep",
    )(sp, bt, cc, d_rep)


S5_LANE_GROUPS = LANES // S5_GROUP


def _chunk_transpose(xs):
    n_arr = len(xs)
    lane_chunk = lax.broadcasted_iota(jnp.int32, xs[0].shape, 1) // S5_GROUP
    d = 1
    while d < n_arr:
        keep = (lane_chunk & d) == 0
        out = list(xs)
        for i in range(n_arr):
            if i & d:
                continue
            lo, hi = xs[i], xs[i + d]
            out[i] = jnp.where(keep, lo, pltpu.roll(hi, d * S5_GROUP, 1))
            out[i + d] = jnp.where(keep, pltpu.roll(lo, LANES - d * S5_GROUP, 1), hi)
        xs = out
        d *= 2
    return xs


def _s5_kernel(uc_ref, ul_ref, toep_ref, endw_ref, outw_ref, a0_ref, a1_ref, yc_ref, yl_ref,
               uc_s, ul_s, e_s, yc_s, yl_s, *, batch, n_ctx, n_lat):
    p = S5_STATE
    gl = S5_LANE_GROUPS
    k = S5_STEP

    def stack(u_ref, us_ref, n_blk):
        for b in range(batch):
            for half in range(k // gl):
                xs = [u_ref[b, pl.ds(half * gl + s, n_blk, stride=k), :] for s in range(gl)]
                ys = _chunk_transpose(xs)
                for g in range(gl):
                    us_ref[g, b * n_blk:(b + 1) * n_blk, half * LANES:(half + 1) * LANES] = ys[g].astype(BF16)

    stack(uc_ref, uc_s, n_ctx)
    stack(ul_ref, ul_s, n_lat)

    def increments(us_ref, n_blk, first_blk):
        for g in range(gl):
            e = _dot(us_ref[g], endw_ref[g])
            for b in range(batch):
                for part in range(2):
                    e_s[part, b, pl.ds(first_blk * gl + g, n_blk, stride=gl), :] = (
                        e[b * n_blk:(b + 1) * n_blk, part * LANES:(part + 1) * LANES])

    increments(uc_s, n_ctx, 0)
    increments(ul_s, n_lat, n_ctx)

    a0 = a0_ref[...]
    a1 = a1_ref[...]
    fwd_lane = lax.broadcasted_iota(jnp.int32, (batch, gl, 2 * p), 2) < p

    def scan(first_blk, n, carry):
        def body(i, c):
            s0, s1 = c
            rf = pl.ds(pl.multiple_of((first_blk + i) * gl, gl), gl)
            rb = pl.ds(pl.multiple_of((first_blk + n - 1 - i) * gl, gl), gl)
            e0 = jnp.where(fwd_lane, e_s[0, :, rf, :], e_s[0, :, rb, :])
            e1 = jnp.where(fwd_lane, e_s[1, :, rf, :], e_s[1, :, rb, :])
            e_s[0, :, rf, 0:p] = s0[:, :, 0:p]
            e_s[0, :, rb, p:2 * p] = s0[:, :, p:2 * p]
            e_s[1, :, rf, 0:p] = s1[:, :, 0:p]
            e_s[1, :, rb, p:2 * p] = s1[:, :, p:2 * p]
            return a0 * s0 - a1 * s1 + e0, a0 * s1 + a1 * s0 + e1
        return lax.fori_loop(0, n, body, carry, unroll=4 if n % 4 == 0 else 1)

    zero = jnp.zeros((batch, gl, 2 * p), F32)
    carry = scan(0, n_ctx, (zero, zero))
    scan(n_ctx, n_lat, carry)

    def outputs(us_ref, ys_ref, y_ref, n_blk, first_blk):
        for g in range(gl):
            state = jnp.concatenate(
                [jnp.concatenate([e_s[part, b, pl.ds(first_blk * gl + g, n_blk, stride=gl), :]
                                  for b in range(batch)], axis=0) for part in range(2)], axis=1)
            ys_ref[g] = _dot_nt(us_ref[g], toep_ref[g]) + _dot_nt(state.astype(BF16), outw_ref[g])
        for b in range(batch):
            for half in range(k // gl):
                ys = [ys_ref[g, b * n_blk:(b + 1) * n_blk, half * LANES:(half + 1) * LANES] for g in range(gl)]
                xs = _chunk_transpose(ys)
                for s in range(gl):
                    y_ref[b, pl.ds(half * gl + s, n_blk, stride=k), :] = xs[s]

    outputs(uc_s, yc_s, yc_ref, n_ctx, 0)
    outputs(ul_s, yl_s, yl_ref, n_lat, n_ctx)


def _s5_mix(u_ctx, u_lat, mats):
    toep, endw, outw, a0, a1 = mats
    bn, lc, w = u_ctx.shape
    ll = u_lat.shape[1]
    gl = S5_LANE_GROUPS
    kc = S5_STEP * S5_GROUP
    n_ctx, n_lat = lc // S5_STEP, ll // S5_STEP
    bb = 4 if bn % 4 == 0 else bn
    seq = lambda n_rows: pl.BlockSpec((bb, n_rows, LANES), lambda i, j: (j, 0, i))
    grp = lambda r, c: pl.BlockSpec((gl, r, c), lambda i, j: (i, 0, 0))
    return pl.pallas_call(
        functools.partial(_s5_kernel, batch=bb, n_ctx=n_ctx, n_lat=n_lat),
        grid=(w // LANES, bn // bb),
        in_specs=[seq(lc), seq(ll), grp(kc, kc), grp(kc, kc), grp(kc, kc),
                  pl.BlockSpec((gl, 2 * S5_STATE), lambda i, j: (i, 0)),
                  pl.BlockSpec((gl, 2 * S5_STATE), lambda i, j: (i, 0))],
        out_specs=[seq(lc), seq(ll)],
        out_shape=[jax.ShapeDtypeStruct((bn, lc, w), F32), jax.ShapeDtypeStruct((bn, ll, w), F32)],
        scratch_shapes=[
            pltpu.VMEM((gl, bb * n_ctx, kc), BF16), pltpu.VMEM((gl, bb * n_lat, kc), BF16),
            pltpu.VMEM((2, bb, (n_ctx + n_lat) * gl, LANES), F32),
            pltpu.VMEM((gl, bb * n_ctx, kc), F32), pltpu.VMEM((gl, bb * n_lat, kc), F32),
        ],
        compiler_params=_cparams(("arbitrary", "arbitrary")),
        name="s5_scan",
    )(u_ctx, u_lat, toep, endw, outw, a0, a1)


def _mlstm_kernel(pc_ref, pl_ref, gc_ref, gl_ref, cw_ref, cb_ref, wq_ref, wkt_ref, wv_ref, gb_ref, mc_ref, ml_ref,
                  q_s, kt_s, vx_s, h_s, st_s, bcol_s, mcol_s, rrow_s, xpad_s, *, n_ctx_rows, n_lat_rows):
    t_n = ML_CHUNK
    width = ML_HEADS * LANES
    nc_ctx, nc_lat = n_ctx_rows // t_n, n_lat_rows // t_n
    nc = nc_ctx + nc_lat
    scale_k = LANES ** -0.5

    def project(p_ref, n_rows, base):
        ones = jnp.ones((n_rows, LANES), BF16)
        pad = SUBLANES
        xpad_s[0:pad, :] = jnp.zeros((pad, LANES), F32)
        xpad_s[pad + n_rows:2 * pad + n_rows, :] = jnp.zeros((pad, LANES), F32)
        for h in range(ML_HEADS):
            cols = slice(h * LANES, (h + 1) * LANES)
            xm_lo = p_ref[0, :, cols]
            xm = xm_lo.astype(F32)
            xpad_s[pad:pad + n_rows, :] = xm
            taps = cw_ref[:, cols]
            n_tap = taps.shape[0]
            assert n_tap // 2 <= pad
            acc = jnp.zeros_like(xm) + cb_ref[:, cols]
            for j in range(n_tap):
                d = j - n_tap // 2
                sh = xm if d == 0 else xpad_s[pad + d:pad + d + n_rows, :]
                acc = acc + sh * taps[j:j + 1, :]
            xc = _silu(acc).astype(BF16)
            q_s[base:base + n_rows, cols] = _dot(xc, wq_ref[h]).astype(BF16)
            kt = (_dot_nt(wkt_ref[h], xc) * scale_k).astype(BF16)
            for c in range(n_rows // t_n):
                kt_s[base // t_n + c, cols, :] = kt[:, c * t_n:(c + 1) * t_n]
            vx_s[base:base + n_rows, 2 * h * LANES:(2 * h + 1) * LANES] = (
                _dot(xm_lo, wv_ref[h]).astype(BF16))
            vx_s[base:base + n_rows, (2 * h + 1) * LANES:(2 * h + 2) * LANES] = ones

    project(pc_ref, n_ctx_rows, 0)
    project(pl_ref, n_lat_rows, n_ctx_rows)

    ti = lax.broadcasted_iota(jnp.int32, (t_n, t_n), 0)
    si = lax.broadcasted_iota(jnp.int32, (t_n, t_n), 1)
    tri_f = (si <= ti).astype(BF16)
    tri_b = (si >= ti).astype(BF16)
    lane = lax.broadcasted_iota(jnp.int32, (t_n, LANES), 1)
    trow = lax.broadcasted_iota(jnp.int32, (t_n, LANES), 0)

    def gate_prep(g_ref, n_chunks, base_chunk):
        def body(c, _):
            rows = pl.ds(pl.multiple_of(c * t_n, t_n), t_n)
            gcol = g_ref[0, rows, :] + gb_ref[...]
            lf = jax.nn.log_sigmoid(gcol)
            hi = lf.astype(BF16)
            lo = (lf - hi.astype(F32)).astype(BF16)
            pre = _dot(tri_f, hi) + _dot(tri_f, lo)
            suf = _dot(tri_b, hi) + _dot(tri_b, lo)
            bsum = jnp.where(lane < 2 * ML_HEADS, pre, suf)
            rcol = gcol - pltpu.roll(bsum, LANES - ML_HEADS, 1)
            pmax, smax = rcol, rcol
            step = 1
            while step < t_n:
                pmax = jnp.maximum(pmax, jnp.where(trow >= step, pltpu.roll(pmax, step, 0), NEG_BIG))
                smax = jnp.maximum(smax, jnp.where(trow < t_n - step, pltpu.roll(smax, t_n - step, 0), NEG_BIG))
                step *= 2
            bcol_s[base_chunk + c] = bsum
            mcol_s[base_chunk + c] = jnp.where(lane < 2 * ML_HEADS, pmax, smax)
            rrow_s[base_chunk + c] = rcol.T[0:4 * ML_HEADS, :]
            return 0
        lax.fori_loop(0, n_chunks, body, 0, unroll=4 if n_chunks % 4 == 0 else (2 if n_chunks % 2 == 0 else 1))

    gate_prep(gc_ref, nc_ctx, 0)
    gate_prep(gl_ref, nc_lat, nc_ctx)

    def one_dir(c, h, d, m_in):
        cols = slice(h * LANES, (h + 1) * LANES)
        xcols = slice(2 * h * LANES, (2 * h + 2) * LANES)
        li = 2 * d * ML_HEADS + h
        mask = (si <= ti) if d == 0 else (si >= ti)
        last = t_n - 1 if d == 0 else 0
        rows = pl.ds(pl.multiple_of(c * t_n, t_n), t_n)
        q = q_s[rows, cols]
        kt = kt_s[c, cols, :]
        vx = vx_s[rows, xcols]
        r_row = rrow_s[c][li:li + 1, :]
        run_max = jnp.broadcast_to(mcol_s[c][:, li:li + 1], (t_n, LANES))
        b_rep = jnp.broadcast_to(bcol_s[c][:, li + ML_HEADS:li + ML_HEADS + 1], (t_n, LANES))
        mm = jnp.maximum(m_in, run_max)
        dmat = jnp.exp(jnp.where(mask, r_row - mm, NEG_BIG))
        sm = (dmat * _dot(q, kt)).astype(BF16)
        intra = _dot(sm, vx)
        cross = _dot(q, st_s[2 * h + d].astype(BF16))
        inter = jnp.exp(m_in - mm)
        num = intra[:, :LANES] + inter * cross[:, :LANES]
        den = intra[:, LANES:] + inter * cross[:, LANES:]
        h_s[rows, cols] = h_s[rows, cols] + num / jnp.maximum(jnp.abs(den), jnp.exp(-b_rep - mm))
        r_top = run_max[last:last + 1, :]
        b_last = b_rep[last:last + 1, :]
        g_max = b_last + r_top
        m_new = jnp.maximum(b_last + m_in, g_max)
        dec = jnp.exp(b_last + m_in - m_new)
        inj = jnp.exp(g_max - m_new)
        kw = (kt.astype(F32) * (jnp.exp(r_row - r_top) * inj)).astype(BF16)
        inc = _dot(kw, vx)
        st_s[2 * h + d] = jnp.concatenate([dec, dec], axis=1) * st_s[2 * h + d] + inc
        return m_new

    st_s[...] = jnp.zeros_like(st_s)
    h_s[...] = jnp.zeros_like(h_s)

    def body(i, carry):
        cb = jnp.where(i < nc_ctx, nc_ctx - 1 - i, nc + nc_ctx - 1 - i)
        return tuple(one_dir(i if d == 0 else cb, h, d, carry[2 * h + d])
                     for h in range(ML_HEADS) for d in range(2))

    zero = jnp.zeros((1, LANES), F32)
    lax.fori_loop(0, nc, body, (zero,) * (2 * ML_HEADS), unroll=2 if nc % 2 == 0 else 1)

    mc_ref[0] = h_s[0:n_ctx_rows, :].astype(BF16)
    ml_ref[0] = h_s[n_ctx_rows:n_ctx_rows + n_lat_rows, :].astype(BF16)


def _mlstm_mix(p_ctx, p_lat, g_ctx, g_lat, conv_w, conv_b, wq, wk, wv, gate_b, cast_items):
    bn, lc, pw = p_ctx.shape
    cast = _CastAlong(cast_items, (bn,))
    ll = p_lat.shape[1]
    width = ML_HEADS * LANES
    lt = lc + ll
    nct = lt // ML_CHUNK
    gb = jnp.zeros((1, LANES), F32).at[0, :4 * ML_HEADS].set(gate_b.astype(F32))
    full2 = lambda a: pl.BlockSpec(a.shape, lambda b: (0,) * a.ndim)
    conv_b2 = conv_b.reshape(1, width)
    assert ML_CHUNK == LANES
    wq, wkt, wv = wq.astype(BF16), jnp.swapaxes(wk, 1, 2).astype(BF16), wv.astype(BF16)
    outs = pl.pallas_call(
        cast.wrap(functools.partial(_mlstm_kernel, n_ctx_rows=lc, n_lat_rows=ll), n_in=10, n_out=2),
        grid=(bn,),
        in_specs=[
            pl.BlockSpec((1, lc, pw), lambda b: (b, 0, 0)),
            pl.BlockSpec((1, ll, pw), lambda b: (b, 0, 0)),
            pl.BlockSpec((1, lc, LANES), lambda b: (b, 0, 0)),
            pl.BlockSpec((1, ll, LANES), lambda b: (b, 0, 0)),
            full2(conv_w), full2(conv_b2), full2(wq), full2(wkt), full2(wv), full2(gb),
        ] + cast.in_specs(),
        out_specs=[
            pl.BlockSpec((1, lc, width), lambda b: (b, 0, 0)),
            pl.BlockSpec((1, ll, width), lambda b: (b, 0, 0)),
        ] + cast.out_specs(),
        out_shape=[jax.ShapeDtypeStruct((bn, lc, width), BF16), jax.ShapeDtypeStruct((bn, ll, width), BF16)]
        + cast.out_shapes(),
        scratch_shapes=[
            pltpu.VMEM((lt, width), BF16),
            pltpu.VMEM((nct, width, ML_CHUNK), BF16),
            pltpu.VMEM((lt, 2 * width), BF16),
            pltpu.VMEM((lt, width), F32),
            pltpu.VMEM((2 * ML_HEADS, LANES, 2 * LANES), F32),
            pltpu.VMEM((nct, ML_CHUNK, LANES), F32), pltpu.VMEM((nct, ML_CHUNK, LANES), F32),
            pltpu.VMEM((nct, 4 * ML_HEADS, ML_CHUNK), F32),
            pltpu.VMEM((max(lc, ll) + 2 * SUBLANES, LANES), F32),
        ],
        compiler_params=_cparams(("arbitrary",)),
        name="mlstm_mix",
    )(p_ctx, p_lat, g_ctx, g_lat, conv_w, conv_b2, wq, wkt, wv, gb, *cast.operands())
    return outs[0], outs[1], outs[2:]


def _ffn_tail(x, y, gate_mix, g2, shift, scale, gate_ffn, w1_ref, w3_ref, w2_ref, hidden_chunk):
    x1 = x + gate_mix * y
    h2 = _norm_mod(x1, g2, shift, scale).astype(BF16)
    hidden = w1_ref.shape[1]
    acc = jnp.zeros_like(x1)
    start = 0
    while start < hidden:
        cs = slice(start, min(start + hidden_chunk, hidden))
        z = (_silu(_dot(h2, w1_ref[:, cs])) * _dot(h2, w3_ref[:, cs])).astype(BF16)
        acc = acc + _dot(z, w2_ref[cs, :])
        start += hidden_chunk
    return x1 + gate_ffn * acc


def _final_norm(x, g):
    return x * lax.rsqrt(jnp.mean(x * x, axis=-1, keepdims=True) + EPS) * g


def _outproj_even_kernel(x_ref, ys_ref, m_ref, og_ref, gm_ref, sh_ref, sc_ref, gf_ref, g2_ref, gw_ref, gbias_ref,
                         ng_ref, wo_ref, w1_ref, w3_ref, w2_ref, o_ref, *, hidden_chunk, heads):
    ys = jax.nn.gelu(ys_ref[0])
    s = ys * jax.nn.sigmoid(_dot(ys.astype(BF16), gw_ref[...]) + gbias_ref[...])
    dh = m_ref.shape[-1] // heads
    gated = []
    for h in range(heads):
        cols = slice(h * dh, (h + 1) * dh)
        hh = m_ref[0, :, cols].astype(F32)
        mu = jnp.mean(hh, axis=1, keepdims=True)
        cen = hh - mu
        var = jnp.mean(cen * cen, axis=1, keepdims=True)
        normed = cen * lax.rsqrt(var + EPS) * ng_ref[:, cols]
        gated.append((jax.nn.sigmoid(og_ref[0, :, cols].astype(F32)) * normed).astype(BF16))
    sw = ys_ref.shape[-1]
    y = _dot(s.astype(BF16), wo_ref[0:sw, :]) + _dot(jnp.concatenate(gated, axis=1), wo_ref[sw:, :])
    o_ref[0] = _ffn_tail(x_ref[0], y, gm_ref[0], g2_ref[...], sh_ref[0], sc_ref[0], gf_ref[0],
                         w1_ref, w3_ref, w2_ref, hidden_chunk)


def _outproj_odd_kernel(x_ref, a_ref, gate_ref, gm_ref, sh_ref, sc_ref, gf_ref, g2_ref, ng_ref, wo_ref, w1_ref,
                        w3_ref, w2_ref, fg_ref, o_ref, *, hidden_chunk, heads):
    hv = a_ref.shape[-1] // heads
    y = jnp.zeros(x_ref.shape[1:], F32)
    for h in range(heads):
        cols = slice(h * hv, (h + 1) * hv)
        o = a_ref[0, :, cols].astype(F32)
        mu = jnp.mean(o, axis=1, keepdims=True)
        cen = o - mu
        var = jnp.mean(cen * cen, axis=1, keepdims=True)
        normed = cen * lax.rsqrt(var + EPS) * ng_ref[:, cols]
        gated = (_silu(gate_ref[0, :, cols].astype(F32)) * normed).astype(BF16)
        y = y + _dot(gated, wo_ref[cols, :])
    out = _ffn_tail(x_ref[0], y, gm_ref[0], g2_ref[...], sh_ref[0], sc_ref[0], gf_ref[0],
                    w1_ref, w3_ref, w2_ref, hidden_chunk)
    o_ref[0] = _final_norm(out, fg_ref[...])


def _hidden_chunk(hidden):
    return -(-hidden // (2 * MXU_WIDTH)) * MXU_WIDTH


def _mod_specs(d, row_of_batch, slots):
    return [pl.BlockSpec((1, 1, d), lambda b, i, s=s: (row_of_batch(b) * N_MOD + s, 0, 0)) for s in slots]


def _outproj_even(x, ys, m, p, mods, row_of_batch, g2, glu_w, glu_b, norm_g, w_out, ffn, cast_items=()):
    bn, ln, d = x.shape
    sw, mw = ys.shape[-1], m.shape[-1]
    tm = _token_tile(ln)
    grid = (bn, ln // tm)
    cast = _CastAlong(cast_items, grid)
    res = lambda a: _resident(a.shape, lambda b, i: (0,) * a.ndim)
    tok = lambda w: pl.BlockSpec((1, tm, w), lambda b, i: (b, i, 0))
    row = lambda w: pl.BlockSpec((1, w), lambda b, i: (0, 0))
    kern = functools.partial(_outproj_even_kernel, hidden_chunk=_hidden_chunk(ffn[0].shape[-1]), heads=ML_HEADS)
    outs = pl.pallas_call(
        cast.wrap(kern, n_in=16, n_out=1),
        grid=grid,
        in_specs=[tok(d), tok(sw), tok(mw), pl.BlockSpec((1, tm, mw), lambda b, i: (b, i, 1))]
        + _mod_specs(d, row_of_batch, (2, 3, 4, 5))
        + [row(d), res(glu_w), row(sw), row(mw), res(w_out)] + [res(w) for w in ffn] + cast.in_specs(),
        out_specs=[tok(d)] + cast.out_specs(),
        out_shape=[jax.ShapeDtypeStruct((bn, ln, d), F32)] + cast.out_shapes(),
        compiler_params=_cparams(("arbitrary", "arbitrary")),
        name="outproj_ffn_even",
    )(x, ys, m, p, mods, mods, mods, mods, g2, glu_w, glu_b.reshape(1, sw), norm_g.reshape(1, mw), w_out, *ffn,
      *cast.operands())
    return outs[0], outs[1:]


def _outproj_odd(x, a, gate, mods, row_of_batch, g2, norm_g, w_out, ffn, final_g):
    bn, ln, d = x.shape
    vw = a.shape[-1]
    tm = _token_tile(ln)
    res = lambda a_: _resident(a_.shape, lambda b, i: (0,) * a_.ndim)
    tok = lambda w: pl.BlockSpec((1, tm, w), lambda b, i: (b, i, 0))
    row = lambda w: pl.BlockSpec((1, w), lambda b, i: (0, 0))
    return pl.pallas_call(
        functools.partial(_outproj_odd_kernel, hidden_chunk=_hidden_chunk(ffn[0].shape[-1]), heads=RET_HEADS),
        grid=(bn, ln // tm),
        in_specs=[tok(d), tok(vw), tok(vw)] + _mod_specs(d, row_of_batch, (2, 3, 4, 5))
        + [row(d), row(vw), res(w_out)] + [res(w) for w in ffn] + [row(d)],
        out_specs=tok(d),
        out_shape=jax.ShapeDtypeStruct((bn, ln, d), F32),
        compiler_params=_cparams(("arbitrary", "arbitrary")),
        name="outproj_ffn_odd",
    )(x, a, gate, mods, mods, mods, mods, g2, norm_g.reshape(1, vw), w_out, *ffn, final_g)


def _rope_heads(t, cos, sin, heads, hk):
    half = hk // 2
    out = []
    for h in range(heads):
        t1 = t[:, h * hk:h * hk + half]
        t2 = t[:, h * hk + half:(h + 1) * hk]
        out.append(t1 * cos - t2 * sin)
        out.append(t2 * cos + t1 * sin)
    return jnp.concatenate(out, axis=-1)


def _inproj_odd_kernel(x_ref, sh_ref, sc_ref, g_ref, cos_ref, sin_ref, w_ref, q_ref, k_ref, v_ref, gate_ref,
                       *, qk, vw, heads):
    h = _norm_mod(x_ref[0], g_ref[...], sh_ref[0], sc_ref[0]).astype(BF16)
    hk = qk // heads
    cos, sin = cos_ref[...], sin_ref[...]
    q = _rope_heads(_dot(h, w_ref[:, 0:qk]), cos, sin, heads, hk)
    q_ref[0] = (q * (hk ** -0.5)).astype(BF16)
    k = _rope_heads(_dot(h, w_ref[:, qk:2 * qk]), cos, sin, heads, hk)
    k_ref[0] = k.astype(BF16)
    v_ref[0] = _dot(h, w_ref[:, 2 * qk:2 * qk + vw]).astype(BF16)
    gate_ref[0] = _dot(h, w_ref[:, 2 * qk + vw:2 * qk + 2 * vw]).astype(BF16)


def _inproj_odd_ctx_kernel(x_ref, sh_ref, sc_ref, g_ref, w_ref, k_ref, v_ref, *, qk, vw):
    h = _norm_mod(x_ref[0], g_ref[...], sh_ref[0], sc_ref[0]).astype(BF16)
    k_ref[0] = _dot(h, w_ref[:, qk:2 * qk]).astype(BF16)
    v_ref[0] = _dot(h, w_ref[:, 2 * qk:2 * qk + vw]).astype(BF16)


def _inproj_odd(x, mods, row_of_batch, g, cos, sin, w, qk, vw, cast_items=()):
    bn, ln, d = x.shape
    tm = _token_tile(ln)
    grid = (bn, ln // tm)
    cast = _CastAlong(cast_items, grid)
    tok = lambda w_: pl.BlockSpec((1, tm, w_), lambda b, i: (b, i, 0))
    half = cos.shape[1]
    outs = pl.pallas_call(
        cast.wrap(functools.partial(_inproj_odd_kernel, qk=qk, vw=vw, heads=RET_HEADS), n_in=7, n_out=4),
        grid=grid,
        in_specs=[tok(d)] + _mod_specs(d, row_of_batch, (0, 1))
        + [pl.BlockSpec((1, d), lambda b, i: (0, 0)),
           pl.BlockSpec((tm, half), lambda b, i: (i, 0)), pl.BlockSpec((tm, half), lambda b, i: (i, 0)),
           _resident(w.shape, lambda b, i: (0, 0))] + cast.in_specs(),
        out_specs=[tok(qk), tok(qk), tok(vw), tok(vw)] + cast.out_specs(),
        out_shape=[jax.ShapeDtypeStruct((bn, ln, qk), BF16), jax.ShapeDtypeStruct((bn, ln, qk), BF16),
                   jax.ShapeDtypeStruct((bn, ln, vw), BF16), jax.ShapeDtypeStruct((bn, ln, vw), BF16)]
        + cast.out_shapes(),
        compiler_params=_cparams(("arbitrary", "arbitrary")),
        name="inproj_odd",
    )(x, mods, mods, g, cos, sin, w, *cast.operands())
    return outs[:4], outs[4:]


def _inproj_odd_ctx(x, mods, row_of_batch, g, w_kv, qk, vw):
    bn, ln, d = x.shape
    tm = _token_tile(ln)
    tok = lambda w_: pl.BlockSpec((1, tm, w_), lambda b, i: (b, i, 0))
    return pl.pallas_call(
        functools.partial(_inproj_odd_ctx_kernel, qk=qk, vw=vw),
        grid=(bn, ln // tm),
        in_specs=[tok(d)] + _mod_specs(d, row_of_batch, (0, 1))
        + [pl.BlockSpec((1, d), lambda b, i: (0, 0)), _resident(w_kv.shape, lambda b, i: (0, 0))],
        out_specs=[tok(qk), tok(vw)],
        out_shape=[jax.ShapeDtypeStruct((bn, ln, qk), BF16), jax.ShapeDtypeStruct((bn, ln, vw), BF16)],
        compiler_params=_cparams(("arbitrary", "arbitrary")),
        name="inproj_odd_ctx",
    )(x, mods, mods, g, w_kv)


def _retention_kernel(lg_ref, q_ref, k_ref, v_ref, kc_ref, vc_ref, o_ref, acc_s, sf_s, sb_s,
                      *, chunk, n_ctx_rows, n_lat_rows):
    t_n = chunk
    h = pl.program_id(1)
    lgf = jnp.full((1, 1), lg_ref[0, h], F32)
    lgb = jnp.full((1, 1), lg_ref[1, h], F32)
    nc_ctx, nc_lat = n_ctx_rows // t_n, n_lat_rows // t_n
    ti = lax.broadcasted_iota(jnp.int32, (t_n, t_n), 0)
    si = lax.broadcasted_iota(jnp.int32, (t_n, t_n), 1)
    diff = (ti - si).astype(F32)
    decay = jnp.where(diff >= 0, jnp.exp(lgf * jnp.maximum(diff, 0.0)), jnp.exp(lgb * jnp.maximum(-diff, 0.0)))
    pos = lax.broadcasted_iota(jnp.int32, (t_n, 1), 0).astype(F32)
    inter_f = jnp.exp(lgf * (pos + 1.0))
    inter_b = jnp.exp(lgb * (t_n - pos))
    wend_f = jnp.exp(lgf * (t_n - 1.0 - pos))
    wend_b = jnp.exp(lgb * pos)
    cd_f = jnp.exp(lgf * t_n)
    cd_b = jnp.exp(lgb * t_n)

    def rows_of(c):
        return pl.ds(pl.multiple_of(c * t_n, t_n), t_n)

    def bump(s_ref, k, v, wend, cd):
        kw = (k.astype(F32) * wend).astype(BF16)
        s_ref[...] = cd * s_ref[...] + _dot_tn(kw, v)

    sf_s[...] = jnp.zeros_like(sf_s)
    sb_s[...] = jnp.zeros_like(sb_s)

    def ctx_f(c, _):
        bump(sf_s, kc_ref[0, rows_of(c), :], vc_ref[0, rows_of(c), :], wend_f, cd_f)
        return 0
    lax.fori_loop(0, nc_ctx, ctx_f, 0)

    def ctx_b(i, _):
        c = nc_ctx - 1 - i
        bump(sb_s, kc_ref[0, rows_of(c), :], vc_ref[0, rows_of(c), :], wend_b, cd_b)
        return 0
    lax.fori_loop(0, nc_ctx, ctx_b, 0)

    def forward_part(rows):
        q, k, v = q_ref[0, rows, :], k_ref[0, rows, :], v_ref[0, rows, :]
        scores = (_dot_nt(q, k) * decay).astype(BF16)
        part = _dot(scores, v) + inter_f * _dot(q, sf_s[...].astype(BF16))
        bump(sf_s, k, v, wend_f, cd_f)
        return part

    def backward_part(rows):
        q, k, v = q_ref[0, rows, :], k_ref[0, rows, :], v_ref[0, rows, :]
        part = inter_b * _dot(q, sb_s[...].astype(BF16))
        bump(sb_s, k, v, wend_b, cd_b)
        return part

    def first_half(i, _):
        rf, rb = rows_of(i), rows_of(nc_lat - 1 - i)
        acc_s[rf, :] = forward_part(rf)
        acc_s[rb, :] = backward_part(rb)
        return 0
    pair_unroll = 4 if (nc_lat // 2) % 4 == 0 else 1
    lax.fori_loop(0, nc_lat // 2, first_half, 0, unroll=pair_unroll)

    def second_half(i, _):
        rf, rb = rows_of(i), rows_of(nc_lat - 1 - i)
        o_ref[0, rf, :] = (acc_s[rf, :] + forward_part(rf)).astype(BF16)
        o_ref[0, rb, :] = (acc_s[rb, :] + backward_part(rb)).astype(BF16)
        return 0
    lax.fori_loop(nc_lat // 2, nc_lat, second_half, 0, unroll=pair_unroll)


def _retention_mix(q, k, v, k_ctx, v_ctx, log_gamma):
    bn, ll, qk = q.shape
    lc = k_ctx.shape[1]
    vw = v.shape[-1]
    hk, hv = qk // RET_HEADS, vw // RET_HEADS
    chunk = 256 if (ll % 512 == 0 and lc % 256 == 0) else 128
    assert (ll // chunk) % 2 == 0, "latent chunks are visited in forward/backward pairs"
    head = lambda n_rows, w: pl.BlockSpec((1, n_rows, w), lambda b, h: (b, 0, h))
    return pl.pallas_call(
        functools.partial(_retention_kernel, chunk=chunk, n_ctx_rows=lc, n_lat_rows=ll),
        grid=(bn, RET_HEADS),
        in_specs=[
            pl.BlockSpec(memory_space=pltpu.SMEM),
            head(ll, hk), head(ll, hk), head(ll, hv), head(lc, hk), head(lc, hv),
        ],
        out_specs=head(ll, hv),
        out_shape=jax.ShapeDtypeStruct((bn, ll, vw), BF16),
        scratch_shapes=[pltpu.VMEM((ll, hv), F32), pltpu.VMEM((hk, hv), F32), pltpu.VMEM((hk, hv), F32)],
        compiler_params=_cparams(("arbitrary", "arbitrary")),
        name="retention_mix",
    )(log_gamma.astype(F32), q, k, v, k_ctx, v_ctx)


def _grid_rope(n_pos, hk):
    rows = n_pos // GRID_W
    row = jnp.repeat(jnp.arange(rows, dtype=F32), GRID_W)
    col = jnp.tile(jnp.arange(GRID_W, dtype=F32), rows)
    n_freq = hk // 4
    inv = ROPE_BASE ** (-jnp.arange(n_freq, dtype=F32) / n_freq)
    ang = jnp.concatenate([row[:, None] * inv, col[:, None] * inv], -1)
    return jnp.cos(ang), jnp.sin(ang)


def kernel(x, c, ctx, c_ctx, ada_w, ada_b, norm1_g, norm2_g, ab_w_in, ab_w_out, s5_a_re, s5_a_im, s5_log_dt,
           s5_b_re, s5_b_im, s5_c_re, s5_c_im, s5_d, s5_glu_w, s5_glu_b, ml_conv_w, ml_conv_b, ml_wq, ml_wk,
           ml_wv, ml_gate_b, ml_norm_g, ret_w_in, ret_w_out, ret_log_gamma, ret_norm_g, ffn_w1, ffn_w3, ffn_w2,
           final_g):
    bn, ln, d = x.shape
    depth = ada_w.shape[0]
    assert depth == 2, "one S5 || mLSTM layer followed by one retention layer"
    s5_width = s5_d.shape[-1]
    ml_width = ml_norm_g.shape[-1]
    assert ml_width == ML_HEADS * LANES and s5_width % S5_GROUP == 0

    r_pad = -(-(bn + 1) // SUBLANES) * SUBLANES
    vec = jnp.zeros((r_pad, d), F32).at[:bn].set(c).at[bn].set(c_ctx)
    mods_all = _modulation(vec, ada_w, ada_b).reshape(depth, r_pad * N_MOD, 1, d)
    lat_row = lambda b: b
    ctx_row = lambda b: bn

    mods = mods_all[0]
    g1 = norm1_g[0].reshape(1, d)
    g2 = norm2_g[0].reshape(1, d)
    w_in = ab_w_in[0]
    n_gate = w_in.shape[1] - s5_width - 2 * ml_width
    w_in = jnp.pad(w_in, ((0, 0), (0, LANES - n_gate))).astype(BF16)
    u_lat, p_lat, g_lat = _inproj_even(x, mods, lat_row, g1, w_in, s5_width)
    lc = ctx.shape[1]
    flat = lambda t: t.reshape(1, bn * lc, t.shape[-1])
    unflat = lambda t: t.reshape(bn, lc, t.shape[-1])
    u_ctx, p_ctx, g_ctx = (unflat(t) for t in _inproj_even(flat(ctx), mods, ctx_row, g1, w_in, s5_width))
    mats = _s5_prep(s5_a_re[0], s5_a_im[0], s5_log_dt[0], s5_b_re[0], s5_b_im[0], s5_c_re[0], s5_c_im[0],
                        s5_d[0])
    ys_ctx, ys_lat = _s5_mix(u_ctx, u_lat, mats)
    m_ctx, m_lat, ffn = _mlstm_mix(p_ctx, p_lat, g_ctx, g_lat, ml_conv_w[0], ml_conv_b[0], ml_wq[0], ml_wk[0],
                                   ml_wv[0], ml_gate_b[0], [(ffn_w1, 0), (ffn_w3, 0), (ffn_w2, 0)])
    glu_w = s5_glu_w[0].astype(BF16)
    w_out = ab_w_out[0].astype(BF16)
    x, (ret_in,) = _outproj_even(x, ys_lat, m_lat, p_lat, mods, lat_row, g2, glu_w, s5_glu_b[0], ml_norm_g[0],
                                 w_out, ffn, [(ret_w_in, 0)])
    ctx, _ = _outproj_even(flat(ctx), flat(ys_ctx), flat(m_ctx), flat(p_ctx), mods, ctx_row, g2, glu_w,
                           s5_glu_b[0], ml_norm_g[0], w_out, ffn)

    mods = mods_all[1]
    g1 = norm1_g[1].reshape(1, d)
    g2 = norm2_g[1].reshape(1, d)
    vw = ret_norm_g.shape[-1]
    qk = (ret_w_in.shape[-1] - 2 * vw) // 2
    cos, sin = _grid_rope(ln, qk // RET_HEADS)
    (q, k, v, gate), ffn = _inproj_odd(x, mods, lat_row, g1, cos, sin, ret_in, qk, vw,
                                       [(ffn_w1, 1), (ffn_w3, 1), (ffn_w2, 1)])
    k_ctx, v_ctx = (unflat(t) for t in _inproj_odd_ctx(ctx, mods, ctx_row, g1, ret_in, qk, vw))
    a = _retention_mix(q, k, v, k_ctx, v_ctx, ret_log_gamma[0])
    w_out = ret_w_out[0].astype(BF16)
    return _outproj_odd(x, a, gate, mods, lat_row, g2, ret_norm_g[0], w_out, ffn, final_g.reshape(1, d))
```
